```python
import math
import jax, jax.numpy as jnp
from jax import lax
import numpy as np

D_MODEL = 2048
BATCH = 16
SEQ = 2048
DEPTH = 1
DEC_BATCH = 8
DEC_SEQ = 32
PAST_LEN = 1024

CHUNK = 64
WINDOW = 128
WINDOW_CHUNKS = WINDOW // CHUNK
HEAD_DIM = 64
N_HEADS = D_MODEL // HEAD_DIM
N_KV_HEADS = N_HEADS // 8
GROUP = N_HEADS // N_KV_HEADS
D_ATTN = N_HEADS * HEAD_DIM
D_KV = N_KV_HEADS * HEAD_DIM
D_CONV = D_MODEL
CONV_WIDTH = 3
D_FF = -(-8 * D_MODEL // (3 * 256)) * 256
ROPE_THETA = 10000.0
EPS = 1e-6
SCALE = HEAD_DIM ** -0.5
NEG_INF = -1e30
SPLITS = [D_ATTN, D_ATTN + D_KV, D_ATTN + 2 * D_KV,
          D_ATTN + 2 * D_KV + D_CONV, D_ATTN + 2 * D_KV + 2 * D_CONV,
          D_ATTN + 2 * D_KV + 3 * D_CONV, D_ATTN + 2 * D_KV + 3 * D_CONV + D_MODEL]
D_IN_PROJ = D_ATTN + 2 * D_KV + 3 * D_CONV + 2 * D_MODEL

kernel_name = 'hybrid_swa_sink_shortconv_stream_step'


def _rmsnorm(x, g):
    xf = x.astype(jnp.float32)
    y = xf * lax.rsqrt(jnp.mean(xf * xf, axis=-1, keepdims=True) + EPS)
    return (y * g.astype(jnp.float32)).astype(x.dtype)


def _rope(x, pos):
    half = HEAD_DIM // 2
    inv = jnp.exp(-math.log(ROPE_THETA) * jnp.arange(half, dtype=jnp.float32) * (2.0 / HEAD_DIM))
    ang = pos.astype(jnp.float32)[:, None] * inv[None, :]
    cos = jnp.cos(ang)[:, None, :]
    sin = jnp.sin(ang)[:, None, :]
    xf = x.astype(jnp.float32)
    x1, x2 = xf[..., :half], xf[..., half:]
    return jnp.concatenate([x1 * cos - x2 * sin, x2 * cos + x1 * sin], axis=-1).astype(x.dtype)


def _sink_softmax(s, sink):
    m = jnp.maximum(jnp.max(s, axis=-1, keepdims=True), sink)
    p = jnp.exp(s - m)
    return p / (jnp.sum(p, axis=-1, keepdims=True) + jnp.exp(sink - m))


def _mixer_inputs(x, pos, norm_g, w_in, q_norm_g, k_norm_g):
    b, t, _ = x.shape
    xn = _rmsnorm(x, norm_g)
    q, k, v, b_gate, c_gate, h, g_attn, g_conv = jnp.split(xn @ w_in, SPLITS, axis=-1)
    q = _rope(_rmsnorm(q.reshape(b, t, N_HEADS, HEAD_DIM), q_norm_g), pos)
    k = _rope(_rmsnorm(k.reshape(b, t, N_KV_HEADS, HEAD_DIM), k_norm_g), pos)
    v = v.reshape(b, t, N_KV_HEADS, HEAD_DIM)
    u = c_gate * h
    return q, k, v, u, b_gate, g_attn, g_conv


def _causal_conv(up, conv_w, t):
    return sum(up[:, j:j + t] * conv_w[j] for j in range(CONV_WIDTH))


def _merge_and_ffn(x, attn, conv_y, b_gate, g_attn, g_conv, w_attn_proj, w_conv_proj, w_out,
                   norm_ffn_g, w_ffn_gate, w_ffn_up, w_ffn_down):
    z = b_gate * conv_y
    merged = jax.nn.sigmoid(g_attn) * (attn @ w_attn_proj) + jax.nn.sigmoid(g_conv) * (z @ w_conv_proj)
    x = x + merged @ w_out
    xn = _rmsnorm(x, norm_ffn_g)
    return x + (jax.nn.silu(xn @ w_ffn_gate) * (xn @ w_ffn_up)) @ w_ffn_down


def _banded_window_attention(q, k, v, sinks):
    b, s = q.shape[:2]
    nc = s // CHUNK
    nb = WINDOW_CHUNKS + 1
    pad = ((0, 0), (WINDOW_CHUNKS * CHUNK, 0), (0, 0), (0, 0))
    kp = jnp.pad(k, pad).reshape(b, nc + WINDOW_CHUNKS, CHUNK, N_KV_HEADS, HEAD_DIM)
    vp = jnp.pad(v, pad).reshape(b, nc + WINDOW_CHUNKS, CHUNK, N_KV_HEADS, HEAD_DIM)
    kb = jnp.concatenate([kp[:, j:j + nc] for j in range(nb)], axis=2)
    vb = jnp.concatenate([vp[:, j:j + nc] for j in range(nb)], axis=2)
    qb = q.reshape(b, nc, CHUNK, N_KV_HEADS, GROUP, HEAD_DIM)
    sc = jnp.einsum('bcqkgd,bcskd->bckgqs', qb, kb).astype(jnp.float32) * SCALE
    valid = (jnp.arange(nc)[:, None] + jnp.arange(nb)[None, :]) >= WINDOW_CHUNKS
    valid = jnp.repeat(valid, CHUNK, axis=1)
    sc = jnp.where(valid[None, :, None, None, None, :], sc, NEG_INF)
    sink = sinks.astype(jnp.float32).reshape(N_KV_HEADS, GROUP)[None, None, :, :, None, None]
    probs = _sink_softmax(sc, sink)
    o = jnp.einsum('bckgqs,bcskd->bcqkgd', probs.astype(v.dtype), vb)
    return o.reshape(b, s, D_ATTN)


def _prompt_layer(x, norm_mix_g, w_in, q_norm_g, k_norm_g, attn_sinks, conv_w, w_attn_proj,
                  w_conv_proj, w_out, norm_ffn_g, w_ffn_gate, w_ffn_up, w_ffn_down):
    s = x.shape[1]
    pos = jnp.arange(s)
    q, k, v, u, b_gate, g_attn, g_conv = _mixer_inputs(x, pos, norm_mix_g, w_in, q_norm_g, k_norm_g)
    attn = _banded_window_attention(q, k, v, attn_sinks)
    up = jnp.pad(u, ((0, 0), (CONV_WIDTH - 1, 0), (0, 0)))
    conv_y = _causal_conv(up, conv_w, s)
    y = _merge_and_ffn(x, attn, conv_y, b_gate, g_attn, g_conv, w_attn_proj, w_conv_proj, w_out,
                       norm_ffn_g, w_ffn_gate, w_ffn_up, w_ffn_down)
    return y, k[:, -WINDOW:], v[:, -WINDOW:], up[:, -(CONV_WIDTH - 1):]


def _sample_layer(x, cache_k, cache_v, state_conv, norm_mix_g, w_in, q_norm_g, k_norm_g, attn_sinks,
                  conv_w, w_attn_proj, w_conv_proj, w_out, norm_ffn_g, w_ffn_gate, w_ffn_up, w_ffn_down):
    b, t, _ = x.shape
    pos = PAST_LEN + jnp.arange(t)
    q, k, v, u, b_gate, g_attn, g_conv = _mixer_inputs(x, pos, norm_mix_g, w_in, q_norm_g, k_norm_g)
    kc = jnp.concatenate([cache_k, k], axis=1)
    vc = jnp.concatenate([cache_v, v], axis=1)
    qg = q.reshape(b, t, N_KV_HEADS, GROUP, HEAD_DIM)
    sc = jnp.einsum('bqkgd,bskd->bkgqs', qg, kc).astype(jnp.float32) * SCALE
    sink = attn_sinks.astype(jnp.float32).reshape(N_KV_HEADS, GROUP)[None, :, :, None, None]
    probs = _sink_softmax(sc, sink)
    attn = jnp.einsum('bkgqs,bskd->bqkgd', probs.astype(vc.dtype), vc).reshape(b, t, D_ATTN)
    up = jnp.concatenate([state_conv, u], axis=1)
    conv_y = _causal_conv(up, conv_w, t)
    y = _merge_and_ffn(x, attn, conv_y, b_gate, g_attn, g_conv, w_attn_proj, w_conv_proj, w_out,
                       norm_ffn_g, w_ffn_gate, w_ffn_up, w_ffn_down)
    return y, k, v, up[:, -(CONV_WIDTH - 1):]


def setup_inputs(seed: int = 0) -> dict:
    key = jax.random.key(seed)
    ks = jax.random.split(key, 18)
    win_cache = min(WINDOW, PAST_LEN)

    def nrm(k, shape, scale):
        return jax.random.normal(k, shape, jnp.float32) * scale

    return {
        'x_prompt': nrm(ks[0], (BATCH, SEQ, D_MODEL), 1.0),
        'x_sample': nrm(ks[1], (DEC_BATCH, DEC_SEQ, D_MODEL), 1.0),
        'cache_k': nrm(ks[2], (DEPTH, DEC_BATCH, win_cache, N_KV_HEADS, HEAD_DIM), 1.0),
        'cache_v': nrm(ks[3], (DEPTH, DEC_BATCH, win_cache, N_KV_HEADS, HEAD_DIM), 1.0),
        'state_conv': nrm(ks[4], (DEPTH, DEC_BATCH, CONV_WIDTH - 1, D_CONV), 1.0),
        'norm_mix_g': 1.0 + nrm(ks[5], (DEPTH, D_MODEL), 0.02),
        'w_in': nrm(ks[6], (DEPTH, D_MODEL, D_IN_PROJ), D_MODEL ** -0.5),
        'q_norm_g': 1.0 + nrm(ks[7], (DEPTH, HEAD_DIM), 0.02),
        'k_norm_g': 1.0 + nrm(ks[8], (DEPTH, HEAD_DIM), 0.02),
        'attn_sinks': nrm(ks[9], (DEPTH, N_HEADS), 0.5),
        'conv_w': nrm(ks[10], (DEPTH, CONV_WIDTH, D_CONV), CONV_WIDTH ** -0.5),
        'w_attn_proj': nrm(ks[11], (DEPTH, D_ATTN, D_MODEL), D_ATTN ** -0.5),
        'w_conv_proj': nrm(ks[12], (DEPTH, D_CONV, D_MODEL), D_CONV ** -0.5),
        'w_out': nrm(ks[13], (DEPTH, D_MODEL, D_MODEL), D_MODEL ** -0.5),
        'norm_ffn_g': 1.0 + nrm(ks[14], (DEPTH, D_MODEL), 0.02),
        'w_ffn_gate': nrm(ks[15], (DEPTH, D_MODEL, D_FF), D_MODEL ** -0.5),
        'w_ffn_up': nrm(ks[16], (DEPTH, D_MODEL, D_FF), D_MODEL ** -0.5),
        'w_ffn_down': nrm(ks[17], (DEPTH, D_FF, D_MODEL), D_FF ** -0.5),
    }


def reference(x_prompt, x_sample, cache_k, cache_v, state_conv, norm_mix_g, w_in, q_norm_g, k_norm_g,
              attn_sinks, conv_w, w_attn_proj, w_conv_proj, w_out, norm_ffn_g, w_ffn_gate, w_ffn_up,
              w_ffn_down):
    yp, ys = x_prompt, x_sample
    kp_l, vp_l, cp_l, ks_l, vs_l, cs_l = [], [], [], [], [], []
    for l in range(DEPTH):
        w = (norm_mix_g[l], w_in[l], q_norm_g[l], k_norm_g[l], attn_sinks[l], conv_w[l],
             w_attn_proj[l], w_conv_proj[l], w_out[l], norm_ffn_g[l], w_ffn_gate[l], w_ffn_up[l],
             w_ffn_down[l])
        yp, kp, vp, cp = _prompt_layer(yp, *w)
        ys, ksn, vsn, csn = _sample_layer(ys, cache_k[l], cache_v[l], state_conv[l], *w)
        kp_l.append(kp); vp_l.append(vp); cp_l.append(cp)
        ks_l.append(ksn); vs_l.append(vsn); cs_l.append(csn)
    return (yp, ys, jnp.stack(kp_l), jnp.stack(vp_l), jnp.stack(cp_l),
            jnp.stack(ks_l), jnp.stack(vs_l), jnp.stack(cs_l))
```

```python
import functools
import math

import jax
import jax.numpy as jnp
from jax import lax
from jax.experimental import pallas as pl
from jax.experimental.pallas import tpu as pltpu

F32 = jnp.float32
BF16 = jnp.bfloat16

CHUNK = 64
WINDOW = 128
PAST_LEN = 1024
ROPE_THETA = 10000.0
EPS = 1e-6
NEG_INF = -1e30
HEAD_DIM = 64
LANES = 128
PAIR = 2 * CHUNK
KEY_SPAN = PAIR + WINDOW

V7X_VMEM_BYTES = 64 * 1024 * 1024
VMEM_CAP_BYTES = V7X_VMEM_BYTES - 6 * 1024 * 1024


def _params(semantics, vmem_estimate):
    limit = int(min(max(vmem_estimate, 16 * 1024 * 1024), VMEM_CAP_BYTES))
    return pltpu.CompilerParams(dimension_semantics=semantics, vmem_limit_bytes=limit)


def _nbytes(shape, dtype):
    return math.prod(shape) * jnp.dtype(dtype).itemsize


def _dot(a, b):
    return jnp.dot(a, b, preferred_element_type=F32)


def _dot_nt(a, b):
    return lax.dot_general(a, b, (((1,), (1,)), ((), ())), preferred_element_type=F32)


def _lane_iota(shape):
    return lax.broadcasted_iota(jnp.int32, shape, 1)


def _rmsnorm_rows(x, g):
    y = x * lax.rsqrt(jnp.mean(x * x, axis=-1, keepdims=True) + EPS)
    return y * g


def _rmsnorm_kernel(x_ref, g_ref, o_ref):
    o_ref[...] = _rmsnorm_rows(x_ref[...], g_ref[...]).astype(BF16)


def _rmsnorm(x, g, tm):
    m, d = x.shape
    est = 2 * (_nbytes((tm, d), F32) + _nbytes((tm, d), BF16)) + 2 * _nbytes((tm, d), F32)
    return pl.pallas_call(
        _rmsnorm_kernel,
        out_shape=jax.ShapeDtypeStruct((m, d), BF16),
        grid=(m // tm,),
        in_specs=[pl.BlockSpec((tm, d), lambda i: (i, 0)),
                  pl.BlockSpec((1, d), lambda i: (0, 0))],
        out_specs=pl.BlockSpec((tm, d), lambda i: (i, 0)),
        compiler_params=_params(("parallel",), est),
        name="rmsnorm",
    )(x, g)


def _headnorm_rope(y, g, cos, sin):
    lane = _lane_iota(y.shape)
    low = lane < HEAD_DIM
    s = y * y
    s_low = jnp.sum(jnp.where(low, s, 0.0), axis=1, keepdims=True)
    s_high = jnp.sum(jnp.where(low, 0.0, s), axis=1, keepdims=True)
    ms = jnp.where(low, s_low, s_high) * (1.0 / HEAD_DIM)
    yn = y * lax.rsqrt(ms + EPS) * g
    half = HEAD_DIM // 2
    ahead = pltpu.roll(yn, LANES - half, axis=1)
    behind = pltpu.roll(yn, half, axis=1)
    swapped = jnp.where((lane & (HEAD_DIM - 1)) < half, ahead, behind)
    return yn * cos + swapped * sin


def _dup_halves(x):
    low = _lane_iota(x.shape) < HEAD_DIM
    r = pltpu.roll(x, HEAD_DIM, axis=1)
    return jnp.where(low, x, r), jnp.where(low, r, x)


def _q_kernel(xn_ref, w_ref, g_ref, cos_ref, sin_ref, o_ref, *, scale):
    y = _dot(xn_ref[...], w_ref[...])
    g = g_ref[...]
    cos = cos_ref[...]
    sin = sin_ref[...]
    for c in range(y.shape[1] // LANES):
        sl = slice(c * LANES, (c + 1) * LANES)
        o_ref[:, sl] = (_headnorm_rope(y[:, sl], g, cos, sin) * scale).astype(BF16)


def _q_proj(xn, w_in, g2, cos_t, sin_t, *, tm, tn, n_cols, scale):
    m, d = xn.shape
    rope_blocks = cos_t.shape[0] // tm
    est = (2 * (_nbytes((tm, d), BF16) + _nbytes((d, tn), BF16) + _nbytes((tm, tn), BF16)
                + 2 * _nbytes((tm, LANES), F32)) + 4 * _nbytes((tm, tn), F32))
    return pl.pallas_call(
        functools.partial(_q_kernel, scale=scale),
        out_shape=jax.ShapeDtypeStruct((m, n_cols), BF16),
        grid=(m // tm, n_cols // tn),
        in_specs=[pl.BlockSpec((tm, d), lambda i, j: (i, 0)),
                  pl.BlockSpec((d, tn), lambda i, j: (0, j)),
                  pl.BlockSpec((1, LANES), lambda i, j: (0, 0)),
                  pl.BlockSpec((tm, LANES), lambda i, j: (i % rope_blocks, 0)),
                  pl.BlockSpec((tm, LANES), lambda i, j: (i % rope_blocks, 0))],
        out_specs=pl.BlockSpec((tm, tn), lambda i, j: (i, j)),
        compiler_params=_params(("parallel", "arbitrary"), est),
        name="q_proj",
    )(xn, w_in, g2, cos_t, sin_t)


def _kv_kernel(xn_ref, w_ref, g_ref, cos_ref, sin_ref, k_ref, v_ref, k2_ref, v2_ref, *, d_kv):
    y = _dot(xn_ref[...], w_ref[...])
    g = g_ref[...]
    cos = cos_ref[...]
    sin = sin_ref[...]
    for c in range(d_kv // LANES):
        sl = slice(c * LANES, (c + 1) * LANES)
        k = _headnorm_rope(y[:, sl], g, cos, sin)
        v = y[:, d_kv + c * LANES:d_kv + (c + 1) * LANES]
        k_ref[:, sl] = k
        v_ref[:, sl] = v
        ka, kb = _dup_halves(k)
        va, vb = _dup_halves(v)
        lo = slice(2 * c * LANES, (2 * c + 1) * LANES)
        hi = slice((2 * c + 1) * LANES, (2 * c + 2) * LANES)
        k2_ref[:, lo] = ka.astype(BF16)
        k2_ref[:, hi] = kb.astype(BF16)
        v2_ref[:, lo] = va.astype(BF16)
        v2_ref[:, hi] = vb.astype(BF16)


def _kv_proj(xn, w_in, g2, cos_t, sin_t, *, tm, col_block, d_kv):
    m, d = xn.shape
    tn = 2 * d_kv
    rope_blocks = cos_t.shape[0] // tm
    est = (2 * (_nbytes((tm, d), BF16) + _nbytes((d, tn), BF16) + 2 * _nbytes((tm, d_kv), F32)
                + 2 * _nbytes((tm, tn), BF16) + 2 * _nbytes((tm, LANES), F32))
           + 4 * _nbytes((tm, tn), F32))
    return pl.pallas_call(
        functools.partial(_kv_kernel, d_kv=d_kv),
        out_shape=(jax.ShapeDtypeStruct((m, d_kv), F32), jax.ShapeDtypeStruct((m, d_kv), F32),
                   jax.ShapeDtypeStruct((m, tn), BF16), jax.ShapeDtypeStruct((m, tn), BF16)),
        grid=(m // tm,),
        in_specs=[pl.BlockSpec((tm, d), lambda i: (i, 0)),
                  pl.BlockSpec((d, tn), lambda i: (0, col_block)),
                  pl.BlockSpec((1, LANES), lambda i: (0, 0)),
                  pl.BlockSpec((tm, LANES), lambda i: (i % rope_blocks, 0)),
                  pl.BlockSpec((tm, LANES), lambda i: (i % rope_blocks, 0))],
        out_specs=(pl.BlockSpec((tm, d_kv), lambda i: (i, 0)),
                   pl.BlockSpec((tm, d_kv), lambda i: (i, 0)),
                   pl.BlockSpec((tm, tn), lambda i: (i, 0)),
                   pl.BlockSpec((tm, tn), lambda i: (i, 0))),
        compiler_params=_params(("parallel",), est),
        name="kv_proj",
    )(xn, w_in, g2, cos_t, sin_t)


def _conv_kernel(*refs, seq_len, tail_rows, has_state):
    if has_state:
        xn_ref, wb_ref, wc_ref, wh_ref, cw_ref, p1_ref, p2_ref, z_ref, u_ref = refs
    else:
        xn_ref, wb_ref, wc_ref, wh_ref, cw_ref, z_ref, u_ref = refs
    xn = xn_ref[...]
    u = _dot(xn, wc_ref[...]) * _dot(xn, wh_ref[...])
    tm = u.shape[0]
    pos = lax.broadcasted_iota(jnp.int32, u.shape, 0) & (seq_len - 1)
    prev1 = p1_ref[...] if has_state else 0.0
    prev2 = p2_ref[...] if has_state else 0.0
    u1 = jnp.where(pos == 0, prev1, pltpu.roll(u, 1, axis=0))
    u2 = jnp.where(pos < 2, prev2, pltpu.roll(u, 2, axis=0))
    w = cw_ref[...]
    y = u2 * w[0:1] + u1 * w[1:2] + u * w[2:3]
    z_ref[...] = (_dot(xn, wb_ref[...]) * y).astype(BF16)
    u_ref[...] = u[tm - tail_rows:, :]


def _conv_branch(xn, w_in, conv_w, prev, *, tm, tn, seq_len, tail_rows, col_b, col_c, col_h):
    m, d = xn.shape
    n = conv_w.shape[1]
    has_state = prev is not None
    assert seq_len & (seq_len - 1) == 0 and tm % seq_len == 0
    wspec = lambda off: pl.BlockSpec((d, tn), lambda i, j: (0, off // tn + j))
    in_specs = [pl.BlockSpec((tm, d), lambda i, j: (i, 0)),
                wspec(col_b), wspec(col_c), wspec(col_h),
                pl.BlockSpec((conv_w.shape[0], tn), lambda i, j: (0, j))]
    args = [xn, w_in, w_in, w_in, conv_w]
    if has_state:
        in_specs += [pl.BlockSpec((tm, tn), lambda i, j: (i, j))] * 2
        args += list(prev)
    est = (2 * (_nbytes((tm, d), BF16) + 3 * _nbytes((d, tn), BF16) + _nbytes((tm, tn), BF16)
                + _nbytes((tail_rows, tn), F32)) + 8 * _nbytes((tm, tn), F32))
    return pl.pallas_call(
        functools.partial(_conv_kernel, seq_len=seq_len, tail_rows=tail_rows, has_state=has_state),
        out_shape=(jax.ShapeDtypeStruct((m, n), BF16),
                   jax.ShapeDtypeStruct((m // tm * tail_rows, n), F32)),
        grid=(m // tm, n // tn),
        in_specs=in_specs,
        out_specs=(pl.BlockSpec((tm, tn), lambda i, j: (i, j)),
                   pl.BlockSpec((tail_rows, tn), lambda i, j: (i, j))),
        compiler_params=_params(("parallel", "arbitrary"), est),
        name="conv_branch",
    )(*args)


def _stack_heads(q_ref, rows, col0, group):
    parts = []
    for g in range(group):
        blk = q_ref[rows, col0 + (g // 2) * LANES:col0 + (g // 2 + 1) * LANES]
        keep = (_lane_iota(blk.shape) >= HEAD_DIM) == bool(g % 2)
        parts.append(jnp.where(keep, blk, jnp.zeros_like(blk)))
    return jnp.concatenate(parts, axis=0)


def _sink_column(sink_ref, head0, group, rows):
    return jnp.concatenate(
        [jnp.full((rows, 1), sink_ref[head0 + g], F32) for g in range(group)], axis=0)


def _store_heads(o_ref, rows, col0, o, group, nrows):
    for lp in range(group // 2):
        a = o[(2 * lp) * nrows:(2 * lp + 1) * nrows]
        b = o[(2 * lp + 1) * nrows:(2 * lp + 2) * nrows]
        low = _lane_iota(a.shape) < HEAD_DIM
        o_ref[rows, col0 + lp * LANES:col0 + (lp + 1) * LANES] = jnp.where(low, a, b).astype(BF16)


def _attn_kernel(sink_ref, q_ref, k_ref, v_ref, o_ref, *, tq, n_kv, group):
    t = pl.program_id(1)
    for p in range(tq // PAIR):
        rows = slice(p * PAIR, (p + 1) * PAIR)
        pair = t * (tq // PAIR) + p
        kstart = pl.multiple_of(jnp.maximum(pair - 1, 0) * PAIR, PAIR)
        shape = (group * PAIR, KEY_SPAN)
        q_chunk = 2 * pair + ((lax.broadcasted_iota(jnp.int32, shape, 0) & (PAIR - 1)) >> 6)
        k_chunk = (kstart >> 6) + (_lane_iota(shape) >> 6)
        back = q_chunk - k_chunk
        valid = (back >= 0) & (back <= WINDOW // CHUNK)
        for kv in range(n_kv):
            kk = k_ref[pl.ds(kstart, KEY_SPAN), kv * LANES:(kv + 1) * LANES]
            vv = v_ref[pl.ds(kstart, KEY_SPAN), kv * LANES:(kv + 1) * LANES]
            col0 = kv * group * HEAD_DIM
            lhs = _stack_heads(q_ref, rows, col0, group)
            s = jnp.where(valid, _dot_nt(lhs, kk), NEG_INF)
            sink = _sink_column(sink_ref, kv * group, group, PAIR)
            m = jnp.maximum(jnp.max(s, axis=1, keepdims=True), sink)
            e = jnp.exp(s - m)
            den = jnp.sum(e, axis=1, keepdims=True) + jnp.exp(sink - m)
            o = _dot(e.astype(BF16), vv) / den
            _store_heads(o_ref, rows, col0, o, group, PAIR)


def _attention(q, k2, v2, sinks, *, batch, seq, tq, n_kv, group):
    m, d = q.shape
    steps = seq // tq
    est = (2 * (2 * _nbytes((tq, d), BF16) + 2 * _nbytes((seq, n_kv * LANES), BF16))
           + 8 * _nbytes((group * PAIR, KEY_SPAN), F32))
    return pl.pallas_call(
        functools.partial(_attn_kernel, tq=tq, n_kv=n_kv, group=group),
        out_shape=jax.ShapeDtypeStruct((m, d), BF16),
        grid=(batch, steps),
        in_specs=[pl.BlockSpec(memory_space=pltpu.SMEM),
                  pl.BlockSpec((tq, d), lambda b, t: (b * steps + t, 0)),
                  pl.BlockSpec((seq, n_kv * LANES), lambda b, t: (b, 0)),
                  pl.BlockSpec((seq, n_kv * LANES), lambda b, t: (b, 0))],
        out_specs=pl.BlockSpec((tq, d), lambda b, t: (b * steps + t, 0)),
        compiler_params=_params(("parallel", "arbitrary"), est),
        name="attention",
    )(sinks, q, k2, v2)


def _attn_sample_kernel(sink_ref, q_ref, kn_ref, vn_ref, kc_ref, vc_ref, o_ref, *, n_kv, group):
    t = q_ref.shape[0]
    for kv in range(n_kv):
        pair_sl = slice((kv // 2) * LANES, (kv // 2 + 1) * LANES)
        kc = _dup_halves(kc_ref[:, pair_sl])[kv % 2].astype(BF16)
        vc = _dup_halves(vc_ref[:, pair_sl])[kv % 2].astype(BF16)
        kn = kn_ref[:, kv * LANES:(kv + 1) * LANES]
        vn = vn_ref[:, kv * LANES:(kv + 1) * LANES]
        col0 = kv * group * HEAD_DIM
        lhs = _stack_heads(q_ref, slice(0, t), col0, group)
        s_c = _dot_nt(lhs, kc)
        s_n = _dot_nt(lhs, kn)
        sink = _sink_column(sink_ref, kv * group, group, t)
        m = jnp.maximum(jnp.maximum(jnp.max(s_c, axis=1, keepdims=True),
                                    jnp.max(s_n, axis=1, keepdims=True)), sink)
        e_c = jnp.exp(s_c - m)
        e_n = jnp.exp(s_n - m)
        den = (jnp.sum(e_c, axis=1, keepdims=True) + jnp.sum(e_n, axis=1, keepdims=True)
               + jnp.exp(sink - m))
        o = (_dot(e_c.astype(BF16), vc) + _dot(e_n.astype(BF16), vn)) / den
        _store_heads(o_ref, slice(0, t), col0, o, group, t)


def _attention_sample(q, k2, v2, cache_k, cache_v, sinks, *, batch, t_new, n_kv, group):
    m, d = q.shape
    win = cache_k.shape[0] // batch
    d_kv = cache_k.shape[1]
    est = 4 * 1024 * 1024
    return pl.pallas_call(
        functools.partial(_attn_sample_kernel, n_kv=n_kv, group=group),
        out_shape=jax.ShapeDtypeStruct((m, d), BF16),
        grid=(batch,),
        in_specs=[pl.BlockSpec(memory_space=pltpu.SMEM),
                  pl.BlockSpec((t_new, d), lambda b: (b, 0)),
                  pl.BlockSpec((t_new, n_kv * LANES), lambda b: (b, 0)),
                  pl.BlockSpec((t_new, n_kv * LANES), lambda b: (b, 0)),
                  pl.BlockSpec((win, d_kv), lambda b: (b, 0)),
                  pl.BlockSpec((win, d_kv), lambda b: (b, 0))],
        out_specs=pl.BlockSpec((t_new, d), lambda b: (b, 0)),
        compiler_params=_params(("parallel",), est),
        name="attention_sample",
    )(sinks, q, k2, v2, cache_k, cache_v)


def _merge_kernel(xn_ref, a_ref, z_ref, wga_ref, wgc_ref, wap_ref, wcp_ref, o_ref):
    xn = xn_ref[...]
    attn = jax.nn.sigmoid(_dot(xn, wga_ref[...])) * _dot(a_ref[...], wap_ref[...])
    conv = jax.nn.sigmoid(_dot(xn, wgc_ref[...])) * _dot(z_ref[...], wcp_ref[...])
    o_ref[...] = (attn + conv).astype(BF16)


def _merge(xn, attn, z, w_in, w_ap, w_cp, *, tm, tn, col_ga, col_gc):
    m, d = xn.shape
    act = pl.BlockSpec((tm, d), lambda i, j: (i, 0))
    wspec = lambda off: pl.BlockSpec((d, tn), lambda i, j: (0, off // tn + j))
    est = (2 * (3 * _nbytes((tm, d), BF16) + 4 * _nbytes((d, tn), BF16) + _nbytes((tm, tn), BF16))
           + 6 * _nbytes((tm, tn), F32))
    return pl.pallas_call(
        _merge_kernel,
        out_shape=jax.ShapeDtypeStruct((m, d), BF16),
        grid=(m // tm, d // tn),
        in_specs=[act, act, act, wspec(col_ga), wspec(col_gc), wspec(0), wspec(0)],
        out_specs=pl.BlockSpec((tm, tn), lambda i, j: (i, j)),
        compiler_params=_params(("parallel", "arbitrary"), est),
        name="merge",
    )(xn, attn, z, w_in, w_in, w_ap, w_cp)


def _out_kernel(x_ref, m_ref, w_ref, o_ref):
    o_ref[...] = x_ref[...] + _dot(m_ref[...], w_ref[...])


def _out_proj(x, merged, w_out, *, tm, tn):
    m, d = x.shape
    est = (2 * (2 * _nbytes((tm, tn), F32) + _nbytes((tm, d), BF16) + _nbytes((d, tn), BF16))
           + 2 * _nbytes((tm, tn), F32))
    return pl.pallas_call(
        _out_kernel,
        out_shape=jax.ShapeDtypeStruct((m, d), F32),
        grid=(m // tm, d // tn),
        in_specs=[pl.BlockSpec((tm, tn), lambda i, j: (i, j)),
                  pl.BlockSpec((tm, d), lambda i, j: (i, 0)),
                  pl.BlockSpec((d, tn), lambda i, j: (0, j))],
        out_specs=pl.BlockSpec((tm, tn), lambda i, j: (i, j)),
        compiler_params=_params(("parallel", "arbitrary"), est),
        name="out_proj",
    )(x, merged, w_out)


def _ffn_kernel(x_ref, g_ref, wg_ref, wu_ref, wd_ref, o_ref, xn_ref):
    @pl.when(pl.program_id(1) == 0)
    def _():
        x = x_ref[...]
        xn_ref[...] = _rmsnorm_rows(x, g_ref[...]).astype(BF16)
        o_ref[...] = x

    xn = xn_ref[...]
    h = jax.nn.silu(_dot(xn, wg_ref[...])) * _dot(xn, wu_ref[...])
    o_ref[...] += _dot(h.astype(BF16), wd_ref[...])


def _ffn(x, g, w_gate, w_up, w_down, *, tm, tf):
    m, d = x.shape
    d_ff = w_gate.shape[1]
    est = (2 * (2 * _nbytes((tm, d), F32) + 3 * _nbytes((d, tf), BF16)) + _nbytes((tm, d), BF16)
           + 4 * _nbytes((tm, tf), F32) + _nbytes((tm, d), F32))
    return pl.pallas_call(
        _ffn_kernel,
        out_shape=jax.ShapeDtypeStruct((m, d), F32),
        grid=(m // tm, d_ff // tf),
        in_specs=[pl.BlockSpec((tm, d), lambda i, f: (i, 0)),
                  pl.BlockSpec((1, d), lambda i, f: (0, 0)),
                  pl.BlockSpec((d, tf), lambda i, f: (0, f)),
                  pl.BlockSpec((d, tf), lambda i, f: (0, f)),
                  pl.BlockSpec((tf, d), lambda i, f: (f, 0))],
        out_specs=pl.BlockSpec((tm, d), lambda i, f: (i, 0)),
        scratch_shapes=[pltpu.VMEM((tm, d), BF16)],
        compiler_params=_params(("parallel", "arbitrary"), est),
        name="ffn",
    )(x, g, w_gate, w_up, w_down)


def _rope_tables(pos, repeat):
    half = HEAD_DIM // 2
    inv = jnp.exp(-math.log(ROPE_THETA) * jnp.arange(half, dtype=F32) * (2.0 / HEAD_DIM))
    ang = pos.astype(F32)[:, None] * inv[None, :]
    cos = jnp.cos(ang)
    sin = jnp.sin(ang)
    cos_t = jnp.tile(jnp.concatenate([cos, cos], axis=-1), (repeat, LANES // HEAD_DIM))
    sin_t = jnp.tile(jnp.concatenate([-sin, sin], axis=-1), (repeat, LANES // HEAD_DIM))
    return cos_t, sin_t


def _tile_pair(v):
    return jnp.tile(v.reshape(1, HEAD_DIM), (1, LANES // HEAD_DIM))


def _layer(x, w, *, batch, seq, pos, tiles, conv_prev, sample_cache):
    d = x.shape[1]
    n_heads = w["sinks"].shape[0]
    d_attn = n_heads * HEAD_DIM
    d_kv = w["d_kv"]
    n_kv = d_kv // HEAD_DIM
    group = n_heads // n_kv
    d_conv = w["conv_w"].shape[1]
    col_k = d_attn
    col_b = d_attn + 2 * d_kv
    col_c = col_b + d_conv
    col_h = col_c + d_conv
    col_ga = col_h + d_conv
    col_gc = col_ga + d
    tm = tiles["tm"]

    rope_rows = max(seq, tm)
    cos_t, sin_t = _rope_tables(pos, rope_rows // seq)

    xn = _rmsnorm(x, w["norm_mix_g"], tiles["tm_norm"])
    q = _q_proj(xn, w["w_in"], _tile_pair(w["q_norm_g"]), cos_t, sin_t,
                tm=tm, tn=tiles["tn_q"], n_cols=d_attn, scale=HEAD_DIM ** -0.5)
    k, v, k2, v2 = _kv_proj(xn, w["w_in"], _tile_pair(w["k_norm_g"]), cos_t, sin_t,
                            tm=tm, col_block=col_k // (2 * d_kv), d_kv=d_kv)
    z, u_tail = _conv_branch(xn, w["w_in"], w["conv_w"], conv_prev,
                             tm=tiles["tm_conv"], tn=tiles["tn_conv"], seq_len=seq,
                             tail_rows=tiles["conv_tail"], col_b=col_b, col_c=col_c, col_h=col_h)
    if sample_cache is None:
        attn = _attention(q, k2, v2, w["sinks"], batch=batch, seq=seq, tq=tiles["tq"],
                          n_kv=n_kv, group=group)
    else:
        attn = _attention_sample(q, k2, v2, sample_cache[0], sample_cache[1], w["sinks"],
                                 batch=batch, t_new=seq, n_kv=n_kv, group=group)
    merged = _merge(xn, attn, z, w["w_in"], w["w_attn_proj"], w["w_conv_proj"],
                    tm=tm, tn=tiles["tn_merge"], col_ga=col_ga, col_gc=col_gc)
    x1 = _out_proj(x, merged, w["w_out"], tm=tm, tn=tiles["tn_out"])
    y = _ffn(x1, w["norm_ffn_g"], w["w_ffn_gate"], w["w_ffn_up"], w["w_ffn_down"],
             tm=tiles["tm_ffn"], tf=tiles["tf"])
    return y, k, v, u_tail


def kernel(x_prompt, x_sample, cache_k, cache_v, state_conv, norm_mix_g, w_in, q_norm_g, k_norm_g,
           attn_sinks, conv_w, w_attn_proj, w_conv_proj, w_out, norm_ffn_g, w_ffn_gate, w_ffn_up,
           w_ffn_down):
    depth = w_in.shape[0]
    assert depth == 1
    b_p, s_p, d = x_prompt.shape
    b_s, s_s, _ = x_sample.shape
    n_kv = cache_k.shape[3]
    d_kv = n_kv * HEAD_DIM
    win = cache_k.shape[2]
    n_state = state_conv.shape[2]

    w = {
        "d_kv": d_kv,
        "norm_mix_g": norm_mix_g[0].reshape(1, d),
        "w_in": w_in[0].astype(BF16),
        "q_norm_g": q_norm_g[0],
        "k_norm_g": k_norm_g[0],
        "sinks": attn_sinks[0],
        "conv_w": conv_w[0],
        "w_attn_proj": w_attn_proj[0].astype(BF16),
        "w_conv_proj": w_conv_proj[0].astype(BF16),
        "w_out": w_out[0].astype(BF16),
        "norm_ffn_g": norm_ffn_g[0].reshape(1, d),
        "w_ffn_gate": w_ffn_gate[0].astype(BF16),
        "w_ffn_up": w_ffn_up[0].astype(BF16),
        "w_ffn_down": w_ffn_down[0].astype(BF16),
    }

    m_p = b_p * s_p
    tm_p = min(1024, s_p)
    tiles_p = dict(tm=tm_p, tm_norm=min(512, s_p), tn_q=512, tm_conv=s_p, tn_conv=256, conv_tail=8,
                   tq=min(256, s_p), tn_merge=256, tn_out=512, tm_ffn=min(512, s_p), tf=512)
    y_p, k_p, v_p, u_p = _layer(x_prompt.reshape(m_p, d), w, batch=b_p, seq=s_p,
                                pos=jnp.arange(s_p), tiles=tiles_p, conv_prev=None,
                                sample_cache=None)
    keep = min(WINDOW, s_p)
    new_k_p = k_p.reshape(b_p, s_p, n_kv, HEAD_DIM)[:, s_p - keep:][None]
    new_v_p = v_p.reshape(b_p, s_p, n_kv, HEAD_DIM)[:, s_p - keep:][None]
    new_c_p = u_p.reshape(b_p, tiles_p["conv_tail"], -1)[:, tiles_p["conv_tail"] - n_state:][None]

    m_s = b_s * s_s
    st = state_conv[0]
    prev1 = jnp.pad(st[:, 1:2], ((0, 0), (0, s_s - 1), (0, 0))).reshape(m_s, -1)
    prev2 = jnp.pad(st, ((0, 0), (0, s_s - n_state), (0, 0))).reshape(m_s, -1)
    tiles_s = dict(tm=m_s, tm_norm=m_s, tn_q=512, tm_conv=m_s, tn_conv=256, conv_tail=m_s,
                   tq=None, tn_merge=256, tn_out=512, tm_ffn=m_s, tf=512)
    y_s, k_s, v_s, u_s = _layer(x_sample.reshape(m_s, d), w, batch=b_s, seq=s_s,
                                pos=PAST_LEN + jnp.arange(s_s), tiles=tiles_s,
                                conv_prev=(prev1, prev2),
                                sample_cache=(cache_k[0].reshape(b_s * win, d_kv),
                                              cache_v[0].reshape(b_s * win, d_kv)))
    new_k_s = k_s.reshape(b_s, s_s, n_kv, HEAD_DIM)[None]
    new_v_s = v_s.reshape(b_s, s_s, n_kv, HEAD_DIM)[None]
    new_c_s = u_s.reshape(b_s, s_s, -1)[:, s_s - n_state:][None]

    return (y_p.reshape(b_p, s_p, d), y_s.reshape(b_s, s_s, d), new_k_p, new_v_p, new_c_p,
            new_k_s, new_v_s, new_c_s)
```

```python
import functools
import math

import jax
import jax.numpy as jnp
from jax import lax
from jax.experimental import pallas as pl
from jax.experimental.pallas import tpu as pltpu

F32 = jnp.float32
BF16 = jnp.bfloat16

CHUNK = 64
WINDOW = 128
PAST_LEN = 1024
ROPE_THETA = 10000.0
EPS = 1e-6
NEG_INF = -1e30
HEAD_DIM = 64
LANES = 128
PAIR = 2 * CHUNK
KEY_SPAN = PAIR + WINDOW

V7X_MXU_DIM = 256
V7X_VMEM_BYTES = 64 * 1024 * 1024
VMEM_CAP_BYTES = V7X_VMEM_BYTES - 6 * 1024 * 1024


def _params(semantics, vmem_estimate):
    limit = int(min(max(vmem_estimate, 16 * 1024 * 1024), VMEM_CAP_BYTES))
    return pltpu.CompilerParams(dimension_semantics=semantics, vmem_limit_bytes=limit)


def _nbytes(shape, dtype):
    return math.prod(shape) * jnp.dtype(dtype).itemsize


def _dot(a, b):
    return jnp.dot(a, b, preferred_element_type=F32)


def _dot_nt(a, b):
    return lax.dot_general(a, b, (((1,), (1,)), ((), ())), preferred_element_type=F32)


def _lane_iota(shape):
    return lax.broadcasted_iota(jnp.int32, shape, 1)


def _rmsnorm_rows(x, g):
    y = x * lax.rsqrt(jnp.mean(x * x, axis=-1, keepdims=True) + EPS)
    return y * g


def _rmsnorm_kernel(x_ref, g_ref, o_ref):
    o_ref[...] = _rmsnorm_rows(x_ref[...], g_ref[...]).astype(BF16)


def _rmsnorm(x, g, tm):
    m, d = x.shape
    est = 2 * (_nbytes((tm, d), F32) + _nbytes((tm, d), BF16)) + 2 * _nbytes((tm, d), F32)
    return pl.pallas_call(
        _rmsnorm_kernel,
        out_shape=jax.ShapeDtypeStruct((m, d), BF16),
        grid=(m // tm,),
        in_specs=[pl.BlockSpec((tm, d), lambda i: (i, 0)),
                  pl.BlockSpec((1, d), lambda i: (0, 0))],
        out_specs=pl.BlockSpec((tm, d), lambda i: (i, 0)),
        compiler_params=_params(("parallel",), est),
        name="rmsnorm",
    )(x, g)


def _headnorm_rope(y, t_same, t_swap, seg):
    width = y.shape[1]
    s = y * y
    s_hi = s.astype(BF16)
    s_lo = (s - s_hi.astype(F32)).astype(BF16)
    ms = _dot(s_hi, seg) + _dot(s_lo, seg)
    half = HEAD_DIM // 2
    ahead = pltpu.roll(y, width - half, axis=1)
    behind = pltpu.roll(y, half, axis=1)
    swapped = jnp.where((_lane_iota(y.shape) & (HEAD_DIM - 1)) < half, ahead, behind)
    reps = width // t_same.shape[1]
    t_same = jnp.concatenate([t_same] * reps, axis=1)
    t_swap = jnp.concatenate([t_swap] * reps, axis=1)
    return (y * t_same + swapped * t_swap) * lax.rsqrt(ms + EPS)


def _dup_halves(x):
    low = _lane_iota(x.shape) < HEAD_DIM
    r = pltpu.roll(x, HEAD_DIM, axis=1)
    return jnp.where(low, x, r), jnp.where(low, r, x)


def _qkv_kernel(xn_ref, w_ref, seg_ref, qs_ref, qw_ref, ks_ref, kw_ref,
                q_ref, k_ref, v_ref, k2_ref, v2_ref, *, n_q_blocks, d_kv, rows):
    j = pl.program_id(1)
    tm, tn = q_ref.shape
    chunks = [slice(r * rows, (r + 1) * rows) for r in range(tm // rows)]
    seg = seg_ref[...]
    slab = seg.shape[0]

    @pl.when(j < n_q_blocks)
    def _():
        for rs in chunks:
            y = _dot(xn_ref[rs, :], w_ref[...])
            for c in range(tn // slab):
                sl = slice(c * slab, (c + 1) * slab)
                q_ref[rs, sl] = _headnorm_rope(y[:, sl], qs_ref[rs, :], qw_ref[rs, :],
                                               seg).astype(BF16)

    @pl.when(j == n_q_blocks)
    def _():
        for rs in chunks:
            y = _dot(xn_ref[rs, :], w_ref[...])
            k_all = jnp.concatenate(
                [_headnorm_rope(y[:, c * slab:(c + 1) * slab], ks_ref[rs, :], kw_ref[rs, :], seg)
                 for c in range(d_kv // slab)], axis=1)
            for c in range(d_kv // LANES):
                sl = slice(c * LANES, (c + 1) * LANES)
                k = k_all[:, sl]
                v = y[:, d_kv + c * LANES:d_kv + (c + 1) * LANES]
                k_ref[rs, sl] = k
                v_ref[rs, sl] = v
                ka, kb = _dup_halves(k)
                va, vb = _dup_halves(v)
                lo = slice(2 * c * LANES, (2 * c + 1) * LANES)
                hi = slice((2 * c + 1) * LANES, (2 * c + 2) * LANES)
                k2_ref[rs, lo] = ka.astype(BF16)
                k2_ref[rs, hi] = kb.astype(BF16)
                v2_ref[rs, lo] = va.astype(BF16)
                v2_ref[rs, hi] = vb.astype(BF16)


def _qkv_proj(xn, w_in, tables, *, tm, d_attn, d_kv, rows):
    m, d = xn.shape
    tn = 2 * d_kv
    n_q_blocks = d_attn // tn
    rope_blocks = tables[0].shape[0] // tm
    tab = pl.BlockSpec((tm, LANES), lambda i, j: (i % rope_blocks, 0))
    row_out = lambda width: pl.BlockSpec((tm, width), lambda i, j: (i, 0))
    head_of = jnp.arange(V7X_MXU_DIM) // HEAD_DIM
    seg = jnp.where(head_of[:, None] == head_of[None, :], 1.0 / HEAD_DIM, 0.0).astype(BF16)
    est = (2 * (_nbytes((tm, d), BF16) + _nbytes((d, tn), BF16) + 3 * _nbytes((tm, tn), BF16)
                + 2 * _nbytes((tm, d_kv), F32) + 4 * _nbytes((tm, LANES), F32))
           + 6 * _nbytes((rows, tn), F32))
    return pl.pallas_call(
        functools.partial(_qkv_kernel, n_q_blocks=n_q_blocks, d_kv=d_kv, rows=rows),
        out_shape=(jax.ShapeDtypeStruct((m, d_attn), BF16),
                   jax.ShapeDtypeStruct((m, d_kv), F32), jax.ShapeDtypeStruct((m, d_kv), F32),
                   jax.ShapeDtypeStruct((m, tn), BF16), jax.ShapeDtypeStruct((m, tn), BF16)),
        grid=(m // tm, n_q_blocks + 1),
        in_specs=[pl.BlockSpec((tm, d), lambda i, j: (i, 0)),
                  pl.BlockSpec((d, tn), lambda i, j: (0, j)),
                  pl.BlockSpec(seg.shape, lambda i, j: (0, 0)),
                  tab, tab, tab, tab],
        out_specs=(pl.BlockSpec((tm, tn), lambda i, j: (i, jnp.minimum(j, n_q_blocks - 1))),
                   row_out(d_kv), row_out(d_kv), row_out(tn), row_out(tn)),
        compiler_params=_params(("parallel", "arbitrary"), est),
        name="qkv_proj",
    )(xn, w_in, seg, *tables)


def _conv_kernel(*refs, seq_len, tail_rows, has_state):
    if has_state:
        xn_ref, wb_ref, wc_ref, wh_ref, cw_ref, p1_ref, p2_ref, z_ref, u_ref = refs
    else:
        xn_ref, wb_ref, wc_ref, wh_ref, cw_ref, z_ref, u_ref = refs
    xn = xn_ref[...]
    u = _dot(xn, wc_ref[...]) * _dot(xn, wh_ref[...])
    tm = u.shape[0]
    pos = lax.broadcasted_iota(jnp.int32, u.shape, 0) & (seq_len - 1)
    prev1 = p1_ref[...] if has_state else 0.0
    prev2 = p2_ref[...] if has_state else 0.0
    u1 = jnp.where(pos == 0, prev1, pltpu.roll(u, 1, axis=0))
    u2 = jnp.where(pos < 2, prev2, pltpu.roll(u, 2, axis=0))
    w = cw_ref[...]
    y = u2 * w[0:1] + u1 * w[1:2] + u * w[2:3]
    z_ref[...] = (_dot(xn, wb_ref[...]) * y).astype(BF16)
    u_ref[...] = u[tm - tail_rows:, :]


def _conv_branch(xn, w_in, conv_w, prev, *, tm, tn, seq_len, tail_rows, col_b, col_c, col_h):
    m, d = xn.shape
    n = conv_w.shape[1]
    has_state = prev is not None
    assert seq_len & (seq_len - 1) == 0 and tm % seq_len == 0
    wspec = lambda off: pl.BlockSpec((d, tn), lambda i, j: (0, off // tn + j))
    in_specs = [pl.BlockSpec((tm, d), lambda i, j: (i, 0)),
                wspec(col_b), wspec(col_c), wspec(col_h),
                pl.BlockSpec((conv_w.shape[0], tn), lambda i, j: (0, j))]
    args = [xn, w_in, w_in, w_in, conv_w]
    if has_state:
        in_specs += [pl.BlockSpec((tm, tn), lambda i, j: (i, j))] * 2
        args += list(prev)
    est = (2 * (_nbytes((tm, d), BF16) + 3 * _nbytes((d, tn), BF16) + _nbytes((tm, tn), BF16)
                + _nbytes((tail_rows, tn), F32)) + 8 * _nbytes((tm, tn), F32))
    return pl.pallas_call(
        functools.partial(_conv_kernel, seq_len=seq_len, tail_rows=tail_rows, has_state=has_state),
        out_shape=(jax.ShapeDtypeStruct((m, n), BF16),
                   jax.ShapeDtypeStruct((m // tm * tail_rows, n), F32)),
        grid=(m // tm, n // tn),
        in_specs=in_specs,
        out_specs=(pl.BlockSpec((tm, tn), lambda i, j: (i, j)),
                   pl.BlockSpec((tail_rows, tn), lambda i, j: (i, j))),
        compiler_params=_params(("parallel", "arbitrary"), est),
        name="conv_branch",
    )(*args)


def _stack_heads(q_ref, rows, col0, group):
    parts = []
    for g in range(group):
        blk = q_ref[rows, col0 + (g // 2) * LANES:col0 + (g // 2 + 1) * LANES]
        keep = (_lane_iota(blk.shape) >= HEAD_DIM) == bool(g % 2)
        parts.append(jnp.where(keep, blk, jnp.zeros_like(blk)))
    return jnp.concatenate(parts, axis=0)


def _sink_column(sink_ref, head0, group, rows):
    return jnp.concatenate(
        [jnp.full((rows, 1), sink_ref[head0 + g], F32) for g in range(group)], axis=0)


def _store_heads(o_ref, rows, col0, o, group, nrows):
    for lp in range(group // 2):
        a = o[(2 * lp) * nrows:(2 * lp + 1) * nrows]
        b = o[(2 * lp + 1) * nrows:(2 * lp + 2) * nrows]
        low = _lane_iota(a.shape) < HEAD_DIM
        o_ref[rows, col0 + lp * LANES:col0 + (lp + 1) * LANES] = jnp.where(low, a, b).astype(BF16)


def _attn_kernel(sink_ref, q_ref, k_ref, v_ref, o_ref, *, tq, n_kv, group):
    t = pl.program_id(1)
    for p in range(tq // PAIR):
        rows = slice(p * PAIR, (p + 1) * PAIR)
        pair = t * (tq // PAIR) + p
        kstart = pl.multiple_of(jnp.maximum(pair - 1, 0) * PAIR, PAIR)
        shape = (KEY_SPAN, PAIR)
        k_chunk = (kstart >> 6) + (lax.broadcasted_iota(jnp.int32, shape, 0) >> 6)
        back = 2 * pair + (_lane_iota(shape) >> 6) - k_chunk
        valid = (back >= 0) & (back <= WINDOW // CHUNK)
        for kv in range(n_kv):
            kk = k_ref[pl.ds(kstart, KEY_SPAN), kv * LANES:(kv + 1) * LANES]
            vv = v_ref[pl.ds(kstart, KEY_SPAN), kv * LANES:(kv + 1) * LANES]
            vt = vv.astype(F32).T.astype(BF16)
            col0 = kv * group * HEAD_DIM
            st = _dot_nt(kk, _stack_heads(q_ref, rows, col0, group))
            probs, rden = [], []
            for g in range(group):
                s = jnp.where(valid, st[:, g * PAIR:(g + 1) * PAIR], NEG_INF)
                sink = sink_ref[kv * group + g]
                m = jnp.maximum(jnp.max(s, axis=0, keepdims=True), sink)
                e = jnp.exp(s - m)
                den = jnp.sum(e, axis=0, keepdims=True) + jnp.exp(sink - m)
                probs.append(e.astype(BF16))
                rden.append(1.0 / den)
            ot = _dot(vt, jnp.concatenate(probs, axis=1)) * jnp.concatenate(rden, axis=1)
            for lp in range(group // 2):
                a = ot[:HEAD_DIM, (2 * lp) * PAIR:(2 * lp + 1) * PAIR]
                b = ot[HEAD_DIM:, (2 * lp + 1) * PAIR:(2 * lp + 2) * PAIR]
                o_ref[rows, col0 + lp * LANES:col0 + (lp + 1) * LANES] = (
                    jnp.concatenate([a, b], axis=0).T.astype(BF16))


def _attention(q, k2, v2, sinks, *, batch, seq, tq, n_kv, group):
    m, d = q.shape
    steps = seq // tq
    est = (2 * (2 * _nbytes((tq, d), BF16) + 2 * _nbytes((seq, n_kv * LANES), BF16))
           + 8 * _nbytes((group * PAIR, KEY_SPAN), F32))
    return pl.pallas_call(
        functools.partial(_attn_kernel, tq=tq, n_kv=n_kv, group=group),
        out_shape=jax.ShapeDtypeStruct((m, d), BF16),
        grid=(batch, steps),
        in_specs=[pl.BlockSpec(memory_space=pltpu.SMEM),
                  pl.BlockSpec((tq, d), lambda b, t: (b * steps + t, 0)),
                  pl.BlockSpec((seq, n_kv * LANES), lambda b, t: (b, 0)),
                  pl.BlockSpec((seq, n_kv * LANES), lambda b, t: (b, 0))],
        out_specs=pl.BlockSpec((tq, d), lambda b, t: (b * steps + t, 0)),
        compiler_params=_params(("parallel", "arbitrary"), est),
        name="attention",
    )(sinks, q, k2, v2)


def _attn_sample_kernel(sink_ref, q_ref, kn_ref, vn_ref, kc_ref, vc_ref, o_ref, *, n_kv, group):
    t = q_ref.shape[0]
    for kv in range(n_kv):
        pair_sl = slice((kv // 2) * LANES, (kv // 2 + 1) * LANES)
        kc = _dup_halves(kc_ref[:, pair_sl])[kv % 2].astype(BF16)
        vc = _dup_halves(vc_ref[:, pair_sl])[kv % 2].astype(BF16)
        kn = kn_ref[:, kv * LANES:(kv + 1) * LANES]
        vn = vn_ref[:, kv * LANES:(kv + 1) * LANES]
        col0 = kv * group * HEAD_DIM
        lhs = _stack_heads(q_ref, slice(0, t), col0, group)
        s_c = _dot_nt(lhs, kc)
        s_n = _dot_nt(lhs, kn)
        sink = _sink_column(sink_ref, kv * group, group, t)
        m = jnp.maximum(jnp.maximum(jnp.max(s_c, axis=1, keepdims=True),
                                    jnp.max(s_n, axis=1, keepdims=True)), sink)
        e_c = jnp.exp(s_c - m)
        e_n = jnp.exp(s_n - m)
        den = (jnp.sum(e_c, axis=1, keepdims=True) + jnp.sum(e_n, axis=1, keepdims=True)
               + jnp.exp(sink - m))
        o = (_dot(e_c.astype(BF16), vc) + _dot(e_n.astype(BF16), vn)) / den
        _store_heads(o_ref, slice(0, t), col0, o, group, t)


def _attention_sample(q, k2, v2, cache_k, cache_v, sinks, *, batch, t_new, n_kv, group):
    m, d = q.shape
    win = cache_k.shape[0] // batch
    d_kv = cache_k.shape[1]
    est = 4 * 1024 * 1024
    return pl.pallas_call(
        functools.partial(_attn_sample_kernel, n_kv=n_kv, group=group),
        out_shape=jax.ShapeDtypeStruct((m, d), BF16),
        grid=(batch,),
        in_specs=[pl.BlockSpec(memory_space=pltpu.SMEM),
                  pl.BlockSpec((t_new, d), lambda b: (b, 0)),
                  pl.BlockSpec((t_new, n_kv * LANES), lambda b: (b, 0)),
                  pl.BlockSpec((t_new, n_kv * LANES), lambda b: (b, 0)),
                  pl.BlockSpec((win, d_kv), lambda b: (b, 0)),
                  pl.BlockSpec((win, d_kv), lambda b: (b, 0))],
        out_specs=pl.BlockSpec((t_new, d), lambda b: (b, 0)),
        compiler_params=_params(("parallel",), est),
        name="attention_sample",
    )(sinks, q, k2, v2, cache_k, cache_v)


def _merge_kernel(xn_ref, a_ref, z_ref, wga_ref, wgc_ref, wap_ref, wcp_ref, o_ref):
    xn = xn_ref[...]
    attn = jax.nn.sigmoid(_dot(xn, wga_ref[...])) * _dot(a_ref[...], wap_ref[...])
    conv = jax.nn.sigmoid(_dot(xn, wgc_ref[...])) * _dot(z_ref[...], wcp_ref[...])
    o_ref[...] = (attn + conv).astype(BF16)


def _merge(xn, attn, z, w_in, w_ap, w_cp, *, tm, tn, col_ga, col_gc):
    m, d = xn.shape
    act = pl.BlockSpec((tm, d), lambda i, j: (i, 0))
    wspec = lambda off: pl.BlockSpec((d, tn), lambda i, j: (0, off // tn + j))
    est = (2 * (3 * _nbytes((tm, d), BF16) + 4 * _nbytes((d, tn), BF16) + _nbytes((tm, tn), BF16))
           + 6 * _nbytes((tm, tn), F32))
    return pl.pallas_call(
        _merge_kernel,
        out_shape=jax.ShapeDtypeStruct((m, d), BF16),
        grid=(m // tm, d // tn),
        in_specs=[act, act, act, wspec(col_ga), wspec(col_gc), wspec(0), wspec(0)],
        out_specs=pl.BlockSpec((tm, tn), lambda i, j: (i, j)),
        compiler_params=_params(("parallel", "arbitrary"), est),
        name="merge",
    )(xn, attn, z, w_in, w_in, w_ap, w_cp)


def _out_kernel(x_ref, m_ref, w_ref, o_ref):
    o_ref[...] = x_ref[...] + _dot(m_ref[...], w_ref[...])


def _out_proj(x, merged, w_out, *, tm, tn):
    m, d = x.shape
    est = (2 * (2 * _nbytes((tm, tn), F32) + _nbytes((tm, d), BF16) + _nbytes((d, tn), BF16))
           + 2 * _nbytes((tm, tn), F32))
    return pl.pallas_call(
        _out_kernel,
        out_shape=jax.ShapeDtypeStruct((m, d), F32),
        grid=(m // tm, d // tn),
        in_specs=[pl.BlockSpec((tm, tn), lambda i, j: (i, j)),
                  pl.BlockSpec((tm, d), lambda i, j: (i, 0)),
                  pl.BlockSpec((d, tn), lambda i, j: (0, j))],
        out_specs=pl.BlockSpec((tm, tn), lambda i, j: (i, j)),
        compiler_params=_params(("parallel", "arbitrary"), est),
        name="out_proj",
    )(x, merged, w_out)


def _ffn_kernel(x_ref, g_ref, wg_ref, wu_ref, wd_ref, o_ref, xn_ref):
    @pl.when(pl.program_id(1) == 0)
    def _():
        x = x_ref[...]
        xn_ref[...] = _rmsnorm_rows(x, g_ref[...]).astype(BF16)
        o_ref[...] = x

    xn = xn_ref[...]
    h = jax.nn.silu(_dot(xn, wg_ref[...])) * _dot(xn, wu_ref[...])
    o_ref[...] += _dot(h.astype(BF16), wd_ref[...])


def _ffn(x, g, w_gate, w_up, w_down, *, tm, tf):
    m, d = x.shape
    d_ff = w_gate.shape[1]
    est = (2 * (2 * _nbytes((tm, d), F32) + 3 * _nbytes((d, tf), BF16)) + _nbytes((tm, d), BF16)
           + 4 * _nbytes((tm, tf), F32) + _nbytes((tm, d), F32))
    return pl.pallas_call(
        _ffn_kernel,
        out_shape=jax.ShapeDtypeStruct((m, d), F32),
        grid=(m // tm, d_ff // tf),
        in_specs=[pl.BlockSpec((tm, d), lambda i, f: (i, 0)),
                  pl.BlockSpec((1, d), lambda i, f: (0, 0)),
                  pl.BlockSpec((d, tf), lambda i, f: (0, f)),
                  pl.BlockSpec((d, tf), lambda i, f: (0, f)),
                  pl.BlockSpec((tf, d), lambda i, f: (f, 0))],
        out_specs=pl.BlockSpec((tm, d), lambda i, f: (i, 0)),
        scratch_shapes=[pltpu.VMEM((tm, d), BF16)],
        compiler_params=_params(("parallel", "arbitrary"), est),
        name="ffn",
    )(x, g, w_gate, w_up, w_down)


def _rope_tables(pos, repeat, g, scale):
    half = HEAD_DIM // 2
    inv = jnp.exp(-math.log(ROPE_THETA) * jnp.arange(half, dtype=F32) * (2.0 / HEAD_DIM))
    ang = pos.astype(F32)[:, None] * inv[None, :]
    cos = jnp.cos(ang)
    sin = jnp.sin(ang)
    g_swap = jnp.concatenate([g[half:], g[:half]])
    t_same = g[None, :] * jnp.concatenate([cos, cos], axis=-1) * scale
    t_swap = g_swap[None, :] * jnp.concatenate([-sin, sin], axis=-1) * scale
    reps = (repeat, LANES // HEAD_DIM)
    return jnp.tile(t_same, reps), jnp.tile(t_swap, reps)


def _layer(x, w, *, batch, seq, pos, tiles, conv_prev, sample_cache):
    d = x.shape[1]
    n_heads = w["sinks"].shape[0]
    d_attn = n_heads * HEAD_DIM
    d_kv = w["d_kv"]
    n_kv = d_kv // HEAD_DIM
    group = n_heads // n_kv
    d_conv = w["conv_w"].shape[1]
    col_b = d_attn + 2 * d_kv
    col_c = col_b + d_conv
    col_h = col_c + d_conv
    col_ga = col_h + d_conv
    col_gc = col_ga + d
    tm = tiles["tm"]

    rope_rows = max(seq, tm)
    tables = (_rope_tables(pos, rope_rows // seq, w["q_norm_g"], HEAD_DIM ** -0.5)
              + _rope_tables(pos, rope_rows // seq, w["k_norm_g"], 1.0))

    xn = _rmsnorm(x, w["norm_mix_g"], tiles["tm_norm"])
    q, k, v, k2, v2 = _qkv_proj(xn, w["w_in"], tables, tm=tm, d_attn=d_attn, d_kv=d_kv,
                                rows=tiles["qkv_rows"])
    z, u_tail = _conv_branch(xn, w["w_in"], w["conv_w"], conv_prev,
                             tm=tiles["tm_conv"], tn=tiles["tn_conv"], seq_len=seq,
                             tail_rows=tiles["conv_tail"], col_b=col_b, col_c=col_c, col_h=col_h)
    if sample_cache is None:
        attn = _attention(q, k2, v2, w["sinks"], batch=batch, seq=seq, tq=tiles["tq"],
                          n_kv=n_kv, group=group)
    else:
        attn = _attention_sample(q, k2, v2, sample_cache[0], sample_cache[1], w["sinks"],
                                 batch=batch, t_new=seq, n_kv=n_kv, group=group)
    merged = _merge(xn, attn, z, w["w_in"], w["w_attn_proj"], w["w_conv_proj"],
                    tm=tm, tn=tiles["tn_merge"], col_ga=col_ga, col_gc=col_gc)
    x1 = _out_proj(x, merged, w["w_out"], tm=tm, tn=tiles["tn_out"])
    y = _ffn(x1, w["norm_ffn_g"], w["w_ffn_gate"], w["w_ffn_up"], w["w_ffn_down"],
             tm=tiles["tm_ffn"], tf=tiles["tf"])
    return y, k, v, u_tail


def kernel(x_prompt, x_sample, cache_k, cache_v, state_conv, norm_mix_g, w_in, q_norm_g, k_norm_g,
           attn_sinks, conv_w, w_attn_proj, w_conv_proj, w_out, norm_ffn_g, w_ffn_gate, w_ffn_up,
           w_ffn_down):
    depth = w_in.shape[0]
    assert depth == 1
    b_p, s_p, d = x_prompt.shape
    b_s, s_s, _ = x_sample.shape
    n_kv = cache_k.shape[3]
    d_kv = n_kv * HEAD_DIM
    win = cache_k.shape[2]
    n_state = state_conv.shape[2]

    w = {
        "d_kv": d_kv,
        "norm_mix_g": norm_mix_g[0].reshape(1, d),
        "w_in": w_in[0].astype(BF16),
        "q_norm_g": q_norm_g[0],
        "k_norm_g": k_norm_g[0],
        "sinks": attn_sinks[0],
        "conv_w": conv_w[0],
        "w_attn_proj": w_attn_proj[0].astype(BF16),
        "w_conv_proj": w_conv_proj[0].astype(BF16),
        "w_out": w_out[0].astype(BF16),
        "norm_ffn_g": norm_ffn_g[0].reshape(1, d),
        "w_ffn_gate": w_ffn_gate[0].astype(BF16),
        "w_ffn_up": w_ffn_up[0].astype(BF16),
        "w_ffn_down": w_ffn_down[0].astype(BF16),
    }

    m_p = b_p * s_p
    tm_p = min(1024, s_p)
    tiles_p = dict(tm=tm_p, tm_norm=min(512, s_p), qkv_rows=256, tm_conv=s_p, tn_conv=256, conv_tail=8,
                   tq=min(256, s_p), tn_merge=256, tn_out=512, tm_ffn=min(512, s_p), tf=512)
    y_p, k_p, v_p, u_p = _layer(x_prompt.reshape(m_p, d), w, batch=b_p, seq=s_p,
                                pos=jnp.arange(s_p), tiles=tiles_p, conv_prev=None,
                                sample_cache=None)
    keep = min(WINDOW, s_p)
    new_k_p = k_p.reshape(b_p, s_p, n_kv, HEAD_DIM)[:, s_p - keep:][None]
    new_v_p = v_p.reshape(b_p, s_p, n_kv, HEAD_DIM)[:, s_p - keep:][None]
    new_c_p = u_p.reshape(b_p, tiles_p["conv_tail"], -1)[:, tiles_p["conv_tail"] - n_state:][None]

    m_s = b_s * s_s
    st = state_conv[0]
    prev1 = jnp.pad(st[:, 1:2], ((0, 0), (0, s_s - 1), (0, 0))).reshape(m_s, -1)
    prev2 = jnp.pad(st, ((0, 0), (0, s_s - n_state), (0, 0))).reshape(m_s, -1)
    tiles_s = dict(tm=m_s, tm_norm=m_s, qkv_rows=m_s, tm_conv=m_s, tn_conv=256, conv_tail=m_s,
                   tq=None, tn_merge=256, tn_out=512, tm_ffn=m_s, tf=512)
    y_s, k_s, v_s, u_s = _layer(x_sample.reshape(m_s, d), w, batch=b_s, seq=s_s,
                                pos=PAST_LEN + jnp.arange(s_s), tiles=tiles_s,
                                conv_prev=(prev1, prev2),
                                sample_cache=(cache_k[0].reshape(b_s * win, d_kv),
                                              cache_v[0].reshape(b_s * win, d_kv)))
    new_k_s = k_s.reshape(b_s, s_s, n_kv, HEAD_DIM)[None]
    new_v_s = v_s.reshape(b_s, s_s, n_kv, HEAD_DIM)[None]
    new_c_s = u_s.reshape(b_s, s_s, -1)[:, s_s - n_state:][None]

    return (y_p.reshape(b_p, s_p, d), y_s.reshape(b_s, s_s, d), new_k_p, new_v_p, new_c_p,
            new_k_s, new_v_s, new_c_s)
```

```python
import functools
import math

import jax
import jax.numpy as jnp
from jax import lax
from jax.experimental import pallas as pl
from jax.experimental.pallas import tpu as pltpu

F32 = jnp.float32
BF16 = jnp.bfloat16

CHUNK = 64
WINDOW = 128
PAST_LEN = 1024
ROPE_THETA = 10000.0
EPS = 1e-6
NEG_INF = -1e30
HEAD_DIM = 64
LANES = 128
PAIR = 2 * CHUNK
KEY_SPAN = PAIR + WINDOW

V7X_MXU_DIM = 256
V7X_VMEM_BYTES = 64 * 1024 * 1024
VMEM_CAP_BYTES = V7X_VMEM_BYTES - 6 * 1024 * 1024


def _params(semantics, vmem_estimate):
    limit = int(min(max(vmem_estimate, 16 * 1024 * 1024), VMEM_CAP_BYTES))
    return pltpu.CompilerParams(dimension_semantics=semantics, vmem_limit_bytes=limit)


def _nbytes(shape, dtype):
    return math.prod(shape) * jnp.dtype(dtype).itemsize


def _dot(a, b):
    return jnp.dot(a, b, preferred_element_type=F32)


def _dot_nt(a, b):
    return lax.dot_general(a, b, (((1,), (1,)), ((), ())), preferred_element_type=F32)


def _lane_iota(shape):
    return lax.broadcasted_iota(jnp.int32, shape, 1)


def _rmsnorm_rows(x, g):
    y = x * lax.rsqrt(jnp.mean(x * x, axis=-1, keepdims=True) + EPS)
    return y * g


def _rmsnorm_kernel(x_ref, g_ref, o_ref):
    o_ref[...] = _rmsnorm_rows(x_ref[...], g_ref[...]).astype(BF16)


def _rmsnorm(x, g, tm):
    m, d = x.shape
    est = 2 * (_nbytes((tm, d), F32) + _nbytes((tm, d), BF16)) + 2 * _nbytes((tm, d), F32)
    return pl.pallas_call(
        _rmsnorm_kernel,
        out_shape=jax.ShapeDtypeStruct((m, d), BF16),
        grid=(m // tm,),
        in_specs=[pl.BlockSpec((tm, d), lambda i: (i, 0)),
                  pl.BlockSpec((1, d), lambda i: (0, 0))],
        out_specs=pl.BlockSpec((tm, d), lambda i: (i, 0)),
        compiler_params=_params(("parallel",), est),
        name="rmsnorm",
    )(x, g)


def _headnorm_rope(y, t_same, t_swap, seg):
    width = y.shape[1]
    s = y * y
    s_hi = s.astype(BF16)
    s_lo = (s - s_hi.astype(F32)).astype(BF16)
    ms = _dot(s_hi, seg) + _dot(s_lo, seg)
    half = HEAD_DIM // 2
    ahead = pltpu.roll(y, width - half, axis=1)
    behind = pltpu.roll(y, half, axis=1)
    swapped = jnp.where((_lane_iota(y.shape) & (HEAD_DIM - 1)) < half, ahead, behind)
    reps = width // t_same.shape[1]
    t_same = jnp.concatenate([t_same] * reps, axis=1)
    t_swap = jnp.concatenate([t_swap] * reps, axis=1)
    return (y * t_same + swapped * t_swap) * lax.rsqrt(ms + EPS)


def _dup_halves(x):
    low = _lane_iota(x.shape) < HEAD_DIM
    r = pltpu.roll(x, HEAD_DIM, axis=1)
    return jnp.where(low, x, r), jnp.where(low, r, x)


def _qkv_kernel(xn_ref, w_ref, seg_ref, qs_ref, qw_ref, ks_ref, kw_ref,
                q_ref, k_ref, v_ref, k2_ref, v2_ref, *, n_q_blocks, d_kv, rows):
    j = pl.program_id(1)
    tm, tn = q_ref.shape
    chunks = [slice(r * rows, (r + 1) * rows) for r in range(tm // rows)]
    seg = seg_ref[...]
    slab = seg.shape[0]

    @pl.when(j < n_q_blocks)
    def _():
        for rs in chunks:
            y = _dot(xn_ref[rs, :], w_ref[...])
            for c in range(tn // slab):
                sl = slice(c * slab, (c + 1) * slab)
                q_ref[rs, sl] = _headnorm_rope(y[:, sl], qs_ref[rs, :], qw_ref[rs, :],
                                               seg).astype(BF16)

    @pl.when(j == n_q_blocks)
    def _():
        for rs in chunks:
            y = _dot(xn_ref[rs, :], w_ref[...])
            k_all = jnp.concatenate(
                [_headnorm_rope(y[:, c * slab:(c + 1) * slab], ks_ref[rs, :], kw_ref[rs, :], seg)
                 for c in range(d_kv // slab)], axis=1)
            for c in range(d_kv // LANES):
                sl = slice(c * LANES, (c + 1) * LANES)
                k = k_all[:, sl]
                v = y[:, d_kv + c * LANES:d_kv + (c + 1) * LANES]
                k_ref[rs, sl] = k
                v_ref[rs, sl] = v
                ka, kb = _dup_halves(k)
                va, vb = _dup_halves(v)
                lo = slice(2 * c * LANES, (2 * c + 1) * LANES)
                hi = slice((2 * c + 1) * LANES, (2 * c + 2) * LANES)
                k2_ref[rs, lo] = ka.astype(BF16)
                k2_ref[rs, hi] = kb.astype(BF16)
                v2_ref[rs, lo] = va.astype(BF16)
                v2_ref[rs, hi] = vb.astype(BF16)


def _qkv_proj(xn, w_in, tables, *, tm, d_attn, d_kv, rows):
    m, d = xn.shape
    tn = 2 * d_kv
    n_q_blocks = d_attn // tn
    rope_blocks = tables[0].shape[0] // tm
    tab = pl.BlockSpec((tm, LANES), lambda i, j: (i % rope_blocks, 0))
    row_out = lambda width: pl.BlockSpec((tm, width), lambda i, j: (i, 0))
    head_of = jnp.arange(V7X_MXU_DIM) // HEAD_DIM
    seg = jnp.where(head_of[:, None] == head_of[None, :], 1.0 / HEAD_DIM, 0.0).astype(BF16)
    est = (2 * (_nbytes((tm, d), BF16) + _nbytes((d, tn), BF16) + 3 * _nbytes((tm, tn), BF16)
                + 2 * _nbytes((tm, d_kv), F32) + 4 * _nbytes((tm, LANES), F32))
           + 6 * _nbytes((rows, tn), F32))
    return pl.pallas_call(
        functools.partial(_qkv_kernel, n_q_blocks=n_q_blocks, d_kv=d_kv, rows=rows),
        out_shape=(jax.ShapeDtypeStruct((m, d_attn), BF16),
                   jax.ShapeDtypeStruct((m, d_kv), F32), jax.ShapeDtypeStruct((m, d_kv), F32),
                   jax.ShapeDtypeStruct((m, tn), BF16), jax.ShapeDtypeStruct((m, tn), BF16)),
        grid=(m // tm, n_q_blocks + 1),
        in_specs=[pl.BlockSpec((tm, d), lambda i, j: (i, 0)),
                  pl.BlockSpec((d, tn), lambda i, j: (0, j)),
                  pl.BlockSpec(seg.shape, lambda i, j: (0, 0)),
                  tab, tab, tab, tab],
        out_specs=(pl.BlockSpec((tm, tn), lambda i, j: (i, jnp.minimum(j, n_q_blocks - 1))),
                   row_out(d_kv), row_out(d_kv), row_out(tn), row_out(tn)),
        compiler_params=_params(("parallel", "arbitrary"), est),
        name="qkv_proj",
    )(xn, w_in, seg, *tables)


def _conv_kernel(*refs, seq_len, tail_rows, has_state):
    if has_state:
        xn_ref, wb_ref, wc_ref, wh_ref, cw_ref, p1_ref, p2_ref, z_ref, u_ref = refs
    else:
        xn_ref, wb_ref, wc_ref, wh_ref, cw_ref, z_ref, u_ref = refs
    xn = xn_ref[...]
    u = _dot(xn, wc_ref[...]) * _dot(xn, wh_ref[...])
    tm = u.shape[0]
    pos = lax.broadcasted_iota(jnp.int32, u.shape, 0) & (seq_len - 1)
    prev1 = p1_ref[...] if has_state else 0.0
    prev2 = p2_ref[...] if has_state else 0.0
    u1 = jnp.where(pos == 0, prev1, pltpu.roll(u, 1, axis=0))
    u2 = jnp.where(pos < 2, prev2, pltpu.roll(u, 2, axis=0))
    w = cw_ref[...]
    y = u2 * w[0:1] + u1 * w[1:2] + u * w[2:3]
    z_ref[...] = (_dot(xn, wb_ref[...]) * y).astype(BF16)
    u_ref[...] = u[tm - tail_rows:, :]


def _conv_branch(xn, w_in, conv_w, prev, *, tm, tn, seq_len, tail_rows, col_b, col_c, col_h):
    m, d = xn.shape
    n = conv_w.shape[1]
    has_state = prev is not None
    assert seq_len & (seq_len - 1) == 0 and tm % seq_len == 0
    wspec = lambda off: pl.BlockSpec((d, tn), lambda i, j: (0, off // tn + j))
    in_specs = [pl.BlockSpec((tm, d), lambda i, j: (i, 0)),
                wspec(col_b), wspec(col_c), wspec(col_h),
                pl.BlockSpec((conv_w.shape[0], tn), lambda i, j: (0, j))]
    args = [xn, w_in, w_in, w_in, conv_w]
    if has_state:
        in_specs += [pl.BlockSpec((tm, tn), lambda i, j: (i, j))] * 2
        args += list(prev)
    est = (2 * (_nbytes((tm, d), BF16) + 3 * _nbytes((d, tn), BF16) + _nbytes((tm, tn), BF16)
                + _nbytes((tail_rows, tn), F32)) + 8 * _nbytes((tm, tn), F32))
    return pl.pallas_call(
        functools.partial(_conv_kernel, seq_len=seq_len, tail_rows=tail_rows, has_state=has_state),
        out_shape=(jax.ShapeDtypeStruct((m, n), BF16),
                   jax.ShapeDtypeStruct((m // tm * tail_rows, n), F32)),
        grid=(m // tm, n // tn),
        in_specs=in_specs,
        out_specs=(pl.BlockSpec((tm, tn), lambda i, j: (i, j)),
                   pl.BlockSpec((tail_rows, tn), lambda i, j: (i, j))),
        compiler_params=_params(("parallel", "arbitrary"), est),
        name="conv_branch",
    )(*args)


def _stack_heads(q_ref, rows, col0, group):
    parts = []
    for g in range(group):
        blk = q_ref[rows, col0 + (g // 2) * LANES:col0 + (g // 2 + 1) * LANES]
        keep = (_lane_iota(blk.shape) >= HEAD_DIM) == bool(g % 2)
        parts.append(jnp.where(keep, blk, jnp.zeros_like(blk)))
    return jnp.concatenate(parts, axis=0)


def _sink_column(sink_ref, head0, group, rows):
    return jnp.concatenate(
        [jnp.full((rows, 1), sink_ref[head0 + g], F32) for g in range(group)], axis=0)


def _store_heads(o_ref, rows, col0, o, group, nrows):
    for lp in range(group // 2):
        a = o[(2 * lp) * nrows:(2 * lp + 1) * nrows]
        b = o[(2 * lp + 1) * nrows:(2 * lp + 2) * nrows]
        low = _lane_iota(a.shape) < HEAD_DIM
        o_ref[rows, col0 + lp * LANES:col0 + (lp + 1) * LANES] = jnp.where(low, a, b).astype(BF16)


def _attn_pair(pair, sink_ref, q_ref, k_ref, v_ref, o_ref, n_kv, group, has_prev):
    def block_rows(index):
        start = index * PAIR
        return pl.ds(start if isinstance(start, int) else pl.multiple_of(start, PAIR), PAIR)

    rows = block_rows(pair)
    prev = block_rows(pair - 1) if has_prev else None
    first_q = _lane_iota((CHUNK, PAIR)) < CHUNK

    def keys_of(ref, kv):
        cols = slice(kv * LANES, (kv + 1) * LANES)
        if has_prev:
            return jnp.concatenate([ref[prev, cols], ref[rows, cols]], axis=0)
        return ref[rows, cols]

    def scores(kv):
        return _dot_nt(keys_of(k_ref, kv), _stack_heads(q_ref, rows, kv * group * HEAD_DIM, group))

    st_next = scores(0)
    for kv in range(n_kv):
        st = st_next
        if kv + 1 < n_kv:
            st_next = scores(kv + 1)
        vt = keys_of(v_ref, kv).astype(F32).T.astype(BF16)
        col0 = kv * group * HEAD_DIM
        probs, rden = [], []
        for g in range(group):
            sg = st[:, g * PAIR:(g + 1) * PAIR]
            blocks = [sg[c * CHUNK:(c + 1) * CHUNK] for c in range(sg.shape[0] // CHUNK)]
            if has_prev:
                s = jnp.concatenate([jnp.where(first_q, blocks[0], blocks[3]),
                                     blocks[1], blocks[2]], axis=0)
            else:
                s = jnp.concatenate([blocks[0], jnp.where(first_q, NEG_INF, blocks[1])], axis=0)
            sink = sink_ref[kv * group + g]
            m = jnp.maximum(jnp.max(s, axis=0, keepdims=True), sink)
            e = jnp.exp2(s - m)
            den = jnp.sum(e, axis=0, keepdims=True) + jnp.exp2(sink - m)
            e = e.astype(BF16)
            if has_prev:
                e_03, e_1, e_2 = (e[c * CHUNK:(c + 1) * CHUNK] for c in range(3))
                zero = jnp.zeros_like(e_03)
                e = jnp.concatenate([jnp.where(first_q, e_03, zero), e_1, e_2,
                                     jnp.where(first_q, zero, e_03)], axis=0)
            probs.append(e)
            rden.append(1.0 / den)
        ot = _dot(vt, jnp.concatenate(probs, axis=1)) * jnp.concatenate(rden, axis=1)
        for lp in range(group // 2):
            a = ot[:HEAD_DIM, (2 * lp) * PAIR:(2 * lp + 1) * PAIR]
            b = ot[HEAD_DIM:, (2 * lp + 1) * PAIR:(2 * lp + 2) * PAIR]
            o_ref[rows, col0 + lp * LANES:col0 + (lp + 1) * LANES] = (
                jnp.concatenate([a, b], axis=0).T.astype(BF16))


def _attn_kernel(sink_ref, q_ref, k_ref, v_ref, o_ref, *, n_kv, group):
    def body(pair, carry):
        _attn_pair(pair, sink_ref, q_ref, k_ref, v_ref, o_ref, n_kv, group, True)
        return carry

    _attn_pair(0, sink_ref, q_ref, k_ref, v_ref, o_ref, n_kv, group, False)
    lax.fori_loop(1, q_ref.shape[0] // PAIR, body, 0)


def _attention(q, k2, v2, sinks, *, batch, seq, n_kv, group):
    m, d = q.shape
    assert WINDOW == PAIR and seq % PAIR == 0
    est = (2 * (2 * _nbytes((seq, d), BF16) + 2 * _nbytes((seq, n_kv * LANES), BF16))
           + 8 * _nbytes((group * PAIR, KEY_SPAN), F32))
    return pl.pallas_call(
        functools.partial(_attn_kernel, n_kv=n_kv, group=group),
        out_shape=jax.ShapeDtypeStruct((m, d), BF16),
        grid=(batch,),
        in_specs=[pl.BlockSpec(memory_space=pltpu.SMEM),
                  pl.BlockSpec((seq, d), lambda b: (b, 0)),
                  pl.BlockSpec((seq, n_kv * LANES), lambda b: (b, 0)),
                  pl.BlockSpec((seq, n_kv * LANES), lambda b: (b, 0))],
        out_specs=pl.BlockSpec((seq, d), lambda b: (b, 0)),
        compiler_params=_params(("parallel",), est),
        name="attention",
    )(sinks, q, k2, v2)


def _attn_sample_kernel(sink_ref, q_ref, kn_ref, vn_ref, kc_ref, vc_ref, o_ref, *, n_kv, group):
    t = q_ref.shape[0]
    for kv in range(n_kv):
        pair_sl = slice((kv // 2) * LANES, (kv // 2 + 1) * LANES)
        kc = _dup_halves(kc_ref[:, pair_sl])[kv % 2].astype(BF16)
        vc = _dup_halves(vc_ref[:, pair_sl])[kv % 2].astype(BF16)
        kn = kn_ref[:, kv * LANES:(kv + 1) * LANES]
        vn = vn_ref[:, kv * LANES:(kv + 1) * LANES]
        col0 = kv * group * HEAD_DIM
        lhs = _stack_heads(q_ref, slice(0, t), col0, group)
        s_c = _dot_nt(lhs, kc)
        s_n = _dot_nt(lhs, kn)
        sink = _sink_column(sink_ref, kv * group, group, t)
        m = jnp.maximum(jnp.maximum(jnp.max(s_c, axis=1, keepdims=True),
                                    jnp.max(s_n, axis=1, keepdims=True)), sink)
        e_c = jnp.exp2(s_c - m)
        e_n = jnp.exp2(s_n - m)
        den = (jnp.sum(e_c, axis=1, keepdims=True) + jnp.sum(e_n, axis=1, keepdims=True)
               + jnp.exp2(sink - m))
        o = (_dot(e_c.astype(BF16), vc) + _dot(e_n.astype(BF16), vn)) / den
        _store_heads(o_ref, slice(0, t), col0, o, group, t)


def _attention_sample(q, k2, v2, cache_k, cache_v, sinks, *, batch, t_new, n_kv, group):
    m, d = q.shape
    win = cache_k.shape[0] // batch
    d_kv = cache_k.shape[1]
    est = 4 * 1024 * 1024
    return pl.pallas_call(
        functools.partial(_attn_sample_kernel, n_kv=n_kv, group=group),
        out_shape=jax.ShapeDtypeStruct((m, d), BF16),
        grid=(batch,),
        in_specs=[pl.BlockSpec(memory_space=pltpu.SMEM),
                  pl.BlockSpec((t_new, d), lambda b: (b, 0)),
                  pl.BlockSpec((t_new, n_kv * LANES), lambda b: (b, 0)),
                  pl.BlockSpec((t_new, n_kv * LANES), lambda b: (b, 0)),
                  pl.BlockSpec((win, d_kv), lambda b: (b, 0)),
                  pl.BlockSpec((win, d_kv), lambda b: (b, 0))],
        out_specs=pl.BlockSpec((t_new, d), lambda b: (b, 0)),
        compiler_params=_params(("parallel",), est),
        name="attention_sample",
    )(sinks, q, k2, v2, cache_k, cache_v)


def _merge_kernel(xn_ref, a_ref, z_ref, wga_ref, wgc_ref, wap_ref, wcp_ref, o_ref):
    xn = xn_ref[...]
    attn = jax.nn.sigmoid(_dot(xn, wga_ref[...])) * _dot(a_ref[...], wap_ref[...])
    conv = jax.nn.sigmoid(_dot(xn, wgc_ref[...])) * _dot(z_ref[...], wcp_ref[...])
    o_ref[...] = (attn + conv).astype(BF16)


def _merge(xn, attn, z, w_in, w_ap, w_cp, *, tm, tn, col_ga, col_gc):
    m, d = xn.shape
    act = pl.BlockSpec((tm, d), lambda i, j: (i, 0))
    wspec = lambda off: pl.BlockSpec((d, tn), lambda i, j: (0, off // tn + j))
    est = (2 * (3 * _nbytes((tm, d), BF16) + 4 * _nbytes((d, tn), BF16) + _nbytes((tm, tn), BF16))
           + 6 * _nbytes((tm, tn), F32))
    return pl.pallas_call(
        _merge_kernel,
        out_shape=jax.ShapeDtypeStruct((m, d), BF16),
        grid=(m // tm, d // tn),
        in_specs=[act, act, act, wspec(col_ga), wspec(col_gc), wspec(0), wspec(0)],
        out_specs=pl.BlockSpec((tm, tn), lambda i, j: (i, j)),
        compiler_params=_params(("parallel", "arbitrary"), est),
        name="merge",
    )(xn, attn, z, w_in, w_in, w_ap, w_cp)


def _out_kernel(x_ref, m_ref, w_ref, o_ref):
    o_ref[...] = x_ref[...] + _dot(m_ref[...], w_ref[...])


def _out_proj(x, merged, w_out, *, tm, tn):
    m, d = x.shape
    est = (2 * (2 * _nbytes((tm, tn), F32) + _nbytes((tm, d), BF16) + _nbytes((d, tn), BF16))
           + 2 * _nbytes((tm, tn), F32))
    return pl.pallas_call(
        _out_kernel,
        out_shape=jax.ShapeDtypeStruct((m, d), F32),
        grid=(m // tm, d // tn),
        in_specs=[pl.BlockSpec((tm, tn), lambda i, j: (i, j)),
                  pl.BlockSpec((tm, d), lambda i, j: (i, 0)),
                  pl.BlockSpec((d, tn), lambda i, j: (0, j))],
        out_specs=pl.BlockSpec((tm, tn), lambda i, j: (i, j)),
        compiler_params=_params(("parallel", "arbitrary"), est),
        name="out_proj",
    )(x, merged, w_out)


def _ffn_kernel(x_ref, g_ref, wg_ref, wu_ref, wd_ref, o_ref, xn_ref):
    @pl.when(pl.program_id(1) == 0)
    def _():
        x = x_ref[...]
        xn_ref[...] = _rmsnorm_rows(x, g_ref[...]).astype(BF16)
        o_ref[...] = x

    xn = xn_ref[...]
    h = jax.nn.silu(_dot(xn, wg_ref[...])) * _dot(xn, wu_ref[...])
    o_ref[...] += _dot(h.astype(BF16), wd_ref[...])


def _ffn(x, g, w_gate, w_up, w_down, *, tm, tf):
    m, d = x.shape
    d_ff = w_gate.shape[1]
    est = (3 * _nbytes((tm, d), F32) + 6 * _nbytes((d, tf), BF16) + _nbytes((tm, d), BF16)
           + 4 * _nbytes((tm, tf), F32) + _nbytes((tm, d), F32))
    return pl.pallas_call(
        _ffn_kernel,
        out_shape=jax.ShapeDtypeStruct((m, d), F32),
        grid=(m // tm, d_ff // tf),
        in_specs=[pl.BlockSpec((tm, d), lambda i, f: (i, 0), pipeline_mode=pl.Buffered(1)),
                  pl.BlockSpec((1, d), lambda i, f: (0, 0)),
                  pl.BlockSpec((d, tf), lambda i, f: (0, f)),
                  pl.BlockSpec((d, tf), lambda i, f: (0, f)),
                  pl.BlockSpec((tf, d), lambda i, f: (f, 0))],
        out_specs=pl.BlockSpec((tm, d), lambda i, f: (i, 0)),
        scratch_shapes=[pltpu.VMEM((tm, d), BF16)],
        compiler_params=_params(("parallel", "arbitrary"), est),
        name="ffn",
    )(x, g, w_gate, w_up, w_down)


def _rope_tables(pos, repeat, g, scale):
    half = HEAD_DIM // 2
    inv = jnp.exp(-math.log(ROPE_THETA) * jnp.arange(half, dtype=F32) * (2.0 / HEAD_DIM))
    ang = pos.astype(F32)[:, None] * inv[None, :]
    cos = jnp.cos(ang)
    sin = jnp.sin(ang)
    g_swap = jnp.concatenate([g[half:], g[:half]])
    t_same = g[None, :] * jnp.concatenate([cos, cos], axis=-1) * scale
    t_swap = g_swap[None, :] * jnp.concatenate([-sin, sin], axis=-1) * scale
    reps = (repeat, LANES // HEAD_DIM)
    return jnp.tile(t_same, reps), jnp.tile(t_swap, reps)


def _layer(x, w, *, batch, seq, pos, tiles, conv_prev, sample_cache):
    d = x.shape[1]
    n_heads = w["sinks"].shape[0]
    d_attn = n_heads * HEAD_DIM
    d_kv = w["d_kv"]
    n_kv = d_kv // HEAD_DIM
    group = n_heads // n_kv
    d_conv = w["conv_w"].shape[1]
    col_b = d_attn + 2 * d_kv
    col_c = col_b + d_conv
    col_h = col_c + d_conv
    col_ga = col_h + d_conv
    col_gc = col_ga + d
    tm = tiles["tm"]

    rope_rows = max(seq, tm)
    log2e = math.log2(math.e)
    tables = (_rope_tables(pos, rope_rows // seq, w["q_norm_g"], HEAD_DIM ** -0.5 * log2e)
              + _rope_tables(pos, rope_rows // seq, w["k_norm_g"], 1.0))
    sinks = w["sinks"] * log2e

    xn = _rmsnorm(x, w["norm_mix_g"], tiles["tm_norm"])
    q, k, v, k2, v2 = _qkv_proj(xn, w["w_in"], tables, tm=tm, d_attn=d_attn, d_kv=d_kv,
                                rows=tiles["qkv_rows"])
    z, u_tail = _conv_branch(xn, w["w_in"], w["conv_w"], conv_prev,
                             tm=tiles["tm_conv"], tn=tiles["tn_conv"], seq_len=seq,
                             tail_rows=tiles["conv_tail"], col_b=col_b, col_c=col_c, col_h=col_h)
    if sample_cache is None:
        attn = _attention(q, k2, v2, sinks, batch=batch, seq=seq, n_kv=n_kv, group=group)
    else:
        attn = _attention_sample(q, k2, v2, sample_cache[0], sample_cache[1], sinks,
                                 batch=batch, t_new=seq, n_kv=n_kv, group=group)
    merged = _merge(xn, attn, z, w["w_in"], w["w_attn_proj"], w["w_conv_proj"],
                    tm=tm, tn=tiles["tn_merge"], col_ga=col_ga, col_gc=col_gc)
    x1 = _out_proj(x, merged, w["w_out"], tm=tm, tn=tiles["tn_out"])
    y = _ffn(x1, w["norm_ffn_g"], w["w_ffn_gate"], w["w_ffn_up"], w["w_ffn_down"],
             tm=tiles["tm_ffn"], tf=tiles["tf"])
    return y, k, v, u_tail


def kernel(x_prompt, x_sample, cache_k, cache_v, state_conv, norm_mix_g, w_in, q_norm_g, k_norm_g,
           attn_sinks, conv_w, w_attn_proj, w_conv_proj, w_out, norm_ffn_g, w_ffn_gate, w_ffn_up,
           w_ffn_down):
    depth = w_in.shape[0]
    assert depth == 1
    b_p, s_p, d = x_prompt.shape
    b_s, s_s, _ = x_sample.shape
    n_kv = cache_k.shape[3]
    d_kv = n_kv * HEAD_DIM
    win = cache_k.shape[2]
    n_state = state_conv.shape[2]

    w = {
        "d_kv": d_kv,
        "norm_mix_g": norm_mix_g[0].reshape(1, d),
        "w_in": w_in[0].astype(BF16),
        "q_norm_g": q_norm_g[0],
        "k_norm_g": k_norm_g[0],
        "sinks": attn_sinks[0],
        "conv_w": conv_w[0],
        "w_attn_proj": w_attn_proj[0].astype(BF16),
        "w_conv_proj": w_conv_proj[0].astype(BF16),
        "w_out": w_out[0].astype(BF16),
        "norm_ffn_g": norm_ffn_g[0].reshape(1, d),
        "w_ffn_gate": w_ffn_gate[0].astype(BF16),
        "w_ffn_up": w_ffn_up[0].astype(BF16),
        "w_ffn_down": w_ffn_down[0].astype(BF16),
    }

    m_p = b_p * s_p
    tm_p = min(1024, s_p)
    tiles_p = dict(tm=tm_p, tm_norm=tm_p, qkv_rows=256, tm_conv=s_p, tn_conv=256, conv_tail=8,
                   tn_merge=512, tn_out=1024, tm_ffn=tm_p, tf=512)
    y_p, k_p, v_p, u_p = _layer(x_prompt.reshape(m_p, d), w, batch=b_p, seq=s_p,
                                pos=jnp.arange(s_p), tiles=tiles_p, conv_prev=None,
                                sample_cache=None)
    keep = min(WINDOW, s_p)
    new_k_p = k_p.reshape(b_p, s_p, n_kv, HEAD_DIM)[:, s_p - keep:][None]
    new_v_p = v_p.reshape(b_p, s_p, n_kv, HEAD_DIM)[:, s_p - keep:][None]
    new_c_p = u_p.reshape(b_p, tiles_p["conv_tail"], -1)[:, tiles_p["conv_tail"] - n_state:][None]

    m_s = b_s * s_s
    st = state_conv[0]
    prev1 = jnp.pad(st[:, 1:2], ((0, 0), (0, s_s - 1), (0, 0))).reshape(m_s, -1)
    prev2 = jnp.pad(st, ((0, 0), (0, s_s - n_state), (0, 0))).reshape(m_s, -1)
    tiles_s = dict(tm=m_s, tm_norm=m_s, qkv_rows=m_s, tm_conv=m_s, tn_conv=256, conv_tail=m_s,
                   tn_merge=256, tn_out=512, tm_ffn=m_s, tf=512)
    y_s, k_s, v_s, u_s = _layer(x_sample.reshape(m_s, d), w, batch=b_s, seq=s_s,
                                pos=PAST_LEN + jnp.arange(s_s), tiles=tiles_s,
                                conv_prev=(prev1, prev2),
                                sample_cache=(cache_k[0].reshape(b_s * win, d_kv),
                                              cache_v[0].reshape(b_s * win, d_kv)))
    new_k_s = k_s.reshape(b_s, s_s, n_kv, HEAD_DIM)[None]
    new_v_s = v_s.reshape(b_s, s_s, n_kv, HEAD_DIM)[None]
    new_c_s = u_s.reshape(b_s, s_s, -1)[:, s_s - n_state:][None]

    return (y_p.reshape(b_p, s_p, d), y_s.reshape(b_s, s_s, d), new_k_p, new_v_p, new_c_p,
            new_k_s, new_v_s, new_c_s)
```

```python
import functools
import math

import jax
import jax.numpy as jnp
from jax import lax
from jax.experimental import pallas as pl
from jax.experimental.pallas import tpu as pltpu

F32 = jnp.float32
BF16 = jnp.bfloat16

CHUNK = 64
WINDOW = 128
PAST_LEN = 1024
ROPE_THETA = 10000.0
EPS = 1e-6
NEG_INF = -1e30
HEAD_DIM = 64
LANES = 128
PAIR = 2 * CHUNK
KEY_SPAN = PAIR + WINDOW

V7X_MXU_DIM = 256
V7X_VMEM_BYTES = 64 * 1024 * 1024
VMEM_CAP_BYTES = V7X_VMEM_BYTES - 6 * 1024 * 1024


def _params(semantics, vmem_estimate):
    limit = int(min(max(vmem_estimate, 16 * 1024 * 1024), VMEM_CAP_BYTES))
    return pltpu.CompilerParams(dimension_semantics=semantics, vmem_limit_bytes=limit)


def _nbytes(shape, dtype):
    return math.prod(shape) * jnp.dtype(dtype).itemsize


def _dot(a, b):
    return jnp.dot(a, b, preferred_element_type=F32)


def _dot_nt(a, b):
    return lax.dot_general(a, b, (((1,), (1,)), ((), ())), preferred_element_type=F32)


def _lane_iota(shape):
    return lax.broadcasted_iota(jnp.int32, shape, 1)


def _rmsnorm_rows(x, g):
    y = x * lax.rsqrt(jnp.mean(x * x, axis=-1, keepdims=True) + EPS)
    return y * g


def _headnorm_rope(y, t_same, t_swap, seg):
    width = y.shape[1]
    s = y * y
    s_hi = s.astype(BF16)
    s_lo = (s - s_hi.astype(F32)).astype(BF16)
    ms = _dot(s_hi, seg) + _dot(s_lo, seg)
    half = HEAD_DIM // 2
    ahead = pltpu.roll(y, width - half, axis=1)
    behind = pltpu.roll(y, half, axis=1)
    swapped = jnp.where((_lane_iota(y.shape) & (HEAD_DIM - 1)) < half, ahead, behind)
    reps = width // t_same.shape[1]
    t_same = jnp.concatenate([t_same] * reps, axis=1)
    t_swap = jnp.concatenate([t_swap] * reps, axis=1)
    return (y * t_same + swapped * t_swap) * lax.rsqrt(ms + EPS)


def _dup_halves(x):
    low = _lane_iota(x.shape) < HEAD_DIM
    r = pltpu.roll(x, HEAD_DIM, axis=1)
    return jnp.where(low, x, r), jnp.where(low, r, x)


def _qkv_kernel(x_ref, g_ref, w_ref, seg_ref, qs_ref, qw_ref, ks_ref, kw_ref,
                xn_ref, q_ref, k_ref, v_ref, k2_ref, v2_ref, *, n_q_blocks, d_kv, rows):
    j = pl.program_id(1)
    tm, tn = q_ref.shape
    chunks = [slice(r * rows, (r + 1) * rows) for r in range(tm // rows)]
    seg = seg_ref[...]
    slab = seg.shape[0]

    def q_block(normalize):
        for rs in chunks:
            if normalize:
                xn_ref[rs, :] = _rmsnorm_rows(x_ref[rs, :], g_ref[...]).astype(BF16)
            y = _dot(xn_ref[rs, :], w_ref[...])
            for c in range(tn // slab):
                sl = slice(c * slab, (c + 1) * slab)
                q_ref[rs, sl] = _headnorm_rope(y[:, sl], qs_ref[rs, :], qw_ref[rs, :],
                                               seg).astype(BF16)

    pl.when(j == 0)(functools.partial(q_block, True))
    pl.when((j > 0) & (j < n_q_blocks))(functools.partial(q_block, False))

    @pl.when(j == n_q_blocks)
    def _():
        for rs in chunks:
            y = _dot(xn_ref[rs, :], w_ref[...])
            k_all = jnp.concatenate(
                [_headnorm_rope(y[:, c * slab:(c + 1) * slab], ks_ref[rs, :], kw_ref[rs, :], seg)
                 for c in range(d_kv // slab)], axis=1)
            for c in range(d_kv // LANES):
                sl = slice(c * LANES, (c + 1) * LANES)
                k = k_all[:, sl]
                v = y[:, d_kv + c * LANES:d_kv + (c + 1) * LANES]
                k_ref[rs, sl] = k
                v_ref[rs, sl] = v
                ka, kb = _dup_halves(k)
                va, vb = _dup_halves(v)
                lo = slice(2 * c * LANES, (2 * c + 1) * LANES)
                hi = slice((2 * c + 1) * LANES, (2 * c + 2) * LANES)
                k2_ref[rs, lo] = ka.astype(BF16)
                k2_ref[rs, hi] = kb.astype(BF16)
                v2_ref[rs, lo] = va.astype(BF16)
                v2_ref[rs, hi] = vb.astype(BF16)


def _qkv_proj(x, g, w_in, tables, *, tm, d_attn, d_kv, rows):
    m, d = x.shape
    tn = 2 * d_kv
    n_q_blocks = d_attn // tn
    rope_blocks = tables[0].shape[0] // tm
    tab = pl.BlockSpec((tm, LANES), lambda i, j: (i % rope_blocks, 0))
    row_out = lambda width: pl.BlockSpec((tm, width), lambda i, j: (i, 0))
    head_of = jnp.arange(V7X_MXU_DIM) // HEAD_DIM
    seg = jnp.where(head_of[:, None] == head_of[None, :], 1.0 / HEAD_DIM, 0.0).astype(BF16)
    est = (2 * (_nbytes((tm, d), F32) + _nbytes((tm, d), BF16) + _nbytes((d, tn), BF16)
                + 3 * _nbytes((tm, tn), BF16) + 2 * _nbytes((tm, d_kv), F32)
                + 4 * _nbytes((tm, LANES), F32))
           + 6 * _nbytes((rows, tn), F32) + 2 * _nbytes((rows, d), F32))
    return pl.pallas_call(
        functools.partial(_qkv_kernel, n_q_blocks=n_q_blocks, d_kv=d_kv, rows=rows),
        out_shape=(jax.ShapeDtypeStruct((m, d), BF16), jax.ShapeDtypeStruct((m, d_attn), BF16),
                   jax.ShapeDtypeStruct((m, d_kv), F32), jax.ShapeDtypeStruct((m, d_kv), F32),
                   jax.ShapeDtypeStruct((m, tn), BF16), jax.ShapeDtypeStruct((m, tn), BF16)),
        grid=(m // tm, n_q_blocks + 1),
        in_specs=[pl.BlockSpec((tm, d), lambda i, j: (i, 0)),
                  pl.BlockSpec((1, d), lambda i, j: (0, 0)),
                  pl.BlockSpec((d, tn), lambda i, j: (0, j)),
                  pl.BlockSpec(seg.shape, lambda i, j: (0, 0)),
                  tab, tab, tab, tab],
        out_specs=(row_out(d),
                   pl.BlockSpec((tm, tn), lambda i, j: (i, jnp.minimum(j, n_q_blocks - 1))),
                   row_out(d_kv), row_out(d_kv), row_out(tn), row_out(tn)),
        compiler_params=_params(("parallel", "arbitrary"), est),
        name="qkv_proj",
    )(x, g, w_in, seg, *tables)


def _conv_kernel(*refs, seq_len, tail_rows, has_state, rows):
    if has_state:
        xn_ref, wb_ref, wc_ref, wh_ref, cw_ref, p1_ref, p2_ref, z_ref, u_ref = refs
    else:
        xn_ref, wb_ref, wc_ref, wh_ref, cw_ref, z_ref, u_ref = refs
    tm, tn = z_ref.shape
    w = cw_ref[...]
    row = lax.broadcasted_iota(jnp.int32, (rows, tn), 0)
    last1 = last2 = None
    for r in range(tm // rows):
        rs = slice(r * rows, (r + 1) * rows)
        xn = xn_ref[rs, :]
        u = _dot(xn, wc_ref[...]) * _dot(xn, wh_ref[...])
        u1 = pltpu.roll(u, 1, axis=0)
        u2 = pltpu.roll(u, 2, axis=0)
        if r > 0:
            u1 = jnp.where(row == 0, last1, u1)
            u2 = jnp.where(row == 0, last2, jnp.where(row == 1, last1, u2))
        if (r * rows) % seq_len == 0 or seq_len < rows:
            pos = (row + r * rows) & (seq_len - 1)
            u1 = jnp.where(pos == 0, p1_ref[rs, :] if has_state else 0.0, u1)
            u2 = jnp.where(pos < 2, p2_ref[rs, :] if has_state else 0.0, u2)
        y = u2 * w[0:1] + u1 * w[1:2] + u * w[2:3]
        z_ref[rs, :] = (_dot(xn, wb_ref[...]) * y).astype(BF16)
        last1, last2 = u[rows - 1:rows], u[rows - 2:rows - 1]
    u_ref[...] = u[rows - tail_rows:]


def _conv_branch(xn, w_in, conv_w, prev, *, tm, tn, rows, seq_len, tail_rows, col_b, col_c, col_h):
    m, d = xn.shape
    n = conv_w.shape[1]
    has_state = prev is not None
    assert seq_len & (seq_len - 1) == 0 and tm % seq_len == 0
    assert tm % rows == 0 and tail_rows <= rows and (rows % seq_len == 0 or seq_len % rows == 0)
    wspec = lambda off: pl.BlockSpec((d, tn), lambda i, j: (0, off // tn + j))
    in_specs = [pl.BlockSpec((tm, d), lambda i, j: (i, 0)),
                wspec(col_b), wspec(col_c), wspec(col_h),
                pl.BlockSpec((conv_w.shape[0], tn), lambda i, j: (0, j))]
    args = [xn, w_in, w_in, w_in, conv_w]
    if has_state:
        in_specs += [pl.BlockSpec((tm, tn), lambda i, j: (i, j))] * 2
        args += list(prev)
    est = (2 * (_nbytes((tm, d), BF16) + 3 * _nbytes((d, tn), BF16) + _nbytes((tm, tn), BF16)
                + _nbytes((tail_rows, tn), F32)) + 8 * _nbytes((tm, tn), F32))
    return pl.pallas_call(
        functools.partial(_conv_kernel, seq_len=seq_len, tail_rows=tail_rows, has_state=has_state,
                          rows=rows),
        out_shape=(jax.ShapeDtypeStruct((m, n), BF16),
                   jax.ShapeDtypeStruct((m // tm * tail_rows, n), F32)),
        grid=(m // tm, n // tn),
        in_specs=in_specs,
        out_specs=(pl.BlockSpec((tm, tn), lambda i, j: (i, j)),
                   pl.BlockSpec((tail_rows, tn), lambda i, j: (i, j))),
        compiler_params=_params(("parallel", "arbitrary"), est),
        name="conv_branch",
    )(*args)


def _stack_heads(q_ref, rows, col0, group):
    parts = []
    for g in range(group):
        blk = q_ref[rows, col0 + (g // 2) * LANES:col0 + (g // 2 + 1) * LANES]
        keep = (_lane_iota(blk.shape) >= HEAD_DIM) == bool(g % 2)
        parts.append(jnp.where(keep, blk, jnp.zeros_like(blk)))
    return jnp.concatenate(parts, axis=0)


def _sink_column(sink_ref, head0, group, rows):
    return jnp.concatenate(
        [jnp.full((rows, 1), sink_ref[head0 + g], F32) for g in range(group)], axis=0)


def _store_heads(o_ref, rows, col0, o, group, nrows):
    for lp in range(group // 2):
        a = o[(2 * lp) * nrows:(2 * lp + 1) * nrows]
        b = o[(2 * lp + 1) * nrows:(2 * lp + 2) * nrows]
        low = _lane_iota(a.shape) < HEAD_DIM
        o_ref[rows, col0 + lp * LANES:col0 + (lp + 1) * LANES] = jnp.where(low, a, b).astype(BF16)


def _attn_pair(pair, sink_ref, q_ref, k_ref, v_ref, o_ref, n_kv, group, has_prev):
    def block_rows(index):
        start = index * PAIR
        return pl.ds(start if isinstance(start, int) else pl.multiple_of(start, PAIR), PAIR)

    rows = block_rows(pair)
    prev = block_rows(pair - 1) if has_prev else None
    first_q = _lane_iota((CHUNK, PAIR)) < CHUNK

    def keys_of(ref, kv):
        cols = slice(kv * LANES, (kv + 1) * LANES)
        if has_prev:
            return jnp.concatenate([ref[prev, cols], ref[rows, cols]], axis=0)
        return ref[rows, cols]

    def scores(kv):
        return _dot_nt(keys_of(k_ref, kv), _stack_heads(q_ref, rows, kv * group * HEAD_DIM, group))

    st_next = scores(0)
    for kv in range(n_kv):
        st = st_next
        if kv + 1 < n_kv:
            st_next = scores(kv + 1)
        vt = keys_of(v_ref, kv).astype(F32).T.astype(BF16)
        col0 = kv * group * HEAD_DIM
        probs, rden = [], []
        for g in range(group):
            sg = st[:, g * PAIR:(g + 1) * PAIR]
            blocks = [sg[c * CHUNK:(c + 1) * CHUNK] for c in range(sg.shape[0] // CHUNK)]
            if has_prev:
                s = jnp.concatenate([jnp.where(first_q, blocks[0], blocks[3]),
                                     blocks[1], blocks[2]], axis=0)
            else:
                s = jnp.concatenate([blocks[0], jnp.where(first_q, NEG_INF, blocks[1])], axis=0)
            sink = sink_ref[kv * group + g]
            m = jnp.maximum(jnp.max(s, axis=0, keepdims=True), sink)
            e = jnp.exp2(s - m)
            den = jnp.sum(e, axis=0, keepdims=True) + jnp.exp2(sink - m)
            e = e.astype(BF16)
            if has_prev:
                e_03, e_1, e_2 = (e[c * CHUNK:(c + 1) * CHUNK] for c in range(3))
                zero = jnp.zeros_like(e_03)
                e = jnp.concatenate([jnp.where(first_q, e_03, zero), e_1, e_2,
                                     jnp.where(first_q, zero, e_03)], axis=0)
            probs.append(e)
            rden.append(1.0 / den)
        ot = _dot(vt, jnp.concatenate(probs, axis=1)) * jnp.concatenate(rden, axis=1)
        for lp in range(group // 2):
            a = ot[:HEAD_DIM, (2 * lp) * PAIR:(2 * lp + 1) * PAIR]
            b = ot[HEAD_DIM:, (2 * lp + 1) * PAIR:(2 * lp + 2) * PAIR]
            o_ref[rows, col0 + lp * LANES:col0 + (lp + 1) * LANES] = (
                jnp.concatenate([a, b], axis=0).T.astype(BF16))


def _attn_kernel(sink_ref, q_ref, k_ref, v_ref, o_ref, *, n_kv, group):
    def body(pair, carry):
        _attn_pair(pair, sink_ref, q_ref, k_ref, v_ref, o_ref, n_kv, group, True)
        return carry

    _attn_pair(0, sink_ref, q_ref, k_ref, v_ref, o_ref, n_kv, group, False)
    lax.fori_loop(1, q_ref.shape[0] // PAIR, body, 0)


def _attention(q, k2, v2, sinks, *, batch, seq, n_kv, group):
    m, d = q.shape
    assert WINDOW == PAIR and seq % PAIR == 0
    est = (2 * (2 * _nbytes((seq, d), BF16) + 2 * _nbytes((seq, n_kv * LANES), BF16))
           + 8 * _nbytes((group * PAIR, KEY_SPAN), F32))
    return pl.pallas_call(
        functools.partial(_attn_kernel, n_kv=n_kv, group=group),
        out_shape=jax.ShapeDtypeStruct((m, d), BF16),
        grid=(batch,),
        in_specs=[pl.BlockSpec(memory_space=pltpu.SMEM),
                  pl.BlockSpec((seq, d), lambda b: (b, 0)),
                  pl.BlockSpec((seq, n_kv * LANES), lambda b: (b, 0)),
                  pl.BlockSpec((seq, n_kv * LANES), lambda b: (b, 0))],
        out_specs=pl.BlockSpec((seq, d), lambda b: (b, 0)),
        compiler_params=_params(("parallel",), est),
        name="attention",
    )(sinks, q, k2, v2)


def _attn_sample_kernel(sink_ref, q_ref, kn_ref, vn_ref, kc_ref, vc_ref, o_ref, *, n_kv, group):
    t = q_ref.shape[0]
    for kv in range(n_kv):
        pair_sl = slice((kv // 2) * LANES, (kv // 2 + 1) * LANES)
        kc = _dup_halves(kc_ref[:, pair_sl])[kv % 2].astype(BF16)
        vc = _dup_halves(vc_ref[:, pair_sl])[kv % 2].astype(BF16)
        kn = kn_ref[:, kv * LANES:(kv + 1) * LANES]
        vn = vn_ref[:, kv * LANES:(kv + 1) * LANES]
        col0 = kv * group * HEAD_DIM
        lhs = _stack_heads(q_ref, slice(0, t), col0, group)
        s_c = _dot_nt(lhs, kc)
        s_n = _dot_nt(lhs, kn)
        sink = _sink_column(sink_ref, kv * group, group, t)
        m = jnp.maximum(jnp.maximum(jnp.max(s_c, axis=1, keepdims=True),
                                    jnp.max(s_n, axis=1, keepdims=True)), sink)
        e_c = jnp.exp2(s_c - m)
        e_n = jnp.exp2(s_n - m)
        den = (jnp.sum(e_c, axis=1, keepdims=True) + jnp.sum(e_n, axis=1, keepdims=True)
               + jnp.exp2(sink - m))
        o = (_dot(e_c.astype(BF16), vc) + _dot(e_n.astype(BF16), vn)) / den
        _store_heads(o_ref, slice(0, t), col0, o, group, t)


def _attention_sample(q, k2, v2, cache_k, cache_v, sinks, *, batch, t_new, n_kv, group):
    m, d = q.shape
    win = cache_k.shape[0] // batch
    d_kv = cache_k.shape[1]
    est = 4 * 1024 * 1024
    return pl.pallas_call(
        functools.partial(_attn_sample_kernel, n_kv=n_kv, group=group),
        out_shape=jax.ShapeDtypeStruct((m, d), BF16),
        grid=(batch,),
        in_specs=[pl.BlockSpec(memory_space=pltpu.SMEM),
                  pl.BlockSpec((t_new, d), lambda b: (b, 0)),
                  pl.BlockSpec((t_new, n_kv * LANES), lambda b: (b, 0)),
                  pl.BlockSpec((t_new, n_kv * LANES), lambda b: (b, 0)),
                  pl.BlockSpec((win, d_kv), lambda b: (b, 0)),
                  pl.BlockSpec((win, d_kv), lambda b: (b, 0))],
        out_specs=pl.BlockSpec((t_new, d), lambda b: (b, 0)),
        compiler_params=_params(("parallel",), est),
        name="attention_sample",
    )(sinks, q, k2, v2, cache_k, cache_v)


def _merge_kernel(xn_ref, a_ref, z_ref, wga_ref, wgc_ref, wap_ref, wcp_ref, o_ref):
    xn = xn_ref[...]
    attn = jax.nn.sigmoid(_dot(xn, wga_ref[...])) * _dot(a_ref[...], wap_ref[...])
    conv = jax.nn.sigmoid(_dot(xn, wgc_ref[...])) * _dot(z_ref[...], wcp_ref[...])
    o_ref[...] = (attn + conv).astype(BF16)


def _merge(xn, attn, z, w_in, w_ap, w_cp, *, tm, tn, col_ga, col_gc):
    m, d = xn.shape
    act = pl.BlockSpec((tm, d), lambda i, j: (i, 0))
    wspec = lambda off: pl.BlockSpec((d, tn), lambda i, j: (0, off // tn + j))
    est = (2 * (3 * _nbytes((tm, d), BF16) + 4 * _nbytes((d, tn), BF16) + _nbytes((tm, tn), BF16))
           + 6 * _nbytes((tm, tn), F32))
    return pl.pallas_call(
        _merge_kernel,
        out_shape=jax.ShapeDtypeStruct((m, d), BF16),
        grid=(m // tm, d // tn),
        in_specs=[act, act, act, wspec(col_ga), wspec(col_gc), wspec(0), wspec(0)],
        out_specs=pl.BlockSpec((tm, tn), lambda i, j: (i, j)),
        compiler_params=_params(("parallel", "arbitrary"), est),
        name="merge",
    )(xn, attn, z, w_in, w_in, w_ap, w_cp)


def _out_kernel(x_ref, m_ref, w_ref, g_ref, x1_ref, xn_ref, *, rows):
    for r in range(x_ref.shape[0] // rows):
        rs = slice(r * rows, (r + 1) * rows)
        x1 = x_ref[rs, :] + _dot(m_ref[rs, :], w_ref[...])
        x1_ref[rs, :] = x1
        xn_ref[rs, :] = _rmsnorm_rows(x1, g_ref[...]).astype(BF16)


def _out_proj(x, merged, w_out, g_ffn, *, tm, rows):
    m, d = x.shape
    est = (2 * (2 * _nbytes((tm, d), F32) + 2 * _nbytes((tm, d), BF16)) + _nbytes((d, d), BF16)
           + 4 * _nbytes((rows, d), F32))
    return pl.pallas_call(
        functools.partial(_out_kernel, rows=rows),
        out_shape=(jax.ShapeDtypeStruct((m, d), F32), jax.ShapeDtypeStruct((m, d), BF16)),
        grid=(m // tm,),
        in_specs=[pl.BlockSpec((tm, d), lambda i: (i, 0)),
                  pl.BlockSpec((tm, d), lambda i: (i, 0)),
                  pl.BlockSpec((d, d), lambda i: (0, 0), pipeline_mode=pl.Buffered(1)),
                  pl.BlockSpec((1, d), lambda i: (0, 0))],
        out_specs=(pl.BlockSpec((tm, d), lambda i: (i, 0)),
                   pl.BlockSpec((tm, d), lambda i: (i, 0))),
        compiler_params=_params(("parallel",), est),
        name="out_proj",
    )(x, merged, w_out, g_ffn)


def _ffn_kernel(x_ref, xn_ref, wg_ref, wu_ref, wd_ref, o_ref):
    @pl.when(pl.program_id(1) == 0)
    def _():
        o_ref[...] = x_ref[...]

    xn = xn_ref[...]
    h = jax.nn.silu(_dot(xn, wg_ref[...])) * _dot(xn, wu_ref[...])
    o_ref[...] += _dot(h.astype(BF16), wd_ref[...])


def _ffn(x, xn, w_gate, w_up, w_down, *, tm, tf):
    m, d = x.shape
    d_ff = w_gate.shape[1]
    est = (2 * (2 * _nbytes((tm, d), F32) + _nbytes((tm, d), BF16) + 3 * _nbytes((d, tf), BF16))
           + 4 * _nbytes((tm, tf), F32) + _nbytes((tm, d), F32))
    row = lambda i, f: (i, 0)
    return pl.pallas_call(
        _ffn_kernel,
        out_shape=jax.ShapeDtypeStruct((m, d), F32),
        grid=(m // tm, d_ff // tf),
        in_specs=[pl.BlockSpec((tm, d), row),
                  pl.BlockSpec((tm, d), row),
                  pl.BlockSpec((d, tf), lambda i, f: (0, f)),
                  pl.BlockSpec((d, tf), lambda i, f: (0, f)),
                  pl.BlockSpec((tf, d), lambda i, f: (f, 0))],
        out_specs=pl.BlockSpec((tm, d), row),
        compiler_params=_params(("parallel", "arbitrary"), est),
        name="ffn",
    )(x, xn, w_gate, w_up, w_down)


def _rope_tables(pos, repeat, g, scale):
    half = HEAD_DIM // 2
    inv = jnp.exp(-math.log(ROPE_THETA) * jnp.arange(half, dtype=F32) * (2.0 / HEAD_DIM))
    ang = pos.astype(F32)[:, None] * inv[None, :]
    cos = jnp.cos(ang)
    sin = jnp.sin(ang)
    g_swap = jnp.concatenate([g[half:], g[:half]])
    t_same = g[None, :] * jnp.concatenate([cos, cos], axis=-1) * scale
    t_swap = g_swap[None, :] * jnp.concatenate([-sin, sin], axis=-1) * scale
    reps = (repeat, LANES // HEAD_DIM)
    return jnp.tile(t_same, reps), jnp.tile(t_swap, reps)


def _layer(x, w, *, batch, seq, pos, tiles, conv_prev, sample_cache):
    d = x.shape[1]
    n_heads = w["sinks"].shape[0]
    d_attn = n_heads * HEAD_DIM
    d_kv = w["d_kv"]
    n_kv = d_kv // HEAD_DIM
    group = n_heads // n_kv
    d_conv = w["conv_w"].shape[1]
    col_b = d_attn + 2 * d_kv
    col_c = col_b + d_conv
    col_h = col_c + d_conv
    col_ga = col_h + d_conv
    col_gc = col_ga + d
    tm = tiles["tm"]

    rope_rows = max(seq, tm)
    log2e = math.log2(math.e)
    tables = (_rope_tables(pos, rope_rows // seq, w["q_norm_g"], HEAD_DIM ** -0.5 * log2e)
              + _rope_tables(pos, rope_rows // seq, w["k_norm_g"], 1.0))
    sinks = w["sinks"] * log2e

    xn, q, k, v, k2, v2 = _qkv_proj(x, w["norm_mix_g"], w["w_in"], tables, tm=tm, d_attn=d_attn,
                                    d_kv=d_kv, rows=tiles["qkv_rows"])
    z, u_tail = _conv_branch(xn, w["w_in"], w["conv_w"], conv_prev,
                             tm=tiles["tm_conv"], tn=tiles["tn_conv"], rows=tiles["conv_rows"],
                             seq_len=seq,
                             tail_rows=tiles["conv_tail"], col_b=col_b, col_c=col_c, col_h=col_h)
    if sample_cache is None:
        attn = _attention(q, k2, v2, sinks, batch=batch, seq=seq, n_kv=n_kv, group=group)
    else:
        attn = _attention_sample(q, k2, v2, sample_cache[0], sample_cache[1], sinks,
                                 batch=batch, t_new=seq, n_kv=n_kv, group=group)
    merged = _merge(xn, attn, z, w["w_in"], w["w_attn_proj"], w["w_conv_proj"],
                    tm=tm, tn=tiles["tn_merge"], col_ga=col_ga, col_gc=col_gc)
    x1, xn_ffn = _out_proj(x, merged, w["w_out"], w["norm_ffn_g"],
                           tm=tiles["tm_out"], rows=tiles["out_rows"])
    y = _ffn(x1, xn_ffn, w["w_ffn_gate"], w["w_ffn_up"], w["w_ffn_down"],
             tm=tiles["tm_ffn"], tf=tiles["tf"])
    return y, k, v, u_tail


def kernel(x_prompt, x_sample, cache_k, cache_v, state_conv, norm_mix_g, w_in, q_norm_g, k_norm_g,
           attn_sinks, conv_w, w_attn_proj, w_conv_proj, w_out, norm_ffn_g, w_ffn_gate, w_ffn_up,
           w_ffn_down):
    depth = w_in.shape[0]
    assert depth == 1
    b_p, s_p, d = x_prompt.shape
    b_s, s_s, _ = x_sample.shape
    n_kv = cache_k.shape[3]
    d_kv = n_kv * HEAD_DIM
    win = cache_k.shape[2]
    n_state = state_conv.shape[2]

    w = {
        "d_kv": d_kv,
        "norm_mix_g": norm_mix_g[0].reshape(1, d),
        "w_in": w_in[0].astype(BF16),
        "q_norm_g": q_norm_g[0],
        "k_norm_g": k_norm_g[0],
        "sinks": attn_sinks[0],
        "conv_w": conv_w[0],
        "w_attn_proj": w_attn_proj[0].astype(BF16),
        "w_conv_proj": w_conv_proj[0].astype(BF16),
        "w_out": w_out[0].astype(BF16),
        "norm_ffn_g": norm_ffn_g[0].reshape(1, d),
        "w_ffn_gate": w_ffn_gate[0].astype(BF16),
        "w_ffn_up": w_ffn_up[0].astype(BF16),
        "w_ffn_down": w_ffn_down[0].astype(BF16),
    }

    m_p = b_p * s_p
    tm_p = min(1024, s_p)
    tiles_p = dict(tm=tm_p, qkv_rows=256, tm_conv=s_p, tn_conv=256, conv_tail=8,
                   conv_rows=min(512, s_p), tn_merge=512, tm_out=min(512, s_p), out_rows=256,
                   tm_ffn=min(512, s_p), tf=512)
    y_p, k_p, v_p, u_p = _layer(x_prompt.reshape(m_p, d), w, batch=b_p, seq=s_p,
                                pos=jnp.arange(s_p), tiles=tiles_p, conv_prev=None,
                                sample_cache=None)
    keep = min(WINDOW, s_p)
    new_k_p = k_p.reshape(b_p, s_p, n_kv, HEAD_DIM)[:, s_p - keep:][None]
    new_v_p = v_p.reshape(b_p, s_p, n_kv, HEAD_DIM)[:, s_p - keep:][None]
    new_c_p = u_p.reshape(b_p, tiles_p["conv_tail"], -1)[:, tiles_p["conv_tail"] - n_state:][None]

    m_s = b_s * s_s
    st = state_conv[0]
    prev1 = jnp.pad(st[:, 1:2], ((0, 0), (0, s_s - 1), (0, 0))).reshape(m_s, -1)
    prev2 = jnp.pad(st, ((0, 0), (0, s_s - n_state), (0, 0))).reshape(m_s, -1)
    tiles_s = dict(tm=m_s, qkv_rows=m_s, tm_conv=m_s, tn_conv=256, conv_tail=m_s,
                   conv_rows=m_s, tn_merge=256, tm_out=m_s, out_rows=m_s, tm_ffn=m_s, tf=512)
    y_s, k_s, v_s, u_s = _layer(x_sample.reshape(m_s, d), w, batch=b_s, seq=s_s,
                                pos=PAST_LEN + jnp.arange(s_s), tiles=tiles_s,
                                conv_prev=(prev1, prev2),
                                sample_cache=(cache_k[0].reshape(b_s * win, d_kv),
                                              cache_v[0].reshape(b_s * win, d_kv)))
    new_k_s = k_s.reshape(b_s, s_s, n_kv, HEAD_DIM)[None]
    new_v_s = v_s.reshape(b_s, s_s, n_kv, HEAD_DIM)[None]
    new_c_s = u_s.reshape(b_s, s_s, -1)[:, s_s - n_state:][None]

    return (y_p.reshape(b_p, s_p, d), y_s.reshape(b_s, s_s, d), new_k_p, new_v_p, new_c_p,
            new_k_s, new_v_s, new_c_s)
```

```python
import functools
import math

import jax
import jax.numpy as jnp
from jax import lax
from jax.experimental import pallas as pl
from jax.experimental.pallas import tpu as pltpu

F32 = jnp.float32
BF16 = jnp.bfloat16

CHUNK = 64
WINDOW = 128
PAST_LEN = 1024
ROPE_THETA = 10000.0
EPS = 1e-6
NEG_INF = -1e30
HEAD_DIM = 64
LANES = 128
PAIR = 2 * CHUNK
KEY_SPAN = PAIR + WINDOW

V7X_MXU_DIM = 256
V7X_VMEM_BYTES = 64 * 1024 * 1024
VMEM_CAP_BYTES = V7X_VMEM_BYTES - 6 * 1024 * 1024


def _params(semantics, vmem_estimate):
    limit = int(min(max(vmem_estimate, 16 * 1024 * 1024), VMEM_CAP_BYTES))
    return pltpu.CompilerParams(dimension_semantics=semantics, vmem_limit_bytes=limit)


def _nbytes(shape, dtype):
    return math.prod(shape) * jnp.dtype(dtype).itemsize


def _dot(a, b):
    return jnp.dot(a, b, preferred_element_type=F32)


def _dot_nt(a, b):
    return lax.dot_general(a, b, (((1,), (1,)), ((), ())), preferred_element_type=F32)


def _lane_iota(shape):
    return lax.broadcasted_iota(jnp.int32, shape, 1)


def _rmsnorm_rows(x, g):
    y = x * lax.rsqrt(jnp.mean(x * x, axis=-1, keepdims=True) + EPS)
    return y * g


def _headnorm_rope(y, t_same, t_swap, seg):
    width = y.shape[1]
    s = y * y
    s_hi = s.astype(BF16)
    s_lo = (s - s_hi.astype(F32)).astype(BF16)
    ms = _dot(s_hi, seg) + _dot(s_lo, seg)
    half = HEAD_DIM // 2
    ahead = pltpu.roll(y, width - half, axis=1)
    behind = pltpu.roll(y, half, axis=1)
    swapped = jnp.where((_lane_iota(y.shape) & (HEAD_DIM - 1)) < half, ahead, behind)
    reps = width // t_same.shape[1]
    t_same = jnp.concatenate([t_same] * reps, axis=1)
    t_swap = jnp.concatenate([t_swap] * reps, axis=1)
    return (y * t_same + swapped * t_swap) * lax.rsqrt(ms + EPS)


def _dup_halves(x):
    low = _lane_iota(x.shape) < HEAD_DIM
    r = pltpu.roll(x, HEAD_DIM, axis=1)
    return jnp.where(low, x, r), jnp.where(low, r, x)


def _qkv_kernel(x_ref, g_ref, w_ref, seg_ref, qs_ref, qw_ref, ks_ref, kw_ref,
                xn_ref, q_ref, k_ref, v_ref, k2_ref, v2_ref, *, n_q_blocks, d_kv, rows):
    j = pl.program_id(1)
    tm, tn = q_ref.shape
    chunks = [slice(r * rows, (r + 1) * rows) for r in range(tm // rows)]
    seg = seg_ref[...]
    slab = seg.shape[0]

    def q_block(normalize):
        for rs in chunks:
            if normalize:
                xn_ref[rs, :] = _rmsnorm_rows(x_ref[rs, :], g_ref[...]).astype(BF16)
            y = _dot(xn_ref[rs, :], w_ref[...])
            for c in range(tn // slab):
                sl = slice(c * slab, (c + 1) * slab)
                q_ref[rs, sl] = _headnorm_rope(y[:, sl], qs_ref[rs, :], qw_ref[rs, :],
                                               seg).astype(BF16)

    pl.when(j == 0)(functools.partial(q_block, True))
    pl.when((j > 0) & (j < n_q_blocks))(functools.partial(q_block, False))

    @pl.when(j == n_q_blocks)
    def _():
        for rs in chunks:
            y = _dot(xn_ref[rs, :], w_ref[...])
            k_all = jnp.concatenate(
                [_headnorm_rope(y[:, c * slab:(c + 1) * slab], ks_ref[rs, :], kw_ref[rs, :], seg)
                 for c in range(d_kv // slab)], axis=1)
            for c in range(d_kv // LANES):
                sl = slice(c * LANES, (c + 1) * LANES)
                k = k_all[:, sl]
                v = y[:, d_kv + c * LANES:d_kv + (c + 1) * LANES]
                k_ref[rs, sl] = k
                v_ref[rs, sl] = v
                ka, kb = _dup_halves(k)
                va, vb = _dup_halves(v)
                lo = slice(2 * c * LANES, (2 * c + 1) * LANES)
                hi = slice((2 * c + 1) * LANES, (2 * c + 2) * LANES)
                k2_ref[rs, lo] = ka.astype(BF16)
                k2_ref[rs, hi] = kb.astype(BF16)
                v2_ref[rs, lo] = va.astype(BF16)
                v2_ref[rs, hi] = vb.astype(BF16)


def _qkv_proj(x, g, w_in, tables, *, tm, d_attn, d_kv, rows):
    m, d = x.shape
    tn = 2 * d_kv
    n_q_blocks = d_attn // tn
    rope_blocks = tables[0].shape[0] // tm
    tab = pl.BlockSpec((tm, LANES), lambda i, j: (i % rope_blocks, 0))
    row_out = lambda width: pl.BlockSpec((tm, width), lambda i, j: (i, 0))
    head_of = jnp.arange(V7X_MXU_DIM) // HEAD_DIM
    seg = jnp.where(head_of[:, None] == head_of[None, :], 1.0 / HEAD_DIM, 0.0).astype(BF16)
    est = (2 * (_nbytes((tm, d), F32) + _nbytes((tm, d), BF16) + _nbytes((d, tn), BF16)
                + 3 * _nbytes((tm, tn), BF16) + 2 * _nbytes((tm, d_kv), F32)
                + 4 * _nbytes((tm, LANES), F32))
           + 6 * _nbytes((rows, tn), F32) + 2 * _nbytes((rows, d), F32))
    return pl.pallas_call(
        functools.partial(_qkv_kernel, n_q_blocks=n_q_blocks, d_kv=d_kv, rows=rows),
        out_shape=(jax.ShapeDtypeStruct((m, d), BF16), jax.ShapeDtypeStruct((m, d_attn), BF16),
                   jax.ShapeDtypeStruct((m, d_kv), F32), jax.ShapeDtypeStruct((m, d_kv), F32),
                   jax.ShapeDtypeStruct((m, tn), BF16), jax.ShapeDtypeStruct((m, tn), BF16)),
        grid=(m // tm, n_q_blocks + 1),
        in_specs=[pl.BlockSpec((tm, d), lambda i, j: (i, 0)),
                  pl.BlockSpec((1, d), lambda i, j: (0, 0)),
                  pl.BlockSpec((d, tn), lambda i, j: (0, j)),
                  pl.BlockSpec(seg.shape, lambda i, j: (0, 0)),
                  tab, tab, tab, tab],
        out_specs=(row_out(d),
                   pl.BlockSpec((tm, tn), lambda i, j: (i, jnp.minimum(j, n_q_blocks - 1))),
                   row_out(d_kv), row_out(d_kv), row_out(tn), row_out(tn)),
        compiler_params=_params(("parallel", "arbitrary"), est),
        name="qkv_proj",
    )(x, g, w_in, seg, *tables)


def _conv_kernel(*refs, seq_len, tail_rows, has_state, rows):
    if has_state:
        xn_ref, wb_ref, wc_ref, wh_ref, cw_ref, p1_ref, p2_ref, z_ref, u_ref = refs
    else:
        xn_ref, wb_ref, wc_ref, wh_ref, cw_ref, z_ref, u_ref = refs
    tm, tn = z_ref.shape
    w = cw_ref[...]
    row = lax.broadcasted_iota(jnp.int32, (rows, tn), 0)
    last1 = last2 = None
    for r in range(tm // rows):
        rs = slice(r * rows, (r + 1) * rows)
        xn = xn_ref[rs, :]
        u = _dot(xn, wc_ref[...]) * _dot(xn, wh_ref[...])
        u1 = pltpu.roll(u, 1, axis=0)
        u2 = pltpu.roll(u, 2, axis=0)
        if r > 0:
            u1 = jnp.where(row == 0, last1, u1)
            u2 = jnp.where(row == 0, last2, jnp.where(row == 1, last1, u2))
        if (r * rows) % seq_len == 0 or seq_len < rows:
            pos = (row + r * rows) & (seq_len - 1)
            u1 = jnp.where(pos == 0, p1_ref[rs, :] if has_state else 0.0, u1)
            u2 = jnp.where(pos < 2, p2_ref[rs, :] if has_state else 0.0, u2)
        y = u2 * w[0:1] + u1 * w[1:2] + u * w[2:3]
        z_ref[rs, :] = (_dot(xn, wb_ref[...]) * y).astype(BF16)
        last1, last2 = u[rows - 1:rows], u[rows - 2:rows - 1]
    u_ref[...] = u[rows - tail_rows:]


def _conv_branch(xn, w_in, conv_w, prev, *, tm, tn, rows, seq_len, tail_rows, col_b, col_c, col_h):
    m, d = xn.shape
    n = conv_w.shape[1]
    has_state = prev is not None
    assert seq_len & (seq_len - 1) == 0 and tm % seq_len == 0
    assert tm % rows == 0 and tail_rows <= rows and (rows % seq_len == 0 or seq_len % rows == 0)
    wspec = lambda off: pl.BlockSpec((d, tn), lambda i, j: (0, off // tn + j))
    in_specs = [pl.BlockSpec((tm, d), lambda i, j: (i, 0)),
                wspec(col_b), wspec(col_c), wspec(col_h),
                pl.BlockSpec((conv_w.shape[0], tn), lambda i, j: (0, j))]
    args = [xn, w_in, w_in, w_in, conv_w]
    if has_state:
        in_specs += [pl.BlockSpec((tm, tn), lambda i, j: (i, j))] * 2
        args += list(prev)
    est = (2 * (_nbytes((tm, d), BF16) + 3 * _nbytes((d, tn), BF16) + _nbytes((tm, tn), BF16)
                + _nbytes((tail_rows, tn), F32)) + 8 * _nbytes((tm, tn), F32))
    return pl.pallas_call(
        functools.partial(_conv_kernel, seq_len=seq_len, tail_rows=tail_rows, has_state=has_state,
                          rows=rows),
        out_shape=(jax.ShapeDtypeStruct((m, n), BF16),
                   jax.ShapeDtypeStruct((m // tm * tail_rows, n), F32)),
        grid=(m // tm, n // tn),
        in_specs=in_specs,
        out_specs=(pl.BlockSpec((tm, tn), lambda i, j: (i, j)),
                   pl.BlockSpec((tail_rows, tn), lambda i, j: (i, j))),
        compiler_params=_params(("parallel", "arbitrary"), est),
        name="conv_branch",
    )(*args)


def _stack_heads(q_ref, rows, col0, group):
    parts = []
    for g in range(group):
        blk = q_ref[rows, col0 + (g // 2) * LANES:col0 + (g // 2 + 1) * LANES]
        keep = (_lane_iota(blk.shape) >= HEAD_DIM) == bool(g % 2)
        parts.append(jnp.where(keep, blk, jnp.zeros_like(blk)))
    return jnp.concatenate(parts, axis=0)


def _sink_column(sink_ref, head0, group, rows):
    return jnp.concatenate(
        [jnp.full((rows, 1), sink_ref[head0 + g], F32) for g in range(group)], axis=0)


def _store_heads(o_ref, rows, col0, o, group, nrows):
    for lp in range(group // 2):
        a = o[(2 * lp) * nrows:(2 * lp + 1) * nrows]
        b = o[(2 * lp + 1) * nrows:(2 * lp + 2) * nrows]
        low = _lane_iota(a.shape) < HEAD_DIM
        o_ref[rows, col0 + lp * LANES:col0 + (lp + 1) * LANES] = jnp.where(low, a, b).astype(BF16)


def _attn_pair(pair, sink_ref, q_ref, k_ref, v_ref, o_ref, n_kv, group, has_prev):
    def block_rows(index):
        start = index * PAIR
        return pl.ds(start if isinstance(start, int) else pl.multiple_of(start, PAIR), PAIR)

    rows = block_rows(pair)
    prev = block_rows(pair - 1) if has_prev else None
    first_q = _lane_iota((CHUNK, PAIR)) < CHUNK

    def keys_of(ref, kv):
        cols = slice(kv * LANES, (kv + 1) * LANES)
        if has_prev:
            return jnp.concatenate([ref[prev, cols], ref[rows, cols]], axis=0)
        return ref[rows, cols]

    def scores(kv):
        return _dot_nt(keys_of(k_ref, kv), _stack_heads(q_ref, rows, kv * group * HEAD_DIM, group))

    st_next = scores(0)
    for kv in range(n_kv):
        st = st_next
        if kv + 1 < n_kv:
            st_next = scores(kv + 1)
        vt = keys_of(v_ref, kv).astype(F32).T.astype(BF16)
        col0 = kv * group * HEAD_DIM
        probs, rden = [], []
        for g in range(group):
            sg = st[:, g * PAIR:(g + 1) * PAIR]
            blocks = [sg[c * CHUNK:(c + 1) * CHUNK] for c in range(sg.shape[0] // CHUNK)]
            if has_prev:
                s = jnp.concatenate([jnp.where(first_q, blocks[0], blocks[3]),
                                     blocks[1], blocks[2]], axis=0)
            else:
                s = jnp.concatenate([blocks[0], jnp.where(first_q, NEG_INF, blocks[1])], axis=0)
            sink = sink_ref[kv * group + g]
            m = jnp.maximum(jnp.max(s, axis=0, keepdims=True), sink)
            e = jnp.exp2(s - m)
            den = jnp.sum(e, axis=0, keepdims=True) + jnp.exp2(sink - m)
            e = e.astype(BF16)
            if has_prev:
                e_03, e_1, e_2 = (e[c * CHUNK:(c + 1) * CHUNK] for c in range(3))
                zero = jnp.zeros_like(e_03)
                e = jnp.concatenate([jnp.where(first_q, e_03, zero), e_1, e_2,
                                     jnp.where(first_q, zero, e_03)], axis=0)
            probs.append(e)
            rden.append(1.0 / den)
        ot = _dot(vt, jnp.concatenate(probs, axis=1)) * jnp.concatenate(rden, axis=1)
        for lp in range(group // 2):
            a = ot[:HEAD_DIM, (2 * lp) * PAIR:(2 * lp + 1) * PAIR]
            b = ot[HEAD_DIM:, (2 * lp + 1) * PAIR:(2 * lp + 2) * PAIR]
            o_ref[rows, col0 + lp * LANES:col0 + (lp + 1) * LANES] = (
                jnp.concatenate([a, b], axis=0).T.astype(BF16))


def _attn_kernel(sink_ref, q_ref, k_ref, v_ref, o_ref, *, n_kv, group):
    def body(pair, carry):
        _attn_pair(pair, sink_ref, q_ref, k_ref, v_ref, o_ref, n_kv, group, True)
        return carry

    _attn_pair(0, sink_ref, q_ref, k_ref, v_ref, o_ref, n_kv, group, False)
    lax.fori_loop(1, q_ref.shape[0] // PAIR, body, 0)


def _attention(q, k2, v2, sinks, *, batch, seq, n_kv, group):
    m, d = q.shape
    assert WINDOW == PAIR and seq % PAIR == 0
    est = (2 * (2 * _nbytes((seq, d), BF16) + 2 * _nbytes((seq, n_kv * LANES), BF16))
           + 8 * _nbytes((group * PAIR, KEY_SPAN), F32))
    return pl.pallas_call(
        functools.partial(_attn_kernel, n_kv=n_kv, group=group),
        out_shape=jax.ShapeDtypeStruct((m, d), BF16),
        grid=(batch,),
        in_specs=[pl.BlockSpec(memory_space=pltpu.SMEM),
                  pl.BlockSpec((seq, d), lambda b: (b, 0)),
                  pl.BlockSpec((seq, n_kv * LANES), lambda b: (b, 0)),
                  pl.BlockSpec((seq, n_kv * LANES), lambda b: (b, 0))],
        out_specs=pl.BlockSpec((seq, d), lambda b: (b, 0)),
        compiler_params=_params(("parallel",), est),
        name="attention",
    )(sinks, q, k2, v2)


def _attn_sample_kernel(sink_ref, q_ref, kn_ref, vn_ref, kc_ref, vc_ref, o_ref, *, n_kv, group):
    t = q_ref.shape[0]
    for kv in range(n_kv):
        pair_sl = slice((kv // 2) * LANES, (kv // 2 + 1) * LANES)
        kc = _dup_halves(kc_ref[:, pair_sl])[kv % 2].astype(BF16)
        vc = _dup_halves(vc_ref[:, pair_sl])[kv % 2].astype(BF16)
        kn = kn_ref[:, kv * LANES:(kv + 1) * LANES]
        vn = vn_ref[:, kv * LANES:(kv + 1) * LANES]
        col0 = kv * group * HEAD_DIM
        lhs = _stack_heads(q_ref, slice(0, t), col0, group)
        s_c = _dot_nt(lhs, kc)
        s_n = _dot_nt(lhs, kn)
        sink = _sink_column(sink_ref, kv * group, group, t)
        m = jnp.maximum(jnp.maximum(jnp.max(s_c, axis=1, keepdims=True),
                                    jnp.max(s_n, axis=1, keepdims=True)), sink)
        e_c = jnp.exp2(s_c - m)
        e_n = jnp.exp2(s_n - m)
        den = (jnp.sum(e_c, axis=1, keepdims=True) + jnp.sum(e_n, axis=1, keepdims=True)
               + jnp.exp2(sink - m))
        o = (_dot(e_c.astype(BF16), vc) + _dot(e_n.astype(BF16), vn)) / den
        _store_heads(o_ref, slice(0, t), col0, o, group, t)


def _attention_sample(q, k2, v2, cache_k, cache_v, sinks, *, batch, t_new, n_kv, group):
    m, d = q.shape
    win = cache_k.shape[0] // batch
    d_kv = cache_k.shape[1]
    est = 4 * 1024 * 1024
    return pl.pallas_call(
        functools.partial(_attn_sample_kernel, n_kv=n_kv, group=group),
        out_shape=jax.ShapeDtypeStruct((m, d), BF16),
        grid=(batch,),
        in_specs=[pl.BlockSpec(memory_space=pltpu.SMEM),
                  pl.BlockSpec((t_new, d), lambda b: (b, 0)),
                  pl.BlockSpec((t_new, n_kv * LANES), lambda b: (b, 0)),
                  pl.BlockSpec((t_new, n_kv * LANES), lambda b: (b, 0)),
                  pl.BlockSpec((win, d_kv), lambda b: (b, 0)),
                  pl.BlockSpec((win, d_kv), lambda b: (b, 0))],
        out_specs=pl.BlockSpec((t_new, d), lambda b: (b, 0)),
        compiler_params=_params(("parallel",), est),
        name="attention_sample",
    )(sinks, q, k2, v2, cache_k, cache_v)


def _merge_kernel(xn_ref, a_ref, z_ref, wga_ref, wgc_ref, wap_ref, wcp_ref, o_ref):
    xn = xn_ref[...]
    attn = jax.nn.sigmoid(_dot(xn, wga_ref[...])) * _dot(a_ref[...], wap_ref[...])
    conv = jax.nn.sigmoid(_dot(xn, wgc_ref[...])) * _dot(z_ref[...], wcp_ref[...])
    o_ref[...] = (attn + conv).astype(BF16)


def _merge(xn, attn, z, w_in, w_ap, w_cp, *, tm, tn, col_ga, col_gc):
    m, d = xn.shape
    act = pl.BlockSpec((tm, d), lambda i, j: (i, 0))
    wspec = lambda off: pl.BlockSpec((d, tn), lambda i, j: (0, off // tn + j))
    est = (2 * (3 * _nbytes((tm, d), BF16) + 4 * _nbytes((d, tn), BF16) + _nbytes((tm, tn), BF16))
           + 6 * _nbytes((tm, tn), F32))
    return pl.pallas_call(
        _merge_kernel,
        out_shape=jax.ShapeDtypeStruct((m, d), BF16),
        grid=(m // tm, d // tn),
        in_specs=[act, act, act, wspec(col_ga), wspec(col_gc), wspec(0), wspec(0)],
        out_specs=pl.BlockSpec((tm, tn), lambda i, j: (i, j)),
        compiler_params=_params(("parallel", "arbitrary"), est),
        name="merge",
    )(xn, attn, z, w_in, w_in, w_ap, w_cp)


def _out_kernel(x_ref, m_ref, w_ref, g_ref, x1_ref, xn_ref, *, rows):
    for r in range(x_ref.shape[0] // rows):
        rs = slice(r * rows, (r + 1) * rows)
        x1 = x_ref[rs, :] + _dot(m_ref[rs, :], w_ref[...])
        x1_ref[rs, :] = x1
        xn_ref[rs, :] = _rmsnorm_rows(x1, g_ref[...]).astype(BF16)


def _out_proj(x, merged, w_out, g_ffn, *, tm, rows):
    m, d = x.shape
    est = (2 * (2 * _nbytes((tm, d), F32) + 2 * _nbytes((tm, d), BF16)) + _nbytes((d, d), BF16)
           + 4 * _nbytes((rows, d), F32))
    return pl.pallas_call(
        functools.partial(_out_kernel, rows=rows),
        out_shape=(jax.ShapeDtypeStruct((m, d), F32), jax.ShapeDtypeStruct((m, d), BF16)),
        grid=(m // tm,),
        in_specs=[pl.BlockSpec((tm, d), lambda i: (i, 0)),
                  pl.BlockSpec((tm, d), lambda i: (i, 0)),
                  pl.BlockSpec((d, d), lambda i: (0, 0), pipeline_mode=pl.Buffered(1)),
                  pl.BlockSpec((1, d), lambda i: (0, 0))],
        out_specs=(pl.BlockSpec((tm, d), lambda i: (i, 0)),
                   pl.BlockSpec((tm, d), lambda i: (i, 0))),
        compiler_params=_params(("parallel",), est),
        name="out_proj",
    )(x, merged, w_out, g_ffn)


def _ffn_kernel(x_ref, xn_ref, wg_ref, wu_ref, wd_ref, o_ref):
    f = pl.program_id(1)
    slab = x_ref.shape[1]
    n_slabs = o_ref.shape[1] // slab

    @pl.when(f == 0)
    def _():
        o_ref[...] = jnp.zeros_like(o_ref)

    xn = xn_ref[...]
    h = jax.nn.silu(_dot(xn, wg_ref[...])) * _dot(xn, wu_ref[...])
    o_ref[...] += _dot(h.astype(BF16), wd_ref[...])

    for s in range(n_slabs):
        @pl.when(f == s)
        def _(s=s):
            o_ref[:, s * slab:(s + 1) * slab] += x_ref[...]


def _ffn(x, xn, w_gate, w_up, w_down, *, tm, tf, slab):
    m, d = x.shape
    d_ff = w_gate.shape[1]
    n_slabs = d // slab
    assert n_slabs <= d_ff // tf
    est = (2 * (_nbytes((tm, slab), F32) + _nbytes((tm, d), F32) + _nbytes((tm, d), BF16)
                + 3 * _nbytes((d, tf), BF16))
           + 4 * _nbytes((tm, tf), F32) + _nbytes((tm, d), F32))
    row = lambda i, f: (i, 0)
    return pl.pallas_call(
        _ffn_kernel,
        out_shape=jax.ShapeDtypeStruct((m, d), F32),
        grid=(m // tm, d_ff // tf),
        in_specs=[pl.BlockSpec((tm, slab), lambda i, f: (i, jnp.minimum(f, n_slabs - 1))),
                  pl.BlockSpec((tm, d), row),
                  pl.BlockSpec((d, tf), lambda i, f: (0, f)),
                  pl.BlockSpec((d, tf), lambda i, f: (0, f)),
                  pl.BlockSpec((tf, d), lambda i, f: (f, 0))],
        out_specs=pl.BlockSpec((tm, d), row),
        compiler_params=_params(("parallel", "arbitrary"), est),
        name="ffn",
    )(x, xn, w_gate, w_up, w_down)


def _rope_tables(pos, repeat, g, scale):
    half = HEAD_DIM // 2
    inv = jnp.exp(-math.log(ROPE_THETA) * jnp.arange(half, dtype=F32) * (2.0 / HEAD_DIM))
    ang = pos.astype(F32)[:, None] * inv[None, :]
    cos = jnp.cos(ang)
    sin = jnp.sin(ang)
    g_swap = jnp.concatenate([g[half:], g[:half]])
    t_same = g[None, :] * jnp.concatenate([cos, cos], axis=-1) * scale
    t_swap = g_swap[None, :] * jnp.concatenate([-sin, sin], axis=-1) * scale
    reps = (repeat, LANES // HEAD_DIM)
    return jnp.tile(t_same, reps), jnp.tile(t_swap, reps)


def _layer(x, w, *, batch, seq, pos, tiles, conv_prev, sample_cache):
    d = x.shape[1]
    n_heads = w["sinks"].shape[0]
    d_attn = n_heads * HEAD_DIM
    d_kv = w["d_kv"]
    n_kv = d_kv // HEAD_DIM
    group = n_heads // n_kv
    d_conv = w["conv_w"].shape[1]
    col_b = d_attn + 2 * d_kv
    col_c = col_b + d_conv
    col_h = col_c + d_conv
    col_ga = col_h + d_conv
    col_gc = col_ga + d
    tm = tiles["tm"]

    rope_rows = max(seq, tm)
    log2e = math.log2(math.e)
    tables = (_rope_tables(pos, rope_rows // seq, w["q_norm_g"], HEAD_DIM ** -0.5 * log2e)
              + _rope_tables(pos, rope_rows // seq, w["k_norm_g"], 1.0))
    sinks = w["sinks"] * log2e

    xn, q, k, v, k2, v2 = _qkv_proj(x, w["norm_mix_g"], w["w_in"], tables, tm=tm, d_attn=d_attn,
                                    d_kv=d_kv, rows=tiles["qkv_rows"])
    z, u_tail = _conv_branch(xn, w["w_in"], w["conv_w"], conv_prev,
                             tm=tiles["tm_conv"], tn=tiles["tn_conv"], rows=tiles["conv_rows"],
                             seq_len=seq,
                             tail_rows=tiles["conv_tail"], col_b=col_b, col_c=col_c, col_h=col_h)
    if sample_cache is None:
        attn = _attention(q, k2, v2, sinks, batch=batch, seq=seq, n_kv=n_kv, group=group)
    else:
        attn = _attention_sample(q, k2, v2, sample_cache[0], sample_cache[1], sinks,
                                 batch=batch, t_new=seq, n_kv=n_kv, group=group)
    merged = _merge(xn, attn, z, w["w_in"], w["w_attn_proj"], w["w_conv_proj"],
                    tm=tm, tn=tiles["tn_merge"], col_ga=col_ga, col_gc=col_gc)
    x1, xn_ffn = _out_proj(x, merged, w["w_out"], w["norm_ffn_g"],
                           tm=tiles["tm_out"], rows=tiles["out_rows"])
    y = _ffn(x1, xn_ffn, w["w_ffn_gate"], w["w_ffn_up"], w["w_ffn_down"],
             tm=tiles["tm_ffn"], tf=tiles["tf"], slab=V7X_MXU_DIM)
    return y, k, v, u_tail


def kernel(x_prompt, x_sample, cache_k, cache_v, state_conv, norm_mix_g, w_in, q_norm_g, k_norm_g,
           attn_sinks, conv_w, w_attn_proj, w_conv_proj, w_out, norm_ffn_g, w_ffn_gate, w_ffn_up,
           w_ffn_down):
    depth = w_in.shape[0]
    assert depth == 1
    b_p, s_p, d = x_prompt.shape
    b_s, s_s, _ = x_sample.shape
    n_kv = cache_k.shape[3]
    d_kv = n_kv * HEAD_DIM
    win = cache_k.shape[2]
    n_state = state_conv.shape[2]

    w = {
        "d_kv": d_kv,
        "norm_mix_g": norm_mix_g[0].reshape(1, d),
        "w_in": w_in[0].astype(BF16),
        "q_norm_g": q_norm_g[0],
        "k_norm_g": k_norm_g[0],
        "sinks": attn_sinks[0],
        "conv_w": conv_w[0],
        "w_attn_proj": w_attn_proj[0].astype(BF16),
        "w_conv_proj": w_conv_proj[0].astype(BF16),
        "w_out": w_out[0].astype(BF16),
        "norm_ffn_g": norm_ffn_g[0].reshape(1, d),
        "w_ffn_gate": w_ffn_gate[0].astype(BF16),
        "w_ffn_up": w_ffn_up[0].astype(BF16),
        "w_ffn_down": w_ffn_down[0].astype(BF16),
    }

    m_p = b_p * s_p
    tm_p = min(1024, s_p)
    tiles_p = dict(tm=tm_p, qkv_rows=256, tm_conv=s_p, tn_conv=256, conv_tail=8,
                   conv_rows=min(512, s_p), tn_merge=512, tm_out=min(512, s_p), out_rows=256,
                   tm_ffn=tm_p, tf=512)
    y_p, k_p, v_p, u_p = _layer(x_prompt.reshape(m_p, d), w, batch=b_p, seq=s_p,
                                pos=jnp.arange(s_p), tiles=tiles_p, conv_prev=None,
                                sample_cache=None)
    keep = min(WINDOW, s_p)
    new_k_p = k_p.reshape(b_p, s_p, n_kv, HEAD_DIM)[:, s_p - keep:][None]
    new_v_p = v_p.reshape(b_p, s_p, n_kv, HEAD_DIM)[:, s_p - keep:][None]
    new_c_p = u_p.reshape(b_p, tiles_p["conv_tail"], -1)[:, tiles_p["conv_tail"] - n_state:][None]

    m_s = b_s * s_s
    st = state_conv[0]
    prev1 = jnp.pad(st[:, 1:2], ((0, 0), (0, s_s - 1), (0, 0))).reshape(m_s, -1)
    prev2 = jnp.pad(st, ((0, 0), (0, s_s - n_state), (0, 0))).reshape(m_s, -1)
    tiles_s = dict(tm=m_s, qkv_rows=m_s, tm_conv=m_s, tn_conv=256, conv_tail=m_s,
                   conv_rows=m_s, tn_merge=256, tm_out=m_s, out_rows=m_s, tm_ffn=m_s, tf=512)
    y_s, k_s, v_s, u_s = _layer(x_sample.reshape(m_s, d), w, batch=b_s, seq=s_s,
                                pos=PAST_LEN + jnp.arange(s_s), tiles=tiles_s,
                                conv_prev=(prev1, prev2),
                                sample_cache=(cache_k[0].reshape(b_s * win, d_kv),
                                              cache_v[0].reshape(b_s * win, d_kv)))
    new_k_s = k_s.reshape(b_s, s_s, n_kv, HEAD_DIM)[None]
    new_v_s = v_s.reshape(b_s, s_s, n_kv, HEAD_DIM)[None]
    new_c_s = u_s.reshape(b_s, s_s, -1)[:, s_s - n_state:][None]

    return (y_p.reshape(b_p, s_p, d), y_s.reshape(b_s, s_s, d), new_k_p, new_v_p, new_c_p,
            new_k_s, new_v_s, new_c_s)
```

```python
import functools
import math

import jax
import jax.numpy as jnp
from jax import lax
from jax.experimental import pallas as pl
from jax.experimental.pallas import tpu as pltpu

F32 = jnp.float32
BF16 = jnp.bfloat16

CHUNK = 64
WINDOW = 128
PAST_LEN = 1024
ROPE_THETA = 10000.0
EPS = 1e-6
NEG_INF = -1e30
HEAD_DIM = 64
LANES = 128
PAIR = 2 * CHUNK
KEY_SPAN = PAIR + WINDOW
PAIRS_PER_TRIP = 3

V7X_MXU_DIM = 256
V7X_VMEM_BYTES = 64 * 1024 * 1024
VMEM_CAP_BYTES = V7X_VMEM_BYTES - 6 * 1024 * 1024


def _params(semantics, vmem_estimate):
    limit = int(min(max(vmem_estimate, 16 * 1024 * 1024), VMEM_CAP_BYTES))
    return pltpu.CompilerParams(dimension_semantics=semantics, vmem_limit_bytes=limit)


def _nbytes(shape, dtype):
    return math.prod(shape) * jnp.dtype(dtype).itemsize


def _dot(a, b):
    return jnp.dot(a, b, preferred_element_type=F32)


def _dot_nt(a, b):
    return lax.dot_general(a, b, (((1,), (1,)), ((), ())), preferred_element_type=F32)


def _lane_iota(shape):
    return lax.broadcasted_iota(jnp.int32, shape, 1)


def _rmsnorm_rows(x, g):
    y = x * lax.rsqrt(jnp.mean(x * x, axis=-1, keepdims=True) + EPS)
    return y * g


def _headnorm_rope(y, t_same, t_swap, seg):
    width = y.shape[1]
    s = y * y
    s_hi = s.astype(BF16)
    s_lo = (s - s_hi.astype(F32)).astype(BF16)
    ms = _dot(s_hi, seg) + _dot(s_lo, seg)
    half = HEAD_DIM // 2
    ahead = pltpu.roll(y, width - half, axis=1)
    behind = pltpu.roll(y, half, axis=1)
    swapped = jnp.where((_lane_iota(y.shape) & (HEAD_DIM - 1)) < half, ahead, behind)
    reps = width // t_same.shape[1]
    t_same = jnp.concatenate([t_same] * reps, axis=1)
    t_swap = jnp.concatenate([t_swap] * reps, axis=1)
    return (y * t_same + swapped * t_swap) * lax.rsqrt(ms + EPS)


def _dup_halves(x):
    low = _lane_iota(x.shape) < HEAD_DIM
    r = pltpu.roll(x, HEAD_DIM, axis=1)
    return jnp.where(low, x, r), jnp.where(low, r, x)


def _isolate_halves(x):
    low = _lane_iota(x.shape) < HEAD_DIM
    r = pltpu.roll(x, HEAD_DIM, axis=1)
    return (jnp.where(low, x, 0.0), jnp.where(low, 0.0, r),
            jnp.where(low, r, 0.0), jnp.where(low, 0.0, x))


def _qkv_kernel(*refs, n_q_blocks, d_kv, rows):
    x_refs = refs[:n_q_blocks]
    g_ref, w_ref, seg_ref, qs_ref, qw_ref, ks_ref, kw_ref = refs[n_q_blocks:n_q_blocks + 7]
    xn_ref, q_ref, k_ref, v_ref, k2_ref, v2_ref = refs[n_q_blocks + 7:]
    j = pl.program_id(1)
    tm, tn = q_ref.shape
    d = xn_ref.shape[1]
    xw = d // n_q_blocks
    chunks = [slice(r * rows, (r + 1) * rows) for r in range(tm // rows)]
    seg = seg_ref[...]
    slab = seg.shape[0]

    def q_block(normalize):
        for rs in chunks:
            if normalize:
                xs = [ref[rs, :] for ref in x_refs]
                ss = sum(jnp.sum(xc * xc, axis=-1, keepdims=True) for xc in xs)
                r = lax.rsqrt(ss * (1.0 / d) + EPS)
                for c, xc in enumerate(xs):
                    cs = slice(c * xw, (c + 1) * xw)
                    xn_ref[rs, cs] = (xc * r * g_ref[:, cs]).astype(BF16)
            y = _dot(xn_ref[rs, :], w_ref[...])
            for c in range(tn // slab):
                sl = slice(c * slab, (c + 1) * slab)
                q_ref[rs, sl] = _headnorm_rope(y[:, sl], qs_ref[rs, :], qw_ref[rs, :],
                                               seg).astype(BF16)

    pl.when(j == 0)(functools.partial(q_block, True))
    pl.when((j > 0) & (j < n_q_blocks))(functools.partial(q_block, False))

    @pl.when(j == n_q_blocks)
    def _():
        for rs in chunks:
            y = _dot(xn_ref[rs, :], w_ref[...])
            k_all = jnp.concatenate(
                [_headnorm_rope(y[:, c * slab:(c + 1) * slab], ks_ref[rs, :], kw_ref[rs, :], seg)
                 for c in range(d_kv // slab)], axis=1)
            for c in range(d_kv // LANES):
                sl = slice(c * LANES, (c + 1) * LANES)
                k = k_all[:, sl]
                v = y[:, d_kv + c * LANES:d_kv + (c + 1) * LANES]
                k_ref[rs, sl] = k
                v_ref[rs, sl] = v
                va, vb = _dup_halves(v)
                lo = slice(2 * c * LANES, (2 * c + 1) * LANES)
                hi = slice((2 * c + 1) * LANES, (2 * c + 2) * LANES)
                v2_ref[rs, lo] = va.astype(BF16)
                v2_ref[rs, hi] = vb.astype(BF16)
                for n, piece in enumerate(_isolate_halves(k)):
                    k2_ref[rs, (4 * c + n) * LANES:(4 * c + n + 1) * LANES] = piece.astype(BF16)


def _qkv_proj(x, g, w_in, tables, *, tm, d_attn, d_kv, rows):
    m, d = x.shape
    tn = 2 * d_kv
    n_q_blocks = d_attn // tn
    rope_blocks = tables[0].shape[0] // tm
    tab = pl.BlockSpec((tm, LANES), lambda i, j: (i % rope_blocks, 0))
    row_out = lambda width: pl.BlockSpec((tm, width), lambda i, j: (i, 0))
    last_tile = m // tm - 1

    def x_slab(c):
        return pl.BlockSpec((tm, d // n_q_blocks),
                            lambda i, j: (jnp.minimum(jnp.where(j > c, i + 1, i), last_tile), c))

    head_of = jnp.arange(V7X_MXU_DIM) // HEAD_DIM
    seg = jnp.where(head_of[:, None] == head_of[None, :], 1.0 / HEAD_DIM, 0.0).astype(BF16)
    est = (2 * (_nbytes((tm, d), F32) + _nbytes((tm, d), BF16) + _nbytes((d, tn), BF16)
                + 4 * _nbytes((tm, tn), BF16) + 2 * _nbytes((tm, d_kv), F32)
                + 4 * _nbytes((tm, LANES), F32))
           + 6 * _nbytes((rows, tn), F32) + 2 * _nbytes((rows, d), F32))
    return pl.pallas_call(
        functools.partial(_qkv_kernel, n_q_blocks=n_q_blocks, d_kv=d_kv, rows=rows),
        out_shape=(jax.ShapeDtypeStruct((m, d), BF16), jax.ShapeDtypeStruct((m, d_attn), BF16),
                   jax.ShapeDtypeStruct((m, d_kv), F32), jax.ShapeDtypeStruct((m, d_kv), F32),
                   jax.ShapeDtypeStruct((m, 2 * tn), BF16), jax.ShapeDtypeStruct((m, tn), BF16)),
        grid=(m // tm, n_q_blocks + 1),
        in_specs=[x_slab(c) for c in range(n_q_blocks)] + [
                  pl.BlockSpec((1, d), lambda i, j: (0, 0)),
                  pl.BlockSpec((d, tn), lambda i, j: (0, j)),
                  pl.BlockSpec(seg.shape, lambda i, j: (0, 0)),
                  tab, tab, tab, tab],
        out_specs=(row_out(d),
                   pl.BlockSpec((tm, tn), lambda i, j: (i, jnp.minimum(j, n_q_blocks - 1))),
                   row_out(d_kv), row_out(d_kv), row_out(2 * tn), row_out(tn)),
        compiler_params=_params(("parallel", "arbitrary"), est),
        name="qkv_proj",
    )(*([x] * n_q_blocks), g, w_in, seg, *tables)


def _conv_kernel(*refs, seq_len, tail_rows, has_state, rows):
    if has_state:
        xn_ref, wb_ref, wc_ref, wh_ref, cw_ref, p1_ref, p2_ref, z_ref, u_ref = refs
    else:
        xn_ref, wb_ref, wc_ref, wh_ref, cw_ref, z_ref, u_ref = refs
    tm, tn = z_ref.shape
    w = cw_ref[...]
    row = lax.broadcasted_iota(jnp.int32, (rows, tn), 0)
    last1 = last2 = None
    for r in range(tm // rows):
        rs = slice(r * rows, (r + 1) * rows)
        xn = xn_ref[rs, :]
        u = _dot(xn, wc_ref[...]) * _dot(xn, wh_ref[...])
        u1 = pltpu.roll(u, 1, axis=0)
        u2 = pltpu.roll(u, 2, axis=0)
        if r > 0:
            u1 = jnp.where(row == 0, last1, u1)
            u2 = jnp.where(row == 0, last2, jnp.where(row == 1, last1, u2))
        if (r * rows) % seq_len == 0 or seq_len < rows:
            pos = (row + r * rows) & (seq_len - 1)
            u1 = jnp.where(pos == 0, p1_ref[rs, :] if has_state else 0.0, u1)
            u2 = jnp.where(pos < 2, p2_ref[rs, :] if has_state else 0.0, u2)
        y = u2 * w[0:1] + u1 * w[1:2] + u * w[2:3]
        z_ref[rs, :] = (_dot(xn, wb_ref[...]) * y).astype(BF16)
        last1, last2 = u[rows - 1:rows], u[rows - 2:rows - 1]
    u_ref[...] = u[rows - tail_rows:]


def _conv_branch(xn, w_in, conv_w, prev, *, tm, tn, rows, seq_len, tail_rows, col_b, col_c, col_h):
    m, d = xn.shape
    n = conv_w.shape[1]
    has_state = prev is not None
    assert seq_len & (seq_len - 1) == 0 and tm % seq_len == 0
    assert tm % rows == 0 and tail_rows <= rows and (rows % seq_len == 0 or seq_len % rows == 0)
    wspec = lambda off: pl.BlockSpec((d, tn), lambda i, j: (0, off // tn + j))
    in_specs = [pl.BlockSpec((tm, d), lambda i, j: (i, 0)),
                wspec(col_b), wspec(col_c), wspec(col_h),
                pl.BlockSpec((conv_w.shape[0], tn), lambda i, j: (0, j))]
    args = [xn, w_in, w_in, w_in, conv_w]
    if has_state:
        in_specs += [pl.BlockSpec((tm, tn), lambda i, j: (i, j))] * 2
        args += list(prev)
    est = (2 * (_nbytes((tm, d), BF16) + 3 * _nbytes((d, tn), BF16) + _nbytes((tm, tn), BF16)
                + _nbytes((tail_rows, tn), F32)) + 8 * _nbytes((tm, tn), F32))
    return pl.pallas_call(
        functools.partial(_conv_kernel, seq_len=seq_len, tail_rows=tail_rows, has_state=has_state,
                          rows=rows),
        out_shape=(jax.ShapeDtypeStruct((m, n), BF16),
                   jax.ShapeDtypeStruct((m // tm * tail_rows, n), F32)),
        grid=(m // tm, n // tn),
        in_specs=in_specs,
        out_specs=(pl.BlockSpec((tm, tn), lambda i, j: (i, j)),
                   pl.BlockSpec((tail_rows, tn), lambda i, j: (i, j))),
        compiler_params=_params(("parallel", "arbitrary"), est),
        name="conv_branch",
    )(*args)


def _stack_heads(q_ref, rows, col0, group):
    parts = []
    for g in range(group):
        blk = q_ref[rows, col0 + (g // 2) * LANES:col0 + (g // 2 + 1) * LANES]
        keep = (_lane_iota(blk.shape) >= HEAD_DIM) == bool(g % 2)
        parts.append(jnp.where(keep, blk, jnp.zeros_like(blk)))
    return jnp.concatenate(parts, axis=0)


def _sink_column(sink_ref, head0, group, rows):
    return jnp.concatenate(
        [jnp.full((rows, 1), sink_ref[head0 + g], F32) for g in range(group)], axis=0)


def _store_heads(o_ref, rows, col0, o, group, nrows):
    for lp in range(group // 2):
        a = o[(2 * lp) * nrows:(2 * lp + 1) * nrows]
        b = o[(2 * lp + 1) * nrows:(2 * lp + 2) * nrows]
        low = _lane_iota(a.shape) < HEAD_DIM
        o_ref[rows, col0 + lp * LANES:col0 + (lp + 1) * LANES] = jnp.where(low, a, b).astype(BF16)


def _attn_pairs(pairs, sink_ref, q_ref, k_ref, v_ref, o_ref, n_kv, group):
    def block_rows(index):
        start = index * PAIR
        return pl.ds(start if isinstance(start, int) else pl.multiple_of(start, PAIR), PAIR)

    first_q = _lane_iota((CHUNK, PAIR)) < CHUNK

    def keys_of(ref, pair, has_prev, tile):
        cols = slice(tile * LANES, (tile + 1) * LANES)
        if has_prev:
            return jnp.concatenate([ref[block_rows(pair - 1), cols], ref[block_rows(pair), cols]],
                                   axis=0)
        return ref[block_rows(pair), cols]

    def scores(pair, has_prev, kv):
        col0 = kv * group * HEAD_DIM
        q_tiles = jnp.concatenate(
            [q_ref[block_rows(pair), col0 + lp * LANES:col0 + (lp + 1) * LANES]
             for lp in range(group // 2)], axis=0)
        return [_dot_nt(keys_of(k_ref, pair, has_prev, 2 * kv + half), q_tiles)
                for half in range(2)]

    items = [(pair, has_prev, kv) for pair, has_prev in pairs for kv in range(n_kv)]
    st_next = scores(*items[0])
    for n, (pair, has_prev, kv) in enumerate(items):
        st = st_next
        if n + 1 < len(items):
            st_next = scores(*items[n + 1])
        rows = block_rows(pair)
        vt = keys_of(v_ref, pair, has_prev, kv).astype(F32).T.astype(BF16)
        col0 = kv * group * HEAD_DIM
        probs, rden = [], []
        for g in range(group):
            sg = st[g % 2][:, (g // 2) * PAIR:(g // 2 + 1) * PAIR]
            blocks = [sg[c * CHUNK:(c + 1) * CHUNK] for c in range(sg.shape[0] // CHUNK)]
            if has_prev:
                s = jnp.concatenate([jnp.where(first_q, blocks[0], blocks[3]),
                                     blocks[1], blocks[2]], axis=0)
            else:
                s = jnp.concatenate([blocks[0], jnp.where(first_q, NEG_INF, blocks[1])], axis=0)
            sink = sink_ref[kv * group + g]
            m = jnp.maximum(jnp.max(s, axis=0, keepdims=True), sink)
            e = jnp.exp2(s - m)
            den = jnp.sum(e, axis=0, keepdims=True) + jnp.exp2(sink - m)
            if has_prev:
                e_03, e_12 = e[:CHUNK], e[CHUNK:]
                e = jnp.concatenate([jnp.where(first_q, e_03, 0.0), e_12,
                                     jnp.where(first_q, 0.0, e_03)], axis=0)
            probs.append(e.astype(BF16))
            rden.append(1.0 / den)
        ot = _dot(vt, jnp.concatenate(probs, axis=1)) * jnp.concatenate(rden, axis=1)
        for lp in range(group // 2):
            a = ot[:HEAD_DIM, (2 * lp) * PAIR:(2 * lp + 1) * PAIR]
            b = ot[HEAD_DIM:, (2 * lp + 1) * PAIR:(2 * lp + 2) * PAIR]
            o_ref[rows, col0 + lp * LANES:col0 + (lp + 1) * LANES] = (
                jnp.concatenate([a, b], axis=0).T.astype(BF16))


def _attn_kernel(sink_ref, q_ref, k_ref, v_ref, o_ref, *, n_kv, group):
    def run(pairs):
        _attn_pairs(pairs, sink_ref, q_ref, k_ref, v_ref, o_ref, n_kv, group)

    def trip(it, carry):
        run([(1 + PAIRS_PER_TRIP * it + u, True) for u in range(PAIRS_PER_TRIP)])
        return carry

    n_pairs = q_ref.shape[0] // PAIR
    trips = (n_pairs - 1) // PAIRS_PER_TRIP
    run([(0, False)])
    lax.fori_loop(0, trips, trip, 0)
    leftover = range(1 + trips * PAIRS_PER_TRIP, n_pairs)
    if leftover:
        run([(pair, True) for pair in leftover])


def _attention(q, k2, v2, sinks, *, batch, seq, n_kv, group):
    m, d = q.shape
    assert WINDOW == PAIR and seq % PAIR == 0
    est = (2 * (2 * _nbytes((seq, d), BF16) + 3 * _nbytes((seq, n_kv * LANES), BF16))
           + 8 * _nbytes((group * PAIR, KEY_SPAN), F32))
    return pl.pallas_call(
        functools.partial(_attn_kernel, n_kv=n_kv, group=group),
        out_shape=jax.ShapeDtypeStruct((m, d), BF16),
        grid=(batch,),
        in_specs=[pl.BlockSpec(memory_space=pltpu.SMEM),
                  pl.BlockSpec((seq, d), lambda b: (b, 0)),
                  pl.BlockSpec((seq, 2 * n_kv * LANES), lambda b: (b, 0)),
                  pl.BlockSpec((seq, n_kv * LANES), lambda b: (b, 0))],
        out_specs=pl.BlockSpec((seq, d), lambda b: (b, 0)),
        compiler_params=_params(("parallel",), est),
        name="attention",
    )(sinks, q, k2, v2)


def _attn_sample_kernel(sink_ref, q_ref, kn_ref, vn_ref, kc_ref, vc_ref, o_ref, *, n_kv, group):
    t = q_ref.shape[0]
    for kv in range(n_kv):
        pair_sl = slice((kv // 2) * LANES, (kv // 2 + 1) * LANES)
        kc = _dup_halves(kc_ref[:, pair_sl])[kv % 2].astype(BF16)
        vc = _dup_halves(vc_ref[:, pair_sl])[kv % 2].astype(BF16)
        kn = (kn_ref[:, 2 * kv * LANES:(2 * kv + 1) * LANES]
              + kn_ref[:, (2 * kv + 1) * LANES:(2 * kv + 2) * LANES])
        vn = vn_ref[:, kv * LANES:(kv + 1) * LANES]
        col0 = kv * group * HEAD_DIM
        lhs = _stack_heads(q_ref, slice(0, t), col0, group)
        s_c = _dot_nt(lhs, kc)
        s_n = _dot_nt(lhs, kn)
        sink = _sink_column(sink_ref, kv * group, group, t)
        m = jnp.maximum(jnp.maximum(jnp.max(s_c, axis=1, keepdims=True),
                                    jnp.max(s_n, axis=1, keepdims=True)), sink)
        e_c = jnp.exp2(s_c - m)
        e_n = jnp.exp2(s_n - m)
        den = (jnp.sum(e_c, axis=1, keepdims=True) + jnp.sum(e_n, axis=1, keepdims=True)
               + jnp.exp2(sink - m))
        o = (_dot(e_c.astype(BF16), vc) + _dot(e_n.astype(BF16), vn)) / den
        _store_heads(o_ref, slice(0, t), col0, o, group, t)


def _attention_sample(q, k2, v2, cache_k, cache_v, sinks, *, batch, t_new, n_kv, group):
    m, d = q.shape
    win = cache_k.shape[0] // batch
    d_kv = cache_k.shape[1]
    est = 4 * 1024 * 1024
    return pl.pallas_call(
        functools.partial(_attn_sample_kernel, n_kv=n_kv, group=group),
        out_shape=jax.ShapeDtypeStruct((m, d), BF16),
        grid=(batch,),
        in_specs=[pl.BlockSpec(memory_space=pltpu.SMEM),
                  pl.BlockSpec((t_new, d), lambda b: (b, 0)),
                  pl.BlockSpec((t_new, 2 * n_kv * LANES), lambda b: (b, 0)),
                  pl.BlockSpec((t_new, n_kv * LANES), lambda b: (b, 0)),
                  pl.BlockSpec((win, d_kv), lambda b: (b, 0)),
                  pl.BlockSpec((win, d_kv), lambda b: (b, 0))],
        out_specs=pl.BlockSpec((t_new, d), lambda b: (b, 0)),
        compiler_params=_params(("parallel",), est),
        name="attention_sample",
    )(sinks, q, k2, v2, cache_k, cache_v)


def _merge_kernel(xn_ref, a_ref, z_ref, wga_ref, wgc_ref, wap_ref, wcp_ref, o_ref):
    xn = xn_ref[...]
    attn = jax.nn.sigmoid(_dot(xn, wga_ref[...])) * _dot(a_ref[...], wap_ref[...])
    conv = jax.nn.sigmoid(_dot(xn, wgc_ref[...])) * _dot(z_ref[...], wcp_ref[...])
    o_ref[...] = (attn + conv).astype(BF16)


def _merge(xn, attn, z, w_in, w_ap, w_cp, *, tm, tn, col_ga, col_gc):
    m, d = xn.shape
    act = pl.BlockSpec((tm, d), lambda i, j: (i, 0))
    wspec = lambda off: pl.BlockSpec((d, tn), lambda i, j: (0, off // tn + j))
    est = (2 * (3 * _nbytes((tm, d), BF16) + 4 * _nbytes((d, tn), BF16) + _nbytes((tm, tn), BF16))
           + 6 * _nbytes((tm, tn), F32))
    return pl.pallas_call(
        _merge_kernel,
        out_shape=jax.ShapeDtypeStruct((m, d), BF16),
        grid=(m // tm, d // tn),
        in_specs=[act, act, act, wspec(col_ga), wspec(col_gc), wspec(0), wspec(0)],
        out_specs=pl.BlockSpec((tm, tn), lambda i, j: (i, j)),
        compiler_params=_params(("parallel", "arbitrary"), est),
        name="merge",
    )(xn, attn, z, w_in, w_in, w_ap, w_cp)


def _out_kernel(x_ref, m_ref, w_ref, g_ref, x1_ref, xn_ref, *, rows):
    for r in range(x_ref.shape[0] // rows):
        rs = slice(r * rows, (r + 1) * rows)
        x1 = x_ref[rs, :] + _dot(m_ref[rs, :], w_ref[...])
        x1_ref[rs, :] = x1
        xn_ref[rs, :] = _rmsnorm_rows(x1, g_ref[...]).astype(BF16)


def _out_proj(x, merged, w_out, g_ffn, *, tm, rows):
    m, d = x.shape
    est = (2 * (2 * _nbytes((tm, d), F32) + 2 * _nbytes((tm, d), BF16)) + _nbytes((d, d), BF16)
           + 4 * _nbytes((rows, d), F32))
    return pl.pallas_call(
        functools.partial(_out_kernel, rows=rows),
        out_shape=(jax.ShapeDtypeStruct((m, d), F32), jax.ShapeDtypeStruct((m, d), BF16)),
        grid=(m // tm,),
        in_specs=[pl.BlockSpec((tm, d), lambda i: (i, 0)),
                  pl.BlockSpec((tm, d), lambda i: (i, 0)),
                  pl.BlockSpec((d, d), lambda i: (0, 0), pipeline_mode=pl.Buffered(1)),
                  pl.BlockSpec((1, d), lambda i: (0, 0))],
        out_specs=(pl.BlockSpec((tm, d), lambda i: (i, 0)),
                   pl.BlockSpec((tm, d), lambda i: (i, 0))),
        compiler_params=_params(("parallel",), est),
        name="out_proj",
    )(x, merged, w_out, g_ffn)


def _ffn_kernel(x_ref, xn_ref, wg_ref, wu_ref, wd_ref, o_ref):
    f = pl.program_id(1)
    slab = x_ref.shape[1]
    n_slabs = o_ref.shape[1] // slab

    @pl.when(f == 0)
    def _():
        o_ref[...] = jnp.zeros_like(o_ref)

    xn = xn_ref[...]
    h = jax.nn.silu(_dot(xn, wg_ref[...])) * _dot(xn, wu_ref[...])
    o_ref[...] += _dot(h.astype(BF16), wd_ref[...])

    for s in range(n_slabs):
        @pl.when(f == s)
        def _(s=s):
            o_ref[:, s * slab:(s + 1) * slab] += x_ref[...]


def _ffn(x, xn, w_gate, w_up, w_down, *, tm, tf, slab):
    m, d = x.shape
    d_ff = w_gate.shape[1]
    n_slabs = d // slab
    assert n_slabs <= d_ff // tf
    est = (2 * (_nbytes((tm, slab), F32) + _nbytes((tm, d), F32) + _nbytes((tm, d), BF16)
                + 3 * _nbytes((d, tf), BF16))
           + 4 * _nbytes((tm, tf), F32) + _nbytes((tm, d), F32))
    row = lambda i, f: (i, 0)
    return pl.pallas_call(
        _ffn_kernel,
        out_shape=jax.ShapeDtypeStruct((m, d), F32),
        grid=(m // tm, d_ff // tf),
        in_specs=[pl.BlockSpec((tm, slab), lambda i, f: (i, jnp.minimum(f, n_slabs - 1))),
                  pl.BlockSpec((tm, d), row),
                  pl.BlockSpec((d, tf), lambda i, f: (0, f)),
                  pl.BlockSpec((d, tf), lambda i, f: (0, f)),
                  pl.BlockSpec((tf, d), lambda i, f: (f, 0))],
        out_specs=pl.BlockSpec((tm, d), row),
        compiler_params=_params(("parallel", "arbitrary"), est),
        name="ffn",
    )(x, xn, w_gate, w_up, w_down)


def _rope_tables(pos, repeat, g, scale):
    half = HEAD_DIM // 2
    inv = jnp.exp(-math.log(ROPE_THETA) * jnp.arange(half, dtype=F32) * (2.0 / HEAD_DIM))
    ang = pos.astype(F32)[:, None] * inv[None, :]
    cos = jnp.cos(ang)
    sin = jnp.sin(ang)
    g_swap = jnp.concatenate([g[half:], g[:half]])
    t_same = g[None, :] * jnp.concatenate([cos, cos], axis=-1) * scale
    t_swap = g_swap[None, :] * jnp.concatenate([-sin, sin], axis=-1) * scale
    reps = (repeat, LANES // HEAD_DIM)
    return jnp.tile(t_same, reps), jnp.tile(t_swap, reps)


def _layer(x, w, *, batch, seq, pos, tiles, conv_prev, sample_cache):
    d = x.shape[1]
    n_heads = w["sinks"].shape[0]
    d_attn = n_heads * HEAD_DIM
    d_kv = w["d_kv"]
    n_kv = d_kv // HEAD_DIM
    group = n_heads // n_kv
    d_conv = w["conv_w"].shape[1]
    col_b = d_attn + 2 * d_kv
    col_c = col_b + d_conv
    col_h = col_c + d_conv
    col_ga = col_h + d_conv
    col_gc = col_ga + d
    tm = tiles["tm"]

    rope_rows = max(seq, tm)
    log2e = math.log2(math.e)
    tables = (_rope_tables(pos, rope_rows // seq, w["q_norm_g"], HEAD_DIM ** -0.5 * log2e)
              + _rope_tables(pos, rope_rows // seq, w["k_norm_g"], 1.0))
    sinks = w["sinks"] * log2e

    xn, q, k, v, k2, v2 = _qkv_proj(x, w["norm_mix_g"], w["w_in"], tables, tm=tm, d_attn=d_attn,
                                    d_kv=d_kv, rows=tiles["qkv_rows"])
    z, u_tail = _conv_branch(xn, w["w_in"], w["conv_w"], conv_prev,
                             tm=tiles["tm_conv"], tn=tiles["tn_conv"], rows=tiles["conv_rows"],
                             seq_len=seq,
                             tail_rows=tiles["conv_tail"], col_b=col_b, col_c=col_c, col_h=col_h)
    if sample_cache is None:
        attn = _attention(q, k2, v2, sinks, batch=batch, seq=seq, n_kv=n_kv, group=group)
    else:
        attn = _attention_sample(q, k2, v2, sample_cache[0], sample_cache[1], sinks,
                                 batch=batch, t_new=seq, n_kv=n_kv, group=group)
    merged = _merge(xn, attn, z, w["w_in"], w["w_attn_proj"], w["w_conv_proj"],
                    tm=tm, tn=tiles["tn_merge"], col_ga=col_ga, col_gc=col_gc)
    x1, xn_ffn = _out_proj(x, merged, w["w_out"], w["norm_ffn_g"],
                           tm=tiles["tm_out"], rows=tiles["out_rows"])
    y = _ffn(x1, xn_ffn, w["w_ffn_gate"], w["w_ffn_up"], w["w_ffn_down"],
             tm=tiles["tm_ffn"], tf=tiles["tf"], slab=V7X_MXU_DIM)
    return y, k, v, u_tail


def kernel(x_prompt, x_sample, cache_k, cache_v, state_conv, norm_mix_g, w_in, q_norm_g, k_norm_g,
           attn_sinks, conv_w, w_attn_proj, w_conv_proj, w_out, norm_ffn_g, w_ffn_gate, w_ffn_up,
           w_ffn_down):
    depth = w_in.shape[0]
    assert depth == 1
    b_p, s_p, d = x_prompt.shape
    b_s, s_s, _ = x_sample.shape
    n_kv = cache_k.shape[3]
    d_kv = n_kv * HEAD_DIM
    win = cache_k.shape[2]
    n_state = state_conv.shape[2]

    w = {
        "d_kv": d_kv,
        "norm_mix_g": norm_mix_g[0].reshape(1, d),
        "w_in": w_in[0].astype(BF16),
        "q_norm_g": q_norm_g[0],
        "k_norm_g": k_norm_g[0],
        "sinks": attn_sinks[0],
        "conv_w": conv_w[0],
        "w_attn_proj": w_attn_proj[0].astype(BF16),
        "w_conv_proj": w_conv_proj[0].astype(BF16),
        "w_out": w_out[0].astype(BF16),
        "norm_ffn_g": norm_ffn_g[0].reshape(1, d),
        "w_ffn_gate": w_ffn_gate[0].astype(BF16),
        "w_ffn_up": w_ffn_up[0].astype(BF16),
        "w_ffn_down": w_ffn_down[0].astype(BF16),
    }

    m_p = b_p * s_p
    tm_p = min(1024, s_p)
    tiles_p = dict(tm=tm_p, qkv_rows=256, tm_conv=s_p, tn_conv=256, conv_tail=8,
                   conv_rows=min(512, s_p), tn_merge=512, tm_out=min(512, s_p), out_rows=256,
                   tm_ffn=tm_p, tf=512)
    y_p, k_p, v_p, u_p = _layer(x_prompt.reshape(m_p, d), w, batch=b_p, seq=s_p,
                                pos=jnp.arange(s_p), tiles=tiles_p, conv_prev=None,
                                sample_cache=None)
    keep = min(WINDOW, s_p)
    new_k_p = k_p.reshape(b_p, s_p, n_kv, HEAD_DIM)[:, s_p - keep:][None]
    new_v_p = v_p.reshape(b_p, s_p, n_kv, HEAD_DIM)[:, s_p - keep:][None]
    new_c_p = u_p.reshape(b_p, tiles_p["conv_tail"], -1)[:, tiles_p["conv_tail"] - n_state:][None]

    m_s = b_s * s_s
    st = state_conv[0]
    prev1 = jnp.pad(st[:, 1:2], ((0, 0), (0, s_s - 1), (0, 0))).reshape(m_s, -1)
    prev2 = jnp.pad(st, ((0, 0), (0, s_s - n_state), (0, 0))).reshape(m_s, -1)
    tiles_s = dict(tm=m_s, qkv_rows=m_s, tm_conv=m_s, tn_conv=256, conv_tail=m_s,
                   conv_rows=m_s, tn_merge=256, tm_out=m_s, out_rows=m_s, tm_ffn=m_s, tf=512)
    y_s, k_s, v_s, u_s = _layer(x_sample.reshape(m_s, d), w, batch=b_s, seq=s_s,
                                pos=PAST_LEN + jnp.arange(s_s), tiles=tiles_s,
                                conv_prev=(prev1, prev2),
                                sample_cache=(cache_k[0].reshape(b_s * win, d_kv),
                                              cache_v[0].reshape(b_s * win, d_kv)))
    new_k_s = k_s.reshape(b_s, s_s, n_kv, HEAD_DIM)[None]
    new_v_s = v_s.reshape(b_s, s_s, n_kv, HEAD_DIM)[None]
    new_c_s = u_s.reshape(b_s, s_s, -1)[:, s_s - n_state:][None]

    return (y_p.reshape(b_p, s_p, d), y_s.reshape(b_s, s_s, d), new_k_p, new_v_p, new_c_p,
            new_k_s, new_v_s, new_c_s)
```

```python
import functools
import math

import jax
import jax.numpy as jnp
from jax import lax
from jax.experimental import pallas as pl
from jax.experimental.pallas import tpu as pltpu

F32 = jnp.float32
BF16 = jnp.bfloat16

CHUNK = 64
WINDOW = 128
PAST_LEN = 1024
ROPE_THETA = 10000.0
EPS = 1e-6
NEG_INF = -1e30
HEAD_DIM = 64
LANES = 128
BF16_SUBLANES = 16
PAIR = 2 * CHUNK
KEY_SPAN = PAIR + WINDOW
PAIRS_PER_TRIP = 3

V7X_MXU_DIM = 256
V7X_VMEM_BYTES = 64 * 1024 * 1024
VMEM_CAP_BYTES = V7X_VMEM_BYTES - 6 * 1024 * 1024


def _params(semantics, vmem_estimate):
    limit = int(min(max(vmem_estimate, 16 * 1024 * 1024), VMEM_CAP_BYTES))
    return pltpu.CompilerParams(dimension_semantics=semantics, vmem_limit_bytes=limit)


def _nbytes(shape, dtype):
    return math.prod(shape) * jnp.dtype(dtype).itemsize


def _dot(a, b):
    return jnp.dot(a, b, preferred_element_type=F32)


def _dot_nt(a, b):
    return lax.dot_general(a, b, (((1,), (1,)), ((), ())), preferred_element_type=F32)


def _lane_iota(shape):
    return lax.broadcasted_iota(jnp.int32, shape, 1)


def _rmsnorm_rows(x, g):
    y = x * lax.rsqrt(jnp.mean(x * x, axis=-1, keepdims=True) + EPS)
    return y * g


def _headnorm_rope(y, t_same, t_swap, seg):
    width = y.shape[1]
    s = y * y
    s_hi = s.astype(BF16)
    s_lo = (s - s_hi.astype(F32)).astype(BF16)
    ms = _dot(s_hi, seg) + _dot(s_lo, seg)
    half = HEAD_DIM // 2
    ahead = pltpu.roll(y, width - half, axis=1)
    behind = pltpu.roll(y, half, axis=1)
    swapped = jnp.where((_lane_iota(y.shape) & (HEAD_DIM - 1)) < half, ahead, behind)
    reps = width // t_same.shape[1]
    t_same = jnp.concatenate([t_same] * reps, axis=1)
    t_swap = jnp.concatenate([t_swap] * reps, axis=1)
    return (y * t_same + swapped * t_swap) * lax.rsqrt(ms + EPS)


def _dup_halves(x):
    low = _lane_iota(x.shape) < HEAD_DIM
    r = pltpu.roll(x, HEAD_DIM, axis=1)
    return jnp.where(low, x, r), jnp.where(low, r, x)


def _isolate_halves(x):
    low = _lane_iota(x.shape) < HEAD_DIM
    r = pltpu.roll(x, HEAD_DIM, axis=1)
    return (jnp.where(low, x, 0.0), jnp.where(low, 0.0, r),
            jnp.where(low, r, 0.0), jnp.where(low, 0.0, x))


def _qkv_kernel(*refs, n_q_blocks, d_kv, rows):
    x_refs = refs[:n_q_blocks]
    g_ref, w_ref, seg_ref, qs_ref, qw_ref, ks_ref, kw_ref = refs[n_q_blocks:n_q_blocks + 7]
    xn_ref, q_ref, k_ref, v_ref, k2_ref, v2_ref = refs[n_q_blocks + 7:]
    j = pl.program_id(1)
    tm, tn = q_ref.shape
    d = xn_ref.shape[1]
    xw = d // n_q_blocks
    chunks = [slice(r * rows, (r + 1) * rows) for r in range(tm // rows)]
    seg = seg_ref[...]
    slab = seg.shape[0]

    def q_block(normalize):
        for rs in chunks:
            if normalize:
                xs = [ref[rs, :] for ref in x_refs]
                ss = sum(jnp.sum(xc * xc, axis=-1, keepdims=True) for xc in xs)
                r = lax.rsqrt(ss * (1.0 / d) + EPS)
                for c, xc in enumerate(xs):
                    cs = slice(c * xw, (c + 1) * xw)
                    xn_ref[rs, cs] = (xc * r * g_ref[:, cs]).astype(BF16)
            y = _dot(xn_ref[rs, :], w_ref[...])
            for c in range(tn // slab):
                sl = slice(c * slab, (c + 1) * slab)
                q_ref[rs, sl] = _headnorm_rope(y[:, sl], qs_ref[rs, :], qw_ref[rs, :],
                                               seg).astype(BF16)

    pl.when(j == 0)(functools.partial(q_block, True))
    pl.when((j > 0) & (j < n_q_blocks))(functools.partial(q_block, False))

    @pl.when(j == n_q_blocks)
    def _():
        for rs in chunks:
            y = _dot(xn_ref[rs, :], w_ref[...])
            k_all = jnp.concatenate(
                [_headnorm_rope(y[:, c * slab:(c + 1) * slab], ks_ref[rs, :], kw_ref[rs, :], seg)
                 for c in range(d_kv // slab)], axis=1)
            for c in range(d_kv // LANES):
                sl = slice(c * LANES, (c + 1) * LANES)
                k = k_all[:, sl]
                v = y[:, d_kv + c * LANES:d_kv + (c + 1) * LANES]
                k_ref[rs, sl] = k
                v_ref[rs, sl] = v
                va, vb = _dup_halves(v)
                lo = slice(2 * c * LANES, (2 * c + 1) * LANES)
                hi = slice((2 * c + 1) * LANES, (2 * c + 2) * LANES)
                v2_ref[rs, lo] = va.astype(BF16)
                v2_ref[rs, hi] = vb.astype(BF16)
                for n, piece in enumerate(_isolate_halves(k)):
                    k2_ref[rs, (4 * c + n) * LANES:(4 * c + n + 1) * LANES] = piece.astype(BF16)


def _qkv_proj(x, g, w_in, tables, *, tm, d_attn, d_kv, rows):
    m, d = x.shape
    tn = 2 * d_kv
    n_q_blocks = d_attn // tn
    rope_blocks = tables[0].shape[0] // tm
    tab = pl.BlockSpec((tm, LANES), lambda i, j: (i % rope_blocks, 0))
    row_out = lambda width: pl.BlockSpec((tm, width), lambda i, j: (i, 0))
    last_tile = m // tm - 1

    def x_slab(c):
        return pl.BlockSpec((tm, d // n_q_blocks),
                            lambda i, j: (jnp.minimum(jnp.where(j > c, i + 1, i), last_tile), c))

    head_of = jnp.arange(V7X_MXU_DIM) // HEAD_DIM
    seg = jnp.where(head_of[:, None] == head_of[None, :], 1.0 / HEAD_DIM, 0.0).astype(BF16)
    est = (2 * (_nbytes((tm, d), F32) + _nbytes((tm, d), BF16) + _nbytes((d, tn), BF16)
                + 4 * _nbytes((tm, tn), BF16) + 2 * _nbytes((tm, d_kv), F32)
                + 4 * _nbytes((tm, LANES), F32))
           + 6 * _nbytes((rows, tn), F32) + 2 * _nbytes((rows, d), F32))
    return pl.pallas_call(
        functools.partial(_qkv_kernel, n_q_blocks=n_q_blocks, d_kv=d_kv, rows=rows),
        out_shape=(jax.ShapeDtypeStruct((m, d), BF16), jax.ShapeDtypeStruct((m, d_attn), BF16),
                   jax.ShapeDtypeStruct((m, d_kv), F32), jax.ShapeDtypeStruct((m, d_kv), F32),
                   jax.ShapeDtypeStruct((m, 2 * tn), BF16), jax.ShapeDtypeStruct((m, tn), BF16)),
        grid=(m // tm, n_q_blocks + 1),
        in_specs=[x_slab(c) for c in range(n_q_blocks)] + [
                  pl.BlockSpec((1, d), lambda i, j: (0, 0)),
                  pl.BlockSpec((d, tn), lambda i, j: (0, j)),
                  pl.BlockSpec(seg.shape, lambda i, j: (0, 0)),
                  tab, tab, tab, tab],
        out_specs=(row_out(d),
                   pl.BlockSpec((tm, tn), lambda i, j: (i, jnp.minimum(j, n_q_blocks - 1))),
                   row_out(d_kv), row_out(d_kv), row_out(2 * tn), row_out(tn)),
        compiler_params=_params(("parallel", "arbitrary"), est),
        name="qkv_proj",
    )(*([x] * n_q_blocks), g, w_in, seg, *tables)


def _conv_kernel(*refs, seq_len, tail_rows, has_state, rows, n_cast):
    n_in = 7 if has_state else 5
    xn_ref, wb_ref, wc_ref, wh_ref, cw_ref = refs[:5]
    p1_ref, p2_ref = refs[5:n_in] if has_state else (None, None)
    z_ref, u_ref = refs[n_in + n_cast:n_in + n_cast + 2]
    casts = list(zip(refs[n_in:n_in + n_cast], refs[n_in + n_cast + 2:]))
    tm, tn = z_ref.shape
    w = cw_ref[...]
    row = lax.broadcasted_iota(jnp.int32, (rows, tn), 0)
    last1 = last2 = None
    for r in range(tm // rows):
        rs = slice(r * rows, (r + 1) * rows)
        xn = xn_ref[rs, :]
        u = _dot(xn, wc_ref[...]) * _dot(xn, wh_ref[...])
        u1 = pltpu.roll(u, 1, axis=0)
        u2 = pltpu.roll(u, 2, axis=0)
        if r > 0:
            u1 = jnp.where(row == 0, last1, u1)
            u2 = jnp.where(row == 0, last2, jnp.where(row == 1, last1, u2))
        if (r * rows) % seq_len == 0 or seq_len < rows:
            pos = (row + r * rows) & (seq_len - 1)
            u1 = jnp.where(pos == 0, p1_ref[rs, :] if has_state else 0.0, u1)
            u2 = jnp.where(pos < 2, p2_ref[rs, :] if has_state else 0.0, u2)
        y = u2 * w[0:1] + u1 * w[1:2] + u * w[2:3]
        z_ref[rs, :] = (_dot(xn, wb_ref[...]) * y).astype(BF16)
        last1, last2 = u[rows - 1:rows], u[rows - 2:rows - 1]
        for src_ref, dst_ref in casts[r::tm // rows]:
            dst_ref[...] = src_ref[...].astype(BF16)
    u_ref[...] = u[rows - tail_rows:]


def _conv_branch(xn, w_in, conv_w, prev, cast, *, tm, tn, rows, seq_len, tail_rows,
                 col_b, col_c, col_h):
    m, d = xn.shape
    n = conv_w.shape[1]
    has_state = prev is not None
    assert seq_len & (seq_len - 1) == 0 and tm % seq_len == 0
    assert tm % rows == 0 and tail_rows <= rows and (rows % seq_len == 0 or seq_len % rows == 0)
    n_j = n // tn
    steps = (m // tm) * n_j
    wspec = lambda off: pl.BlockSpec((d, tn), lambda i, j: (0, off // tn + j))
    in_specs = [pl.BlockSpec((tm, d), lambda i, j: (i, 0)),
                wspec(col_b), wspec(col_c), wspec(col_h),
                pl.BlockSpec((conv_w.shape[0], tn), lambda i, j: (0, j))]
    args = [xn, w_in, w_in, w_in, conv_w]
    if has_state:
        in_specs += [pl.BlockSpec((tm, tn), lambda i, j: (i, j))] * 2
        args += list(prev)
    slab_specs = [pl.BlockSpec((c.shape[0] // steps, c.shape[1]), lambda i, j: (i * n_j + j, 0))
                  for c in cast]
    assert all(c.shape[0] % (steps * BF16_SUBLANES) == 0 for c in cast)
    est = (2 * (_nbytes((tm, d), BF16) + 3 * _nbytes((d, tn), BF16) + _nbytes((tm, tn), BF16)
                + _nbytes((tail_rows, tn), F32)) + 8 * _nbytes((tm, tn), F32)
           + sum(3 * _nbytes(c.shape, F32) // steps for c in cast))
    return pl.pallas_call(
        functools.partial(_conv_kernel, seq_len=seq_len, tail_rows=tail_rows, has_state=has_state,
                          rows=rows, n_cast=len(cast)),
        out_shape=(jax.ShapeDtypeStruct((m, n), BF16),
                   jax.ShapeDtypeStruct((m // tm * tail_rows, n), F32),
                   *[jax.ShapeDtypeStruct(c.shape, BF16) for c in cast]),
        grid=(m // tm, n_j),
        in_specs=in_specs + slab_specs,
        out_specs=(pl.BlockSpec((tm, tn), lambda i, j: (i, j)),
                   pl.BlockSpec((tail_rows, tn), lambda i, j: (i, j)),
                   *slab_specs),
        compiler_params=_params(("parallel", "arbitrary"), est),
        name="conv_branch",
    )(*args, *cast)


def _stack_heads(q_ref, rows, col0, group):
    parts = []
    for g in range(group):
        blk = q_ref[rows, col0 + (g // 2) * LANES:col0 + (g // 2 + 1) * LANES]
        keep = (_lane_iota(blk.shape) >= HEAD_DIM) == bool(g % 2)
        parts.append(jnp.where(keep, blk, jnp.zeros_like(blk)))
    return jnp.concatenate(parts, axis=0)


def _sink_column(sink_ref, head0, group, rows):
    return jnp.concatenate(
        [jnp.full((rows, 1), sink_ref[head0 + g], F32) for g in range(group)], axis=0)


def _store_heads(o_ref, rows, col0, o, group, nrows):
    for lp in range(group // 2):
        a = o[(2 * lp) * nrows:(2 * lp + 1) * nrows]
        b = o[(2 * lp + 1) * nrows:(2 * lp + 2) * nrows]
        low = _lane_iota(a.shape) < HEAD_DIM
        o_ref[rows, col0 + lp * LANES:col0 + (lp + 1) * LANES] = jnp.where(low, a, b).astype(BF16)


def _attn_pairs(pairs, sink_ref, q_ref, k_ref, v_ref, o_ref, n_kv, group):
    def block_rows(index):
        start = index * PAIR
        return pl.ds(start if isinstance(start, int) else pl.multiple_of(start, PAIR), PAIR)

    first_q = _lane_iota((CHUNK, PAIR)) < CHUNK

    def keys_of(ref, pair, has_prev, tile):
        cols = slice(tile * LANES, (tile + 1) * LANES)
        if has_prev:
            return jnp.concatenate([ref[block_rows(pair - 1), cols], ref[block_rows(pair), cols]],
                                   axis=0)
        return ref[block_rows(pair), cols]

    def scores(pair, has_prev, kv):
        col0 = kv * group * HEAD_DIM
        q_tiles = jnp.concatenate(
            [q_ref[block_rows(pair), col0 + lp * LANES:col0 + (lp + 1) * LANES]
             for lp in range(group // 2)], axis=0)
        return [_dot_nt(keys_of(k_ref, pair, has_prev, 2 * kv + half), q_tiles)
                for half in range(2)]

    items = [(pair, has_prev, kv) for pair, has_prev in pairs for kv in range(n_kv)]
    st_next = scores(*items[0])
    for n, (pair, has_prev, kv) in enumerate(items):
        st = st_next
        if n + 1 < len(items):
            st_next = scores(*items[n + 1])
        rows = block_rows(pair)
        vt = keys_of(v_ref, pair, has_prev, kv).astype(F32).T.astype(BF16)
        col0 = kv * group * HEAD_DIM
        probs, rden = [], []
        for g in range(group):
            sg = st[g % 2][:, (g // 2) * PAIR:(g // 2 + 1) * PAIR]
            blocks = [sg[c * CHUNK:(c + 1) * CHUNK] for c in range(sg.shape[0] // CHUNK)]
            if has_prev:
                s = jnp.concatenate([jnp.where(first_q, blocks[0], blocks[3]),
                                     blocks[1], blocks[2]], axis=0)
            else:
                s = jnp.concatenate([blocks[0], jnp.where(first_q, NEG_INF, blocks[1])], axis=0)
            sink = sink_ref[kv * group + g]
            m = jnp.maximum(jnp.max(s, axis=0, keepdims=True), sink)
            e = jnp.exp2(s - m)
            den = jnp.sum(e, axis=0, keepdims=True) + jnp.exp2(sink - m)
            if has_prev:
                e_03, e_12 = e[:CHUNK], e[CHUNK:]
                e = jnp.concatenate([jnp.where(first_q, e_03, 0.0), e_12,
                                     jnp.where(first_q, 0.0, e_03)], axis=0)
            probs.append(e.astype(BF16))
            rden.append(1.0 / den)
        ot = _dot(vt, jnp.concatenate(probs, axis=1)) * jnp.concatenate(rden, axis=1)
        for lp in range(group // 2):
            a = ot[:HEAD_DIM, (2 * lp) * PAIR:(2 * lp + 1) * PAIR]
            b = ot[HEAD_DIM:, (2 * lp + 1) * PAIR:(2 * lp + 2) * PAIR]
            o_ref[rows, col0 + lp * LANES:col0 + (lp + 1) * LANES] = (
                jnp.concatenate([a, b], axis=0).T.astype(BF16))


def _attn_kernel(sink_ref, q_ref, k_ref, v_ref, o_ref, *, n_kv, group):
    def run(pairs):
        _attn_pairs(pairs, sink_ref, q_ref, k_ref, v_ref, o_ref, n_kv, group)

    def trip(it, carry):
        run([(1 + PAIRS_PER_TRIP * it + u, True) for u in range(PAIRS_PER_TRIP)])
        return carry

    n_pairs = q_ref.shape[0] // PAIR
    trips = (n_pairs - 1) // PAIRS_PER_TRIP
    run([(0, False)])
    lax.fori_loop(0, trips, trip, 0)
    leftover = range(1 + trips * PAIRS_PER_TRIP, n_pairs)
    if leftover:
        run([(pair, True) for pair in leftover])


def _attention(q, k2, v2, sinks, *, batch, seq, n_kv, group):
    m, d = q.shape
    assert WINDOW == PAIR and seq % PAIR == 0
    est = (2 * (2 * _nbytes((seq, d), BF16) + 3 * _nbytes((seq, n_kv * LANES), BF16))
           + 8 * _nbytes((group * PAIR, KEY_SPAN), F32))
    return pl.pallas_call(
        functools.partial(_attn_kernel, n_kv=n_kv, group=group),
        out_shape=jax.ShapeDtypeStruct((m, d), BF16),
        grid=(batch,),
        in_specs=[pl.BlockSpec(memory_space=pltpu.SMEM),
                  pl.BlockSpec((seq, d), lambda b: (b, 0)),
                  pl.BlockSpec((seq, 2 * n_kv * LANES), lambda b: (b, 0)),
                  pl.BlockSpec((seq, n_kv * LANES), lambda b: (b, 0))],
        out_specs=pl.BlockSpec((seq, d), lambda b: (b, 0)),
        compiler_params=_params(("parallel",), est),
        name="attention",
    )(sinks, q, k2, v2)


def _attn_sample_kernel(sink_ref, q_ref, kn_ref, vn_ref, kc_ref, vc_ref, o_ref, *, n_kv, group):
    t = q_ref.shape[0]
    for kv in range(n_kv):
        pair_sl = slice((kv // 2) * LANES, (kv // 2 + 1) * LANES)
        kc = _dup_halves(kc_ref[:, pair_sl])[kv % 2].astype(BF16)
        vc = _dup_halves(vc_ref[:, pair_sl])[kv % 2].astype(BF16)
        kn = (kn_ref[:, 2 * kv * LANES:(2 * kv + 1) * LANES]
              + kn_ref[:, (2 * kv + 1) * LANES:(2 * kv + 2) * LANES])
        vn = vn_ref[:, kv * LANES:(kv + 1) * LANES]
        col0 = kv * group * HEAD_DIM
        lhs = _stack_heads(q_ref, slice(0, t), col0, group)
        s_c = _dot_nt(lhs, kc)
        s_n = _dot_nt(lhs, kn)
        sink = _sink_column(sink_ref, kv * group, group, t)
        m = jnp.maximum(jnp.maximum(jnp.max(s_c, axis=1, keepdims=True),
                                    jnp.max(s_n, axis=1, keepdims=True)), sink)
        e_c = jnp.exp2(s_c - m)
        e_n = jnp.exp2(s_n - m)
        den = (jnp.sum(e_c, axis=1, keepdims=True) + jnp.sum(e_n, axis=1, keepdims=True)
               + jnp.exp2(sink - m))
        o = (_dot(e_c.astype(BF16), vc) + _dot(e_n.astype(BF16), vn)) / den
        _store_heads(o_ref, slice(0, t), col0, o, group, t)


def _attention_sample(q, k2, v2, cache_k, cache_v, sinks, *, batch, t_new, n_kv, group):
    m, d = q.shape
    win = cache_k.shape[0] // batch
    d_kv = cache_k.shape[1]
    est = 4 * 1024 * 1024
    return pl.pallas_call(
        functools.partial(_attn_sample_kernel, n_kv=n_kv, group=group),
        out_shape=jax.ShapeDtypeStruct((m, d), BF16),
        grid=(batch,),
        in_specs=[pl.BlockSpec(memory_space=pltpu.SMEM),
                  pl.BlockSpec((t_new, d), lambda b: (b, 0)),
                  pl.BlockSpec((t_new, 2 * n_kv * LANES), lambda b: (b, 0)),
                  pl.BlockSpec((t_new, n_kv * LANES), lambda b: (b, 0)),
                  pl.BlockSpec((win, d_kv), lambda b: (b, 0)),
                  pl.BlockSpec((win, d_kv), lambda b: (b, 0))],
        out_specs=pl.BlockSpec((t_new, d), lambda b: (b, 0)),
        compiler_params=_params(("parallel",), est),
        name="attention_sample",
    )(sinks, q, k2, v2, cache_k, cache_v)


def _merge_kernel(xn_ref, a_ref, z_ref, wga_ref, wgc_ref, wap_ref, wcp_ref, o_ref):
    xn = xn_ref[...]
    attn = jax.nn.sigmoid(_dot(xn, wga_ref[...])) * _dot(a_ref[...], wap_ref[...])
    conv = jax.nn.sigmoid(_dot(xn, wgc_ref[...])) * _dot(z_ref[...], wcp_ref[...])
    o_ref[...] = (attn + conv).astype(BF16)


def _merge(xn, attn, z, w_in, w_ap, w_cp, *, tm, tn, col_ga, col_gc):
    m, d = xn.shape
    act = pl.BlockSpec((tm, d), lambda i, j: (i, 0))
    wspec = lambda off: pl.BlockSpec((d, tn), lambda i, j: (0, off // tn + j))
    est = (2 * (3 * _nbytes((tm, d), BF16) + 4 * _nbytes((d, tn), BF16) + _nbytes((tm, tn), BF16))
           + 6 * _nbytes((tm, tn), F32))
    return pl.pallas_call(
        _merge_kernel,
        out_shape=jax.ShapeDtypeStruct((m, d), BF16),
        grid=(m // tm, d // tn),
        in_specs=[act, act, act, wspec(col_ga), wspec(col_gc), wspec(0), wspec(0)],
        out_specs=pl.BlockSpec((tm, tn), lambda i, j: (i, j)),
        compiler_params=_params(("parallel", "arbitrary"), est),
        name="merge",
    )(xn, attn, z, w_in, w_in, w_ap, w_cp)


def _out_kernel(x_ref, m_ref, w_ref, g_ref, x1_ref, xn_ref, *, rows):
    for r in range(x_ref.shape[0] // rows):
        rs = slice(r * rows, (r + 1) * rows)
        x1 = x_ref[rs, :] + _dot(m_ref[rs, :], w_ref[...])
        x1_ref[rs, :] = x1
        xn_ref[rs, :] = _rmsnorm_rows(x1, g_ref[...]).astype(BF16)


def _out_proj(x, merged, w_out, g_ffn, *, tm, rows):
    m, d = x.shape
    est = (2 * (2 * _nbytes((tm, d), F32) + 2 * _nbytes((tm, d), BF16)) + _nbytes((d, d), BF16)
           + 4 * _nbytes((rows, d), F32))
    return pl.pallas_call(
        functools.partial(_out_kernel, rows=rows),
        out_shape=(jax.ShapeDtypeStruct((m, d), F32), jax.ShapeDtypeStruct((m, d), BF16)),
        grid=(m // tm,),
        in_specs=[pl.BlockSpec((tm, d), lambda i: (i, 0)),
                  pl.BlockSpec((tm, d), lambda i: (i, 0)),
                  pl.BlockSpec((d, d), lambda i: (0, 0), pipeline_mode=pl.Buffered(1)),
                  pl.BlockSpec((1, d), lambda i: (0, 0))],
        out_specs=(pl.BlockSpec((tm, d), lambda i: (i, 0)),
                   pl.BlockSpec((tm, d), lambda i: (i, 0))),
        compiler_params=_params(("parallel",), est),
        name="out_proj",
    )(x, merged, w_out, g_ffn)


def _ffn_kernel(x_ref, xn_ref, wg_ref, wu_ref, wd_ref, o_ref):
    f = pl.program_id(1)
    slab = x_ref.shape[1]
    n_slabs = o_ref.shape[1] // slab

    @pl.when(f == 0)
    def _():
        o_ref[...] = jnp.zeros_like(o_ref)

    xn = xn_ref[...]
    h = jax.nn.silu(_dot(xn, wg_ref[...])) * _dot(xn, wu_ref[...])
    o_ref[...] += _dot(h.astype(BF16), wd_ref[...])

    for s in range(n_slabs):
        @pl.when(f == s)
        def _(s=s):
            o_ref[:, s * slab:(s + 1) * slab] += x_ref[...]


def _ffn(x, xn, w_gate, w_up, w_down, *, tm, tf, slab):
    m, d = x.shape
    d_ff = w_gate.shape[1]
    n_slabs = d // slab
    assert n_slabs <= d_ff // tf
    est = (2 * (_nbytes((tm, slab), F32) + _nbytes((tm, d), F32) + _nbytes((tm, d), BF16)
                + 3 * _nbytes((d, tf), BF16))
           + 4 * _nbytes((tm, tf), F32) + _nbytes((tm, d), F32))
    row = lambda i, f: (i, 0)
    return pl.pallas_call(
        _ffn_kernel,
        out_shape=jax.ShapeDtypeStruct((m, d), F32),
        grid=(m // tm, d_ff // tf),
        in_specs=[pl.BlockSpec((tm, slab), lambda i, f: (i, jnp.minimum(f, n_slabs - 1))),
                  pl.BlockSpec((tm, d), row),
                  pl.BlockSpec((d, tf), lambda i, f: (0, f)),
                  pl.BlockSpec((d, tf), lambda i, f: (0, f)),
                  pl.BlockSpec((tf, d), lambda i, f: (f, 0))],
        out_specs=pl.BlockSpec((tm, d), row),
        compiler_params=_params(("parallel", "arbitrary"), est),
        name="ffn",
    )(x, xn, w_gate, w_up, w_down)


def _rope_tables(pos, repeat, g, scale):
    half = HEAD_DIM // 2
    inv = jnp.exp(-math.log(ROPE_THETA) * jnp.arange(half, dtype=F32) * (2.0 / HEAD_DIM))
    ang = pos.astype(F32)[:, None] * inv[None, :]
    cos = jnp.cos(ang)
    sin = jnp.sin(ang)
    g_swap = jnp.concatenate([g[half:], g[:half]])
    t_same = g[None, :] * jnp.concatenate([cos, cos], axis=-1) * scale
    t_swap = g_swap[None, :] * jnp.concatenate([-sin, sin], axis=-1) * scale
    reps = (repeat, LANES // HEAD_DIM)
    return jnp.tile(t_same, reps), jnp.tile(t_swap, reps)


LATE_WEIGHTS = ("w_attn_proj", "w_conv_proj", "w_out", "w_ffn_gate", "w_ffn_up", "w_ffn_down")


def _layer(x, w, *, batch, seq, pos, tiles, conv_prev, sample_cache):
    d = x.shape[1]
    n_heads = w["sinks"].shape[0]
    d_attn = n_heads * HEAD_DIM
    d_kv = w["d_kv"]
    n_kv = d_kv // HEAD_DIM
    group = n_heads // n_kv
    d_conv = w["conv_w"].shape[1]
    col_b = d_attn + 2 * d_kv
    col_c = col_b + d_conv
    col_h = col_c + d_conv
    col_ga = col_h + d_conv
    col_gc = col_ga + d
    tm = tiles["tm"]

    rope_rows = max(seq, tm)
    log2e = math.log2(math.e)
    tables = (_rope_tables(pos, rope_rows // seq, w["q_norm_g"], HEAD_DIM ** -0.5 * log2e)
              + _rope_tables(pos, rope_rows // seq, w["k_norm_g"], 1.0))
    sinks = w["sinks"] * log2e

    xn, q, k, v, k2, v2 = _qkv_proj(x, w["norm_mix_g"], w["w_in"], tables, tm=tm, d_attn=d_attn,
                                    d_kv=d_kv, rows=tiles["qkv_rows"])
    late = [name for name in LATE_WEIGHTS if w[name].dtype != BF16]
    z, u_tail, *late_bf16 = _conv_branch(
        xn, w["w_in"], w["conv_w"], conv_prev,
        [w[name].reshape(d, -1) for name in late],
        tm=tiles["tm_conv"], tn=tiles["tn_conv"], rows=tiles["conv_rows"], seq_len=seq,
        tail_rows=tiles["conv_tail"], col_b=col_b, col_c=col_c, col_h=col_h)
    w = dict(w, **{name: c.reshape(w[name].shape) for name, c in zip(late, late_bf16)})
    if sample_cache is None:
        attn = _attention(q, k2, v2, sinks, batch=batch, seq=seq, n_kv=n_kv, group=group)
    else:
        attn = _attention_sample(q, k2, v2, sample_cache[0], sample_cache[1], sinks,
                                 batch=batch, t_new=seq, n_kv=n_kv, group=group)
    merged = _merge(xn, attn, z, w["w_in"], w["w_attn_proj"], w["w_conv_proj"],
                    tm=tm, tn=tiles["tn_merge"], col_ga=col_ga, col_gc=col_gc)
    x1, xn_ffn = _out_proj(x, merged, w["w_out"], w["norm_ffn_g"],
                           tm=tiles["tm_out"], rows=tiles["out_rows"])
    y = _ffn(x1, xn_ffn, w["w_ffn_gate"], w["w_ffn_up"], w["w_ffn_down"],
             tm=tiles["tm_ffn"], tf=tiles["tf"], slab=V7X_MXU_DIM)
    return y, k, v, u_tail, w


def kernel(x_prompt, x_sample, cache_k, cache_v, state_conv, norm_mix_g, w_in, q_norm_g, k_norm_g,
           attn_sinks, conv_w, w_attn_proj, w_conv_proj, w_out, norm_ffn_g, w_ffn_gate, w_ffn_up,
           w_ffn_down):
    depth = w_in.shape[0]
    assert depth == 1
    b_p, s_p, d = x_prompt.shape
    b_s, s_s, _ = x_sample.shape
    n_kv = cache_k.shape[3]
    d_kv = n_kv * HEAD_DIM
    win = cache_k.shape[2]
    n_state = state_conv.shape[2]

    w = {
        "d_kv": d_kv,
        "norm_mix_g": norm_mix_g[0].reshape(1, d),
        "w_in": w_in[0].astype(BF16),
        "q_norm_g": q_norm_g[0],
        "k_norm_g": k_norm_g[0],
        "sinks": attn_sinks[0],
        "conv_w": conv_w[0],
        "w_attn_proj": w_attn_proj[0],
        "w_conv_proj": w_conv_proj[0],
        "w_out": w_out[0],
        "norm_ffn_g": norm_ffn_g[0].reshape(1, d),
        "w_ffn_gate": w_ffn_gate[0],
        "w_ffn_up": w_ffn_up[0],
        "w_ffn_down": w_ffn_down[0],
    }

    m_p = b_p * s_p
    tm_p = min(1024, s_p)
    tiles_p = dict(tm=tm_p, qkv_rows=256, tm_conv=s_p, tn_conv=256, conv_tail=8,
                   conv_rows=min(512, s_p), tn_merge=512, tm_out=min(512, s_p), out_rows=256,
                   tm_ffn=tm_p, tf=512)
    y_p, k_p, v_p, u_p, w = _layer(x_prompt.reshape(m_p, d), w, batch=b_p, seq=s_p,
                                   pos=jnp.arange(s_p), tiles=tiles_p, conv_prev=None,
                                   sample_cache=None)
    keep = min(WINDOW, s_p)
    new_k_p = k_p.reshape(b_p, s_p, n_kv, HEAD_DIM)[:, s_p - keep:][None]
    new_v_p = v_p.reshape(b_p, s_p, n_kv, HEAD_DIM)[:, s_p - keep:][None]
    new_c_p = u_p.reshape(b_p, tiles_p["conv_tail"], -1)[:, tiles_p["conv_tail"] - n_state:][None]

    m_s = b_s * s_s
    st = state_conv[0]
    prev1 = jnp.pad(st[:, 1:2], ((0, 0), (0, s_s - 1), (0, 0))).reshape(m_s, -1)
    prev2 = jnp.pad(st, ((0, 0), (0, s_s - n_state), (0, 0))).reshape(m_s, -1)
    tiles_s = dict(tm=m_s, qkv_rows=m_s, tm_conv=m_s, tn_conv=256, conv_tail=m_s,
                   conv_rows=m_s, tn_merge=256, tm_out=m_s, out_rows=m_s, tm_ffn=m_s, tf=512)
    y_s, k_s, v_s, u_s, _ = _layer(x_sample.reshape(m_s, d), w, batch=b_s, seq=s_s,
                                   pos=PAST_LEN + jnp.arange(s_s), tiles=tiles_s,
                                   conv_prev=(prev1, prev2),
                                   sample_cache=(cache_k[0].reshape(b_s * win, d_kv),
                                                 cache_v[0].reshape(b_s * win, d_kv)))
    new_k_s = k_s.reshape(b_s, s_s, n_kv, HEAD_DIM)[None]
    new_v_s = v_s.reshape(b_s, s_s, n_kv, HEAD_DIM)[None]
    new_c_s = u_s.reshape(b_s, s_s, -1)[:, s_s - n_state:][None]

    return (y_p.reshape(b_p, s_p, d), y_s.reshape(b_s, s_s, d), new_k_p, new_v_p, new_c_p,
            new_k_s, new_v_s, new_c_s)
```

```python
import functools
import math

import jax
import jax.numpy as jnp
from jax import lax
from jax.experimental import pallas as pl
from jax.experimental.pallas import tpu as pltpu

F32 = jnp.float32
BF16 = jnp.bfloat16

CHUNK = 64
WINDOW = 128
PAST_LEN = 1024
ROPE_THETA = 10000.0
EPS = 1e-6
NEG_INF = -1e30
HEAD_DIM = 64
LANES = 128
BF16_SUBLANES = 16
PAIR = 2 * CHUNK
KEY_SPAN = PAIR + WINDOW
PAIRS_PER_TRIP = 3

V7X_MXU_DIM = 256
V7X_VMEM_BYTES = 64 * 1024 * 1024
VMEM_CAP_BYTES = V7X_VMEM_BYTES - 6 * 1024 * 1024


def _params(semantics, vmem_estimate):
    limit = int(min(max(vmem_estimate, 16 * 1024 * 1024), VMEM_CAP_BYTES))
    return pltpu.CompilerParams(dimension_semantics=semantics, vmem_limit_bytes=limit)


def _nbytes(shape, dtype):
    return math.prod(shape) * jnp.dtype(dtype).itemsize


def _dot(a, b):
    return jnp.dot(a, b, preferred_element_type=F32)


def _dot_nt(a, b):
    return lax.dot_general(a, b, (((1,), (1,)), ((), ())), preferred_element_type=F32)


def _lane_iota(shape):
    return lax.broadcasted_iota(jnp.int32, shape, 1)


def _rmsnorm_rows(x, g):
    y = x * lax.rsqrt(jnp.mean(x * x, axis=-1, keepdims=True) + EPS)
    return y * g


def _headnorm_rope(y, t_same, t_swap, seg):
    width = y.shape[1]
    s = y * y
    s_hi = s.astype(BF16)
    s_lo = (s - s_hi.astype(F32)).astype(BF16)
    ms = _dot(s_hi, seg) + _dot(s_lo, seg)
    half = HEAD_DIM // 2
    ahead = pltpu.roll(y, width - half, axis=1)
    behind = pltpu.roll(y, half, axis=1)
    swapped = jnp.where((_lane_iota(y.shape) & (HEAD_DIM - 1)) < half, ahead, behind)
    reps = width // t_same.shape[1]
    t_same = jnp.concatenate([t_same] * reps, axis=1)
    t_swap = jnp.concatenate([t_swap] * reps, axis=1)
    return (y * t_same + swapped * t_swap) * lax.rsqrt(ms + EPS)


def _dup_halves(x):
    low = _lane_iota(x.shape) < HEAD_DIM
    r = pltpu.roll(x, HEAD_DIM, axis=1)
    return jnp.where(low, x, r), jnp.where(low, r, x)


def _isolate_halves(x):
    low = _lane_iota(x.shape) < HEAD_DIM
    r = pltpu.roll(x, HEAD_DIM, axis=1)
    return (jnp.where(low, x, 0.0), jnp.where(low, 0.0, r),
            jnp.where(low, r, 0.0), jnp.where(low, 0.0, x))


def _qkv_kernel(*refs, n_q_blocks, d_kv, rows):
    x_refs = refs[:n_q_blocks]
    g_ref, w_ref, seg_ref, qs_ref, qw_ref, ks_ref, kw_ref = refs[n_q_blocks:n_q_blocks + 7]
    xn_ref, q_ref, k_ref, v_ref, k2_ref, v2_ref = refs[n_q_blocks + 7:]
    j = pl.program_id(1)
    tm, tn = q_ref.shape
    d = xn_ref.shape[1]
    xw = d // n_q_blocks
    chunks = [slice(r * rows, (r + 1) * rows) for r in range(tm // rows)]
    seg = seg_ref[...]
    slab = seg.shape[0]

    def q_block(normalize):
        for rs in chunks:
            if normalize:
                xs = [ref[rs, :] for ref in x_refs]
                ss = sum(jnp.sum(xc * xc, axis=-1, keepdims=True) for xc in xs)
                r = lax.rsqrt(ss * (1.0 / d) + EPS)
                for c, xc in enumerate(xs):
                    cs = slice(c * xw, (c + 1) * xw)
                    xn_ref[rs, cs] = (xc * r * g_ref[:, cs]).astype(BF16)
            y = _dot(xn_ref[rs, :], w_ref[...])
            for c in range(tn // slab):
                sl = slice(c * slab, (c + 1) * slab)
                q_ref[rs, sl] = _headnorm_rope(y[:, sl], qs_ref[rs, :], qw_ref[rs, :],
                                               seg).astype(BF16)

    pl.when(j == 0)(functools.partial(q_block, True))
    pl.when((j > 0) & (j < n_q_blocks))(functools.partial(q_block, False))

    @pl.when(j == n_q_blocks)
    def _():
        for rs in chunks:
            y = _dot(xn_ref[rs, :], w_ref[...])
            k_all = jnp.concatenate(
                [_headnorm_rope(y[:, c * slab:(c + 1) * slab], ks_ref[rs, :], kw_ref[rs, :], seg)
                 for c in range(d_kv // slab)], axis=1)
            for c in range(d_kv // LANES):
                sl = slice(c * LANES, (c + 1) * LANES)
                k = k_all[:, sl]
                v = y[:, d_kv + c * LANES:d_kv + (c + 1) * LANES]
                k_ref[rs, sl] = k
                v_ref[rs, sl] = v
                va, vb = _dup_halves(v)
                lo = slice(2 * c * LANES, (2 * c + 1) * LANES)
                hi = slice((2 * c + 1) * LANES, (2 * c + 2) * LANES)
                v2_ref[rs, lo] = va.astype(BF16)
                v2_ref[rs, hi] = vb.astype(BF16)
                for n, piece in enumerate(_isolate_halves(k)):
                    k2_ref[rs, (4 * c + n) * LANES:(4 * c + n + 1) * LANES] = piece.astype(BF16)


def _qkv_proj(x, g, w_in, tables, *, tm, d_attn, d_kv, rows):
    m, d = x.shape
    tn = 2 * d_kv
    n_q_blocks = d_attn // tn
    assert m % tm == 0 and tm % rows == 0 and d_attn % tn == 0 and d % n_q_blocks == 0
    rope_blocks = tables[0].shape[0] // tm
    tab = pl.BlockSpec((tm, LANES), lambda i, j: (i % rope_blocks, 0))
    row_out = lambda width: pl.BlockSpec((tm, width), lambda i, j: (i, 0))
    last_tile = m // tm - 1

    def x_slab(c):
        return pl.BlockSpec((tm, d // n_q_blocks),
                            lambda i, j: (jnp.minimum(jnp.where(j > c, i + 1, i), last_tile), c))

    head_of = jnp.arange(V7X_MXU_DIM) // HEAD_DIM
    seg = jnp.where(head_of[:, None] == head_of[None, :], 1.0 / HEAD_DIM, 0.0).astype(BF16)
    est = (2 * (_nbytes((tm, d), F32) + _nbytes((tm, d), BF16) + _nbytes((d, tn), BF16)
                + 4 * _nbytes((tm, tn), BF16) + 2 * _nbytes((tm, d_kv), F32)
                + 4 * _nbytes((tm, LANES), F32))
           + 6 * _nbytes((rows, tn), F32) + 2 * _nbytes((rows, d), F32))
    return pl.pallas_call(
        functools.partial(_qkv_kernel, n_q_blocks=n_q_blocks, d_kv=d_kv, rows=rows),
        out_shape=(jax.ShapeDtypeStruct((m, d), BF16), jax.ShapeDtypeStruct((m, d_attn), BF16),
                   jax.ShapeDtypeStruct((m, d_kv), F32), jax.ShapeDtypeStruct((m, d_kv), F32),
                   jax.ShapeDtypeStruct((m, 2 * tn), BF16), jax.ShapeDtypeStruct((m, tn), BF16)),
        grid=(m // tm, n_q_blocks + 1),
        in_specs=[x_slab(c) for c in range(n_q_blocks)] + [
                  pl.BlockSpec((1, d), lambda i, j: (0, 0)),
                  pl.BlockSpec((d, tn), lambda i, j: (0, j)),
                  pl.BlockSpec(seg.shape, lambda i, j: (0, 0)),
                  tab, tab, tab, tab],
        out_specs=(row_out(d),
                   pl.BlockSpec((tm, tn), lambda i, j: (i, jnp.minimum(j, n_q_blocks - 1))),
                   row_out(d_kv), row_out(d_kv), row_out(2 * tn), row_out(tn)),
        compiler_params=_params(("parallel", "arbitrary"), est),
        name="qkv_proj",
    )(*([x] * n_q_blocks), g, w_in, seg, *tables)


def _conv_kernel(*refs, seq_len, tail_rows, has_state, rows, cast_per_step):
    n_cast = len(cast_per_step)
    n_in = 7 if has_state else 5
    xn_ref, wb_ref, wc_ref, wh_ref, cw_ref = refs[:5]
    p1_ref, p2_ref = refs[5:n_in] if has_state else (None, None)
    z_ref, u_ref = refs[n_in + n_cast:n_in + n_cast + 2]
    casts = list(zip(refs[n_in:n_in + n_cast], refs[n_in + n_cast + 2:]))
    tm, tn = z_ref.shape
    w = cw_ref[...]
    row = lax.broadcasted_iota(jnp.int32, (rows, tn), 0)
    last1 = last2 = None
    for r in range(tm // rows):
        rs = slice(r * rows, (r + 1) * rows)
        xn = xn_ref[rs, :]
        u = _dot(xn, wc_ref[...]) * _dot(xn, wh_ref[...])
        u1 = pltpu.roll(u, 1, axis=0)
        u2 = pltpu.roll(u, 2, axis=0)
        if r > 0:
            u1 = jnp.where(row == 0, last1, u1)
            u2 = jnp.where(row == 0, last2, jnp.where(row == 1, last1, u2))
        if (r * rows) % seq_len == 0 or seq_len < rows:
            pos = (row + r * rows) & (seq_len - 1)
            u1 = jnp.where(pos == 0, p1_ref[rs, :] if has_state else 0.0, u1)
            u2 = jnp.where(pos < 2, p2_ref[rs, :] if has_state else 0.0, u2)
        y = u2 * w[0:1] + u1 * w[1:2] + u * w[2:3]
        z_ref[rs, :] = (_dot(xn, wb_ref[...]) * y).astype(BF16)
        last1, last2 = u[rows - 1:rows], u[rows - 2:rows - 1]
        for (src_ref, dst_ref), per_step in list(zip(casts, cast_per_step))[r::tm // rows]:
            if per_step:
                dst_ref[...] = src_ref[...].astype(BF16)
            else:
                @pl.when(pl.program_id(1) == 0)
                def _(src_ref=src_ref, dst_ref=dst_ref):
                    dst_ref[...] = src_ref[...].astype(BF16)
    u_ref[...] = u[rows - tail_rows:]


def _conv_branch(xn, w_in, conv_w, prev, cast, *, tm, tn, rows, seq_len, tail_rows,
                 col_b, col_c, col_h):
    m, d = xn.shape
    n = conv_w.shape[1]
    has_state = prev is not None
    assert seq_len & (seq_len - 1) == 0 and tm % seq_len == 0
    assert tm % rows == 0 and tail_rows <= rows and (rows % seq_len == 0 or seq_len % rows == 0)
    n_j = n // tn
    steps = (m // tm) * n_j
    wspec = lambda off: pl.BlockSpec((d, tn), lambda i, j: (0, off // tn + j))
    in_specs = [pl.BlockSpec((tm, d), lambda i, j: (i, 0)),
                wspec(col_b), wspec(col_c), wspec(col_h),
                pl.BlockSpec((conv_w.shape[0], tn), lambda i, j: (0, j))]
    args = [xn, w_in, w_in, w_in, conv_w]
    if has_state:
        in_specs += [pl.BlockSpec((tm, tn), lambda i, j: (i, j))] * 2
        args += list(prev)
    cast_per_step = tuple(c.shape[0] % (steps * BF16_SUBLANES) == 0 for c in cast)
    slab_specs, slab_bytes = [], 0
    for c, per_step in zip(cast, cast_per_step):
        parts = steps if per_step else m // tm
        assert c.shape[0] % (parts * BF16_SUBLANES) == 0
        index_map = (lambda i, j: (i * n_j + j, 0)) if per_step else (lambda i, j: (i, 0))
        slab_specs.append(pl.BlockSpec((c.shape[0] // parts, c.shape[1]), index_map))
        slab_bytes += 3 * _nbytes(c.shape, F32) // parts
    est = (2 * (_nbytes((tm, d), BF16) + 3 * _nbytes((d, tn), BF16) + _nbytes((tm, tn), BF16)
                + _nbytes((tail_rows, tn), F32)) + 8 * _nbytes((tm, tn), F32) + slab_bytes)
    return pl.pallas_call(
        functools.partial(_conv_kernel, seq_len=seq_len, tail_rows=tail_rows, has_state=has_state,
                          rows=rows, cast_per_step=cast_per_step),
        out_shape=(jax.ShapeDtypeStruct((m, n), BF16),
                   jax.ShapeDtypeStruct((m // tm * tail_rows, n), F32),
                   *[jax.ShapeDtypeStruct(c.shape, BF16) for c in cast]),
        grid=(m // tm, n_j),
        in_specs=in_specs + slab_specs,
        out_specs=(pl.BlockSpec((tm, tn), lambda i, j: (i, j)),
                   pl.BlockSpec((tail_rows, tn), lambda i, j: (i, j)),
                   *slab_specs),
        compiler_params=_params(("parallel", "arbitrary"), est),
        name="conv_branch",
    )(*args, *cast)


def _stack_heads(q_ref, rows, col0, group):
    parts = []
    for g in range(group):
        blk = q_ref[rows, col0 + (g // 2) * LANES:col0 + (g // 2 + 1) * LANES]
        keep = (_lane_iota(blk.shape) >= HEAD_DIM) == bool(g % 2)
        parts.append(jnp.where(keep, blk, jnp.zeros_like(blk)))
    return jnp.concatenate(parts, axis=0)


def _sink_column(sink_ref, head0, group, rows):
    return jnp.concatenate(
        [jnp.full((rows, 1), sink_ref[head0 + g], F32) for g in range(group)], axis=0)


def _store_heads(o_ref, rows, col0, o, group, nrows):
    for lp in range(group // 2):
        a = o[(2 * lp) * nrows:(2 * lp + 1) * nrows]
        b = o[(2 * lp + 1) * nrows:(2 * lp + 2) * nrows]
        low = _lane_iota(a.shape) < HEAD_DIM
        o_ref[rows, col0 + lp * LANES:col0 + (lp + 1) * LANES] = jnp.where(low, a, b).astype(BF16)


def _attn_pairs(pairs, sink_ref, q_ref, k_ref, v_ref, o_ref, n_kv, group):
    def block_rows(index):
        start = index * PAIR
        return pl.ds(start if isinstance(start, int) else pl.multiple_of(start, PAIR), PAIR)

    first_q = _lane_iota((CHUNK, PAIR)) < CHUNK

    def keys_of(ref, pair, has_prev, tile):
        cols = slice(tile * LANES, (tile + 1) * LANES)
        if has_prev:
            return jnp.concatenate([ref[block_rows(pair - 1), cols], ref[block_rows(pair), cols]],
                                   axis=0)
        return ref[block_rows(pair), cols]

    def scores(pair, has_prev, kv):
        col0 = kv * group * HEAD_DIM
        q_tiles = jnp.concatenate(
            [q_ref[block_rows(pair), col0 + lp * LANES:col0 + (lp + 1) * LANES]
             for lp in range(group // 2)], axis=0)
        return [_dot_nt(keys_of(k_ref, pair, has_prev, 2 * kv + half), q_tiles)
                for half in range(2)]

    items = [(pair, has_prev, kv) for pair, has_prev in pairs for kv in range(n_kv)]
    st_next = scores(*items[0])
    for n, (pair, has_prev, kv) in enumerate(items):
        st = st_next
        if n + 1 < len(items):
            st_next = scores(*items[n + 1])
        rows = block_rows(pair)
        vt = keys_of(v_ref, pair, has_prev, kv).astype(F32).T.astype(BF16)
        col0 = kv * group * HEAD_DIM
        probs, rden = [], []
        for g in range(group):
            sg = st[g % 2][:, (g // 2) * PAIR:(g // 2 + 1) * PAIR]
            blocks = [sg[c * CHUNK:(c + 1) * CHUNK] for c in range(sg.shape[0] // CHUNK)]
            if has_prev:
                s = jnp.concatenate([jnp.where(first_q, blocks[0], blocks[3]),
                                     blocks[1], blocks[2]], axis=0)
            else:
                s = jnp.concatenate([blocks[0], jnp.where(first_q, NEG_INF, blocks[1])], axis=0)
            sink = sink_ref[kv * group + g]
            m = jnp.maximum(jnp.max(s, axis=0, keepdims=True), sink)
            e = jnp.exp2(s - m)
            den = jnp.sum(e, axis=0, keepdims=True) + jnp.exp2(sink - m)
            if has_prev:
                e_03, e_12 = e[:CHUNK], e[CHUNK:]
                e = jnp.concatenate([jnp.where(first_q, e_03, 0.0), e_12,
                                     jnp.where(first_q, 0.0, e_03)], axis=0)
            probs.append(e.astype(BF16))
            rden.append(1.0 / den)
        ot = _dot(vt, jnp.concatenate(probs, axis=1)) * jnp.concatenate(rden, axis=1)
        for lp in range(group // 2):
            a = ot[:HEAD_DIM, (2 * lp) * PAIR:(2 * lp + 1) * PAIR]
            b = ot[HEAD_DIM:, (2 * lp + 1) * PAIR:(2 * lp + 2) * PAIR]
            o_ref[rows, col0 + lp * LANES:col0 + (lp + 1) * LANES] = (
                jnp.concatenate([a, b], axis=0).T.astype(BF16))


def _attn_kernel(sink_ref, q_ref, k_ref, v_ref, o_ref, *, n_kv, group):
    def run(pairs):
        _attn_pairs(pairs, sink_ref, q_ref, k_ref, v_ref, o_ref, n_kv, group)

    def trip(it, carry):
        run([(1 + PAIRS_PER_TRIP * it + u, True) for u in range(PAIRS_PER_TRIP)])
        return carry

    n_pairs = q_ref.shape[0] // PAIR
    trips = (n_pairs - 1) // PAIRS_PER_TRIP
    run([(0, False)])
    lax.fori_loop(0, trips, trip, 0)
    leftover = range(1 + trips * PAIRS_PER_TRIP, n_pairs)
    if leftover:
        run([(pair, True) for pair in leftover])


def _attention(q, k2, v2, sinks, *, batch, seq, n_kv, group):
    m, d = q.shape
    assert WINDOW == PAIR and seq % PAIR == 0
    est = (2 * (2 * _nbytes((seq, d), BF16) + 3 * _nbytes((seq, n_kv * LANES), BF16))
           + 8 * _nbytes((group * PAIR, KEY_SPAN), F32))
    return pl.pallas_call(
        functools.partial(_attn_kernel, n_kv=n_kv, group=group),
        out_shape=jax.ShapeDtypeStruct((m, d), BF16),
        grid=(batch,),
        in_specs=[pl.BlockSpec(memory_space=pltpu.SMEM),
                  pl.BlockSpec((seq, d), lambda b: (b, 0)),
                  pl.BlockSpec((seq, 2 * n_kv * LANES), lambda b: (b, 0)),
                  pl.BlockSpec((seq, n_kv * LANES), lambda b: (b, 0))],
        out_specs=pl.BlockSpec((seq, d), lambda b: (b, 0)),
        compiler_params=_params(("parallel",), est),
        name="attention",
    )(sinks, q, k2, v2)


def _attn_sample_kernel(sink_ref, q_ref, kn_ref, vn_ref, kc_ref, vc_ref, o_ref, *, n_kv, group):
    t = q_ref.shape[0]
    for kv in range(n_kv):
        pair_sl = slice((kv // 2) * LANES, (kv // 2 + 1) * LANES)
        kc = _dup_halves(kc_ref[:, pair_sl])[kv % 2].astype(BF16)
        vc = _dup_halves(vc_ref[:, pair_sl])[kv % 2].astype(BF16)
        kn = (kn_ref[:, 2 * kv * LANES:(2 * kv + 1) * LANES]
              + kn_ref[:, (2 * kv + 1) * LANES:(2 * kv + 2) * LANES])
        vn = vn_ref[:, kv * LANES:(kv + 1) * LANES]
        col0 = kv * group * HEAD_DIM
        lhs = _stack_heads(q_ref, slice(0, t), col0, group)
        s_c = _dot_nt(lhs, kc)
        s_n = _dot_nt(lhs, kn)
        sink = _sink_column(sink_ref, kv * group, group, t)
        m = jnp.maximum(jnp.maximum(jnp.max(s_c, axis=1, keepdims=True),
                                    jnp.max(s_n, axis=1, keepdims=True)), sink)
        e_c = jnp.exp2(s_c - m)
        e_n = jnp.exp2(s_n - m)
        den = (jnp.sum(e_c, axis=1, keepdims=True) + jnp.sum(e_n, axis=1, keepdims=True)
               + jnp.exp2(sink - m))
        o = (_dot(e_c.astype(BF16), vc) + _dot(e_n.astype(BF16), vn)) / den
        _store_heads(o_ref, slice(0, t), col0, o, group, t)


def _attention_sample(q, k2, v2, cache_k, cache_v, sinks, *, batch, t_new, n_kv, group):
    m, d = q.shape
    win = cache_k.shape[0] // batch
    d_kv = cache_k.shape[1]
    est = 4 * 1024 * 1024
    return pl.pallas_call(
        functools.partial(_attn_sample_kernel, n_kv=n_kv, group=group),
        out_shape=jax.ShapeDtypeStruct((m, d), BF16),
        grid=(batch,),
        in_specs=[pl.BlockSpec(memory_space=pltpu.SMEM),
                  pl.BlockSpec((t_new, d), lambda b: (b, 0)),
                  pl.BlockSpec((t_new, 2 * n_kv * LANES), lambda b: (b, 0)),
                  pl.BlockSpec((t_new, n_kv * LANES), lambda b: (b, 0)),
                  pl.BlockSpec((win, d_kv), lambda b: (b, 0)),
                  pl.BlockSpec((win, d_kv), lambda b: (b, 0))],
        out_specs=pl.BlockSpec((t_new, d), lambda b: (b, 0)),
        compiler_params=_params(("parallel",), est),
        name="attention_sample",
    )(sinks, q, k2, v2, cache_k, cache_v)


def _merge_kernel(xn_ref, a_ref, z_ref, wga_ref, wgc_ref, wap_ref, wcp_ref, o_ref):
    xn = xn_ref[...]
    attn = jax.nn.sigmoid(_dot(xn, wga_ref[...])) * _dot(a_ref[...], wap_ref[...])
    conv = jax.nn.sigmoid(_dot(xn, wgc_ref[...])) * _dot(z_ref[...], wcp_ref[...])
    o_ref[...] = (attn + conv).astype(BF16)


def _merge(xn, attn, z, w_in, w_ap, w_cp, *, tm, tn, col_ga, col_gc):
    m, d = xn.shape
    act = pl.BlockSpec((tm, d), lambda i, j: (i, 0))
    wspec = lambda off: pl.BlockSpec((d, tn), lambda i, j: (0, off // tn + j))
    est = (2 * (3 * _nbytes((tm, d), BF16) + 4 * _nbytes((d, tn), BF16) + _nbytes((tm, tn), BF16))
           + 6 * _nbytes((tm, tn), F32))
    return pl.pallas_call(
        _merge_kernel,
        out_shape=jax.ShapeDtypeStruct((m, d), BF16),
        grid=(m // tm, d // tn),
        in_specs=[act, act, act, wspec(col_ga), wspec(col_gc), wspec(0), wspec(0)],
        out_specs=pl.BlockSpec((tm, tn), lambda i, j: (i, j)),
        compiler_params=_params(("parallel", "arbitrary"), est),
        name="merge",
    )(xn, attn, z, w_in, w_in, w_ap, w_cp)


def _out_kernel(x_ref, m_ref, w_ref, g_ref, x1_ref, xn_ref, *, rows):
    for r in range(x_ref.shape[0] // rows):
        rs = slice(r * rows, (r + 1) * rows)
        x1 = x_ref[rs, :] + _dot(m_ref[rs, :], w_ref[...])
        x1_ref[rs, :] = x1
        xn_ref[rs, :] = _rmsnorm_rows(x1, g_ref[...]).astype(BF16)


def _out_proj(x, merged, w_out, g_ffn, *, tm, rows):
    m, d = x.shape
    assert m % tm == 0 and tm % rows == 0
    est = (2 * (2 * _nbytes((tm, d), F32) + 2 * _nbytes((tm, d), BF16)) + _nbytes((d, d), BF16)
           + 4 * _nbytes((rows, d), F32))
    return pl.pallas_call(
        functools.partial(_out_kernel, rows=rows),
        out_shape=(jax.ShapeDtypeStruct((m, d), F32), jax.ShapeDtypeStruct((m, d), BF16)),
        grid=(m // tm,),
        in_specs=[pl.BlockSpec((tm, d), lambda i: (i, 0)),
                  pl.BlockSpec((tm, d), lambda i: (i, 0)),
                  pl.BlockSpec((d, d), lambda i: (0, 0), pipeline_mode=pl.Buffered(1)),
                  pl.BlockSpec((1, d), lambda i: (0, 0))],
        out_specs=(pl.BlockSpec((tm, d), lambda i: (i, 0)),
                   pl.BlockSpec((tm, d), lambda i: (i, 0))),
        compiler_params=_params(("parallel",), est),
        name="out_proj",
    )(x, merged, w_out, g_ffn)


def _ffn_kernel(x_ref, xn_ref, wg_ref, wu_ref, wd_ref, o_ref):
    f = pl.program_id(1)
    slab = x_ref.shape[1]
    n_slabs = o_ref.shape[1] // slab

    @pl.when(f == 0)
    def _():
        o_ref[...] = jnp.zeros_like(o_ref)

    xn = xn_ref[...]
    h = jax.nn.silu(_dot(xn, wg_ref[...])) * _dot(xn, wu_ref[...])
    o_ref[...] += _dot(h.astype(BF16), wd_ref[...])

    for s in range(n_slabs):
        @pl.when(f == s)
        def _(s=s):
            o_ref[:, s * slab:(s + 1) * slab] += x_ref[...]


def _ffn(x, xn, w_gate, w_up, w_down, *, tm, tf, slab):
    m, d = x.shape
    d_ff = w_gate.shape[1]
    n_slabs = d // slab
    assert n_slabs <= d_ff // tf
    est = (2 * (_nbytes((tm, slab), F32) + _nbytes((tm, d), F32) + _nbytes((tm, d), BF16)
                + 3 * _nbytes((d, tf), BF16))
           + 4 * _nbytes((tm, tf), F32) + _nbytes((tm, d), F32))
    row = lambda i, f: (i, 0)
    return pl.pallas_call(
        _ffn_kernel,
        out_shape=jax.ShapeDtypeStruct((m, d), F32),
        grid=(m // tm, d_ff // tf),
        in_specs=[pl.BlockSpec((tm, slab), lambda i, f: (i, jnp.minimum(f, n_slabs - 1))),
                  pl.BlockSpec((tm, d), row),
                  pl.BlockSpec((d, tf), lambda i, f: (0, f)),
                  pl.BlockSpec((d, tf), lambda i, f: (0, f)),
                  pl.BlockSpec((tf, d), lambda i, f: (f, 0))],
        out_specs=pl.BlockSpec((tm, d), row),
        compiler_params=_params(("parallel", "arbitrary"), est),
        name="ffn",
    )(x, xn, w_gate, w_up, w_down)


def _rope_tables(pos, repeat, g, scale):
    half = HEAD_DIM // 2
    inv = jnp.exp(-math.log(ROPE_THETA) * jnp.arange(half, dtype=F32) * (2.0 / HEAD_DIM))
    ang = pos.astype(F32)[:, None] * inv[None, :]
    cos = jnp.cos(ang)
    sin = jnp.sin(ang)
    g_swap = jnp.concatenate([g[half:], g[:half]])
    t_same = g[None, :] * jnp.concatenate([cos, cos], axis=-1) * scale
    t_swap = g_swap[None, :] * jnp.concatenate([-sin, sin], axis=-1) * scale
    reps = (repeat, LANES // HEAD_DIM)
    return jnp.tile(t_same, reps), jnp.tile(t_swap, reps)


LATE_WEIGHTS = ("w_attn_proj", "w_conv_proj", "w_out", "w_ffn_gate", "w_ffn_up", "w_ffn_down")


def _layer(x, w, *, batch, seq, pos, tiles, conv_prev, sample_cache):
    d = x.shape[1]
    n_heads = w["sinks"].shape[0]
    d_attn = n_heads * HEAD_DIM
    d_kv = w["d_kv"]
    n_kv = d_kv // HEAD_DIM
    group = n_heads // n_kv
    d_conv = w["conv_w"].shape[1]
    col_b = d_attn + 2 * d_kv
    col_c = col_b + d_conv
    col_h = col_c + d_conv
    col_ga = col_h + d_conv
    col_gc = col_ga + d
    tm = tiles["tm"]

    rope_rows = max(seq, tm)
    log2e = math.log2(math.e)
    tables = (_rope_tables(pos, rope_rows // seq, w["q_norm_g"], HEAD_DIM ** -0.5 * log2e)
              + _rope_tables(pos, rope_rows // seq, w["k_norm_g"], 1.0))
    sinks = w["sinks"] * log2e

    xn, q, k, v, k2, v2 = _qkv_proj(x, w["norm_mix_g"], w["w_in"], tables, tm=tm, d_attn=d_attn,
                                    d_kv=d_kv, rows=tiles["qkv_rows"])
    late = [name for name in LATE_WEIGHTS if w[name].dtype != BF16]
    z, u_tail, *late_bf16 = _conv_branch(
        xn, w["w_in"], w["conv_w"], conv_prev,
        [w[name] for name in late],
        tm=tiles["tm_conv"], tn=tiles["tn_conv"], rows=tiles["conv_rows"], seq_len=seq,
        tail_rows=tiles["conv_tail"], col_b=col_b, col_c=col_c, col_h=col_h)
    w = dict(w, **dict(zip(late, late_bf16)))
    if sample_cache is None:
        attn = _attention(q, k2, v2, sinks, batch=batch, seq=seq, n_kv=n_kv, group=group)
    else:
        attn = _attention_sample(q, k2, v2, sample_cache[0], sample_cache[1], sinks,
                                 batch=batch, t_new=seq, n_kv=n_kv, group=group)
    merged = _merge(xn, attn, z, w["w_in"], w["w_attn_proj"], w["w_conv_proj"],
                    tm=tm, tn=tiles["tn_merge"], col_ga=col_ga, col_gc=col_gc)
    x1, xn_ffn = _out_proj(x, merged, w["w_out"], w["norm_ffn_g"],
                           tm=tiles["tm_out"], rows=tiles["out_rows"])
    y = _ffn(x1, xn_ffn, w["w_ffn_gate"], w["w_ffn_up"], w["w_ffn_down"],
             tm=tiles["tm_ffn"], tf=tiles["tf"], slab=V7X_MXU_DIM)
    return y, k, v, u_tail, w


def kernel(x_prompt, x_sample, cache_k, cache_v, state_conv, norm_mix_g, w_in, q_norm_g, k_norm_g,
           attn_sinks, conv_w, w_attn_proj, w_conv_proj, w_out, norm_ffn_g, w_ffn_gate, w_ffn_up,
           w_ffn_down):
    depth = w_in.shape[0]
    assert depth == 1
    b_p, s_p, d = x_prompt.shape
    b_s, s_s, _ = x_sample.shape
    n_kv = cache_k.shape[3]
    d_kv = n_kv * HEAD_DIM
    win = cache_k.shape[2]
    n_state = state_conv.shape[2]

    w = {
        "d_kv": d_kv,
        "norm_mix_g": norm_mix_g[0].reshape(1, d),
        "w_in": w_in[0].astype(BF16),
        "q_norm_g": q_norm_g[0],
        "k_norm_g": k_norm_g[0],
        "sinks": attn_sinks[0],
        "conv_w": conv_w[0],
        "w_attn_proj": w_attn_proj[0],
        "w_conv_proj": w_conv_proj[0],
        "w_out": w_out[0],
        "norm_ffn_g": norm_ffn_g[0].reshape(1, d),
        "w_ffn_gate": w_ffn_gate[0],
        "w_ffn_up": w_ffn_up[0],
        "w_ffn_down": w_ffn_down[0],
    }

    m_p = b_p * s_p
    tm_p = min(1024, s_p)
    tiles_p = dict(tm=tm_p, qkv_rows=256, tm_conv=s_p, tn_conv=256, conv_tail=8,
                   conv_rows=min(512, s_p), tn_merge=512, tm_out=min(512, s_p), out_rows=256,
                   tm_ffn=tm_p, tf=512)
    y_p, k_p, v_p, u_p, w = _layer(x_prompt.reshape(m_p, d), w, batch=b_p, seq=s_p,
                                   pos=jnp.arange(s_p), tiles=tiles_p, conv_prev=None,
                                   sample_cache=None)
    keep = min(WINDOW, s_p)
    new_k_p = k_p.reshape(b_p, s_p, n_kv, HEAD_DIM)[:, s_p - keep:][None]
    new_v_p = v_p.reshape(b_p, s_p, n_kv, HEAD_DIM)[:, s_p - keep:][None]
    new_c_p = u_p.reshape(b_p, tiles_p["conv_tail"], -1)[:, tiles_p["conv_tail"] - n_state:][None]

    m_s = b_s * s_s
    st = state_conv[0]
    prev1 = jnp.pad(st[:, 1:2], ((0, 0), (0, s_s - 1), (0, 0))).reshape(m_s, -1)
    prev2 = jnp.pad(st, ((0, 0), (0, s_s - n_state), (0, 0))).reshape(m_s, -1)
    tiles_s = dict(tm=m_s, qkv_rows=m_s, tm_conv=m_s, tn_conv=256, conv_tail=m_s,
                   conv_rows=m_s, tn_merge=256, tm_out=m_s, out_rows=m_s, tm_ffn=m_s, tf=512)
    y_s, k_s, v_s, u_s, _ = _layer(x_sample.reshape(m_s, d), w, batch=b_s, seq=s_s,
                                   pos=PAST_LEN + jnp.arange(s_s), tiles=tiles_s,
                                   conv_prev=(prev1, prev2),
                                   sample_cache=(cache_k[0].reshape(b_s * win, d_kv),
                                                 cache_v[0].reshape(b_s * win, d_kv)))
    new_k_s = k_s.reshape(b_s, s_s, n_kv, HEAD_DIM)[None]
    new_v_s = v_s.reshape(b_s, s_s, n_kv, HEAD_DIM)[None]
    new_c_s = u_s.reshape(b_s, s_s, -1)[:, s_s - n_state:][None]

    return (y_p.reshape(b_p, s_p, d), y_s.reshape(b_s, s_s, d), new_k_p, new_v_p, new_c_p,
            new_k_s, new_v_s, new_c_s)
```

```python
import functools
import math

import jax
import jax.numpy as jnp
from jax import lax
from jax.experimental import pallas as pl
from jax.experimental.pallas import tpu as pltpu

F32 = jnp.float32
BF16 = jnp.bfloat16

CHUNK = 64
WINDOW = 128
PAST_LEN = 1024
ROPE_THETA = 10000.0
EPS = 1e-6
NEG_INF = -1e30
HEAD_DIM = 64
LANES = 128
BF16_SUBLANES = 16
PAIR = 2 * CHUNK
KEY_SPAN = PAIR + WINDOW
PAIRS_PER_TRIP = 3

V7X_MXU_DIM = 256
V7X_VMEM_BYTES = 64 * 1024 * 1024
VMEM_CAP_BYTES = V7X_VMEM_BYTES - 6 * 1024 * 1024


def _params(semantics, vmem_estimate):
    limit = int(min(max(vmem_estimate, 16 * 1024 * 1024), VMEM_CAP_BYTES))
    return pltpu.CompilerParams(dimension_semantics=semantics, vmem_limit_bytes=limit)


def _nbytes(shape, dtype):
    return math.prod(shape) * jnp.dtype(dtype).itemsize


def _dot(a, b):
    return jnp.dot(a, b, preferred_element_type=F32)


def _dot_nt(a, b):
    return lax.dot_general(a, b, (((1,), (1,)), ((), ())), preferred_element_type=F32)


def _lane_iota(shape):
    return lax.broadcasted_iota(jnp.int32, shape, 1)


def _rmsnorm_rows(x, g):
    y = x * lax.rsqrt(jnp.mean(x * x, axis=-1, keepdims=True) + EPS)
    return y * g


def _headnorm_rope(y, t_same, t_swap, seg):
    width = y.shape[1]
    s = y * y
    s_hi = s.astype(BF16)
    s_lo = (s - s_hi.astype(F32)).astype(BF16)
    ms = _dot(s_hi, seg) + _dot(s_lo, seg)
    half = HEAD_DIM // 2
    ahead = pltpu.roll(y, width - half, axis=1)
    behind = pltpu.roll(y, half, axis=1)
    swapped = jnp.where((_lane_iota(y.shape) & (HEAD_DIM - 1)) < half, ahead, behind)
    reps = width // t_same.shape[1]
    t_same = jnp.concatenate([t_same] * reps, axis=1)
    t_swap = jnp.concatenate([t_swap] * reps, axis=1)
    return (y * t_same + swapped * t_swap) * lax.rsqrt(ms + EPS)


def _dup_halves(x):
    low = _lane_iota(x.shape) < HEAD_DIM
    r = pltpu.roll(x, HEAD_DIM, axis=1)
    return jnp.where(low, x, r), jnp.where(low, r, x)


def _isolate_halves(x):
    low = _lane_iota(x.shape) < HEAD_DIM
    r = pltpu.roll(x, HEAD_DIM, axis=1)
    return (jnp.where(low, x, 0.0), jnp.where(low, 0.0, r),
            jnp.where(low, r, 0.0), jnp.where(low, 0.0, x))


def _qkv_kernel(*refs, n_q_blocks, d_kv, rows, tail_keep):
    x_refs = refs[:n_q_blocks]
    g_ref, w_ref, seg_ref, qs_ref, qw_ref, ks_ref, kw_ref = refs[n_q_blocks:n_q_blocks + 7]
    xn_ref, q_ref, k_ref, v_ref, k2_ref, v2_ref = refs[n_q_blocks + 7:]
    j = pl.program_id(1)
    tm, tn = q_ref.shape
    d = xn_ref.shape[1]
    xw = d // n_q_blocks
    chunks = [slice(r * rows, (r + 1) * rows) for r in range(tm // rows)]
    seg = seg_ref[...]
    slab = seg.shape[0]

    def q_block(normalize):
        for rs in chunks:
            if normalize:
                xs = [ref[rs, :] for ref in x_refs]
                ss = sum(jnp.sum(xc * xc, axis=-1, keepdims=True) for xc in xs)
                r = lax.rsqrt(ss * (1.0 / d) + EPS)
                for c, xc in enumerate(xs):
                    cs = slice(c * xw, (c + 1) * xw)
                    xn_ref[rs, cs] = (xc * r * g_ref[:, cs]).astype(BF16)
            y = _dot(xn_ref[rs, :], w_ref[...])
            for c in range(tn // slab):
                sl = slice(c * slab, (c + 1) * slab)
                q_ref[rs, sl] = _headnorm_rope(y[:, sl], qs_ref[rs, :], qw_ref[rs, :],
                                               seg).astype(BF16)

    pl.when(j == 0)(functools.partial(q_block, True))
    pl.when((j > 0) & (j < n_q_blocks))(functools.partial(q_block, False))

    @pl.when(j == n_q_blocks)
    def _():
        for rs in chunks:
            y = _dot(xn_ref[rs, :], w_ref[...])
            k_all = jnp.concatenate(
                [_headnorm_rope(y[:, c * slab:(c + 1) * slab], ks_ref[rs, :], kw_ref[rs, :], seg)
                 for c in range(d_kv // slab)], axis=1)
            if tail_keep is None:
                k_ref[rs, :] = k_all
                v_ref[rs, :] = y[:, d_kv:]
            elif rs is chunks[-1]:
                k_ref[0] = k_all[rows - tail_keep:].T
                v_ref[0] = y[rows - tail_keep:, d_kv:].T
            for c in range(d_kv // LANES):
                sl = slice(c * LANES, (c + 1) * LANES)
                k = k_all[:, sl]
                v = y[:, d_kv + c * LANES:d_kv + (c + 1) * LANES]
                va, vb = _dup_halves(v)
                lo = slice(2 * c * LANES, (2 * c + 1) * LANES)
                hi = slice((2 * c + 1) * LANES, (2 * c + 2) * LANES)
                v2_ref[rs, lo] = va.astype(BF16)
                v2_ref[rs, hi] = vb.astype(BF16)
                for n, piece in enumerate(_isolate_halves(k)):
                    k2_ref[rs, (4 * c + n) * LANES:(4 * c + n + 1) * LANES] = piece.astype(BF16)


def _qkv_proj(x, g, w_in, tables, tail, *, tm, d_attn, d_kv, rows):
    m, d = x.shape
    tn = 2 * d_kv
    n_q_blocks = d_attn // tn
    assert m % tm == 0 and tm % rows == 0 and d_attn % tn == 0 and d % n_q_blocks == 0
    if tail is None:
        kv_shape = jax.ShapeDtypeStruct((m, d_kv), F32)
        kv_spec = pl.BlockSpec((tm, d_kv), lambda i, j: (i, 0))
    else:
        seq, keep = tail
        assert seq % tm == 0 and keep <= rows and keep % LANES == 0
        kv_shape = jax.ShapeDtypeStruct((m // seq, d_kv, keep), F32)
        kv_spec = pl.BlockSpec((1, d_kv, keep), lambda i, j: (i // (seq // tm), 0, 0))
    rope_blocks = tables[0].shape[0] // tm
    tab = pl.BlockSpec((tm, LANES), lambda i, j: (i % rope_blocks, 0))
    row_out = lambda width: pl.BlockSpec((tm, width), lambda i, j: (i, 0))
    last_tile = m // tm - 1

    def x_slab(c):
        return pl.BlockSpec((tm, d // n_q_blocks),
                            lambda i, j: (jnp.minimum(jnp.where(j > c, i + 1, i), last_tile), c))

    head_of = jnp.arange(V7X_MXU_DIM) // HEAD_DIM
    seg = jnp.where(head_of[:, None] == head_of[None, :], 1.0 / HEAD_DIM, 0.0).astype(BF16)
    est = (2 * (_nbytes((tm, d), F32) + _nbytes((tm, d), BF16) + _nbytes((d, tn), BF16)
                + 4 * _nbytes((tm, tn), BF16) + 2 * _nbytes((tm, d_kv), F32)
                + 4 * _nbytes((tm, LANES), F32))
           + 6 * _nbytes((rows, tn), F32) + 2 * _nbytes((rows, d), F32))
    return pl.pallas_call(
        functools.partial(_qkv_kernel, n_q_blocks=n_q_blocks, d_kv=d_kv, rows=rows,
                          tail_keep=None if tail is None else tail[1]),
        out_shape=(jax.ShapeDtypeStruct((m, d), BF16), jax.ShapeDtypeStruct((m, d_attn), BF16),
                   kv_shape, kv_shape,
                   jax.ShapeDtypeStruct((m, 2 * tn), BF16), jax.ShapeDtypeStruct((m, tn), BF16)),
        grid=(m // tm, n_q_blocks + 1),
        in_specs=[x_slab(c) for c in range(n_q_blocks)] + [
                  pl.BlockSpec((1, d), lambda i, j: (0, 0)),
                  pl.BlockSpec((d, tn), lambda i, j: (0, j)),
                  pl.BlockSpec(seg.shape, lambda i, j: (0, 0)),
                  tab, tab, tab, tab],
        out_specs=(row_out(d),
                   pl.BlockSpec((tm, tn), lambda i, j: (i, jnp.minimum(j, n_q_blocks - 1))),
                   kv_spec, kv_spec, row_out(2 * tn), row_out(tn)),
        compiler_params=_params(("arbitrary", "arbitrary"), est),
        name="qkv_proj",
    )(*([x] * n_q_blocks), g, w_in, seg, *tables)


def _conv_kernel(*refs, seq_len, tail_rows, has_state, rows, cast_per_step):
    n_cast = len(cast_per_step)
    n_in = 7 if has_state else 5
    xn_ref, wb_ref, wc_ref, wh_ref, cw_ref = refs[:5]
    p1_ref, p2_ref = refs[5:n_in] if has_state else (None, None)
    z_ref, u_ref = refs[n_in + n_cast:n_in + n_cast + 2]
    casts = list(zip(refs[n_in:n_in + n_cast], refs[n_in + n_cast + 2:]))
    tm, tn = z_ref.shape
    w = cw_ref[...]
    row = lax.broadcasted_iota(jnp.int32, (rows, tn), 0)
    last1 = last2 = None
    for r in range(tm // rows):
        rs = slice(r * rows, (r + 1) * rows)
        xn = xn_ref[rs, :]
        u = _dot(xn, wc_ref[...]) * _dot(xn, wh_ref[...])
        u1 = pltpu.roll(u, 1, axis=0)
        u2 = pltpu.roll(u, 2, axis=0)
        if r > 0:
            u1 = jnp.where(row == 0, last1, u1)
            u2 = jnp.where(row == 0, last2, jnp.where(row == 1, last1, u2))
        if (r * rows) % seq_len == 0 or seq_len < rows:
            pos = (row + r * rows) & (seq_len - 1)
            u1 = jnp.where(pos == 0, p1_ref[rs, :] if has_state else 0.0, u1)
            u2 = jnp.where(pos < 2, p2_ref[rs, :] if has_state else 0.0, u2)
        y = u2 * w[0:1] + u1 * w[1:2] + u * w[2:3]
        z_ref[rs, :] = (_dot(xn, wb_ref[...]) * y).astype(BF16)
        last1, last2 = u[rows - 1:rows], u[rows - 2:rows - 1]
        for (src_ref, dst_ref), per_step in list(zip(casts, cast_per_step))[r::tm // rows]:
            if per_step:
                dst_ref[...] = src_ref[...].astype(BF16)
            else:
                @pl.when(pl.program_id(1) == 0)
                def _(src_ref=src_ref, dst_ref=dst_ref):
                    dst_ref[...] = src_ref[...].astype(BF16)
    u_ref[...] = u[rows - tail_rows:]


def _conv_branch(xn, w_in, conv_w, prev, cast, *, tm, tn, rows, seq_len, tail_rows,
                 col_b, col_c, col_h):
    m, d = xn.shape
    n = conv_w.shape[1]
    has_state = prev is not None
    assert seq_len & (seq_len - 1) == 0 and tm % seq_len == 0
    assert tm % rows == 0 and tail_rows <= rows and (rows % seq_len == 0 or seq_len % rows == 0)
    n_j = n // tn
    steps = (m // tm) * n_j
    wspec = lambda off: pl.BlockSpec((d, tn), lambda i, j: (0, off // tn + j))
    in_specs = [pl.BlockSpec((tm, d), lambda i, j: (i, 0)),
                wspec(col_b), wspec(col_c), wspec(col_h),
                pl.BlockSpec((conv_w.shape[0], tn), lambda i, j: (0, j))]
    args = [xn, w_in, w_in, w_in, conv_w]
    if has_state:
        in_specs += [pl.BlockSpec((tm, tn), lambda i, j: (i, j))] * 2
        args += list(prev)
    cast_per_step = tuple(c.shape[0] % (steps * BF16_SUBLANES) == 0 for c in cast)
    slab_specs, slab_bytes = [], 0
    for c, per_step in zip(cast, cast_per_step):
        parts = steps if per_step else m // tm
        assert c.shape[0] % (parts * BF16_SUBLANES) == 0
        index_map = (lambda i, j: (i * n_j + j, 0)) if per_step else (lambda i, j: (i, 0))
        slab_specs.append(pl.BlockSpec((c.shape[0] // parts, c.shape[1]), index_map))
        slab_bytes += 3 * _nbytes(c.shape, F32) // parts
    est = (2 * (_nbytes((tm, d), BF16) + 3 * _nbytes((d, tn), BF16) + _nbytes((tm, tn), BF16)
                + _nbytes((tail_rows, tn), F32)) + 8 * _nbytes((tm, tn), F32) + slab_bytes)
    return pl.pallas_call(
        functools.partial(_conv_kernel, seq_len=seq_len, tail_rows=tail_rows, has_state=has_state,
                          rows=rows, cast_per_step=cast_per_step),
        out_shape=(jax.ShapeDtypeStruct((m, n), BF16),
                   jax.ShapeDtypeStruct((m // tm * tail_rows, n), F32),
                   *[jax.ShapeDtypeStruct(c.shape, BF16) for c in cast]),
        grid=(m // tm, n_j),
        in_specs=in_specs + slab_specs,
        out_specs=(pl.BlockSpec((tm, tn), lambda i, j: (i, j)),
                   pl.BlockSpec((tail_rows, tn), lambda i, j: (i, j)),
                   *slab_specs),
        compiler_params=_params(("parallel", "arbitrary"), est),
        name="conv_branch",
    )(*args, *cast)


def _stack_heads(q_ref, rows, col0, group):
    parts = []
    for g in range(group):
        blk = q_ref[rows, col0 + (g // 2) * LANES:col0 + (g // 2 + 1) * LANES]
        keep = (_lane_iota(blk.shape) >= HEAD_DIM) == bool(g % 2)
        parts.append(jnp.where(keep, blk, jnp.zeros_like(blk)))
    return jnp.concatenate(parts, axis=0)


def _sink_column(sink_ref, head0, group, rows):
    return jnp.concatenate(
        [jnp.full((rows, 1), sink_ref[head0 + g], F32) for g in range(group)], axis=0)


def _store_heads(o_ref, rows, col0, o, group, nrows):
    for lp in range(group // 2):
        a = o[(2 * lp) * nrows:(2 * lp + 1) * nrows]
        b = o[(2 * lp + 1) * nrows:(2 * lp + 2) * nrows]
        low = _lane_iota(a.shape) < HEAD_DIM
        o_ref[rows, col0 + lp * LANES:col0 + (lp + 1) * LANES] = jnp.where(low, a, b).astype(BF16)


def _attn_pairs(pairs, sink_ref, q_ref, k_ref, v_ref, o_ref, n_kv, group):
    def block_rows(index):
        start = index * PAIR
        return pl.ds(start if isinstance(start, int) else pl.multiple_of(start, PAIR), PAIR)

    first_q = _lane_iota((CHUNK, PAIR)) < CHUNK

    def keys_of(ref, pair, has_prev, tile):
        cols = slice(tile * LANES, (tile + 1) * LANES)
        if has_prev:
            return jnp.concatenate([ref[block_rows(pair - 1), cols], ref[block_rows(pair), cols]],
                                   axis=0)
        return ref[block_rows(pair), cols]

    def scores(pair, has_prev, kv):
        col0 = kv * group * HEAD_DIM
        q_tiles = jnp.concatenate(
            [q_ref[block_rows(pair), col0 + lp * LANES:col0 + (lp + 1) * LANES]
             for lp in range(group // 2)], axis=0)
        return [_dot_nt(keys_of(k_ref, pair, has_prev, 2 * kv + half), q_tiles)
                for half in range(2)]

    items = [(pair, has_prev, kv) for pair, has_prev in pairs for kv in range(n_kv)]
    st_next = scores(*items[0])
    for n, (pair, has_prev, kv) in enumerate(items):
        st = st_next
        if n + 1 < len(items):
            st_next = scores(*items[n + 1])
        rows = block_rows(pair)
        vt = keys_of(v_ref, pair, has_prev, kv).astype(F32).T.astype(BF16)
        col0 = kv * group * HEAD_DIM
        probs, rden = [], []
        for g in range(group):
            sg = st[g % 2][:, (g // 2) * PAIR:(g // 2 + 1) * PAIR]
            blocks = [sg[c * CHUNK:(c + 1) * CHUNK] for c in range(sg.shape[0] // CHUNK)]
            if has_prev:
                s = jnp.concatenate([jnp.where(first_q, blocks[0], blocks[3]),
                                     blocks[1], blocks[2]], axis=0)
            else:
                s = jnp.concatenate([blocks[0], jnp.where(first_q, NEG_INF, blocks[1])], axis=0)
            sink = sink_ref[kv * group + g]
            m = jnp.maximum(jnp.max(s, axis=0, keepdims=True), sink)
            e = jnp.exp2(s - m)
            den = jnp.sum(e, axis=0, keepdims=True) + jnp.exp2(sink - m)
            if has_prev:
                e_03, e_12 = e[:CHUNK], e[CHUNK:]
                e = jnp.concatenate([jnp.where(first_q, e_03, 0.0), e_12,
                                     jnp.where(first_q, 0.0, e_03)], axis=0)
            probs.append(e.astype(BF16))
            rden.append(1.0 / den)
        ot = _dot(vt, jnp.concatenate(probs, axis=1)) * jnp.concatenate(rden, axis=1)
        for lp in range(group // 2):
            a = ot[:HEAD_DIM, (2 * lp) * PAIR:(2 * lp + 1) * PAIR]
            b = ot[HEAD_DIM:, (2 * lp + 1) * PAIR:(2 * lp + 2) * PAIR]
            o_ref[rows, col0 + lp * LANES:col0 + (lp + 1) * LANES] = (
                jnp.concatenate([a, b], axis=0).T.astype(BF16))


def _attn_kernel(sink_ref, q_ref, k_ref, v_ref, o_ref, *, n_kv, group):
    def run(pairs):
        _attn_pairs(pairs, sink_ref, q_ref, k_ref, v_ref, o_ref, n_kv, group)

    def trip(it, carry):
        run([(1 + PAIRS_PER_TRIP * it + u, True) for u in range(PAIRS_PER_TRIP)])
        return carry

    n_pairs = q_ref.shape[0] // PAIR
    trips = (n_pairs - 1) // PAIRS_PER_TRIP
    run([(0, False)])
    lax.fori_loop(0, trips, trip, 0)
    leftover = range(1 + trips * PAIRS_PER_TRIP, n_pairs)
    if leftover:
        run([(pair, True) for pair in leftover])


def _attention(q, k2, v2, sinks, *, batch, seq, n_kv, group):
    m, d = q.shape
    assert WINDOW == PAIR and seq % PAIR == 0
    est = (2 * (2 * _nbytes((seq, d), BF16) + 3 * _nbytes((seq, n_kv * LANES), BF16))
           + 8 * _nbytes((group * PAIR, KEY_SPAN), F32))
    return pl.pallas_call(
        functools.partial(_attn_kernel, n_kv=n_kv, group=group),
        out_shape=jax.ShapeDtypeStruct((m, d), BF16),
        grid=(batch,),
        in_specs=[pl.BlockSpec(memory_space=pltpu.SMEM),
                  pl.BlockSpec((seq, d), lambda b: (b, 0)),
                  pl.BlockSpec((seq, 2 * n_kv * LANES), lambda b: (b, 0)),
                  pl.BlockSpec((seq, n_kv * LANES), lambda b: (b, 0))],
        out_specs=pl.BlockSpec((seq, d), lambda b: (b, 0)),
        compiler_params=_params(("parallel",), est),
        name="attention",
    )(sinks, q, k2, v2)


def _attn_sample_kernel(sink_ref, q_ref, kn_ref, vn_ref, kc_ref, vc_ref, o_ref, *, n_kv, group):
    t = q_ref.shape[0]
    for kv in range(n_kv):
        pair_sl = slice((kv // 2) * LANES, (kv // 2 + 1) * LANES)
        kc = _dup_halves(kc_ref[:, pair_sl])[kv % 2].astype(BF16)
        vc = _dup_halves(vc_ref[:, pair_sl])[kv % 2].astype(BF16)
        kn = (kn_ref[:, 2 * kv * LANES:(2 * kv + 1) * LANES]
              + kn_ref[:, (2 * kv + 1) * LANES:(2 * kv + 2) * LANES])
        vn = vn_ref[:, kv * LANES:(kv + 1) * LANES]
        col0 = kv * group * HEAD_DIM
        lhs = _stack_heads(q_ref, slice(0, t), col0, group)
        s_c = _dot_nt(lhs, kc)
        s_n = _dot_nt(lhs, kn)
        sink = _sink_column(sink_ref, kv * group, group, t)
        m = jnp.maximum(jnp.maximum(jnp.max(s_c, axis=1, keepdims=True),
                                    jnp.max(s_n, axis=1, keepdims=True)), sink)
        e_c = jnp.exp2(s_c - m)
        e_n = jnp.exp2(s_n - m)
        den = (jnp.sum(e_c, axis=1, keepdims=True) + jnp.sum(e_n, axis=1, keepdims=True)
               + jnp.exp2(sink - m))
        o = (_dot(e_c.astype(BF16), vc) + _dot(e_n.astype(BF16), vn)) / den
        _store_heads(o_ref, slice(0, t), col0, o, group, t)


def _attention_sample(q, k2, v2, cache_k, cache_v, sinks, *, batch, t_new, n_kv, group):
    m, d = q.shape
    win = cache_k.shape[0] // batch
    d_kv = cache_k.shape[1]
    est = 4 * 1024 * 1024
    return pl.pallas_call(
        functools.partial(_attn_sample_kernel, n_kv=n_kv, group=group),
        out_shape=jax.ShapeDtypeStruct((m, d), BF16),
        grid=(batch,),
        in_specs=[pl.BlockSpec(memory_space=pltpu.SMEM),
                  pl.BlockSpec((t_new, d), lambda b: (b, 0)),
                  pl.BlockSpec((t_new, 2 * n_kv * LANES), lambda b: (b, 0)),
                  pl.BlockSpec((t_new, n_kv * LANES), lambda b: (b, 0)),
                  pl.BlockSpec((win, d_kv), lambda b: (b, 0)),
                  pl.BlockSpec((win, d_kv), lambda b: (b, 0))],
        out_specs=pl.BlockSpec((t_new, d), lambda b: (b, 0)),
        compiler_params=_params(("parallel",), est),
        name="attention_sample",
    )(sinks, q, k2, v2, cache_k, cache_v)


def _merge_kernel(xn_ref, a_ref, z_ref, wga_ref, wgc_ref, wap_ref, wcp_ref, o_ref):
    xn = xn_ref[...]
    attn = jax.nn.sigmoid(_dot(xn, wga_ref[...])) * _dot(a_ref[...], wap_ref[...])
    conv = jax.nn.sigmoid(_dot(xn, wgc_ref[...])) * _dot(z_ref[...], wcp_ref[...])
    o_ref[...] = (attn + conv).astype(BF16)


def _merge(xn, attn, z, w_in, w_ap, w_cp, *, tm, tn, col_ga, col_gc):
    m, d = xn.shape
    act = pl.BlockSpec((tm, d), lambda i, j: (i, 0))
    wspec = lambda off: pl.BlockSpec((d, tn), lambda i, j: (0, off // tn + j))
    est = (2 * (3 * _nbytes((tm, d), BF16) + 4 * _nbytes((d, tn), BF16) + _nbytes((tm, tn), BF16))
           + 6 * _nbytes((tm, tn), F32))
    return pl.pallas_call(
        _merge_kernel,
        out_shape=jax.ShapeDtypeStruct((m, d), BF16),
        grid=(m // tm, d // tn),
        in_specs=[act, act, act, wspec(col_ga), wspec(col_gc), wspec(0), wspec(0)],
        out_specs=pl.BlockSpec((tm, tn), lambda i, j: (i, j)),
        compiler_params=_params(("parallel", "arbitrary"), est),
        name="merge",
    )(xn, attn, z, w_in, w_in, w_ap, w_cp)


def _out_kernel(x_ref, m_ref, w_ref, g_ref, x1_ref, xn_ref, *, rows):
    for r in range(x_ref.shape[0] // rows):
        rs = slice(r * rows, (r + 1) * rows)
        x1 = x_ref[rs, :] + _dot(m_ref[rs, :], w_ref[...])
        x1_ref[rs, :] = x1
        xn_ref[rs, :] = _rmsnorm_rows(x1, g_ref[...]).astype(BF16)


def _out_proj(x, merged, w_out, g_ffn, *, tm, rows):
    m, d = x.shape
    assert m % tm == 0 and tm % rows == 0
    est = (2 * (2 * _nbytes((tm, d), F32) + 2 * _nbytes((tm, d), BF16)) + _nbytes((d, d), BF16)
           + 4 * _nbytes((rows, d), F32))
    return pl.pallas_call(
        functools.partial(_out_kernel, rows=rows),
        out_shape=(jax.ShapeDtypeStruct((m, d), F32), jax.ShapeDtypeStruct((m, d), BF16)),
        grid=(m // tm,),
        in_specs=[pl.BlockSpec((tm, d), lambda i: (i, 0)),
                  pl.BlockSpec((tm, d), lambda i: (i, 0)),
                  pl.BlockSpec((d, d), lambda i: (0, 0), pipeline_mode=pl.Buffered(1)),
                  pl.BlockSpec((1, d), lambda i: (0, 0))],
        out_specs=(pl.BlockSpec((tm, d), lambda i: (i, 0)),
                   pl.BlockSpec((tm, d), lambda i: (i, 0))),
        compiler_params=_params(("parallel",), est),
        name="out_proj",
    )(x, merged, w_out, g_ffn)


def _ffn_kernel(x_ref, xn_ref, wg_ref, wu_ref, wd_ref, o_ref):
    f = pl.program_id(1)
    slab = x_ref.shape[1]
    n_slabs = o_ref.shape[1] // slab

    @pl.when(f == 0)
    def _():
        o_ref[...] = jnp.zeros_like(o_ref)

    xn = xn_ref[...]
    h = jax.nn.silu(_dot(xn, wg_ref[...])) * _dot(xn, wu_ref[...])
    o_ref[...] += _dot(h.astype(BF16), wd_ref[...])

    for s in range(n_slabs):
        @pl.when(f == s)
        def _(s=s):
            o_ref[:, s * slab:(s + 1) * slab] += x_ref[...]


def _ffn(x, xn, w_gate, w_up, w_down, *, tm, tf, slab):
    m, d = x.shape
    d_ff = w_gate.shape[1]
    n_slabs = d // slab
    assert n_slabs <= d_ff // tf
    est = (2 * (_nbytes((tm, slab), F32) + _nbytes((tm, d), F32) + _nbytes((tm, d), BF16)
                + 3 * _nbytes((d, tf), BF16))
           + 4 * _nbytes((tm, tf), F32) + _nbytes((tm, d), F32))
    row = lambda i, f: (i, 0)
    return pl.pallas_call(
        _ffn_kernel,
        out_shape=jax.ShapeDtypeStruct((m, d), F32),
        grid=(m // tm, d_ff // tf),
        in_specs=[pl.BlockSpec((tm, slab), lambda i, f: (i, jnp.minimum(f, n_slabs - 1))),
                  pl.BlockSpec((tm, d), row),
                  pl.BlockSpec((d, tf), lambda i, f: (0, f)),
                  pl.BlockSpec((d, tf), lambda i, f: (0, f)),
                  pl.BlockSpec((tf, d), lambda i, f: (f, 0))],
        out_specs=pl.BlockSpec((tm, d), row),
        compiler_params=_params(("parallel", "arbitrary"), est),
        name="ffn",
    )(x, xn, w_gate, w_up, w_down)


def _rope_tables(pos, repeat, g, scale):
    half = HEAD_DIM // 2
    inv = jnp.exp(-math.log(ROPE_THETA) * jnp.arange(half, dtype=F32) * (2.0 / HEAD_DIM))
    ang = pos.astype(F32)[:, None] * inv[None, :]
    cos = jnp.cos(ang)
    sin = jnp.sin(ang)
    g_swap = jnp.concatenate([g[half:], g[:half]])
    t_same = g[None, :] * jnp.concatenate([cos, cos], axis=-1) * scale
    t_swap = g_swap[None, :] * jnp.concatenate([-sin, sin], axis=-1) * scale
    reps = (repeat, LANES // HEAD_DIM)
    return jnp.tile(t_same, reps), jnp.tile(t_swap, reps)


LATE_WEIGHTS = ("w_attn_proj", "w_conv_proj", "w_out", "w_ffn_gate", "w_ffn_up", "w_ffn_down")


def _layer(x, w, *, batch, seq, pos, tiles, conv_prev, sample_cache):
    d = x.shape[1]
    n_heads = w["sinks"].shape[0]
    d_attn = n_heads * HEAD_DIM
    d_kv = w["d_kv"]
    n_kv = d_kv // HEAD_DIM
    group = n_heads // n_kv
    d_conv = w["conv_w"].shape[1]
    col_b = d_attn + 2 * d_kv
    col_c = col_b + d_conv
    col_h = col_c + d_conv
    col_ga = col_h + d_conv
    col_gc = col_ga + d
    tm = tiles["tm"]

    rope_rows = max(seq, tm)
    log2e = math.log2(math.e)
    tables = (_rope_tables(pos, rope_rows // seq, w["q_norm_g"], HEAD_DIM ** -0.5 * log2e)
              + _rope_tables(pos, rope_rows // seq, w["k_norm_g"], 1.0))
    sinks = w["sinks"] * log2e

    xn, q, k, v, k2, v2 = _qkv_proj(x, w["norm_mix_g"], w["w_in"], tables, tiles["kv_tail"],
                                    tm=tm, d_attn=d_attn, d_kv=d_kv, rows=tiles["qkv_rows"])
    late = [name for name in LATE_WEIGHTS if w[name].dtype != BF16]
    z, u_tail, *late_bf16 = _conv_branch(
        xn, w["w_in"], w["conv_w"], conv_prev,
        [w[name] for name in late],
        tm=tiles["tm_conv"], tn=tiles["tn_conv"], rows=tiles["conv_rows"], seq_len=seq,
        tail_rows=tiles["conv_tail"], col_b=col_b, col_c=col_c, col_h=col_h)
    w = dict(w, **dict(zip(late, late_bf16)))
    if sample_cache is None:
        attn = _attention(q, k2, v2, sinks, batch=batch, seq=seq, n_kv=n_kv, group=group)
    else:
        attn = _attention_sample(q, k2, v2, sample_cache[0], sample_cache[1], sinks,
                                 batch=batch, t_new=seq, n_kv=n_kv, group=group)
    merged = _merge(xn, attn, z, w["w_in"], w["w_attn_proj"], w["w_conv_proj"],
                    tm=tm, tn=tiles["tn_merge"], col_ga=col_ga, col_gc=col_gc)
    x1, xn_ffn = _out_proj(x, merged, w["w_out"], w["norm_ffn_g"],
                           tm=tiles["tm_out"], rows=tiles["out_rows"])
    y = _ffn(x1, xn_ffn, w["w_ffn_gate"], w["w_ffn_up"], w["w_ffn_down"],
             tm=tiles["tm_ffn"], tf=tiles["tf"], slab=V7X_MXU_DIM)
    return y, k, v, u_tail, w


def kernel(x_prompt, x_sample, cache_k, cache_v, state_conv, norm_mix_g, w_in, q_norm_g, k_norm_g,
           attn_sinks, conv_w, w_attn_proj, w_conv_proj, w_out, norm_ffn_g, w_ffn_gate, w_ffn_up,
           w_ffn_down):
    depth = w_in.shape[0]
    assert depth == 1
    b_p, s_p, d = x_prompt.shape
    b_s, s_s, _ = x_sample.shape
    n_kv = cache_k.shape[3]
    d_kv = n_kv * HEAD_DIM
    win = cache_k.shape[2]
    n_state = state_conv.shape[2]

    w = {
        "d_kv": d_kv,
        "norm_mix_g": norm_mix_g[0].reshape(1, d),
        "w_in": w_in[0].astype(BF16),
        "q_norm_g": q_norm_g[0],
        "k_norm_g": k_norm_g[0],
        "sinks": attn_sinks[0],
        "conv_w": conv_w[0],
        "w_attn_proj": w_attn_proj[0],
        "w_conv_proj": w_conv_proj[0],
        "w_out": w_out[0],
        "norm_ffn_g": norm_ffn_g[0].reshape(1, d),
        "w_ffn_gate": w_ffn_gate[0],
        "w_ffn_up": w_ffn_up[0],
        "w_ffn_down": w_ffn_down[0],
    }

    m_p = b_p * s_p
    tm_p = min(1024, s_p)
    keep = min(WINDOW, s_p)
    tiles_p = dict(tm=tm_p, qkv_rows=256, kv_tail=(s_p, keep), tm_conv=s_p, tn_conv=256, conv_tail=8,
                   conv_rows=min(512, s_p), tn_merge=512, tm_out=min(512, s_p), out_rows=256,
                   tm_ffn=tm_p, tf=512)
    y_p, k_p, v_p, u_p, w = _layer(x_prompt.reshape(m_p, d), w, batch=b_p, seq=s_p,
                                   pos=jnp.arange(s_p), tiles=tiles_p, conv_prev=None,
                                   sample_cache=None)
    new_k_p = k_p.reshape(b_p, n_kv, HEAD_DIM, keep).transpose(0, 3, 1, 2)[None]
    new_v_p = v_p.reshape(b_p, n_kv, HEAD_DIM, keep).transpose(0, 3, 1, 2)[None]
    new_c_p = u_p.reshape(b_p, tiles_p["conv_tail"], -1)[:, tiles_p["conv_tail"] - n_state:][None]

    m_s = b_s * s_s
    st = state_conv[0]
    prev1 = jnp.pad(st[:, 1:2], ((0, 0), (0, s_s - 1), (0, 0))).reshape(m_s, -1)
    prev2 = jnp.pad(st, ((0, 0), (0, s_s - n_state), (0, 0))).reshape(m_s, -1)
    tiles_s = dict(tm=m_s, qkv_rows=m_s, kv_tail=None, tm_conv=m_s, tn_conv=256, conv_tail=m_s,
                   conv_rows=m_s, tn_merge=256, tm_out=m_s, out_rows=m_s, tm_ffn=m_s, tf=512)
    y_s, k_s, v_s, u_s, _ = _layer(x_sample.reshape(m_s, d), w, batch=b_s, seq=s_s,
                                   pos=PAST_LEN + jnp.arange(s_s), tiles=tiles_s,
                                   conv_prev=(prev1, prev2),
                                   sample_cache=(cache_k[0].reshape(b_s * win, d_kv),
                                                 cache_v[0].reshape(b_s * win, d_kv)))
    new_k_s = k_s.reshape(b_s, s_s, n_kv, HEAD_DIM)[None]
    new_v_s = v_s.reshape(b_s, s_s, n_kv, HEAD_DIM)[None]
    new_c_s = u_s.reshape(b_s, s_s, -1)[:, s_s - n_state:][None]

    return (y_p.reshape(b_p, s_p, d), y_s.reshape(b_s, s_s, d), new_k_p, new_v_p, new_c_p,
            new_k_s, new_v_s, new_c_s)
```

```python
import functools
import math

import jax
import jax.numpy as jnp
from jax import lax
from jax.experimental import pallas as pl
from jax.experimental.pallas import tpu as pltpu

F32 = jnp.float32
BF16 = jnp.bfloat16

CHUNK = 64
WINDOW = 128
PAST_LEN = 1024
ROPE_THETA = 10000.0
EPS = 1e-6
NEG_INF = -1e30
HEAD_DIM = 64
LANES = 128
BF16_SUBLANES = 16
PAIR = 2 * CHUNK
KEY_SPAN = PAIR + WINDOW
PAIRS_PER_TRIP = 3

V7X_MXU_DIM = 256
V7X_VMEM_BYTES = 64 * 1024 * 1024
VMEM_CAP_BYTES = V7X_VMEM_BYTES - 6 * 1024 * 1024


def _params(semantics, vmem_estimate):
    limit = int(min(max(vmem_estimate, 16 * 1024 * 1024), VMEM_CAP_BYTES))
    return pltpu.CompilerParams(dimension_semantics=semantics, vmem_limit_bytes=limit)


def _nbytes(shape, dtype):
    return math.prod(shape) * jnp.dtype(dtype).itemsize


def _dot(a, b):
    return jnp.dot(a, b, preferred_element_type=F32)


def _dot_nt(a, b):
    return lax.dot_general(a, b, (((1,), (1,)), ((), ())), preferred_element_type=F32)


def _lane_iota(shape):
    return lax.broadcasted_iota(jnp.int32, shape, 1)


def _rmsnorm_rows(x, g):
    y = x * lax.rsqrt(jnp.mean(x * x, axis=-1, keepdims=True) + EPS)
    return y * g


def _headnorm_rope(y, t_same, t_swap, seg):
    width = y.shape[1]
    s = y * y
    s_hi = s.astype(BF16)
    s_lo = (s - s_hi.astype(F32)).astype(BF16)
    ms = _dot(s_hi, seg) + _dot(s_lo, seg)
    half = HEAD_DIM // 2
    ahead = pltpu.roll(y, width - half, axis=1)
    behind = pltpu.roll(y, half, axis=1)
    swapped = jnp.where((_lane_iota(y.shape) & (HEAD_DIM - 1)) < half, ahead, behind)
    reps = width // t_same.shape[1]
    t_same = jnp.concatenate([t_same] * reps, axis=1)
    t_swap = jnp.concatenate([t_swap] * reps, axis=1)
    return (y * t_same + swapped * t_swap) * lax.rsqrt(ms + EPS)


def _dup_halves(x):
    low = _lane_iota(x.shape) < HEAD_DIM
    r = pltpu.roll(x, HEAD_DIM, axis=1)
    return jnp.where(low, x, r), jnp.where(low, r, x)


def _isolate_halves(x):
    low = _lane_iota(x.shape) < HEAD_DIM
    r = pltpu.roll(x, HEAD_DIM, axis=1)
    return (jnp.where(low, x, 0.0), jnp.where(low, 0.0, r),
            jnp.where(low, r, 0.0), jnp.where(low, 0.0, x))


def _qkv_kernel(*refs, n_q_blocks, d_kv, rows, tail_keep, n_rest):
    x_refs = refs[:n_q_blocks]
    g_ref, w_ref, seg_ref, qs_ref, qw_ref, ks_ref, kw_ref = refs[n_q_blocks:n_q_blocks + 7]
    rest_refs = refs[n_q_blocks + 7:n_q_blocks + 7 + n_rest]
    xn_ref, q_ref, k_ref, v_ref, k2_ref, v2_ref = refs[n_q_blocks + 7 + n_rest:][:6]
    rest_out = refs[-1] if n_rest else None
    j = pl.program_id(1)
    tm, tn = q_ref.shape
    d = xn_ref.shape[1]
    xw = d // n_q_blocks
    chunks = [slice(r * rows, (r + 1) * rows) for r in range(tm // rows)]
    seg = seg_ref[...]
    slab = seg.shape[0]

    def q_block(normalize):
        for rs in chunks:
            if normalize:
                xs = [ref[rs, :] for ref in x_refs]
                ss = sum(jnp.sum(xc * xc, axis=-1, keepdims=True) for xc in xs)
                r = lax.rsqrt(ss * (1.0 / d) + EPS)
                for c, xc in enumerate(xs):
                    cs = slice(c * xw, (c + 1) * xw)
                    xn_ref[rs, cs] = (xc * r * g_ref[:, cs]).astype(BF16)
            y = _dot(xn_ref[rs, :], w_ref[...])
            for c in range(tn // slab):
                sl = slice(c * slab, (c + 1) * slab)
                q_ref[rs, sl] = _headnorm_rope(y[:, sl], qs_ref[rs, :], qw_ref[rs, :],
                                               seg).astype(BF16)
            if normalize:
                for n, ref in list(enumerate(rest_refs))[chunks.index(rs)::len(chunks)]:
                    width = ref.shape[1]
                    rest_out[:, n * width:(n + 1) * width] = ref[...].astype(BF16)

    pl.when(j == 0)(functools.partial(q_block, True))
    pl.when((j > 0) & (j < n_q_blocks))(functools.partial(q_block, False))

    @pl.when(j == n_q_blocks)
    def _():
        for rs in chunks:
            y = _dot(xn_ref[rs, :], w_ref[...])
            k_all = jnp.concatenate(
                [_headnorm_rope(y[:, c * slab:(c + 1) * slab], ks_ref[rs, :], kw_ref[rs, :], seg)
                 for c in range(d_kv // slab)], axis=1)
            if tail_keep is None:
                k_ref[rs, :] = k_all
                v_ref[rs, :] = y[:, d_kv:]
            elif rs is chunks[-1]:
                k_ref[0] = k_all[rows - tail_keep:].T
                v_ref[0] = y[rows - tail_keep:, d_kv:].T
            for c in range(d_kv // LANES):
                sl = slice(c * LANES, (c + 1) * LANES)
                k = k_all[:, sl]
                v = y[:, d_kv + c * LANES:d_kv + (c + 1) * LANES]
                va, vb = _dup_halves(v)
                lo = slice(2 * c * LANES, (2 * c + 1) * LANES)
                hi = slice((2 * c + 1) * LANES, (2 * c + 2) * LANES)
                v2_ref[rs, lo] = va.astype(BF16)
                v2_ref[rs, hi] = vb.astype(BF16)
                for n, piece in enumerate(_isolate_halves(k)):
                    k2_ref[rs, (4 * c + n) * LANES:(4 * c + n + 1) * LANES] = piece.astype(BF16)


def _qkv_proj(x, g, w_in, tables, tail, w_f32, *, tm, d_attn, d_kv, rows):
    m, d = x.shape
    tn = 2 * d_kv
    n_q_blocks = d_attn // tn
    assert m % tm == 0 and tm % rows == 0 and d_attn % tn == 0 and d % n_q_blocks == 0
    n_tiles = m // tm
    rest_specs, rest_shape, rest_out_spec, n_rest = [], (), (), 0
    if w_f32 is not None:
        qkv_cols = d_attn + tn
        n_rest = (w_f32.shape[1] - qkv_cols) // qkv_cols
        assert w_f32.shape[1] == (n_rest + 1) * qkv_cols and d % (n_tiles * BF16_SUBLANES) == 0
        slab_rows = d // n_tiles
        rest_specs = [pl.BlockSpec((slab_rows, qkv_cols), functools.partial(
            lambda n, i, j: (i, n + 1), n)) for n in range(n_rest)]
        rest_shape = (jax.ShapeDtypeStruct((d, n_rest * qkv_cols), BF16),)
        rest_out_spec = (pl.BlockSpec((slab_rows, n_rest * qkv_cols), lambda i, j: (i, 0)),)
    if tail is None:
        kv_shape = jax.ShapeDtypeStruct((m, d_kv), F32)
        kv_spec = pl.BlockSpec((tm, d_kv), lambda i, j: (i, 0))
    else:
        seq, keep = tail
        assert seq % tm == 0 and keep <= rows and keep % LANES == 0
        kv_shape = jax.ShapeDtypeStruct((m // seq, d_kv, keep), F32)
        kv_spec = pl.BlockSpec((1, d_kv, keep), lambda i, j: (i // (seq // tm), 0, 0))
    rope_blocks = tables[0].shape[0] // tm
    tab = pl.BlockSpec((tm, LANES), lambda i, j: (i % rope_blocks, 0))
    row_out = lambda width: pl.BlockSpec((tm, width), lambda i, j: (i, 0))
    last_tile = m // tm - 1

    def x_slab(c):
        return pl.BlockSpec((tm, d // n_q_blocks),
                            lambda i, j: (jnp.minimum(jnp.where(j > c, i + 1, i), last_tile), c))

    head_of = jnp.arange(V7X_MXU_DIM) // HEAD_DIM
    seg = jnp.where(head_of[:, None] == head_of[None, :], 1.0 / HEAD_DIM, 0.0).astype(BF16)
    est = (2 * (_nbytes((tm, d), F32) + _nbytes((tm, d), BF16) + _nbytes((d, tn), BF16)
                + 4 * _nbytes((tm, tn), BF16) + 2 * _nbytes((tm, d_kv), F32)
                + 4 * _nbytes((tm, LANES), F32))
           + 6 * _nbytes((rows, tn), F32) + 2 * _nbytes((rows, d), F32))
    if n_rest:
        est += 3 * _nbytes((d // n_tiles, n_rest * (d_attn + tn)), F32)
    return pl.pallas_call(
        functools.partial(_qkv_kernel, n_q_blocks=n_q_blocks, d_kv=d_kv, rows=rows,
                          tail_keep=None if tail is None else tail[1], n_rest=n_rest),
        out_shape=(jax.ShapeDtypeStruct((m, d), BF16), jax.ShapeDtypeStruct((m, d_attn), BF16),
                   kv_shape, kv_shape,
                   jax.ShapeDtypeStruct((m, 2 * tn), BF16), jax.ShapeDtypeStruct((m, tn), BF16),
                   *rest_shape),
        grid=(n_tiles, n_q_blocks + 1),
        in_specs=[x_slab(c) for c in range(n_q_blocks)] + [
                  pl.BlockSpec((1, d), lambda i, j: (0, 0)),
                  pl.BlockSpec((d, tn), lambda i, j: (0, j)),
                  pl.BlockSpec(seg.shape, lambda i, j: (0, 0)),
                  tab, tab, tab, tab] + rest_specs,
        out_specs=(row_out(d),
                   pl.BlockSpec((tm, tn), lambda i, j: (i, jnp.minimum(j, n_q_blocks - 1))),
                   kv_spec, kv_spec, row_out(2 * tn), row_out(tn), *rest_out_spec),
        compiler_params=_params(("arbitrary", "arbitrary"), est),
        name="qkv_proj",
    )(*([x] * n_q_blocks), g, w_in, seg, *tables, *([w_f32] * n_rest))


def _conv_kernel(*refs, seq_len, tail_rows, has_state, rows, cast_per_step):
    n_cast = len(cast_per_step)
    n_in = 7 if has_state else 5
    xn_ref, wb_ref, wc_ref, wh_ref, cw_ref = refs[:5]
    p1_ref, p2_ref = refs[5:n_in] if has_state else (None, None)
    z_ref, u_ref = refs[n_in + n_cast:n_in + n_cast + 2]
    casts = list(zip(refs[n_in:n_in + n_cast], refs[n_in + n_cast + 2:]))
    tm, tn = z_ref.shape
    w = cw_ref[...]
    row = lax.broadcasted_iota(jnp.int32, (rows, tn), 0)
    last1 = last2 = None
    for r in range(tm // rows):
        rs = slice(r * rows, (r + 1) * rows)
        xn = xn_ref[rs, :]
        u = _dot(xn, wc_ref[...]) * _dot(xn, wh_ref[...])
        u1 = pltpu.roll(u, 1, axis=0)
        u2 = pltpu.roll(u, 2, axis=0)
        if r > 0:
            u1 = jnp.where(row == 0, last1, u1)
            u2 = jnp.where(row == 0, last2, jnp.where(row == 1, last1, u2))
        if (r * rows) % seq_len == 0 or seq_len < rows:
            pos = (row + r * rows) & (seq_len - 1)
            u1 = jnp.where(pos == 0, p1_ref[rs, :] if has_state else 0.0, u1)
            u2 = jnp.where(pos < 2, p2_ref[rs, :] if has_state else 0.0, u2)
        y = u2 * w[0:1] + u1 * w[1:2] + u * w[2:3]
        z_ref[rs, :] = (_dot(xn, wb_ref[...]) * y).astype(BF16)
        last1, last2 = u[rows - 1:rows], u[rows - 2:rows - 1]
        for (src_ref, dst_ref), per_step in list(zip(casts, cast_per_step))[r::tm // rows]:
            if per_step:
                dst_ref[...] = src_ref[...].astype(BF16)
            else:
                @pl.when(pl.program_id(1) == 0)
                def _(src_ref=src_ref, dst_ref=dst_ref):
                    dst_ref[...] = src_ref[...].astype(BF16)
    u_ref[...] = u[rows - tail_rows:]


def _conv_branch(xn, w_in, conv_w, prev, cast, *, tm, tn, rows, seq_len, tail_rows,
                 col_b, col_c, col_h):
    m, d = xn.shape
    n = conv_w.shape[1]
    has_state = prev is not None
    assert seq_len & (seq_len - 1) == 0 and tm % seq_len == 0
    assert tm % rows == 0 and tail_rows <= rows and (rows % seq_len == 0 or seq_len % rows == 0)
    n_j = n // tn
    steps = (m // tm) * n_j
    wspec = lambda off: pl.BlockSpec((d, tn), lambda i, j: (0, off // tn + j))
    in_specs = [pl.BlockSpec((tm, d), lambda i, j: (i, 0)),
                wspec(col_b), wspec(col_c), wspec(col_h),
                pl.BlockSpec((conv_w.shape[0], tn), lambda i, j: (0, j))]
    args = [xn, w_in, w_in, w_in, conv_w]
    if has_state:
        in_specs += [pl.BlockSpec((tm, tn), lambda i, j: (i, j))] * 2
        args += list(prev)
    cast_per_step = tuple(c.shape[0] % (steps * BF16_SUBLANES) == 0 for c in cast)
    slab_specs, slab_bytes = [], 0
    for c, per_step in zip(cast, cast_per_step):
        parts = steps if per_step else m // tm
        assert c.shape[0] % (parts * BF16_SUBLANES) == 0
        index_map = (lambda i, j: (i * n_j + j, 0)) if per_step else (lambda i, j: (i, 0))
        slab_specs.append(pl.BlockSpec((c.shape[0] // parts, c.shape[1]), index_map))
        slab_bytes += 3 * _nbytes(c.shape, F32) // parts
    est = (2 * (_nbytes((tm, d), BF16) + 3 * _nbytes((d, tn), BF16) + _nbytes((tm, tn), BF16)
                + _nbytes((tail_rows, tn), F32)) + 8 * _nbytes((tm, tn), F32) + slab_bytes)
    return pl.pallas_call(
        functools.partial(_conv_kernel, seq_len=seq_len, tail_rows=tail_rows, has_state=has_state,
                          rows=rows, cast_per_step=cast_per_step),
        out_shape=(jax.ShapeDtypeStruct((m, n), BF16),
                   jax.ShapeDtypeStruct((m // tm * tail_rows, n), F32),
                   *[jax.ShapeDtypeStruct(c.shape, BF16) for c in cast]),
        grid=(m // tm, n_j),
        in_specs=in_specs + slab_specs,
        out_specs=(pl.BlockSpec((tm, tn), lambda i, j: (i, j)),
                   pl.BlockSpec((tail_rows, tn), lambda i, j: (i, j)),
                   *slab_specs),
        compiler_params=_params(("parallel", "arbitrary"), est),
        name="conv_branch",
    )(*args, *cast)


def _stack_heads(q_ref, rows, col0, group):
    parts = []
    for g in range(group):
        blk = q_ref[rows, col0 + (g // 2) * LANES:col0 + (g // 2 + 1) * LANES]
        keep = (_lane_iota(blk.shape) >= HEAD_DIM) == bool(g % 2)
        parts.append(jnp.where(keep, blk, jnp.zeros_like(blk)))
    return jnp.concatenate(parts, axis=0)


def _sink_column(sink_ref, head0, group, rows):
    return jnp.concatenate(
        [jnp.full((rows, 1), sink_ref[head0 + g], F32) for g in range(group)], axis=0)


def _store_heads(o_ref, rows, col0, o, group, nrows):
    for lp in range(group // 2):
        a = o[(2 * lp) * nrows:(2 * lp + 1) * nrows]
        b = o[(2 * lp + 1) * nrows:(2 * lp + 2) * nrows]
        low = _lane_iota(a.shape) < HEAD_DIM
        o_ref[rows, col0 + lp * LANES:col0 + (lp + 1) * LANES] = jnp.where(low, a, b).astype(BF16)


def _attn_pairs(pairs, sink_ref, q_ref, k_ref, v_ref, o_ref, n_kv, group):
    def block_rows(index):
        start = index * PAIR
        return pl.ds(start if isinstance(start, int) else pl.multiple_of(start, PAIR), PAIR)

    first_q = _lane_iota((CHUNK, PAIR)) < CHUNK

    def keys_of(ref, pair, has_prev, tile):
        cols = slice(tile * LANES, (tile + 1) * LANES)
        if has_prev:
            return jnp.concatenate([ref[block_rows(pair - 1), cols], ref[block_rows(pair), cols]],
                                   axis=0)
        return ref[block_rows(pair), cols]

    def scores(pair, has_prev, kv):
        col0 = kv * group * HEAD_DIM
        q_tiles = jnp.concatenate(
            [q_ref[block_rows(pair), col0 + lp * LANES:col0 + (lp + 1) * LANES]
             for lp in range(group // 2)], axis=0)
        return [_dot_nt(keys_of(k_ref, pair, has_prev, 2 * kv + half), q_tiles)
                for half in range(2)]

    items = [(pair, has_prev, kv) for pair, has_prev in pairs for kv in range(n_kv)]
    st_next = scores(*items[0])
    for n, (pair, has_prev, kv) in enumerate(items):
        st = st_next
        if n + 1 < len(items):
            st_next = scores(*items[n + 1])
        rows = block_rows(pair)
        vt = keys_of(v_ref, pair, has_prev, kv).astype(F32).T.astype(BF16)
        col0 = kv * group * HEAD_DIM
        probs, rden = [], []
        for g in range(group):
            sg = st[g % 2][:, (g // 2) * PAIR:(g // 2 + 1) * PAIR]
            blocks = [sg[c * CHUNK:(c + 1) * CHUNK] for c in range(sg.shape[0] // CHUNK)]
            if has_prev:
                s = jnp.concatenate([jnp.where(first_q, blocks[0], blocks[3]),
                                     blocks[1], blocks[2]], axis=0)
            else:
                s = jnp.concatenate([blocks[0], jnp.where(first_q, NEG_INF, blocks[1])], axis=0)
            sink = sink_ref[kv * group + g]
            m = jnp.maximum(jnp.max(s, axis=0, keepdims=True), sink)
            e = jnp.exp2(s - m)
            den = jnp.sum(e, axis=0, keepdims=True) + jnp.exp2(sink - m)
            if has_prev:
                e_03, e_12 = e[:CHUNK], e[CHUNK:]
                e = jnp.concatenate([jnp.where(first_q, e_03, 0.0), e_12,
                                     jnp.where(first_q, 0.0, e_03)], axis=0)
            probs.append(e.astype(BF16))
            rden.append(1.0 / den)
        ot = _dot(vt, jnp.concatenate(probs, axis=1)) * jnp.concatenate(rden, axis=1)
        for lp in range(group // 2):
            a = ot[:HEAD_DIM, (2 * lp) * PAIR:(2 * lp + 1) * PAIR]
            b = ot[HEAD_DIM:, (2 * lp + 1) * PAIR:(2 * lp + 2) * PAIR]
            o_ref[rows, col0 + lp * LANES:col0 + (lp + 1) * LANES] = (
                jnp.concatenate([a, b], axis=0).T.astype(BF16))


def _attn_kernel(sink_ref, q_ref, k_ref, v_ref, o_ref, *, n_kv, group):
    def run(pairs):
        _attn_pairs(pairs, sink_ref, q_ref, k_ref, v_ref, o_ref, n_kv, group)

    def trip(it, carry):
        run([(1 + PAIRS_PER_TRIP * it + u, True) for u in range(PAIRS_PER_TRIP)])
        return carry

    n_pairs = q_ref.shape[0] // PAIR
    trips = (n_pairs - 1) // PAIRS_PER_TRIP
    run([(0, False)])
    lax.fori_loop(0, trips, trip, 0)
    leftover = range(1 + trips * PAIRS_PER_TRIP, n_pairs)
    if leftover:
        run([(pair, True) for pair in leftover])


def _attention(q, k2, v2, sinks, *, batch, seq, n_kv, group):
    m, d = q.shape
    assert WINDOW == PAIR and seq % PAIR == 0
    est = (2 * (2 * _nbytes((seq, d), BF16) + 3 * _nbytes((seq, n_kv * LANES), BF16))
           + 8 * _nbytes((group * PAIR, KEY_SPAN), F32))
    return pl.pallas_call(
        functools.partial(_attn_kernel, n_kv=n_kv, group=group),
        out_shape=jax.ShapeDtypeStruct((m, d), BF16),
        grid=(batch,),
        in_specs=[pl.BlockSpec(memory_space=pltpu.SMEM),
                  pl.BlockSpec((seq, d), lambda b: (b, 0)),
                  pl.BlockSpec((seq, 2 * n_kv * LANES), lambda b: (b, 0)),
                  pl.BlockSpec((seq, n_kv * LANES), lambda b: (b, 0))],
        out_specs=pl.BlockSpec((seq, d), lambda b: (b, 0)),
        compiler_params=_params(("parallel",), est),
        name="attention",
    )(sinks, q, k2, v2)


def _attn_sample_kernel(sink_ref, q_ref, kn_ref, vn_ref, kc_ref, vc_ref, o_ref, *, n_kv, group):
    t = q_ref.shape[0]
    for kv in range(n_kv):
        pair_sl = slice((kv // 2) * LANES, (kv // 2 + 1) * LANES)
        kc = _dup_halves(kc_ref[:, pair_sl])[kv % 2].astype(BF16)
        vc = _dup_halves(vc_ref[:, pair_sl])[kv % 2].astype(BF16)
        kn = (kn_ref[:, 2 * kv * LANES:(2 * kv + 1) * LANES]
              + kn_ref[:, (2 * kv + 1) * LANES:(2 * kv + 2) * LANES])
        vn = vn_ref[:, kv * LANES:(kv + 1) * LANES]
        col0 = kv * group * HEAD_DIM
        lhs = _stack_heads(q_ref, slice(0, t), col0, group)
        s_c = _dot_nt(lhs, kc)
        s_n = _dot_nt(lhs, kn)
        sink = _sink_column(sink_ref, kv * group, group, t)
        m = jnp.maximum(jnp.maximum(jnp.max(s_c, axis=1, keepdims=True),
                                    jnp.max(s_n, axis=1, keepdims=True)), sink)
        e_c = jnp.exp2(s_c - m)
        e_n = jnp.exp2(s_n - m)
        den = (jnp.sum(e_c, axis=1, keepdims=True) + jnp.sum(e_n, axis=1, keepdims=True)
               + jnp.exp2(sink - m))
        o = (_dot(e_c.astype(BF16), vc) + _dot(e_n.astype(BF16), vn)) / den
        _store_heads(o_ref, slice(0, t), col0, o, group, t)


def _attention_sample(q, k2, v2, cache_k, cache_v, sinks, *, batch, t_new, n_kv, group):
    m, d = q.shape
    win = cache_k.shape[0] // batch
    d_kv = cache_k.shape[1]
    est = 4 * 1024 * 1024
    return pl.pallas_call(
        functools.partial(_attn_sample_kernel, n_kv=n_kv, group=group),
        out_shape=jax.ShapeDtypeStruct((m, d), BF16),
        grid=(batch,),
        in_specs=[pl.BlockSpec(memory_space=pltpu.SMEM),
                  pl.BlockSpec((t_new, d), lambda b: (b, 0)),
                  pl.BlockSpec((t_new, 2 * n_kv * LANES), lambda b: (b, 0)),
                  pl.BlockSpec((t_new, n_kv * LANES), lambda b: (b, 0)),
                  pl.BlockSpec((win, d_kv), lambda b: (b, 0)),
                  pl.BlockSpec((win, d_kv), lambda b: (b, 0))],
        out_specs=pl.BlockSpec((t_new, d), lambda b: (b, 0)),
        compiler_params=_params(("parallel",), est),
        name="attention_sample",
    )(sinks, q, k2, v2, cache_k, cache_v)


def _merge_kernel(xn_ref, a_ref, z_ref, wga_ref, wgc_ref, wap_ref, wcp_ref, o_ref):
    xn = xn_ref[...]
    attn = jax.nn.sigmoid(_dot(xn, wga_ref[...])) * _dot(a_ref[...], wap_ref[...])
    conv = jax.nn.sigmoid(_dot(xn, wgc_ref[...])) * _dot(z_ref[...], wcp_ref[...])
    o_ref[...] = (attn + conv).astype(BF16)


def _merge(xn, attn, z, w_in, w_ap, w_cp, *, tm, tn, col_ga, col_gc):
    m, d = xn.shape
    act = pl.BlockSpec((tm, d), lambda i, j: (i, 0))
    wspec = lambda off: pl.BlockSpec((d, tn), lambda i, j: (0, off // tn + j))
    est = (2 * (3 * _nbytes((tm, d), BF16) + 4 * _nbytes((d, tn), BF16) + _nbytes((tm, tn), BF16))
           + 6 * _nbytes((tm, tn), F32))
    return pl.pallas_call(
        _merge_kernel,
        out_shape=jax.ShapeDtypeStruct((m, d), BF16),
        grid=(m // tm, d // tn),
        in_specs=[act, act, act, wspec(col_ga), wspec(col_gc), wspec(0), wspec(0)],
        out_specs=pl.BlockSpec((tm, tn), lambda i, j: (i, j)),
        compiler_params=_params(("parallel", "arbitrary"), est),
        name="merge",
    )(xn, attn, z, w_in, w_in, w_ap, w_cp)


def _out_kernel(x_ref, m_ref, w_ref, g_ref, x1_ref, xn_ref, *, rows):
    for r in range(x_ref.shape[0] // rows):
        rs = slice(r * rows, (r + 1) * rows)
        x1 = x_ref[rs, :] + _dot(m_ref[rs, :], w_ref[...])
        x1_ref[rs, :] = x1
        xn_ref[rs, :] = _rmsnorm_rows(x1, g_ref[...]).astype(BF16)


def _out_proj(x, merged, w_out, g_ffn, *, tm, rows):
    m, d = x.shape
    assert m % tm == 0 and tm % rows == 0
    est = (2 * (2 * _nbytes((tm, d), F32) + 2 * _nbytes((tm, d), BF16)) + _nbytes((d, d), BF16)
           + 4 * _nbytes((rows, d), F32))
    return pl.pallas_call(
        functools.partial(_out_kernel, rows=rows),
        out_shape=(jax.ShapeDtypeStruct((m, d), F32), jax.ShapeDtypeStruct((m, d), BF16)),
        grid=(m // tm,),
        in_specs=[pl.BlockSpec((tm, d), lambda i: (i, 0)),
                  pl.BlockSpec((tm, d), lambda i: (i, 0)),
                  pl.BlockSpec((d, d), lambda i: (0, 0), pipeline_mode=pl.Buffered(1)),
                  pl.BlockSpec((1, d), lambda i: (0, 0))],
        out_specs=(pl.BlockSpec((tm, d), lambda i: (i, 0)),
                   pl.BlockSpec((tm, d), lambda i: (i, 0))),
        compiler_params=_params(("parallel",), est),
        name="out_proj",
    )(x, merged, w_out, g_ffn)


def _ffn_kernel(x_ref, xn_ref, wg_ref, wu_ref, wd_ref, o_ref):
    f = pl.program_id(1)
    slab = x_ref.shape[1]
    n_slabs = o_ref.shape[1] // slab

    @pl.when(f == 0)
    def _():
        o_ref[...] = jnp.zeros_like(o_ref)

    xn = xn_ref[...]
    h = jax.nn.silu(_dot(xn, wg_ref[...])) * _dot(xn, wu_ref[...])
    o_ref[...] += _dot(h.astype(BF16), wd_ref[...])

    for s in range(n_slabs):
        @pl.when(f == s)
        def _(s=s):
            o_ref[:, s * slab:(s + 1) * slab] += x_ref[...]


def _ffn(x, xn, w_gate, w_up, w_down, *, tm, tf, slab):
    m, d = x.shape
    d_ff = w_gate.shape[1]
    n_slabs = d // slab
    assert n_slabs <= d_ff // tf
    est = (2 * (_nbytes((tm, slab), F32) + _nbytes((tm, d), F32) + _nbytes((tm, d), BF16)
                + 3 * _nbytes((d, tf), BF16))
           + 4 * _nbytes((tm, tf), F32) + _nbytes((tm, d), F32))
    row = lambda i, f: (i, 0)
    return pl.pallas_call(
        _ffn_kernel,
        out_shape=jax.ShapeDtypeStruct((m, d), F32),
        grid=(m // tm, d_ff // tf),
        in_specs=[pl.BlockSpec((tm, slab), lambda i, f: (i, jnp.minimum(f, n_slabs - 1))),
                  pl.BlockSpec((tm, d), row),
                  pl.BlockSpec((d, tf), lambda i, f: (0, f)),
                  pl.BlockSpec((d, tf), lambda i, f: (0, f)),
                  pl.BlockSpec((tf, d), lambda i, f: (f, 0))],
        out_specs=pl.BlockSpec((tm, d), row),
        compiler_params=_params(("parallel", "arbitrary"), est),
        name="ffn",
    )(x, xn, w_gate, w_up, w_down)


def _rope_tables(pos, repeat, g, scale):
    half = HEAD_DIM // 2
    inv = jnp.exp(-math.log(ROPE_THETA) * jnp.arange(half, dtype=F32) * (2.0 / HEAD_DIM))
    ang = pos.astype(F32)[:, None] * inv[None, :]
    cos = jnp.cos(ang)
    sin = jnp.sin(ang)
    g_swap = jnp.concatenate([g[half:], g[:half]])
    t_same = g[None, :] * jnp.concatenate([cos, cos], axis=-1) * scale
    t_swap = g_swap[None, :] * jnp.concatenate([-sin, sin], axis=-1) * scale
    reps = (repeat, LANES // HEAD_DIM)
    return jnp.tile(t_same, reps), jnp.tile(t_swap, reps)


LATE_WEIGHTS = ("w_attn_proj", "w_conv_proj", "w_out", "w_ffn_gate", "w_ffn_up", "w_ffn_down")


def _layer(x, w, *, batch, seq, pos, tiles, conv_prev, sample_cache):
    d = x.shape[1]
    n_heads = w["sinks"].shape[0]
    d_attn = n_heads * HEAD_DIM
    d_kv = w["d_kv"]
    n_kv = d_kv // HEAD_DIM
    group = n_heads // n_kv
    d_conv = w["conv_w"].shape[1]
    col_b = 0
    col_c = col_b + d_conv
    col_h = col_c + d_conv
    col_ga = col_h + d_conv
    col_gc = col_ga + d
    tm = tiles["tm"]

    rope_rows = max(seq, tm)
    log2e = math.log2(math.e)
    tables = (_rope_tables(pos, rope_rows // seq, w["q_norm_g"], HEAD_DIM ** -0.5 * log2e)
              + _rope_tables(pos, rope_rows // seq, w["k_norm_g"], 1.0))
    sinks = w["sinks"] * log2e

    w_f32 = None if "w_rest" in w else w["w_in_f32"]
    xn, q, k, v, k2, v2, *w_rest = _qkv_proj(
        x, w["norm_mix_g"], w["w_qkv"], tables, tiles["kv_tail"], w_f32,
        tm=tm, d_attn=d_attn, d_kv=d_kv, rows=tiles["qkv_rows"])
    if w_rest:
        w = dict(w, w_rest=w_rest[0])
    late = [name for name in LATE_WEIGHTS if w[name].dtype != BF16]
    z, u_tail, *late_bf16 = _conv_branch(
        xn, w["w_rest"], w["conv_w"], conv_prev,
        [w[name] for name in late],
        tm=tiles["tm_conv"], tn=tiles["tn_conv"], rows=tiles["conv_rows"], seq_len=seq,
        tail_rows=tiles["conv_tail"], col_b=col_b, col_c=col_c, col_h=col_h)
    w = dict(w, **dict(zip(late, late_bf16)))
    if sample_cache is None:
        attn = _attention(q, k2, v2, sinks, batch=batch, seq=seq, n_kv=n_kv, group=group)
    else:
        attn = _attention_sample(q, k2, v2, sample_cache[0], sample_cache[1], sinks,
                                 batch=batch, t_new=seq, n_kv=n_kv, group=group)
    merged = _merge(xn, attn, z, w["w_rest"], w["w_attn_proj"], w["w_conv_proj"],
                    tm=tm, tn=tiles["tn_merge"], col_ga=col_ga, col_gc=col_gc)
    x1, xn_ffn = _out_proj(x, merged, w["w_out"], w["norm_ffn_g"],
                           tm=tiles["tm_out"], rows=tiles["out_rows"])
    y = _ffn(x1, xn_ffn, w["w_ffn_gate"], w["w_ffn_up"], w["w_ffn_down"],
             tm=tiles["tm_ffn"], tf=tiles["tf"], slab=V7X_MXU_DIM)
    return y, k, v, u_tail, w


def kernel(x_prompt, x_sample, cache_k, cache_v, state_conv, norm_mix_g, w_in, q_norm_g, k_norm_g,
           attn_sinks, conv_w, w_attn_proj, w_conv_proj, w_out, norm_ffn_g, w_ffn_gate, w_ffn_up,
           w_ffn_down):
    depth = w_in.shape[0]
    assert depth == 1
    b_p, s_p, d = x_prompt.shape
    b_s, s_s, _ = x_sample.shape
    n_kv = cache_k.shape[3]
    d_kv = n_kv * HEAD_DIM
    win = cache_k.shape[2]
    n_state = state_conv.shape[2]

    w = {
        "d_kv": d_kv,
        "norm_mix_g": norm_mix_g[0].reshape(1, d),
        "w_qkv": w_in[0][:, :attn_sinks.shape[1] * HEAD_DIM + 2 * d_kv].astype(BF16),
        "w_in_f32": w_in[0],
        "q_norm_g": q_norm_g[0],
        "k_norm_g": k_norm_g[0],
        "sinks": attn_sinks[0],
        "conv_w": conv_w[0],
        "w_attn_proj": w_attn_proj[0],
        "w_conv_proj": w_conv_proj[0],
        "w_out": w_out[0],
        "norm_ffn_g": norm_ffn_g[0].reshape(1, d),
        "w_ffn_gate": w_ffn_gate[0],
        "w_ffn_up": w_ffn_up[0],
        "w_ffn_down": w_ffn_down[0],
    }

    m_p = b_p * s_p
    tm_p = min(1024, s_p)
    keep = min(WINDOW, s_p)
    tiles_p = dict(tm=tm_p, qkv_rows=256, kv_tail=(s_p, keep), tm_conv=s_p, tn_conv=256, conv_tail=8,
                   conv_rows=min(512, s_p), tn_merge=512, tm_out=min(512, s_p), out_rows=256,
                   tm_ffn=tm_p, tf=512)
    y_p, k_p, v_p, u_p, w = _layer(x_prompt.reshape(m_p, d), w, batch=b_p, seq=s_p,
                                   pos=jnp.arange(s_p), tiles=tiles_p, conv_prev=None,
                                   sample_cache=None)
    new_k_p = k_p.reshape(b_p, n_kv, HEAD_DIM, keep).transpose(0, 3, 1, 2)[None]
    new_v_p = v_p.reshape(b_p, n_kv, HEAD_DIM, keep).transpose(0, 3, 1, 2)[None]
    new_c_p = u_p.reshape(b_p, tiles_p["conv_tail"], -1)[:, tiles_p["conv_tail"] - n_state:][None]

    m_s = b_s * s_s
    st = state_conv[0]
    prev1 = jnp.pad(st[:, 1:2], ((0, 0), (0, s_s - 1), (0, 0))).reshape(m_s, -1)
    prev2 = jnp.pad(st, ((0, 0), (0, s_s - n_state), (0, 0))).reshape(m_s, -1)
    tiles_s = dict(tm=m_s, qkv_rows=m_s, kv_tail=None, tm_conv=m_s, tn_conv=256, conv_tail=m_s,
                   conv_rows=m_s, tn_merge=256, tm_out=m_s, out_rows=m_s, tm_ffn=m_s, tf=512)
    y_s, k_s, v_s, u_s, _ = _layer(x_sample.reshape(m_s, d), w, batch=b_s, seq=s_s,
                                   pos=PAST_LEN + jnp.arange(s_s), tiles=tiles_s,
                                   conv_prev=(prev1, prev2),
                                   sample_cache=(cache_k[0].reshape(b_s * win, d_kv),
                                                 cache_v[0].reshape(b_s * win, d_kv)))
    new_k_s = k_s.reshape(b_s, s_s, n_kv, HEAD_DIM)[None]
    new_v_s = v_s.reshape(b_s, s_s, n_kv, HEAD_DIM)[None]
    new_c_s = u_s.reshape(b_s, s_s, -1)[:, s_s - n_state:][None]

    return (y_p.reshape(b_p, s_p, d), y_s.reshape(b_s, s_s, d), new_k_p, new_v_p, new_c_p,
            new_k_s, new_v_s, new_c_s)
```

```python
import functools
import math

import jax
import jax.numpy as jnp
from jax import lax
from jax.experimental import pallas as pl
from jax.experimental.pallas import tpu as pltpu

F32 = jnp.float32
BF16 = jnp.bfloat16

CHUNK = 64
WINDOW = 128
PAST_LEN = 1024
ROPE_THETA = 10000.0
EPS = 1e-6
NEG_INF = -1e30
HEAD_DIM = 64
LANES = 128
BF16_SUBLANES = 16
PAIR = 2 * CHUNK
KEY_SPAN = PAIR + WINDOW
PAIRS_PER_TRIP = 3

V7X_MXU_DIM = 256
V7X_VMEM_BYTES = 64 * 1024 * 1024
VMEM_CAP_BYTES = V7X_VMEM_BYTES - 6 * 1024 * 1024


def _params(semantics, vmem_estimate):
    limit = int(min(max(vmem_estimate, 16 * 1024 * 1024), VMEM_CAP_BYTES))
    return pltpu.CompilerParams(dimension_semantics=semantics, vmem_limit_bytes=limit)


def _nbytes(shape, dtype):
    return math.prod(shape) * jnp.dtype(dtype).itemsize


def _dot(a, b):
    return jnp.dot(a, b, preferred_element_type=F32)


def _dot_nt(a, b):
    return lax.dot_general(a, b, (((1,), (1,)), ((), ())), preferred_element_type=F32)


def _lane_iota(shape):
    return lax.broadcasted_iota(jnp.int32, shape, 1)


def _rmsnorm_rows(x, g):
    y = x * lax.rsqrt(jnp.mean(x * x, axis=-1, keepdims=True) + EPS)
    return y * g


def _headnorm_rope(y, t_same, t_swap, seg):
    width = y.shape[1]
    s = y * y
    s_hi = s.astype(BF16)
    s_lo = (s - s_hi.astype(F32)).astype(BF16)
    ms = _dot(s_hi, seg) + _dot(s_lo, seg)
    half = HEAD_DIM // 2
    ahead = pltpu.roll(y, width - half, axis=1)
    behind = pltpu.roll(y, half, axis=1)
    swapped = jnp.where((_lane_iota(y.shape) & (HEAD_DIM - 1)) < half, ahead, behind)
    reps = width // t_same.shape[1]
    t_same = jnp.concatenate([t_same] * reps, axis=1)
    t_swap = jnp.concatenate([t_swap] * reps, axis=1)
    return (y * t_same + swapped * t_swap) * lax.rsqrt(ms + EPS)


def _dup_halves(x):
    low = _lane_iota(x.shape) < HEAD_DIM
    r = pltpu.roll(x, HEAD_DIM, axis=1)
    return jnp.where(low, x, r), jnp.where(low, r, x)


def _isolate_halves(x):
    low = _lane_iota(x.shape) < HEAD_DIM
    r = pltpu.roll(x, HEAD_DIM, axis=1)
    return (jnp.where(low, x, 0.0), jnp.where(low, 0.0, r),
            jnp.where(low, r, 0.0), jnp.where(low, 0.0, x))


def _qkv_kernel(*refs, n_q_blocks, d_kv, rows, tail_keep, n_rest):
    x_refs = refs[:n_q_blocks]
    g_ref, w_ref, seg_ref, qs_ref, qw_ref, ks_ref, kw_ref = refs[n_q_blocks:n_q_blocks + 7]
    rest_refs = refs[n_q_blocks + 7:n_q_blocks + 7 + n_rest]
    xn_ref, q_ref, k_ref, v_ref, k2_ref, v2_ref = refs[n_q_blocks + 7 + n_rest:][:6]
    rest_out = refs[-1] if n_rest else None
    j = pl.program_id(1)
    tm, tn = q_ref.shape
    d = xn_ref.shape[1]
    xw = d // n_q_blocks
    chunks = [slice(r * rows, (r + 1) * rows) for r in range(tm // rows)]
    seg = seg_ref[...]
    slab = seg.shape[0]

    def q_block(normalize):
        for rs in chunks:
            if normalize:
                xs = [ref[rs, :] for ref in x_refs]
                ss = sum(jnp.sum(xc * xc, axis=-1, keepdims=True) for xc in xs)
                r = lax.rsqrt(ss * (1.0 / d) + EPS)
                for c, xc in enumerate(xs):
                    cs = slice(c * xw, (c + 1) * xw)
                    xn_ref[rs, cs] = (xc * r * g_ref[:, cs]).astype(BF16)
            y = _dot(xn_ref[rs, :], w_ref[...])
            for c in range(tn // slab):
                sl = slice(c * slab, (c + 1) * slab)
                q_ref[rs, sl] = _headnorm_rope(y[:, sl], qs_ref[rs, :], qw_ref[rs, :],
                                               seg).astype(BF16)
            if normalize:
                for n, ref in list(enumerate(rest_refs))[chunks.index(rs)::len(chunks)]:
                    width = ref.shape[1]
                    rest_out[:, n * width:(n + 1) * width] = ref[...].astype(BF16)

    pl.when(j == 0)(functools.partial(q_block, True))
    pl.when((j > 0) & (j < n_q_blocks))(functools.partial(q_block, False))

    @pl.when(j == n_q_blocks)
    def _():
        for rs in chunks:
            y = _dot(xn_ref[rs, :], w_ref[...])
            k_all = jnp.concatenate(
                [_headnorm_rope(y[:, c * slab:(c + 1) * slab], ks_ref[rs, :], kw_ref[rs, :], seg)
                 for c in range(d_kv // slab)], axis=1)
            if tail_keep is None:
                k_ref[rs, :] = k_all
                v_ref[rs, :] = y[:, d_kv:]
            elif rs is chunks[-1]:
                k_ref[0] = k_all[rows - tail_keep:].T
                v_ref[0] = y[rows - tail_keep:, d_kv:].T
            for c in range(d_kv // LANES):
                sl = slice(c * LANES, (c + 1) * LANES)
                k = k_all[:, sl]
                v = y[:, d_kv + c * LANES:d_kv + (c + 1) * LANES]
                va, vb = _dup_halves(v)
                lo = slice(2 * c * LANES, (2 * c + 1) * LANES)
                hi = slice((2 * c + 1) * LANES, (2 * c + 2) * LANES)
                v2_ref[rs, lo] = va.astype(BF16)
                v2_ref[rs, hi] = vb.astype(BF16)
                for n, piece in enumerate(_isolate_halves(k)):
                    k2_ref[rs, (4 * c + n) * LANES:(4 * c + n + 1) * LANES] = piece.astype(BF16)


def _qkv_proj(x, g, w_in, tables, tail, w_f32, *, tm, d_attn, d_kv, rows):
    m, d = x.shape
    tn = 2 * d_kv
    n_q_blocks = d_attn // tn
    assert m % tm == 0 and tm % rows == 0 and d_attn % tn == 0 and d % n_q_blocks == 0
    n_tiles = m // tm
    rest_specs, rest_shape, rest_out_spec, n_rest = [], (), (), 0
    if w_f32 is not None:
        qkv_cols = d_attn + tn
        n_rest = (w_f32.shape[1] - qkv_cols) // qkv_cols
        assert w_f32.shape[1] == (n_rest + 1) * qkv_cols and d % (n_tiles * BF16_SUBLANES) == 0
        slab_rows = d // n_tiles
        rest_specs = [pl.BlockSpec((slab_rows, qkv_cols), functools.partial(
            lambda n, i, j: (i, n + 1), n)) for n in range(n_rest)]
        rest_shape = (jax.ShapeDtypeStruct((d, n_rest * qkv_cols), BF16),)
        rest_out_spec = (pl.BlockSpec((slab_rows, n_rest * qkv_cols), lambda i, j: (i, 0)),)
    if tail is None:
        kv_shape = jax.ShapeDtypeStruct((m, d_kv), F32)
        kv_spec = pl.BlockSpec((tm, d_kv), lambda i, j: (i, 0))
    else:
        seq, keep = tail
        assert seq % tm == 0 and keep <= rows and keep % LANES == 0
        kv_shape = jax.ShapeDtypeStruct((m // seq, d_kv, keep), F32)
        kv_spec = pl.BlockSpec((1, d_kv, keep), lambda i, j: (i // (seq // tm), 0, 0))
    rope_blocks = tables[0].shape[0] // tm
    tab = pl.BlockSpec((tm, LANES), lambda i, j: (i % rope_blocks, 0))
    row_out = lambda width: pl.BlockSpec((tm, width), lambda i, j: (i, 0))
    last_tile = m // tm - 1

    def x_slab(c):
        return pl.BlockSpec((tm, d // n_q_blocks),
                            lambda i, j: (jnp.minimum(jnp.where(j > c, i + 1, i), last_tile), c))

    head_of = jnp.arange(V7X_MXU_DIM) // HEAD_DIM
    seg = jnp.where(head_of[:, None] == head_of[None, :], 1.0 / HEAD_DIM, 0.0).astype(BF16)
    est = (2 * (_nbytes((tm, d), F32) + _nbytes((tm, d), BF16) + _nbytes((d, tn), BF16)
                + 4 * _nbytes((tm, tn), BF16) + 2 * _nbytes((tm, d_kv), F32)
                + 4 * _nbytes((tm, LANES), F32))
           + 6 * _nbytes((rows, tn), F32) + 2 * _nbytes((rows, d), F32))
    if n_rest:
        est += 3 * _nbytes((d // n_tiles, n_rest * (d_attn + tn)), F32)
    return pl.pallas_call(
        functools.partial(_qkv_kernel, n_q_blocks=n_q_blocks, d_kv=d_kv, rows=rows,
                          tail_keep=None if tail is None else tail[1], n_rest=n_rest),
        out_shape=(jax.ShapeDtypeStruct((m, d), BF16), jax.ShapeDtypeStruct((m, d_attn), BF16),
                   kv_shape, kv_shape,
                   jax.ShapeDtypeStruct((m, 2 * tn), BF16), jax.ShapeDtypeStruct((m, tn), BF16),
                   *rest_shape),
        grid=(n_tiles, n_q_blocks + 1),
        in_specs=[x_slab(c) for c in range(n_q_blocks)] + [
                  pl.BlockSpec((1, d), lambda i, j: (0, 0)),
                  pl.BlockSpec((d, tn), lambda i, j: (0, j)),
                  pl.BlockSpec(seg.shape, lambda i, j: (0, 0)),
                  tab, tab, tab, tab] + rest_specs,
        out_specs=(row_out(d),
                   pl.BlockSpec((tm, tn), lambda i, j: (i, jnp.minimum(j, n_q_blocks - 1))),
                   kv_spec, kv_spec, row_out(2 * tn), row_out(tn), *rest_out_spec),
        compiler_params=_params(("arbitrary", "arbitrary"), est),
        name="qkv_proj",
    )(*([x] * n_q_blocks), g, w_in, seg, *tables, *([w_f32] * n_rest))


def _conv_kernel(*refs, seq_len, tail_rows, has_state, rows, cast_per_step):
    n_cast = len(cast_per_step)
    n_in = 7 if has_state else 5
    xn_ref, wb_ref, wc_ref, wh_ref, cw_ref = refs[:5]
    p1_ref, p2_ref = refs[5:n_in] if has_state else (None, None)
    z_ref, u_ref = refs[n_in + n_cast:n_in + n_cast + 2]
    casts = list(zip(refs[n_in:n_in + n_cast], refs[n_in + n_cast + 2:]))
    tm, tn = z_ref.shape
    w = cw_ref[...]
    row = lax.broadcasted_iota(jnp.int32, (rows, tn), 0)
    last1 = last2 = None
    for r in range(tm // rows):
        rs = slice(r * rows, (r + 1) * rows)
        xn = xn_ref[rs, :]
        u = _dot(xn, wc_ref[...]) * _dot(xn, wh_ref[...])
        u1 = pltpu.roll(u, 1, axis=0)
        u2 = pltpu.roll(u, 2, axis=0)
        if r > 0:
            u1 = jnp.where(row == 0, last1, u1)
            u2 = jnp.where(row == 0, last2, jnp.where(row == 1, last1, u2))
        if (r * rows) % seq_len == 0 or seq_len < rows:
            pos = (row + r * rows) & (seq_len - 1)
            u1 = jnp.where(pos == 0, p1_ref[rs, :] if has_state else 0.0, u1)
            u2 = jnp.where(pos < 2, p2_ref[rs, :] if has_state else 0.0, u2)
        y = u2 * w[0:1] + u1 * w[1:2] + u * w[2:3]
        z_ref[rs, :] = (_dot(xn, wb_ref[...]) * y).astype(BF16)
        last1, last2 = u[rows - 1:rows], u[rows - 2:rows - 1]
        for (src_ref, dst_ref), per_step in list(zip(casts, cast_per_step))[r::tm // rows]:
            if per_step:
                _copy_slab(src_ref, dst_ref)
            else:
                pl.when(pl.program_id(1) == 0)(functools.partial(_copy_slab, src_ref, dst_ref))
    u_ref[...] = u[rows - tail_rows:]


def _copy_slab(src_ref, dst_ref):
    if len(dst_ref.shape) == 2:
        dst_ref[...] = src_ref[...].astype(BF16)
    else:
        cb = dst_ref.shape[2]
        for b in range(dst_ref.shape[0]):
            dst_ref[b] = src_ref[:, b * cb:(b + 1) * cb].astype(BF16)


def _conv_branch(xn, w_in, conv_w, prev, cast, *, tm, tn, rows, seq_len, tail_rows,
                 col_b, col_c, col_h):
    m, d = xn.shape
    n = conv_w.shape[1]
    has_state = prev is not None
    assert seq_len & (seq_len - 1) == 0 and tm % seq_len == 0
    assert tm % rows == 0 and tail_rows <= rows and (rows % seq_len == 0 or seq_len % rows == 0)
    n_j = n // tn
    steps = (m // tm) * n_j
    wspec = lambda off: pl.BlockSpec((d, tn), lambda i, j: (0, off // tn + j))
    in_specs = [pl.BlockSpec((tm, d), lambda i, j: (i, 0)),
                wspec(col_b), wspec(col_c), wspec(col_h),
                pl.BlockSpec((conv_w.shape[0], tn), lambda i, j: (0, j))]
    args = [xn, w_in, w_in, w_in, conv_w]
    if has_state:
        in_specs += [pl.BlockSpec((tm, tn), lambda i, j: (i, j))] * 2
        args += list(prev)
    cast_per_step = tuple(c.shape[0] % (steps * BF16_SUBLANES) == 0 for c, _ in cast)
    slab_specs, copy_specs, copy_shapes, slab_bytes = [], [], [], 0
    for (c, col_block), per_step in zip(cast, cast_per_step):
        parts = steps if per_step else m // tm
        r, cols = c.shape
        assert r % (parts * BF16_SUBLANES) == 0
        step_of = (lambda i, j: i * n_j + j) if per_step else (lambda i, j: i)
        slab_specs.append(pl.BlockSpec((r // parts, cols),
                                       functools.partial(lambda s, i, j: (s(i, j), 0), step_of)))
        if col_block is None:
            copy_specs.append(slab_specs[-1])
            copy_shapes.append(jax.ShapeDtypeStruct((r, cols), BF16))
        else:
            assert cols % col_block == 0
            copy_specs.append(pl.BlockSpec(
                (cols // col_block, r // parts, col_block),
                functools.partial(lambda s, i, j: (0, s(i, j), 0), step_of)))
            copy_shapes.append(jax.ShapeDtypeStruct((cols // col_block, r, col_block), BF16))
        slab_bytes += 3 * _nbytes(c.shape, F32) // parts
    est = (2 * (_nbytes((tm, d), BF16) + 3 * _nbytes((d, tn), BF16) + _nbytes((tm, tn), BF16)
                + _nbytes((tail_rows, tn), F32)) + 8 * _nbytes((tm, tn), F32) + slab_bytes)
    return pl.pallas_call(
        functools.partial(_conv_kernel, seq_len=seq_len, tail_rows=tail_rows, has_state=has_state,
                          rows=rows, cast_per_step=cast_per_step),
        out_shape=(jax.ShapeDtypeStruct((m, n), BF16),
                   jax.ShapeDtypeStruct((m // tm * tail_rows, n), F32),
                   *copy_shapes),
        grid=(m // tm, n_j),
        in_specs=in_specs + slab_specs,
        out_specs=(pl.BlockSpec((tm, tn), lambda i, j: (i, j)),
                   pl.BlockSpec((tail_rows, tn), lambda i, j: (i, j)),
                   *copy_specs),
        compiler_params=_params(("parallel", "arbitrary"), est),
        name="conv_branch",
    )(*args, *[c for c, _ in cast])


def _stack_heads(q_ref, rows, col0, group):
    parts = []
    for g in range(group):
        blk = q_ref[rows, col0 + (g // 2) * LANES:col0 + (g // 2 + 1) * LANES]
        keep = (_lane_iota(blk.shape) >= HEAD_DIM) == bool(g % 2)
        parts.append(jnp.where(keep, blk, jnp.zeros_like(blk)))
    return jnp.concatenate(parts, axis=0)


def _sink_column(sink_ref, head0, group, rows):
    return jnp.concatenate(
        [jnp.full((rows, 1), sink_ref[head0 + g], F32) for g in range(group)], axis=0)


def _store_heads(o_ref, rows, col0, o, group, nrows):
    for lp in range(group // 2):
        a = o[(2 * lp) * nrows:(2 * lp + 1) * nrows]
        b = o[(2 * lp + 1) * nrows:(2 * lp + 2) * nrows]
        low = _lane_iota(a.shape) < HEAD_DIM
        o_ref[rows, col0 + lp * LANES:col0 + (lp + 1) * LANES] = jnp.where(low, a, b).astype(BF16)


def _attn_pairs(pairs, sink_ref, q_ref, k_ref, v_ref, o_ref, n_kv, group):
    def block_rows(index):
        start = index * PAIR
        return pl.ds(start if isinstance(start, int) else pl.multiple_of(start, PAIR), PAIR)

    first_q = _lane_iota((CHUNK, PAIR)) < CHUNK

    def keys_of(ref, pair, has_prev, tile):
        cols = slice(tile * LANES, (tile + 1) * LANES)
        if has_prev:
            return jnp.concatenate([ref[block_rows(pair - 1), cols], ref[block_rows(pair), cols]],
                                   axis=0)
        return ref[block_rows(pair), cols]

    def scores(pair, has_prev, kv):
        col0 = kv * group * HEAD_DIM
        q_tiles = jnp.concatenate(
            [q_ref[block_rows(pair), col0 + lp * LANES:col0 + (lp + 1) * LANES]
             for lp in range(group // 2)], axis=0)
        return [_dot_nt(keys_of(k_ref, pair, has_prev, 2 * kv + half), q_tiles)
                for half in range(2)]

    items = [(pair, has_prev, kv) for pair, has_prev in pairs for kv in range(n_kv)]
    st_next = scores(*items[0])
    for n, (pair, has_prev, kv) in enumerate(items):
        st = st_next
        if n + 1 < len(items):
            st_next = scores(*items[n + 1])
        rows = block_rows(pair)
        vt = keys_of(v_ref, pair, has_prev, kv).astype(F32).T.astype(BF16)
        col0 = kv * group * HEAD_DIM
        probs, rden = [], []
        for g in range(group):
            sg = st[g % 2][:, (g // 2) * PAIR:(g // 2 + 1) * PAIR]
            blocks = [sg[c * CHUNK:(c + 1) * CHUNK] for c in range(sg.shape[0] // CHUNK)]
            if has_prev:
                s = jnp.concatenate([jnp.where(first_q, blocks[0], blocks[3]),
                                     blocks[1], blocks[2]], axis=0)
            else:
                s = jnp.concatenate([blocks[0], jnp.where(first_q, NEG_INF, blocks[1])], axis=0)
            sink = sink_ref[kv * group + g]
            m = jnp.maximum(jnp.max(s, axis=0, keepdims=True), sink)
            e = jnp.exp2(s - m)
            den = jnp.sum(e, axis=0, keepdims=True) + jnp.exp2(sink - m)
            if has_prev:
                e_03, e_12 = e[:CHUNK], e[CHUNK:]
                e = jnp.concatenate([jnp.where(first_q, e_03, 0.0), e_12,
                                     jnp.where(first_q, 0.0, e_03)], axis=0)
            probs.append(e.astype(BF16))
            rden.append(1.0 / den)
        ot = _dot(vt, jnp.concatenate(probs, axis=1)) * jnp.concatenate(rden, axis=1)
        for lp in range(group // 2):
            a = ot[:HEAD_DIM, (2 * lp) * PAIR:(2 * lp + 1) * PAIR]
            b = ot[HEAD_DIM:, (2 * lp + 1) * PAIR:(2 * lp + 2) * PAIR]
            o_ref[rows, col0 + lp * LANES:col0 + (lp + 1) * LANES] = (
                jnp.concatenate([a, b], axis=0).T.astype(BF16))


def _attn_kernel(sink_ref, q_ref, k_ref, v_ref, o_ref, *, n_kv, group):
    def run(pairs):
        _attn_pairs(pairs, sink_ref, q_ref, k_ref, v_ref, o_ref, n_kv, group)

    def trip(it, carry):
        run([(1 + PAIRS_PER_TRIP * it + u, True) for u in range(PAIRS_PER_TRIP)])
        return carry

    n_pairs = q_ref.shape[0] // PAIR
    trips = (n_pairs - 1) // PAIRS_PER_TRIP
    run([(0, False)])
    lax.fori_loop(0, trips, trip, 0)
    leftover = range(1 + trips * PAIRS_PER_TRIP, n_pairs)
    if leftover:
        run([(pair, True) for pair in leftover])


def _attention(q, k2, v2, sinks, *, batch, seq, n_kv, group):
    m, d = q.shape
    assert WINDOW == PAIR and seq % PAIR == 0
    est = (2 * (2 * _nbytes((seq, d), BF16) + 3 * _nbytes((seq, n_kv * LANES), BF16))
           + 8 * _nbytes((group * PAIR, KEY_SPAN), F32))
    return pl.pallas_call(
        functools.partial(_attn_kernel, n_kv=n_kv, group=group),
        out_shape=jax.ShapeDtypeStruct((m, d), BF16),
        grid=(batch,),
        in_specs=[pl.BlockSpec(memory_space=pltpu.SMEM),
                  pl.BlockSpec((seq, d), lambda b: (b, 0)),
                  pl.BlockSpec((seq, 2 * n_kv * LANES), lambda b: (b, 0)),
                  pl.BlockSpec((seq, n_kv * LANES), lambda b: (b, 0))],
        out_specs=pl.BlockSpec((seq, d), lambda b: (b, 0)),
        compiler_params=_params(("parallel",), est),
        name="attention",
    )(sinks, q, k2, v2)


def _attn_sample_kernel(sink_ref, q_ref, kn_ref, vn_ref, kc_ref, vc_ref, o_ref, *, n_kv, group):
    t = q_ref.shape[0]
    for kv in range(n_kv):
        pair_sl = slice((kv // 2) * LANES, (kv // 2 + 1) * LANES)
        kc = _dup_halves(kc_ref[:, pair_sl])[kv % 2].astype(BF16)
        vc = _dup_halves(vc_ref[:, pair_sl])[kv % 2].astype(BF16)
        kn = (kn_ref[:, 2 * kv * LANES:(2 * kv + 1) * LANES]
              + kn_ref[:, (2 * kv + 1) * LANES:(2 * kv + 2) * LANES])
        vn = vn_ref[:, kv * LANES:(kv + 1) * LANES]
        col0 = kv * group * HEAD_DIM
        lhs = _stack_heads(q_ref, slice(0, t), col0, group)
        s_c = _dot_nt(lhs, kc)
        s_n = _dot_nt(lhs, kn)
        sink = _sink_column(sink_ref, kv * group, group, t)
        m = jnp.maximum(jnp.maximum(jnp.max(s_c, axis=1, keepdims=True),
                                    jnp.max(s_n, axis=1, keepdims=True)), sink)
        e_c = jnp.exp2(s_c - m)
        e_n = jnp.exp2(s_n - m)
        den = (jnp.sum(e_c, axis=1, keepdims=True) + jnp.sum(e_n, axis=1, keepdims=True)
               + jnp.exp2(sink - m))
        o = (_dot(e_c.astype(BF16), vc) + _dot(e_n.astype(BF16), vn)) / den
        _store_heads(o_ref, slice(0, t), col0, o, group, t)


def _attention_sample(q, k2, v2, cache_k, cache_v, sinks, *, batch, t_new, n_kv, group):
    m, d = q.shape
    win = cache_k.shape[0] // batch
    d_kv = cache_k.shape[1]
    est = 4 * 1024 * 1024
    return pl.pallas_call(
        functools.partial(_attn_sample_kernel, n_kv=n_kv, group=group),
        out_shape=jax.ShapeDtypeStruct((m, d), BF16),
        grid=(batch,),
        in_specs=[pl.BlockSpec(memory_space=pltpu.SMEM),
                  pl.BlockSpec((t_new, d), lambda b: (b, 0)),
                  pl.BlockSpec((t_new, 2 * n_kv * LANES), lambda b: (b, 0)),
                  pl.BlockSpec((t_new, n_kv * LANES), lambda b: (b, 0)),
                  pl.BlockSpec((win, d_kv), lambda b: (b, 0)),
                  pl.BlockSpec((win, d_kv), lambda b: (b, 0))],
        out_specs=pl.BlockSpec((t_new, d), lambda b: (b, 0)),
        compiler_params=_params(("parallel",), est),
        name="attention_sample",
    )(sinks, q, k2, v2, cache_k, cache_v)


def _merge_kernel(xn_ref, a_ref, z_ref, wga_ref, wgc_ref, wap_ref, wcp_ref, o_ref):
    xn = xn_ref[...]
    attn = jax.nn.sigmoid(_dot(xn, wga_ref[...])) * _dot(a_ref[...], wap_ref[...])
    conv = jax.nn.sigmoid(_dot(xn, wgc_ref[...])) * _dot(z_ref[...], wcp_ref[...])
    o_ref[...] = (attn + conv).astype(BF16)


def _merge(xn, attn, z, w_in, w_ap, w_cp, *, tm, tn, col_ga, col_gc):
    m, d = xn.shape
    act = pl.BlockSpec((tm, d), lambda i, j: (i, 0))
    wspec = lambda off: pl.BlockSpec((d, tn), lambda i, j: (0, off // tn + j))
    est = (2 * (3 * _nbytes((tm, d), BF16) + 4 * _nbytes((d, tn), BF16) + _nbytes((tm, tn), BF16))
           + 6 * _nbytes((tm, tn), F32))
    return pl.pallas_call(
        _merge_kernel,
        out_shape=jax.ShapeDtypeStruct((m, d), BF16),
        grid=(m // tm, d // tn),
        in_specs=[act, act, act, wspec(col_ga), wspec(col_gc), wspec(0), wspec(0)],
        out_specs=pl.BlockSpec((tm, tn), lambda i, j: (i, j)),
        compiler_params=_params(("parallel", "arbitrary"), est),
        name="merge",
    )(xn, attn, z, w_in, w_in, w_ap, w_cp)


def _out_kernel(x_ref, m_ref, w_ref, g_ref, x1_ref, xn_ref, *, rows):
    for r in range(x_ref.shape[0] // rows):
        rs = slice(r * rows, (r + 1) * rows)
        x1 = x_ref[rs, :] + _dot(m_ref[rs, :], w_ref[...])
        x1_ref[rs, :] = x1
        xn_ref[rs, :] = _rmsnorm_rows(x1, g_ref[...]).astype(BF16)


def _out_proj(x, merged, w_out, g_ffn, *, tm, rows):
    m, d = x.shape
    assert m % tm == 0 and tm % rows == 0
    est = (2 * (2 * _nbytes((tm, d), F32) + 2 * _nbytes((tm, d), BF16)) + _nbytes((d, d), BF16)
           + 4 * _nbytes((rows, d), F32))
    return pl.pallas_call(
        functools.partial(_out_kernel, rows=rows),
        out_shape=(jax.ShapeDtypeStruct((m, d), F32), jax.ShapeDtypeStruct((m, d), BF16)),
        grid=(m // tm,),
        in_specs=[pl.BlockSpec((tm, d), lambda i: (i, 0)),
                  pl.BlockSpec((tm, d), lambda i: (i, 0)),
                  pl.BlockSpec((d, d), lambda i: (0, 0), pipeline_mode=pl.Buffered(1)),
                  pl.BlockSpec((1, d), lambda i: (0, 0))],
        out_specs=(pl.BlockSpec((tm, d), lambda i: (i, 0)),
                   pl.BlockSpec((tm, d), lambda i: (i, 0))),
        compiler_params=_params(("parallel",), est),
        name="out_proj",
    )(x, merged, w_out, g_ffn)


def _ffn_kernel(x_ref, xn_ref, wg_ref, wu_ref, wd_ref, o_ref):
    f = pl.program_id(1)
    slab = x_ref.shape[1]
    n_slabs = o_ref.shape[1] // slab

    @pl.when(f == 0)
    def _():
        o_ref[...] = jnp.zeros_like(o_ref)

    xn = xn_ref[...]
    h = jax.nn.silu(_dot(xn, wg_ref[...])) * _dot(xn, wu_ref[...])
    o_ref[...] += _dot(h.astype(BF16), wd_ref[...])

    for s in range(n_slabs):
        @pl.when(f == s)
        def _(s=s):
            o_ref[:, s * slab:(s + 1) * slab] += x_ref[...]


def _ffn(x, xn, w_gate, w_up, w_down, *, tm, tf, slab):
    m, d = x.shape
    d_ff = w_down.shape[0]
    n_slabs = d // slab
    assert n_slabs <= d_ff // tf and w_gate.shape == w_up.shape == (d_ff // tf, d, tf)
    est = (2 * (_nbytes((tm, slab), F32) + _nbytes((tm, d), F32) + _nbytes((tm, d), BF16)
                + 3 * _nbytes((d, tf), BF16))
           + 4 * _nbytes((tm, tf), F32) + _nbytes((tm, d), F32))
    row = lambda i, f: (i, 0)
    return pl.pallas_call(
        _ffn_kernel,
        out_shape=jax.ShapeDtypeStruct((m, d), F32),
        grid=(m // tm, d_ff // tf),
        in_specs=[pl.BlockSpec((tm, slab), lambda i, f: (i, jnp.minimum(f, n_slabs - 1))),
                  pl.BlockSpec((tm, d), row),
                  pl.BlockSpec((None, d, tf), lambda i, f: (f, 0, 0)),
                  pl.BlockSpec((None, d, tf), lambda i, f: (f, 0, 0)),
                  pl.BlockSpec((tf, d), lambda i, f: (f, 0))],
        out_specs=pl.BlockSpec((tm, d), row),
        compiler_params=_params(("parallel", "arbitrary"), est),
        name="ffn",
    )(x, xn, w_gate, w_up, w_down)


def _rope_tables(pos, repeat, g, scale):
    half = HEAD_DIM // 2
    inv = jnp.exp(-math.log(ROPE_THETA) * jnp.arange(half, dtype=F32) * (2.0 / HEAD_DIM))
    ang = pos.astype(F32)[:, None] * inv[None, :]
    cos = jnp.cos(ang)
    sin = jnp.sin(ang)
    g_swap = jnp.concatenate([g[half:], g[:half]])
    t_same = g[None, :] * jnp.concatenate([cos, cos], axis=-1) * scale
    t_swap = g_swap[None, :] * jnp.concatenate([-sin, sin], axis=-1) * scale
    reps = (repeat, LANES // HEAD_DIM)
    return jnp.tile(t_same, reps), jnp.tile(t_swap, reps)


LATE_WEIGHTS = ("w_attn_proj", "w_conv_proj", "w_out", "w_ffn_gate", "w_ffn_up", "w_ffn_down")
FFN_IN_WEIGHTS = ("w_ffn_gate", "w_ffn_up")


def _layer(x, w, *, batch, seq, pos, tiles, conv_prev, sample_cache):
    d = x.shape[1]
    n_heads = w["sinks"].shape[0]
    d_attn = n_heads * HEAD_DIM
    d_kv = w["d_kv"]
    n_kv = d_kv // HEAD_DIM
    group = n_heads // n_kv
    d_conv = w["conv_w"].shape[1]
    col_b = 0
    col_c = col_b + d_conv
    col_h = col_c + d_conv
    col_ga = col_h + d_conv
    col_gc = col_ga + d
    tm = tiles["tm"]

    rope_rows = max(seq, tm)
    log2e = math.log2(math.e)
    tables = (_rope_tables(pos, rope_rows // seq, w["q_norm_g"], HEAD_DIM ** -0.5 * log2e)
              + _rope_tables(pos, rope_rows // seq, w["k_norm_g"], 1.0))
    sinks = w["sinks"] * log2e

    w_f32 = None if "w_rest" in w else w["w_in_f32"]
    xn, q, k, v, k2, v2, *w_rest = _qkv_proj(
        x, w["norm_mix_g"], w["w_qkv"], tables, tiles["kv_tail"], w_f32,
        tm=tm, d_attn=d_attn, d_kv=d_kv, rows=tiles["qkv_rows"])
    if w_rest:
        w = dict(w, w_rest=w_rest[0])
    late = [name for name in LATE_WEIGHTS if w[name].dtype != BF16]
    z, u_tail, *late_bf16 = _conv_branch(
        xn, w["w_rest"], w["conv_w"], conv_prev,
        [(w[name], tiles["tf"] if name in FFN_IN_WEIGHTS else None) for name in late],
        tm=tiles["tm_conv"], tn=tiles["tn_conv"], rows=tiles["conv_rows"], seq_len=seq,
        tail_rows=tiles["conv_tail"], col_b=col_b, col_c=col_c, col_h=col_h)
    w = dict(w, **dict(zip(late, late_bf16)))
    if sample_cache is None:
        attn = _attention(q, k2, v2, sinks, batch=batch, seq=seq, n_kv=n_kv, group=group)
    else:
        attn = _attention_sample(q, k2, v2, sample_cache[0], sample_cache[1], sinks,
                                 batch=batch, t_new=seq, n_kv=n_kv, group=group)
    merged = _merge(xn, attn, z, w["w_rest"], w["w_attn_proj"], w["w_conv_proj"],
                    tm=tm, tn=tiles["tn_merge"], col_ga=col_ga, col_gc=col_gc)
    x1, xn_ffn = _out_proj(x, merged, w["w_out"], w["norm_ffn_g"],
                           tm=tiles["tm_out"], rows=tiles["out_rows"])
    y = _ffn(x1, xn_ffn, w["w_ffn_gate"], w["w_ffn_up"], w["w_ffn_down"],
             tm=tiles["tm_ffn"], tf=tiles["tf"], slab=V7X_MXU_DIM)
    return y, k, v, u_tail, w


def kernel(x_prompt, x_sample, cache_k, cache_v, state_conv, norm_mix_g, w_in, q_norm_g, k_norm_g,
           attn_sinks, conv_w, w_attn_proj, w_conv_proj, w_out, norm_ffn_g, w_ffn_gate, w_ffn_up,
           w_ffn_down):
    depth = w_in.shape[0]
    assert depth == 1
    b_p, s_p, d = x_prompt.shape
    b_s, s_s, _ = x_sample.shape
    n_kv = cache_k.shape[3]
    d_kv = n_kv * HEAD_DIM
    win = cache_k.shape[2]
    n_state = state_conv.shape[2]

    w = {
        "d_kv": d_kv,
        "norm_mix_g": norm_mix_g[0].reshape(1, d),
        "w_qkv": w_in[0][:, :attn_sinks.shape[1] * HEAD_DIM + 2 * d_kv].astype(BF16),
        "w_in_f32": w_in[0],
        "q_norm_g": q_norm_g[0],
        "k_norm_g": k_norm_g[0],
        "sinks": attn_sinks[0],
        "conv_w": conv_w[0],
        "w_attn_proj": w_attn_proj[0],
        "w_conv_proj": w_conv_proj[0],
        "w_out": w_out[0],
        "norm_ffn_g": norm_ffn_g[0].reshape(1, d),
        "w_ffn_gate": w_ffn_gate[0],
        "w_ffn_up": w_ffn_up[0],
        "w_ffn_down": w_ffn_down[0],
    }

    m_p = b_p * s_p
    tm_p = min(1024, s_p)
    keep = min(WINDOW, s_p)
    tiles_p = dict(tm=tm_p, qkv_rows=256, kv_tail=(s_p, keep), tm_conv=s_p, tn_conv=512, conv_tail=8,
                   conv_rows=min(512, s_p), tn_merge=512, tm_out=min(512, s_p), out_rows=256,
                   tm_ffn=tm_p, tf=512)
    y_p, k_p, v_p, u_p, w = _layer(x_prompt.reshape(m_p, d), w, batch=b_p, seq=s_p,
                                   pos=jnp.arange(s_p), tiles=tiles_p, conv_prev=None,
                                   sample_cache=None)
    new_k_p = k_p.reshape(b_p, n_kv, HEAD_DIM, keep).transpose(0, 3, 1, 2)[None]
    new_v_p = v_p.reshape(b_p, n_kv, HEAD_DIM, keep).transpose(0, 3, 1, 2)[None]
    new_c_p = u_p.reshape(b_p, tiles_p["conv_tail"], -1)[:, tiles_p["conv_tail"] - n_state:][None]

    m_s = b_s * s_s
    st = state_conv[0]
    prev1 = jnp.pad(st[:, 1:2], ((0, 0), (0, s_s - 1), (0, 0))).reshape(m_s, -1)
    prev2 = jnp.pad(st, ((0, 0), (0, s_s - n_state), (0, 0))).reshape(m_s, -1)
    tiles_s = dict(tm=m_s, qkv_rows=m_s, kv_tail=None, tm_conv=m_s, tn_conv=512, conv_tail=m_s,
                   conv_rows=m_s, tn_merge=512, tm_out=m_s, out_rows=m_s, tm_ffn=m_s, tf=512)
    y_s, k_s, v_s, u_s, _ = _layer(x_sample.reshape(m_s, d), w, batch=b_s, seq=s_s,
                                   pos=PAST_LEN + jnp.arange(s_s), tiles=tiles_s,
                                   conv_prev=(prev1, prev2),
                                   sample_cache=(cache_k[0].reshape(b_s * win, d_kv),
                                                 cache_v[0].reshape(b_s * win, d_kv)))
    new_k_s = k_s.reshape(b_s, s_s, n_kv, HEAD_DIM)[None]
    new_v_s = v_s.reshape(b_s, s_s, n_kv, HEAD_DIM)[None]
    new_c_s = u_s.reshape(b_s, s_s, -1)[:, s_s - n_state:][None]

    return (y_p.reshape(b_p, s_p, d), y_s.reshape(b_s, s_s, d), new_k_p, new_v_p, new_c_p,
            new_k_s, new_v_s, new_c_s)
```

```python
import functools
import math

import jax
import jax.numpy as jnp
from jax import lax
from jax.experimental import pallas as pl
from jax.experimental.pallas import tpu as pltpu

F32 = jnp.float32
BF16 = jnp.bfloat16

CHUNK = 64
WINDOW = 128
PAST_LEN = 1024
ROPE_THETA = 10000.0
EPS = 1e-6
NEG_INF = -1e30
HEAD_DIM = 64
LANES = 128
BF16_SUBLANES = 16
PAIR = 2 * CHUNK
KEY_SPAN = PAIR + WINDOW
PAIRS_PER_TRIP = 3

V7X_MXU_DIM = 256
V7X_VMEM_BYTES = 64 * 1024 * 1024
VMEM_CAP_BYTES = V7X_VMEM_BYTES - 6 * 1024 * 1024


def _params(semantics, vmem_estimate):
    limit = int(min(max(vmem_estimate, 16 * 1024 * 1024), VMEM_CAP_BYTES))
    return pltpu.CompilerParams(dimension_semantics=semantics, vmem_limit_bytes=limit)


def _nbytes(shape, dtype):
    return math.prod(shape) * jnp.dtype(dtype).itemsize


def _dot(a, b):
    return jnp.dot(a, b, preferred_element_type=F32)


def _dot_nt(a, b):
    return lax.dot_general(a, b, (((1,), (1,)), ((), ())), preferred_element_type=F32)


def _lane_iota(shape):
    return lax.broadcasted_iota(jnp.int32, shape, 1)


def _rmsnorm_rows(x, g):
    y = x * lax.rsqrt(jnp.mean(x * x, axis=-1, keepdims=True) + EPS)
    return y * g


def _headnorm_rope(y, t_same, t_swap, seg):
    width = y.shape[1]
    s = y * y
    s_hi = s.astype(BF16)
    s_lo = (s - s_hi.astype(F32)).astype(BF16)
    ms = _dot(s_hi, seg) + _dot(s_lo, seg)
    half = HEAD_DIM // 2
    ahead = pltpu.roll(y, width - half, axis=1)
    behind = pltpu.roll(y, half, axis=1)
    swapped = jnp.where((_lane_iota(y.shape) & (HEAD_DIM - 1)) < half, ahead, behind)
    reps = width // t_same.shape[1]
    t_same = jnp.concatenate([t_same] * reps, axis=1)
    t_swap = jnp.concatenate([t_swap] * reps, axis=1)
    return (y * t_same + swapped * t_swap) * lax.rsqrt(ms + EPS)


def _dup_halves(x):
    low = _lane_iota(x.shape) < HEAD_DIM
    r = pltpu.roll(x, HEAD_DIM, axis=1)
    return jnp.where(low, x, r), jnp.where(low, r, x)


def _isolate_halves(x):
    low = _lane_iota(x.shape) < HEAD_DIM
    r = pltpu.roll(x, HEAD_DIM, axis=1)
    return (jnp.where(low, x, 0.0), jnp.where(low, 0.0, r),
            jnp.where(low, r, 0.0), jnp.where(low, 0.0, x))


def _qkv_kernel(*refs, n_q_blocks, d_kv, rows, tail_keep, n_rest):
    x_ref, g_ref, w_ref, seg_ref, qs_ref, qw_ref, ks_ref, kw_ref = refs[:8]
    rest_refs = refs[8:8 + n_rest]
    xn_ref, q_ref, k_ref, v_ref, k2_ref, v2_ref = refs[8 + n_rest:][:6]
    rest_out = refs[-1] if n_rest else None
    tm = q_ref.shape[0]
    tn = 2 * d_kv
    chunks = [slice(r * rows, (r + 1) * rows) for r in range(tm // rows)]
    seg = seg_ref[...]
    slab = seg.shape[0]

    for rs in chunks:
        xn = _rmsnorm_rows(x_ref[rs, :], g_ref[...]).astype(BF16)
        xn_ref[rs, :] = xn
        for jb in range(n_q_blocks):
            y = _dot(xn, w_ref[:, jb * tn:(jb + 1) * tn])
            for c in range(tn // slab):
                sl = slice(c * slab, (c + 1) * slab)
                q_ref[rs, jb * tn + c * slab:jb * tn + (c + 1) * slab] = _headnorm_rope(
                    y[:, sl], qs_ref[rs, :], qw_ref[rs, :], seg).astype(BF16)

        y = _dot(xn, w_ref[:, n_q_blocks * tn:])
        k_all = jnp.concatenate(
            [_headnorm_rope(y[:, c * slab:(c + 1) * slab], ks_ref[rs, :], kw_ref[rs, :], seg)
             for c in range(d_kv // slab)], axis=1)
        if tail_keep is None:
            k_ref[rs, :] = k_all
            v_ref[rs, :] = y[:, d_kv:]
        elif rs is chunks[-1]:
            k_ref[0] = k_all[rows - tail_keep:].T
            v_ref[0] = y[rows - tail_keep:, d_kv:].T
        for c in range(d_kv // LANES):
            sl = slice(c * LANES, (c + 1) * LANES)
            k = k_all[:, sl]
            v = y[:, d_kv + c * LANES:d_kv + (c + 1) * LANES]
            va, vb = _dup_halves(v)
            lo = slice(2 * c * LANES, (2 * c + 1) * LANES)
            hi = slice((2 * c + 1) * LANES, (2 * c + 2) * LANES)
            v2_ref[rs, lo] = va.astype(BF16)
            v2_ref[rs, hi] = vb.astype(BF16)
            for n, piece in enumerate(_isolate_halves(k)):
                k2_ref[rs, (4 * c + n) * LANES:(4 * c + n + 1) * LANES] = piece.astype(BF16)

        for n, ref in list(enumerate(rest_refs))[chunks.index(rs)::len(chunks)]:
            width = ref.shape[1]
            rest_out[:, n * width:(n + 1) * width] = ref[...].astype(BF16)


def _qkv_proj(x, g, w_in, tables, tail, w_f32, *, tm, d_attn, d_kv, rows):
    m, d = x.shape
    tn = 2 * d_kv
    n_q_blocks = d_attn // tn
    qkv_cols = d_attn + tn
    assert m % tm == 0 and tm % rows == 0 and d_attn % tn == 0 and w_in.shape == (d, qkv_cols)
    n_tiles = m // tm
    rest_specs, rest_shape, rest_out_spec, n_rest = [], (), (), 0
    if w_f32 is not None:
        n_rest = (w_f32.shape[1] - qkv_cols) // qkv_cols
        assert w_f32.shape[1] == (n_rest + 1) * qkv_cols and d % (n_tiles * BF16_SUBLANES) == 0
        slab_rows = d // n_tiles
        rest_specs = [pl.BlockSpec((slab_rows, qkv_cols), functools.partial(
            lambda n, i: (i, n + 1), n)) for n in range(n_rest)]
        rest_shape = (jax.ShapeDtypeStruct((d, n_rest * qkv_cols), BF16),)
        rest_out_spec = (pl.BlockSpec((slab_rows, n_rest * qkv_cols), lambda i: (i, 0)),)
    if tail is None:
        kv_shape = jax.ShapeDtypeStruct((m, d_kv), F32)
        kv_spec = pl.BlockSpec((tm, d_kv), lambda i: (i, 0))
    else:
        seq, keep = tail
        assert seq % tm == 0 and keep <= rows and keep % LANES == 0
        kv_shape = jax.ShapeDtypeStruct((m // seq, d_kv, keep), F32)
        kv_spec = pl.BlockSpec((1, d_kv, keep), lambda i: (i // (seq // tm), 0, 0))
    rope_blocks = tables[0].shape[0] // tm
    tab = pl.BlockSpec((tm, LANES), lambda i: (i % rope_blocks, 0))
    row = lambda width: pl.BlockSpec((tm, width), lambda i: (i, 0))
    fixed = lambda shape: pl.BlockSpec(shape, lambda i: (0, 0), pipeline_mode=pl.Buffered(1))
    head_of = jnp.arange(V7X_MXU_DIM) // HEAD_DIM
    seg = jnp.where(head_of[:, None] == head_of[None, :], 1.0 / HEAD_DIM, 0.0).astype(BF16)
    est = (2 * (_nbytes((tm, d), F32) + _nbytes((tm, d), BF16) + _nbytes((tm, d_attn), BF16)
                + 3 * _nbytes((tm, tn), BF16) + 2 * _nbytes((tm, d_kv), F32)
                + 4 * _nbytes((tm, LANES), F32))
           + _nbytes(w_in.shape, BF16) + 6 * _nbytes((rows, tn), F32) + 2 * _nbytes((rows, d), F32))
    if n_rest:
        est += 3 * _nbytes((d // n_tiles, n_rest * qkv_cols), F32)
    return pl.pallas_call(
        functools.partial(_qkv_kernel, n_q_blocks=n_q_blocks, d_kv=d_kv, rows=rows,
                          tail_keep=None if tail is None else tail[1], n_rest=n_rest),
        out_shape=(jax.ShapeDtypeStruct((m, d), BF16), jax.ShapeDtypeStruct((m, d_attn), BF16),
                   kv_shape, kv_shape,
                   jax.ShapeDtypeStruct((m, 2 * tn), BF16), jax.ShapeDtypeStruct((m, tn), BF16),
                   *rest_shape),
        grid=(n_tiles,),
        in_specs=[row(d), fixed((1, d)), fixed(w_in.shape), fixed(seg.shape),
                  tab, tab, tab, tab] + rest_specs,
        out_specs=(row(d), row(d_attn), kv_spec, kv_spec, row(2 * tn), row(tn), *rest_out_spec),
        compiler_params=_params(("arbitrary",), est),
        name="qkv_proj",
    )(x, g, w_in, seg, *tables, *([w_f32] * n_rest))


def _conv_kernel(*refs, seq_len, tail_rows, has_state, rows, cast_per_step):
    n_cast = len(cast_per_step)
    n_in = 7 if has_state else 5
    xn_ref, wb_ref, wc_ref, wh_ref, cw_ref = refs[:5]
    p1_ref, p2_ref = refs[5:n_in] if has_state else (None, None)
    z_ref, u_ref = refs[n_in + n_cast:n_in + n_cast + 2]
    casts = list(zip(refs[n_in:n_in + n_cast], refs[n_in + n_cast + 2:]))
    tm, tn = z_ref.shape
    w = cw_ref[...]
    row = lax.broadcasted_iota(jnp.int32, (rows, tn), 0)
    last1 = last2 = None
    for r in range(tm // rows):
        rs = slice(r * rows, (r + 1) * rows)
        xn = xn_ref[rs, :]
        u = _dot(xn, wc_ref[...]) * _dot(xn, wh_ref[...])
        u1 = pltpu.roll(u, 1, axis=0)
        u2 = pltpu.roll(u, 2, axis=0)
        if r > 0:
            u1 = jnp.where(row == 0, last1, u1)
            u2 = jnp.where(row == 0, last2, jnp.where(row == 1, last1, u2))
        if (r * rows) % seq_len == 0 or seq_len < rows:
            pos = (row + r * rows) & (seq_len - 1)
            u1 = jnp.where(pos == 0, p1_ref[rs, :] if has_state else 0.0, u1)
            u2 = jnp.where(pos < 2, p2_ref[rs, :] if has_state else 0.0, u2)
        y = u2 * w[0:1] + u1 * w[1:2] + u * w[2:3]
        z_ref[rs, :] = (_dot(xn, wb_ref[...]) * y).astype(BF16)
        last1, last2 = u[rows - 1:rows], u[rows - 2:rows - 1]
        for (src_ref, dst_ref), per_step in list(zip(casts, cast_per_step))[r::tm // rows]:
            if per_step:
                _copy_slab(src_ref, dst_ref)
            else:
                pl.when(pl.program_id(1) == 0)(functools.partial(_copy_slab, src_ref, dst_ref))
    u_ref[...] = u[rows - tail_rows:]


def _copy_slab(src_ref, dst_ref):
    if len(dst_ref.shape) == 2:
        dst_ref[...] = src_ref[...].astype(BF16)
    else:
        cb = dst_ref.shape[2]
        for b in range(dst_ref.shape[0]):
            dst_ref[b] = src_ref[:, b * cb:(b + 1) * cb].astype(BF16)


def _conv_branch(xn, w_in, conv_w, prev, cast, *, tm, tn, rows, seq_len, tail_rows,
                 col_b, col_c, col_h):
    m, d = xn.shape
    n = conv_w.shape[1]
    has_state = prev is not None
    assert seq_len & (seq_len - 1) == 0 and tm % seq_len == 0
    assert tm % rows == 0 and tail_rows <= rows and (rows % seq_len == 0 or seq_len % rows == 0)
    n_j = n // tn
    steps = (m // tm) * n_j
    wspec = lambda off: pl.BlockSpec((d, tn), lambda i, j: (0, off // tn + j))
    in_specs = [pl.BlockSpec((tm, d), lambda i, j: (i, 0)),
                wspec(col_b), wspec(col_c), wspec(col_h),
                pl.BlockSpec((conv_w.shape[0], tn), lambda i, j: (0, j))]
    args = [xn, w_in, w_in, w_in, conv_w]
    if has_state:
        in_specs += [pl.BlockSpec((tm, tn), lambda i, j: (i, j))] * 2
        args += list(prev)
    cast_per_step = tuple(c.shape[0] % (steps * BF16_SUBLANES) == 0 for c, _ in cast)
    slab_specs, copy_specs, copy_shapes, slab_bytes = [], [], [], 0
    for (c, col_block), per_step in zip(cast, cast_per_step):
        parts = steps if per_step else m // tm
        r, cols = c.shape
        assert r % (parts * BF16_SUBLANES) == 0
        step_of = (lambda i, j: i * n_j + j) if per_step else (lambda i, j: i)
        slab_specs.append(pl.BlockSpec((r // parts, cols),
                                       functools.partial(lambda s, i, j: (s(i, j), 0), step_of)))
        if col_block is None:
            copy_specs.append(slab_specs[-1])
            copy_shapes.append(jax.ShapeDtypeStruct((r, cols), BF16))
        else:
            assert cols % col_block == 0
            copy_specs.append(pl.BlockSpec(
                (cols // col_block, r // parts, col_block),
                functools.partial(lambda s, i, j: (0, s(i, j), 0), step_of)))
            copy_shapes.append(jax.ShapeDtypeStruct((cols // col_block, r, col_block), BF16))
        slab_bytes += 3 * _nbytes(c.shape, F32) // parts
    est = (2 * (_nbytes((tm, d), BF16) + 3 * _nbytes((d, tn), BF16) + _nbytes((tm, tn), BF16)
                + _nbytes((tail_rows, tn), F32)) + 8 * _nbytes((tm, tn), F32) + slab_bytes)
    return pl.pallas_call(
        functools.partial(_conv_kernel, seq_len=seq_len, tail_rows=tail_rows, has_state=has_state,
                          rows=rows, cast_per_step=cast_per_step),
        out_shape=(jax.ShapeDtypeStruct((m, n), BF16),
                   jax.ShapeDtypeStruct((m // tm * tail_rows, n), F32),
                   *copy_shapes),
        grid=(m // tm, n_j),
        in_specs=in_specs + slab_specs,
        out_specs=(pl.BlockSpec((tm, tn), lambda i, j: (i, j)),
                   pl.BlockSpec((tail_rows, tn), lambda i, j: (i, j)),
                   *copy_specs),
        compiler_params=_params(("parallel", "arbitrary"), est),
        name="conv_branch",
    )(*args, *[c for c, _ in cast])


def _stack_heads(q_ref, rows, col0, group):
    parts = []
    for g in range(group):
        blk = q_ref[rows, col0 + (g // 2) * LANES:col0 + (g // 2 + 1) * LANES]
        keep = (_lane_iota(blk.shape) >= HEAD_DIM) == bool(g % 2)
        parts.append(jnp.where(keep, blk, jnp.zeros_like(blk)))
    return jnp.concatenate(parts, axis=0)


def _sink_column(sink_ref, head0, group, rows):
    return jnp.concatenate(
        [jnp.full((rows, 1), sink_ref[head0 + g], F32) for g in range(group)], axis=0)


def _store_heads(o_ref, rows, col0, o, group, nrows):
    for lp in range(group // 2):
        a = o[(2 * lp) * nrows:(2 * lp + 1) * nrows]
        b = o[(2 * lp + 1) * nrows:(2 * lp + 2) * nrows]
        low = _lane_iota(a.shape) < HEAD_DIM
        o_ref[rows, col0 + lp * LANES:col0 + (lp + 1) * LANES] = jnp.where(low, a, b).astype(BF16)


def _attn_pairs(pairs, sink_ref, q_ref, k_ref, v_ref, o_ref, n_kv, group):
    def block_rows(index):
        start = index * PAIR
        return pl.ds(start if isinstance(start, int) else pl.multiple_of(start, PAIR), PAIR)

    first_q = _lane_iota((CHUNK, PAIR)) < CHUNK

    def keys_of(ref, pair, has_prev, tile):
        cols = slice(tile * LANES, (tile + 1) * LANES)
        if has_prev:
            return jnp.concatenate([ref[block_rows(pair - 1), cols], ref[block_rows(pair), cols]],
                                   axis=0)
        return ref[block_rows(pair), cols]

    def scores(pair, has_prev, kv):
        col0 = kv * group * HEAD_DIM
        q_tiles = jnp.concatenate(
            [q_ref[block_rows(pair), col0 + lp * LANES:col0 + (lp + 1) * LANES]
             for lp in range(group // 2)], axis=0)
        return [_dot_nt(keys_of(k_ref, pair, has_prev, 2 * kv + half), q_tiles)
                for half in range(2)]

    items = [(pair, has_prev, kv) for pair, has_prev in pairs for kv in range(n_kv)]
    st_next = scores(*items[0])
    for n, (pair, has_prev, kv) in enumerate(items):
        st = st_next
        if n + 1 < len(items):
            st_next = scores(*items[n + 1])
        rows = block_rows(pair)
        vt = keys_of(v_ref, pair, has_prev, kv).astype(F32).T.astype(BF16)
        col0 = kv * group * HEAD_DIM
        probs, rden = [], []
        for g in range(group):
            sg = st[g % 2][:, (g // 2) * PAIR:(g // 2 + 1) * PAIR]
            blocks = [sg[c * CHUNK:(c + 1) * CHUNK] for c in range(sg.shape[0] // CHUNK)]
            if has_prev:
                s = jnp.concatenate([jnp.where(first_q, blocks[0], blocks[3]),
                                     blocks[1], blocks[2]], axis=0)
            else:
                s = jnp.concatenate([blocks[0], jnp.where(first_q, NEG_INF, blocks[1])], axis=0)
            sink = sink_ref[kv * group + g]
            m = jnp.maximum(jnp.max(s, axis=0, keepdims=True), sink)
            e = jnp.exp2(s - m)
            den = jnp.sum(e, axis=0, keepdims=True) + jnp.exp2(sink - m)
            if has_prev:
                e_03, e_12 = e[:CHUNK], e[CHUNK:]
                e = jnp.concatenate([jnp.where(first_q, e_03, 0.0), e_12,
                                     jnp.where(first_q, 0.0, e_03)], axis=0)
            probs.append(e.astype(BF16))
            rden.append(1.0 / den)
        ot = _dot(vt, jnp.concatenate(probs, axis=1)) * jnp.concatenate(rden, axis=1)
        for lp in range(group // 2):
            a = ot[:HEAD_DIM, (2 * lp) * PAIR:(2 * lp + 1) * PAIR]
            b = ot[HEAD_DIM:, (2 * lp + 1) * PAIR:(2 * lp + 2) * PAIR]
            o_ref[rows, col0 + lp * LANES:col0 + (lp + 1) * LANES] = (
                jnp.concatenate([a, b], axis=0).T.astype(BF16))


def _attn_kernel(sink_ref, q_ref, k_ref, v_ref, o_ref, *, n_kv, group):
    def run(pairs):
        _attn_pairs(pairs, sink_ref, q_ref, k_ref, v_ref, o_ref, n_kv, group)

    def trip(it, carry):
        run([(1 + PAIRS_PER_TRIP * it + u, True) for u in range(PAIRS_PER_TRIP)])
        return carry

    n_pairs = q_ref.shape[0] // PAIR
    trips = (n_pairs - 1) // PAIRS_PER_TRIP
    run([(0, False)])
    lax.fori_loop(0, trips, trip, 0)
    leftover = range(1 + trips * PAIRS_PER_TRIP, n_pairs)
    if leftover:
        run([(pair, True) for pair in leftover])


def _attention(q, k2, v2, sinks, *, batch, seq, n_kv, group):
    m, d = q.shape
    assert WINDOW == PAIR and seq % PAIR == 0
    est = (2 * (2 * _nbytes((seq, d), BF16) + 3 * _nbytes((seq, n_kv * LANES), BF16))
           + 8 * _nbytes((group * PAIR, KEY_SPAN), F32))
    return pl.pallas_call(
        functools.partial(_attn_kernel, n_kv=n_kv, group=group),
        out_shape=jax.ShapeDtypeStruct((m, d), BF16),
        grid=(batch,),
        in_specs=[pl.BlockSpec(memory_space=pltpu.SMEM),
                  pl.BlockSpec((seq, d), lambda b: (b, 0)),
                  pl.BlockSpec((seq, 2 * n_kv * LANES), lambda b: (b, 0)),
                  pl.BlockSpec((seq, n_kv * LANES), lambda b: (b, 0))],
        out_specs=pl.BlockSpec((seq, d), lambda b: (b, 0)),
        compiler_params=_params(("parallel",), est),
        name="attention",
    )(sinks, q, k2, v2)


def _attn_sample_kernel(sink_ref, q_ref, kn_ref, vn_ref, kc_ref, vc_ref, o_ref, *, n_kv, group):
    t = q_ref.shape[0]
    for kv in range(n_kv):
        pair_sl = slice((kv // 2) * LANES, (kv // 2 + 1) * LANES)
        kc = _dup_halves(kc_ref[:, pair_sl])[kv % 2].astype(BF16)
        vc = _dup_halves(vc_ref[:, pair_sl])[kv % 2].astype(BF16)
        kn = (kn_ref[:, 2 * kv * LANES:(2 * kv + 1) * LANES]
              + kn_ref[:, (2 * kv + 1) * LANES:(2 * kv + 2) * LANES])
        vn = vn_ref[:, kv * LANES:(kv + 1) * LANES]
        col0 = kv * group * HEAD_DIM
        lhs = _stack_heads(q_ref, slice(0, t), col0, group)
        s_c = _dot_nt(lhs, kc)
        s_n = _dot_nt(lhs, kn)
        sink = _sink_column(sink_ref, kv * group, group, t)
        m = jnp.maximum(jnp.maximum(jnp.max(s_c, axis=1, keepdims=True),
                                    jnp.max(s_n, axis=1, keepdims=True)), sink)
        e_c = jnp.exp2(s_c - m)
        e_n = jnp.exp2(s_n - m)
        den = (jnp.sum(e_c, axis=1, keepdims=True) + jnp.sum(e_n, axis=1, keepdims=True)
               + jnp.exp2(sink - m))
        o = (_dot(e_c.astype(BF16), vc) + _dot(e_n.astype(BF16), vn)) / den
        _store_heads(o_ref, slice(0, t), col0, o, group, t)


def _attention_sample(q, k2, v2, cache_k, cache_v, sinks, *, batch, t_new, n_kv, group):
    m, d = q.shape
    win = cache_k.shape[0] // batch
    d_kv = cache_k.shape[1]
    est = 4 * 1024 * 1024
    return pl.pallas_call(
        functools.partial(_attn_sample_kernel, n_kv=n_kv, group=group),
        out_shape=jax.ShapeDtypeStruct((m, d), BF16),
        grid=(batch,),
        in_specs=[pl.BlockSpec(memory_space=pltpu.SMEM),
                  pl.BlockSpec((t_new, d), lambda b: (b, 0)),
                  pl.BlockSpec((t_new, 2 * n_kv * LANES), lambda b: (b, 0)),
                  pl.BlockSpec((t_new, n_kv * LANES), lambda b: (b, 0)),
                  pl.BlockSpec((win, d_kv), lambda b: (b, 0)),
                  pl.BlockSpec((win, d_kv), lambda b: (b, 0))],
        out_specs=pl.BlockSpec((t_new, d), lambda b: (b, 0)),
        compiler_params=_params(("parallel",), est),
        name="attention_sample",
    )(sinks, q, k2, v2, cache_k, cache_v)


def _merge_kernel(xn_ref, a_ref, z_ref, wga_ref, wgc_ref, wap_ref, wcp_ref, o_ref):
    xn = xn_ref[...]
    attn = jax.nn.sigmoid(_dot(xn, wga_ref[...])) * _dot(a_ref[...], wap_ref[...])
    conv = jax.nn.sigmoid(_dot(xn, wgc_ref[...])) * _dot(z_ref[...], wcp_ref[...])
    o_ref[...] = (attn + conv).astype(BF16)


def _merge(xn, attn, z, w_in, w_ap, w_cp, *, tm, tn, col_ga, col_gc):
    m, d = xn.shape
    act = pl.BlockSpec((tm, d), lambda i, j: (i, 0))
    wspec = lambda off: pl.BlockSpec((d, tn), lambda i, j: (0, off // tn + j))
    est = (2 * (3 * _nbytes((tm, d), BF16) + 4 * _nbytes((d, tn), BF16) + _nbytes((tm, tn), BF16))
           + 6 * _nbytes((tm, tn), F32))
    return pl.pallas_call(
        _merge_kernel,
        out_shape=jax.ShapeDtypeStruct((m, d), BF16),
        grid=(m // tm, d // tn),
        in_specs=[act, act, act, wspec(col_ga), wspec(col_gc), wspec(0), wspec(0)],
        out_specs=pl.BlockSpec((tm, tn), lambda i, j: (i, j)),
        compiler_params=_params(("parallel", "arbitrary"), est),
        name="merge",
    )(xn, attn, z, w_in, w_in, w_ap, w_cp)


def _out_kernel(x_ref, m_ref, w_ref, g_ref, x1_ref, xn_ref, *, rows):
    for r in range(x_ref.shape[0] // rows):
        rs = slice(r * rows, (r + 1) * rows)
        x1 = x_ref[rs, :] + _dot(m_ref[rs, :], w_ref[...])
        x1_ref[rs, :] = x1
        xn_ref[rs, :] = _rmsnorm_rows(x1, g_ref[...]).astype(BF16)


def _out_proj(x, merged, w_out, g_ffn, *, tm, rows):
    m, d = x.shape
    assert m % tm == 0 and tm % rows == 0
    est = (2 * (2 * _nbytes((tm, d), F32) + 2 * _nbytes((tm, d), BF16)) + _nbytes((d, d), BF16)
           + 4 * _nbytes((rows, d), F32))
    return pl.pallas_call(
        functools.partial(_out_kernel, rows=rows),
        out_shape=(jax.ShapeDtypeStruct((m, d), F32), jax.ShapeDtypeStruct((m, d), BF16)),
        grid=(m // tm,),
        in_specs=[pl.BlockSpec((tm, d), lambda i: (i, 0)),
                  pl.BlockSpec((tm, d), lambda i: (i, 0)),
                  pl.BlockSpec((d, d), lambda i: (0, 0), pipeline_mode=pl.Buffered(1)),
                  pl.BlockSpec((1, d), lambda i: (0, 0))],
        out_specs=(pl.BlockSpec((tm, d), lambda i: (i, 0)),
                   pl.BlockSpec((tm, d), lambda i: (i, 0))),
        compiler_params=_params(("parallel",), est),
        name="out_proj",
    )(x, merged, w_out, g_ffn)


def _ffn_kernel(x_ref, xn_ref, wg_ref, wu_ref, wd_ref, o_ref):
    f = pl.program_id(1)
    slab = x_ref.shape[1]
    n_slabs = o_ref.shape[1] // slab

    @pl.when(f == 0)
    def _():
        o_ref[...] = jnp.zeros_like(o_ref)

    xn = xn_ref[...]
    h = jax.nn.silu(_dot(xn, wg_ref[...])) * _dot(xn, wu_ref[...])
    o_ref[...] += _dot(h.astype(BF16), wd_ref[...])

    for s in range(n_slabs):
        @pl.when(f == s)
        def _(s=s):
            o_ref[:, s * slab:(s + 1) * slab] += x_ref[...]


def _ffn(x, xn, w_gate, w_up, w_down, *, tm, tf, slab):
    m, d = x.shape
    d_ff = w_down.shape[0]
    n_slabs = d // slab
    assert n_slabs <= d_ff // tf and w_gate.shape == w_up.shape == (d_ff // tf, d, tf)
    est = (2 * (_nbytes((tm, slab), F32) + _nbytes((tm, d), F32) + _nbytes((tm, d), BF16)
                + 3 * _nbytes((d, tf), BF16))
           + 4 * _nbytes((tm, tf), F32) + _nbytes((tm, d), F32))
    row = lambda i, f: (i, 0)
    return pl.pallas_call(
        _ffn_kernel,
        out_shape=jax.ShapeDtypeStruct((m, d), F32),
        grid=(m // tm, d_ff // tf),
        in_specs=[pl.BlockSpec((tm, slab), lambda i, f: (i, jnp.minimum(f, n_slabs - 1))),
                  pl.BlockSpec((tm, d), row),
                  pl.BlockSpec((None, d, tf), lambda i, f: (f, 0, 0)),
                  pl.BlockSpec((None, d, tf), lambda i, f: (f, 0, 0)),
                  pl.BlockSpec((tf, d), lambda i, f: (f, 0))],
        out_specs=pl.BlockSpec((tm, d), row),
        compiler_params=_params(("parallel", "arbitrary"), est),
        name="ffn",
    )(x, xn, w_gate, w_up, w_down)


def _rope_tables(pos, repeat, g, scale):
    half = HEAD_DIM // 2
    inv = jnp.exp(-math.log(ROPE_THETA) * jnp.arange(half, dtype=F32) * (2.0 / HEAD_DIM))
    ang = pos.astype(F32)[:, None] * inv[None, :]
    cos = jnp.cos(ang)
    sin = jnp.sin(ang)
    g_swap = jnp.concatenate([g[half:], g[:half]])
    t_same = g[None, :] * jnp.concatenate([cos, cos], axis=-1) * scale
    t_swap = g_swap[None, :] * jnp.concatenate([-sin, sin], axis=-1) * scale
    reps = (repeat, LANES // HEAD_DIM)
    return jnp.tile(t_same, reps), jnp.tile(t_swap, reps)


LATE_WEIGHTS = ("w_attn_proj", "w_conv_proj", "w_out", "w_ffn_gate", "w_ffn_up", "w_ffn_down")
FFN_IN_WEIGHTS = ("w_ffn_gate", "w_ffn_up")


def _layer(x, w, *, batch, seq, pos, tiles, conv_prev, sample_cache):
    d = x.shape[1]
    n_heads = w["sinks"].shape[0]
    d_attn = n_heads * HEAD_DIM
    d_kv = w["d_kv"]
    n_kv = d_kv // HEAD_DIM
    group = n_heads // n_kv
    d_conv = w["conv_w"].shape[1]
    col_b = 0
    col_c = col_b + d_conv
    col_h = col_c + d_conv
    col_ga = col_h + d_conv
    col_gc = col_ga + d
    tm = tiles["tm"]

    rope_rows = max(seq, tiles["tm_qkv"])
    log2e = math.log2(math.e)
    tables = (_rope_tables(pos, rope_rows // seq, w["q_norm_g"], HEAD_DIM ** -0.5 * log2e)
              + _rope_tables(pos, rope_rows // seq, w["k_norm_g"], 1.0))
    sinks = w["sinks"] * log2e

    w_f32 = None if "w_rest" in w else w["w_in_f32"]
    xn, q, k, v, k2, v2, *w_rest = _qkv_proj(
        x, w["norm_mix_g"], w["w_qkv"], tables, tiles["kv_tail"], w_f32,
        tm=tiles["tm_qkv"], d_attn=d_attn, d_kv=d_kv, rows=tiles["qkv_rows"])
    if w_rest:
        w = dict(w, w_rest=w_rest[0])
    late = [name for name in LATE_WEIGHTS if w[name].dtype != BF16]
    z, u_tail, *late_bf16 = _conv_branch(
        xn, w["w_rest"], w["conv_w"], conv_prev,
        [(w[name], tiles["tf"] if name in FFN_IN_WEIGHTS else None) for name in late],
        tm=tiles["tm_conv"], tn=tiles["tn_conv"], rows=tiles["conv_rows"], seq_len=seq,
        tail_rows=tiles["conv_tail"], col_b=col_b, col_c=col_c, col_h=col_h)
    w = dict(w, **dict(zip(late, late_bf16)))
    if sample_cache is None:
        attn = _attention(q, k2, v2, sinks, batch=batch, seq=seq, n_kv=n_kv, group=group)
    else:
        attn = _attention_sample(q, k2, v2, sample_cache[0], sample_cache[1], sinks,
                                 batch=batch, t_new=seq, n_kv=n_kv, group=group)
    merged = _merge(xn, attn, z, w["w_rest"], w["w_attn_proj"], w["w_conv_proj"],
                    tm=tm, tn=tiles["tn_merge"], col_ga=col_ga, col_gc=col_gc)
    x1, xn_ffn = _out_proj(x, merged, w["w_out"], w["norm_ffn_g"],
                           tm=tiles["tm_out"], rows=tiles["out_rows"])
    y = _ffn(x1, xn_ffn, w["w_ffn_gate"], w["w_ffn_up"], w["w_ffn_down"],
             tm=tiles["tm_ffn"], tf=tiles["tf"], slab=V7X_MXU_DIM)
    return y, k, v, u_tail, w


def kernel(x_prompt, x_sample, cache_k, cache_v, state_conv, norm_mix_g, w_in, q_norm_g, k_norm_g,
           attn_sinks, conv_w, w_attn_proj, w_conv_proj, w_out, norm_ffn_g, w_ffn_gate, w_ffn_up,
           w_ffn_down):
    depth = w_in.shape[0]
    assert depth == 1
    b_p, s_p, d = x_prompt.shape
    b_s, s_s, _ = x_sample.shape
    n_kv = cache_k.shape[3]
    d_kv = n_kv * HEAD_DIM
    win = cache_k.shape[2]
    n_state = state_conv.shape[2]

    w = {
        "d_kv": d_kv,
        "norm_mix_g": norm_mix_g[0].reshape(1, d),
        "w_qkv": w_in[0][:, :attn_sinks.shape[1] * HEAD_DIM + 2 * d_kv].astype(BF16),
        "w_in_f32": w_in[0],
        "q_norm_g": q_norm_g[0],
        "k_norm_g": k_norm_g[0],
        "sinks": attn_sinks[0],
        "conv_w": conv_w[0],
        "w_attn_proj": w_attn_proj[0],
        "w_conv_proj": w_conv_proj[0],
        "w_out": w_out[0],
        "norm_ffn_g": norm_ffn_g[0].reshape(1, d),
        "w_ffn_gate": w_ffn_gate[0],
        "w_ffn_up": w_ffn_up[0],
        "w_ffn_down": w_ffn_down[0],
    }

    m_p = b_p * s_p
    tm_p = min(1024, s_p)
    keep = min(WINDOW, s_p)
    tiles_p = dict(tm=tm_p, tm_qkv=min(512, s_p), qkv_rows=256, kv_tail=(s_p, keep), tm_conv=s_p, tn_conv=512, conv_tail=8,
                   conv_rows=min(512, s_p), tn_merge=512, tm_out=min(512, s_p), out_rows=256,
                   tm_ffn=tm_p, tf=512)
    y_p, k_p, v_p, u_p, w = _layer(x_prompt.reshape(m_p, d), w, batch=b_p, seq=s_p,
                                   pos=jnp.arange(s_p), tiles=tiles_p, conv_prev=None,
                                   sample_cache=None)
    new_k_p = k_p.reshape(b_p, n_kv, HEAD_DIM, keep).transpose(0, 3, 1, 2)[None]
    new_v_p = v_p.reshape(b_p, n_kv, HEAD_DIM, keep).transpose(0, 3, 1, 2)[None]
    new_c_p = u_p.reshape(b_p, tiles_p["conv_tail"], -1)[:, tiles_p["conv_tail"] - n_state:][None]

    m_s = b_s * s_s
    st = state_conv[0]
    prev1 = jnp.pad(st[:, 1:2], ((0, 0), (0, s_s - 1), (0, 0))).reshape(m_s, -1)
    prev2 = jnp.pad(st, ((0, 0), (0, s_s - n_state), (0, 0))).reshape(m_s, -1)
    tiles_s = dict(tm=m_s, tm_qkv=m_s, qkv_rows=m_s, kv_tail=None, tm_conv=m_s, tn_conv=512, conv_tail=m_s,
                   conv_rows=m_s, tn_merge=512, tm_out=m_s, out_rows=m_s, tm_ffn=m_s, tf=512)
    y_s, k_s, v_s, u_s, _ = _layer(x_sample.reshape(m_s, d), w, batch=b_s, seq=s_s,
                                   pos=PAST_LEN + jnp.arange(s_s), tiles=tiles_s,
                                   conv_prev=(prev1, prev2),
                                   sample_cache=(cache_k[0].reshape(b_s * win, d_kv),
                                                 cache_v[0].reshape(b_s * win, d_kv)))
    new_k_s = k_s.reshape(b_s, s_s, n_kv, HEAD_DIM)[None]
    new_v_s = v_s.reshape(b_s, s_s, n_kv, HEAD_DIM)[None]
    new_c_s = u_s.reshape(b_s, s_s, -1)[:, s_s - n_state:][None]

    return (y_p.reshape(b_p, s_p, d), y_s.reshape(b_s, s_s, d), new_k_p, new_v_p, new_c_p,
            new_k_s, new_v_s, new_c_s)
```

```python
import functools
import math

import jax
import jax.numpy as jnp
from jax import lax
from jax.experimental import pallas as pl
from jax.experimental.pallas import tpu as pltpu

F32 = jnp.float32
BF16 = jnp.bfloat16

CHUNK = 64
WINDOW = 128
PAST_LEN = 1024
ROPE_THETA = 10000.0
EPS = 1e-6
NEG_INF = -1e30
HEAD_DIM = 64
LANES = 128
BF16_SUBLANES = 16
PAIR = 2 * CHUNK
KEY_SPAN = PAIR + WINDOW
PAIRS_PER_TRIP = 3

V7X_MXU_DIM = 256
V7X_VMEM_BYTES = 64 * 1024 * 1024
VMEM_CAP_BYTES = V7X_VMEM_BYTES - 6 * 1024 * 1024


def _params(semantics, vmem_estimate):
    limit = int(min(max(vmem_estimate, 16 * 1024 * 1024), VMEM_CAP_BYTES))
    return pltpu.CompilerParams(dimension_semantics=semantics, vmem_limit_bytes=limit)


def _nbytes(shape, dtype):
    return math.prod(shape) * jnp.dtype(dtype).itemsize


def _dot(a, b):
    return jnp.dot(a, b, preferred_element_type=F32)


def _dot_nt(a, b):
    return lax.dot_general(a, b, (((1,), (1,)), ((), ())), preferred_element_type=F32)


def _lane_iota(shape):
    return lax.broadcasted_iota(jnp.int32, shape, 1)


def _rmsnorm_rows(x, g):
    y = x * lax.rsqrt(jnp.mean(x * x, axis=-1, keepdims=True) + EPS)
    return y * g


def _headnorm_rope(y, t_same, t_swap, seg):
    width = y.shape[1]
    s = y * y
    s_hi = s.astype(BF16)
    s_lo = (s - s_hi.astype(F32)).astype(BF16)
    ms = _dot(s_hi, seg) + _dot(s_lo, seg)
    half = HEAD_DIM // 2
    ahead = pltpu.roll(y, width - half, axis=1)
    behind = pltpu.roll(y, half, axis=1)
    swapped = jnp.where((_lane_iota(y.shape) & (HEAD_DIM - 1)) < half, ahead, behind)
    reps = width // t_same.shape[1]
    t_same = jnp.concatenate([t_same] * reps, axis=1)
    t_swap = jnp.concatenate([t_swap] * reps, axis=1)
    return (y * t_same + swapped * t_swap) * lax.rsqrt(ms + EPS)


def _dup_halves(x):
    low = _lane_iota(x.shape) < HEAD_DIM
    r = pltpu.roll(x, HEAD_DIM, axis=1)
    return jnp.where(low, x, r), jnp.where(low, r, x)


def _isolate_halves(x):
    low = _lane_iota(x.shape) < HEAD_DIM
    r = pltpu.roll(x, HEAD_DIM, axis=1)
    return (jnp.where(low, x, 0.0), jnp.where(low, 0.0, r),
            jnp.where(low, r, 0.0), jnp.where(low, 0.0, x))


def _qkv_kernel(*refs, n_q_blocks, d_kv, rows, tail_keep, n_rest):
    x_ref, g_ref, w_ref, seg_ref, qs_ref, qw_ref, ks_ref, kw_ref = refs[:8]
    rest_refs = refs[8:8 + n_rest]
    xn_ref, q_ref, k_ref, v_ref, k2_ref, v2_ref = refs[8 + n_rest:][:6]
    rest_out = refs[-1] if n_rest else None
    tm = q_ref.shape[0]
    tn = 2 * d_kv
    chunks = [slice(r * rows, (r + 1) * rows) for r in range(tm // rows)]
    seg = seg_ref[...]
    slab = seg.shape[0]

    for rs in chunks:
        xn = _rmsnorm_rows(x_ref[rs, :], g_ref[...]).astype(BF16)
        xn_ref[rs, :] = xn
        for jb in range(n_q_blocks):
            y = _dot(xn, w_ref[:, jb * tn:(jb + 1) * tn])
            for c in range(tn // slab):
                sl = slice(c * slab, (c + 1) * slab)
                q_ref[rs, jb * tn + c * slab:jb * tn + (c + 1) * slab] = _headnorm_rope(
                    y[:, sl], qs_ref[rs, :], qw_ref[rs, :], seg).astype(BF16)

        y = _dot(xn, w_ref[:, n_q_blocks * tn:])
        k_all = jnp.concatenate(
            [_headnorm_rope(y[:, c * slab:(c + 1) * slab], ks_ref[rs, :], kw_ref[rs, :], seg)
             for c in range(d_kv // slab)], axis=1)
        if tail_keep is None:
            k_ref[rs, :] = k_all
            v_ref[rs, :] = y[:, d_kv:]
        elif rs is chunks[-1]:
            k_ref[0] = k_all[rows - tail_keep:].T
            v_ref[0] = y[rows - tail_keep:, d_kv:].T
        for c in range(d_kv // LANES):
            sl = slice(c * LANES, (c + 1) * LANES)
            k = k_all[:, sl]
            v = y[:, d_kv + c * LANES:d_kv + (c + 1) * LANES]
            va, vb = _dup_halves(v)
            lo = slice(2 * c * LANES, (2 * c + 1) * LANES)
            hi = slice((2 * c + 1) * LANES, (2 * c + 2) * LANES)
            v2_ref[rs, lo] = va.astype(BF16)
            v2_ref[rs, hi] = vb.astype(BF16)
            for n, piece in enumerate(_isolate_halves(k)):
                k2_ref[rs, (4 * c + n) * LANES:(4 * c + n + 1) * LANES] = piece.astype(BF16)

        for n, ref in list(enumerate(rest_refs))[chunks.index(rs)::len(chunks)]:
            width = ref.shape[1]
            rest_out[:, n * width:(n + 1) * width] = ref[...].astype(BF16)


def _qkv_proj(x, g, w_in, tables, tail, w_f32, *, tm, d_attn, d_kv, rows):
    m, d = x.shape
    tn = 2 * d_kv
    n_q_blocks = d_attn // tn
    qkv_cols = d_attn + tn
    assert m % tm == 0 and tm % rows == 0 and d_attn % tn == 0 and w_in.shape == (d, qkv_cols)
    n_tiles = m // tm
    rest_specs, rest_shape, rest_out_spec, n_rest = [], (), (), 0
    if w_f32 is not None:
        n_rest = (w_f32.shape[1] - qkv_cols) // qkv_cols
        assert w_f32.shape[1] == (n_rest + 1) * qkv_cols and d % (n_tiles * BF16_SUBLANES) == 0
        slab_rows = d // n_tiles
        rest_specs = [pl.BlockSpec((slab_rows, qkv_cols), functools.partial(
            lambda n, i: (i, n + 1), n)) for n in range(n_rest)]
        rest_shape = (jax.ShapeDtypeStruct((d, n_rest * qkv_cols), BF16),)
        rest_out_spec = (pl.BlockSpec((slab_rows, n_rest * qkv_cols), lambda i: (i, 0)),)
    if tail is None:
        kv_shape = jax.ShapeDtypeStruct((m, d_kv), F32)
        kv_spec = pl.BlockSpec((tm, d_kv), lambda i: (i, 0))
    else:
        seq, keep = tail
        assert seq % tm == 0 and keep <= rows and keep % LANES == 0
        kv_shape = jax.ShapeDtypeStruct((m // seq, d_kv, keep), F32)
        kv_spec = pl.BlockSpec((1, d_kv, keep), lambda i: (i // (seq // tm), 0, 0))
    rope_blocks = tables[0].shape[0] // tm
    tab = pl.BlockSpec((tm, LANES), lambda i: (i % rope_blocks, 0))
    row = lambda width: pl.BlockSpec((tm, width), lambda i: (i, 0))
    fixed = lambda shape: pl.BlockSpec(shape, lambda i: (0, 0), pipeline_mode=pl.Buffered(1))
    head_of = jnp.arange(V7X_MXU_DIM) // HEAD_DIM
    seg = jnp.where(head_of[:, None] == head_of[None, :], 1.0 / HEAD_DIM, 0.0).astype(BF16)
    est = (2 * (_nbytes((tm, d), F32) + _nbytes((tm, d), BF16) + _nbytes((tm, d_attn), BF16)
                + 3 * _nbytes((tm, tn), BF16) + 2 * _nbytes((tm, d_kv), F32)
                + 4 * _nbytes((tm, LANES), F32))
           + _nbytes(w_in.shape, BF16) + 6 * _nbytes((rows, tn), F32) + 2 * _nbytes((rows, d), F32))
    if n_rest:
        est += 3 * _nbytes((d // n_tiles, n_rest * qkv_cols), F32)
    return pl.pallas_call(
        functools.partial(_qkv_kernel, n_q_blocks=n_q_blocks, d_kv=d_kv, rows=rows,
                          tail_keep=None if tail is None else tail[1], n_rest=n_rest),
        out_shape=(jax.ShapeDtypeStruct((m, d), BF16), jax.ShapeDtypeStruct((m, d_attn), BF16),
                   kv_shape, kv_shape,
                   jax.ShapeDtypeStruct((m, 2 * tn), BF16), jax.ShapeDtypeStruct((m, tn), BF16),
                   *rest_shape),
        grid=(n_tiles,),
        in_specs=[row(d), fixed((1, d)), fixed(w_in.shape), fixed(seg.shape),
                  tab, tab, tab, tab] + rest_specs,
        out_specs=(row(d), row(d_attn), kv_spec, kv_spec, row(2 * tn), row(tn), *rest_out_spec),
        compiler_params=_params(("arbitrary",), est),
        name="qkv_proj",
    )(x, g, w_in, seg, *tables, *([w_f32] * n_rest))


def _conv_kernel(*refs, seq_len, tail_rows, has_state, rows, cast_per_step):
    n_cast = len(cast_per_step)
    n_in = 7 if has_state else 5
    xn_ref, wb_ref, wc_ref, wh_ref, cw_ref = refs[:5]
    p1_ref, p2_ref = refs[5:n_in] if has_state else (None, None)
    z_ref, u_ref = refs[n_in + n_cast:n_in + n_cast + 2]
    casts = list(zip(refs[n_in:n_in + n_cast], refs[n_in + n_cast + 2:]))
    tm, tn = z_ref.shape
    w = cw_ref[...]
    row = lax.broadcasted_iota(jnp.int32, (rows, tn), 0)
    last1 = last2 = None
    for r in range(tm // rows):
        rs = slice(r * rows, (r + 1) * rows)
        xn = xn_ref[rs, :]
        u = _dot(xn, wc_ref[...]) * _dot(xn, wh_ref[...])
        u1 = pltpu.roll(u, 1, axis=0)
        u2 = pltpu.roll(u, 2, axis=0)
        if r > 0:
            u1 = jnp.where(row == 0, last1, u1)
            u2 = jnp.where(row == 0, last2, jnp.where(row == 1, last1, u2))
        if (r * rows) % seq_len == 0 or seq_len < rows:
            pos = (row + r * rows) & (seq_len - 1)
            u1 = jnp.where(pos == 0, p1_ref[rs, :] if has_state else 0.0, u1)
            u2 = jnp.where(pos < 2, p2_ref[rs, :] if has_state else 0.0, u2)
        y = u2 * w[0:1] + u1 * w[1:2] + u * w[2:3]
        z_ref[rs, :] = (_dot(xn, wb_ref[...]) * y).astype(BF16)
        last1, last2 = u[rows - 1:rows], u[rows - 2:rows - 1]
        for (src_ref, dst_ref), per_step in list(zip(casts, cast_per_step))[r::tm // rows]:
            if per_step:
                _copy_slab(src_ref, dst_ref)
            else:
                pl.when(pl.program_id(1) == 0)(functools.partial(_copy_slab, src_ref, dst_ref))
    u_ref[...] = u[rows - tail_rows:]


def _copy_slab(src_ref, dst_ref):
    if len(dst_ref.shape) == 2:
        dst_ref[...] = src_ref[...].astype(BF16)
    else:
        cb = dst_ref.shape[2]
        for b in range(dst_ref.shape[0]):
            dst_ref[b] = src_ref[:, b * cb:(b + 1) * cb].astype(BF16)


def _conv_branch(xn, w_in, conv_w, prev, cast, *, tm, tn, rows, seq_len, tail_rows,
                 col_b, col_c, col_h):
    m, d = xn.shape
    n = conv_w.shape[1]
    has_state = prev is not None
    assert seq_len & (seq_len - 1) == 0 and tm % seq_len == 0
    assert tm % rows == 0 and tail_rows <= rows and (rows % seq_len == 0 or seq_len % rows == 0)
    n_j = n // tn
    steps = (m // tm) * n_j
    wspec = lambda off: pl.BlockSpec((d, tn), lambda i, j: (0, off // tn + j))
    in_specs = [pl.BlockSpec((tm, d), lambda i, j: (i, 0)),
                wspec(col_b), wspec(col_c), wspec(col_h),
                pl.BlockSpec((conv_w.shape[0], tn), lambda i, j: (0, j))]
    args = [xn, w_in, w_in, w_in, conv_w]
    if has_state:
        in_specs += [pl.BlockSpec((tm, tn), lambda i, j: (i, j))] * 2
        args += list(prev)
    cast_per_step = tuple(c.shape[0] % (steps * BF16_SUBLANES) == 0 for c, _ in cast)
    slab_specs, copy_specs, copy_shapes, slab_bytes = [], [], [], 0
    for (c, col_block), per_step in zip(cast, cast_per_step):
        parts = steps if per_step else m // tm
        r, cols = c.shape
        assert r % (parts * BF16_SUBLANES) == 0
        step_of = (lambda i, j: i * n_j + j) if per_step else (lambda i, j: i)
        slab_specs.append(pl.BlockSpec((r // parts, cols),
                                       functools.partial(lambda s, i, j: (s(i, j), 0), step_of)))
        if col_block is None:
            copy_specs.append(slab_specs[-1])
            copy_shapes.append(jax.ShapeDtypeStruct((r, cols), BF16))
        else:
            assert cols % col_block == 0
            copy_specs.append(pl.BlockSpec(
                (cols // col_block, r // parts, col_block),
                functools.partial(lambda s, i, j: (0, s(i, j), 0), step_of)))
            copy_shapes.append(jax.ShapeDtypeStruct((cols // col_block, r, col_block), BF16))
        slab_bytes += 3 * _nbytes(c.shape, F32) // parts
    est = (2 * (_nbytes((tm, d), BF16) + 3 * _nbytes((d, tn), BF16) + _nbytes((tm, tn), BF16)
                + _nbytes((tail_rows, tn), F32)) + 8 * _nbytes((tm, tn), F32) + slab_bytes)
    return pl.pallas_call(
        functools.partial(_conv_kernel, seq_len=seq_len, tail_rows=tail_rows, has_state=has_state,
                          rows=rows, cast_per_step=cast_per_step),
        out_shape=(jax.ShapeDtypeStruct((m, n), BF16),
                   jax.ShapeDtypeStruct((m // tm * tail_rows, n), F32),
                   *copy_shapes),
        grid=(m // tm, n_j),
        in_specs=in_specs + slab_specs,
        out_specs=(pl.BlockSpec((tm, tn), lambda i, j: (i, j)),
                   pl.BlockSpec((tail_rows, tn), lambda i, j: (i, j)),
                   *copy_specs),
        compiler_params=_params(("parallel", "arbitrary"), est),
        name="conv_branch",
    )(*args, *[c for c, _ in cast])


def _stack_heads(q_ref, rows, col0, group):
    parts = []
    for g in range(group):
        blk = q_ref[rows, col0 + (g // 2) * LANES:col0 + (g // 2 + 1) * LANES]
        keep = (_lane_iota(blk.shape) >= HEAD_DIM) == bool(g % 2)
        parts.append(jnp.where(keep, blk, jnp.zeros_like(blk)))
    return jnp.concatenate(parts, axis=0)


def _sink_column(sink_ref, head0, group, rows):
    return jnp.concatenate(
        [jnp.full((rows, 1), sink_ref[head0 + g], F32) for g in range(group)], axis=0)


def _store_heads(o_ref, rows, col0, o, group, nrows):
    for lp in range(group // 2):
        a = o[(2 * lp) * nrows:(2 * lp + 1) * nrows]
        b = o[(2 * lp + 1) * nrows:(2 * lp + 2) * nrows]
        low = _lane_iota(a.shape) < HEAD_DIM
        o_ref[rows, col0 + lp * LANES:col0 + (lp + 1) * LANES] = jnp.where(low, a, b).astype(BF16)


def _attn_pairs(pairs, sink_ref, q_ref, k_ref, v_ref, o_ref, n_kv, group):
    def block_rows(index):
        start = index * PAIR
        return pl.ds(start if isinstance(start, int) else pl.multiple_of(start, PAIR), PAIR)

    first_q = _lane_iota((CHUNK, PAIR)) < CHUNK

    def keys_of(ref, pair, has_prev, tile):
        cols = slice(tile * LANES, (tile + 1) * LANES)
        if has_prev:
            return jnp.concatenate([ref[block_rows(pair - 1), cols], ref[block_rows(pair), cols]],
                                   axis=0)
        return ref[block_rows(pair), cols]

    def scores(pair, has_prev, kv):
        col0 = kv * group * HEAD_DIM
        q_tiles = jnp.concatenate(
            [q_ref[block_rows(pair), col0 + lp * LANES:col0 + (lp + 1) * LANES]
             for lp in range(group // 2)], axis=0)
        return [_dot_nt(keys_of(k_ref, pair, has_prev, 2 * kv + half), q_tiles)
                for half in range(2)]

    items = [(pair, has_prev, kv) for pair, has_prev in pairs for kv in range(n_kv)]
    st_next = scores(*items[0])
    for n, (pair, has_prev, kv) in enumerate(items):
        st = st_next
        if n + 1 < len(items):
            st_next = scores(*items[n + 1])
        rows = block_rows(pair)
        vt = keys_of(v_ref, pair, has_prev, kv).astype(F32).T.astype(BF16)
        col0 = kv * group * HEAD_DIM
        probs, rden = [], []
        for g in range(group):
            sg = st[g % 2][:, (g // 2) * PAIR:(g // 2 + 1) * PAIR]
            blocks = [sg[c * CHUNK:(c + 1) * CHUNK] for c in range(sg.shape[0] // CHUNK)]
            if has_prev:
                s = jnp.concatenate([jnp.where(first_q, blocks[0], blocks[3]),
                                     blocks[1], blocks[2]], axis=0)
            else:
                s = jnp.concatenate([blocks[0], jnp.where(first_q, NEG_INF, blocks[1])], axis=0)
            sink = sink_ref[kv * group + g]
            m = jnp.maximum(jnp.max(s, axis=0, keepdims=True), sink)
            e = jnp.exp2(s - m)
            den = jnp.sum(e, axis=0, keepdims=True) + jnp.exp2(sink - m)
            if has_prev:
                e_03, e_12 = e[:CHUNK], e[CHUNK:]
                e = jnp.concatenate([jnp.where(first_q, e_03, 0.0), e_12,
                                     jnp.where(first_q, 0.0, e_03)], axis=0)
            probs.append(e.astype(BF16))
            rden.append(1.0 / den)
        ot = _dot(vt, jnp.concatenate(probs, axis=1)) * jnp.concatenate(rden, axis=1)
        for lp in range(group // 2):
            a = ot[:HEAD_DIM, (2 * lp) * PAIR:(2 * lp + 1) * PAIR]
            b = ot[HEAD_DIM:, (2 * lp + 1) * PAIR:(2 * lp + 2) * PAIR]
            o_ref[rows, col0 + lp * LANES:col0 + (lp + 1) * LANES] = (
                jnp.concatenate([a, b], axis=0).T.astype(BF16))


def _attn_kernel(sink_ref, q_ref, k_ref, v_ref, o_ref, *, n_kv, group):
    def run(pairs):
        _attn_pairs(pairs, sink_ref, q_ref, k_ref, v_ref, o_ref, n_kv, group)

    def trip(it, carry):
        run([(1 + PAIRS_PER_TRIP * it + u, True) for u in range(PAIRS_PER_TRIP)])
        return carry

    n_pairs = q_ref.shape[0] // PAIR
    trips = (n_pairs - 1) // PAIRS_PER_TRIP
    run([(0, False)])
    lax.fori_loop(0, trips, trip, 0)
    leftover = range(1 + trips * PAIRS_PER_TRIP, n_pairs)
    if leftover:
        run([(pair, True) for pair in leftover])


def _attention(q, k2, v2, sinks, *, batch, seq, n_kv, group):
    m, d = q.shape
    assert WINDOW == PAIR and seq % PAIR == 0
    est = (2 * (2 * _nbytes((seq, d), BF16) + 3 * _nbytes((seq, n_kv * LANES), BF16))
           + 8 * _nbytes((group * PAIR, KEY_SPAN), F32))
    return pl.pallas_call(
        functools.partial(_attn_kernel, n_kv=n_kv, group=group),
        out_shape=jax.ShapeDtypeStruct((m, d), BF16),
        grid=(batch,),
        in_specs=[pl.BlockSpec(memory_space=pltpu.SMEM),
                  pl.BlockSpec((seq, d), lambda b: (b, 0)),
                  pl.BlockSpec((seq, 2 * n_kv * LANES), lambda b: (b, 0)),
                  pl.BlockSpec((seq, n_kv * LANES), lambda b: (b, 0))],
        out_specs=pl.BlockSpec((seq, d), lambda b: (b, 0)),
        compiler_params=_params(("parallel",), est),
        name="attention",
    )(sinks, q, k2, v2)


def _attn_sample_kernel(sink_ref, q_ref, kn_ref, vn_ref, kc_ref, vc_ref, o_ref, *, n_kv, group):
    t = q_ref.shape[0]
    for kv in range(n_kv):
        pair_sl = slice((kv // 2) * LANES, (kv // 2 + 1) * LANES)
        kc = _dup_halves(kc_ref[:, pair_sl])[kv % 2].astype(BF16)
        vc = _dup_halves(vc_ref[:, pair_sl])[kv % 2].astype(BF16)
        kn = (kn_ref[:, 2 * kv * LANES:(2 * kv + 1) * LANES]
              + kn_ref[:, (2 * kv + 1) * LANES:(2 * kv + 2) * LANES])
        vn = vn_ref[:, kv * LANES:(kv + 1) * LANES]
        col0 = kv * group * HEAD_DIM
        lhs = _stack_heads(q_ref, slice(0, t), col0, group)
        s_c = _dot_nt(lhs, kc)
        s_n = _dot_nt(lhs, kn)
        sink = _sink_column(sink_ref, kv * group, group, t)
        m = jnp.maximum(jnp.maximum(jnp.max(s_c, axis=1, keepdims=True),
                                    jnp.max(s_n, axis=1, keepdims=True)), sink)
        e_c = jnp.exp2(s_c - m)
        e_n = jnp.exp2(s_n - m)
        den = (jnp.sum(e_c, axis=1, keepdims=True) + jnp.sum(e_n, axis=1, keepdims=True)
               + jnp.exp2(sink - m))
        o = (_dot(e_c.astype(BF16), vc) + _dot(e_n.astype(BF16), vn)) / den
        _store_heads(o_ref, slice(0, t), col0, o, group, t)


def _attention_sample(q, k2, v2, cache_k, cache_v, sinks, *, batch, t_new, n_kv, group):
    m, d = q.shape
    win = cache_k.shape[0] // batch
    d_kv = cache_k.shape[1]
    est = 4 * 1024 * 1024
    return pl.pallas_call(
        functools.partial(_attn_sample_kernel, n_kv=n_kv, group=group),
        out_shape=jax.ShapeDtypeStruct((m, d), BF16),
        grid=(batch,),
        in_specs=[pl.BlockSpec(memory_space=pltpu.SMEM),
                  pl.BlockSpec((t_new, d), lambda b: (b, 0)),
                  pl.BlockSpec((t_new, 2 * n_kv * LANES), lambda b: (b, 0)),
                  pl.BlockSpec((t_new, n_kv * LANES), lambda b: (b, 0)),
                  pl.BlockSpec((win, d_kv), lambda b: (b, 0)),
                  pl.BlockSpec((win, d_kv), lambda b: (b, 0))],
        out_specs=pl.BlockSpec((t_new, d), lambda b: (b, 0)),
        compiler_params=_params(("parallel",), est),
        name="attention_sample",
    )(sinks, q, k2, v2, cache_k, cache_v)


def _merge_kernel(xn_ref, a_ref, z_ref, wga_ref, wgc_ref, wap_ref, wcp_ref, o_ref):
    xn = xn_ref[...]
    attn = jax.nn.sigmoid(_dot(xn, wga_ref[...])) * _dot(a_ref[...], wap_ref[...])
    conv = jax.nn.sigmoid(_dot(xn, wgc_ref[...])) * _dot(z_ref[...], wcp_ref[...])
    o_ref[...] = (attn + conv).astype(BF16)


def _merge(xn, attn, z, w_in, w_ap, w_cp, *, tm, tn, col_ga, col_gc):
    m, d = xn.shape
    act = pl.BlockSpec((tm, d), lambda i, j: (i, 0))
    wspec = lambda off: pl.BlockSpec((d, tn), lambda i, j: (0, off // tn + j))
    est = (2 * (3 * _nbytes((tm, d), BF16) + 4 * _nbytes((d, tn), BF16) + _nbytes((tm, tn), BF16))
           + 6 * _nbytes((tm, tn), F32))
    return pl.pallas_call(
        _merge_kernel,
        out_shape=jax.ShapeDtypeStruct((m, d), BF16),
        grid=(m // tm, d // tn),
        in_specs=[act, act, act, wspec(col_ga), wspec(col_gc), wspec(0), wspec(0)],
        out_specs=pl.BlockSpec((tm, tn), lambda i, j: (i, j)),
        compiler_params=_params(("parallel", "arbitrary"), est),
        name="merge",
    )(xn, attn, z, w_in, w_in, w_ap, w_cp)


def _out_kernel(x_ref, m_ref, w_ref, g_ref, x1_ref, xn_ref, *, rows):
    for r in range(x_ref.shape[0] // rows):
        rs = slice(r * rows, (r + 1) * rows)
        x1 = x_ref[rs, :] + _dot(m_ref[rs, :], w_ref[...])
        x1_ref[rs, :] = x1
        xn_ref[rs, :] = _rmsnorm_rows(x1, g_ref[...]).astype(BF16)


def _out_proj(x, merged, w_out, g_ffn, *, tm, rows):
    m, d = x.shape
    assert m % tm == 0 and tm % rows == 0
    est = (2 * (2 * _nbytes((tm, d), F32) + 2 * _nbytes((tm, d), BF16)) + _nbytes((d, d), BF16)
           + 4 * _nbytes((rows, d), F32))
    return pl.pallas_call(
        functools.partial(_out_kernel, rows=rows),
        out_shape=(jax.ShapeDtypeStruct((m, d), F32), jax.ShapeDtypeStruct((m, d), BF16)),
        grid=(m // tm,),
        in_specs=[pl.BlockSpec((tm, d), lambda i: (i, 0)),
                  pl.BlockSpec((tm, d), lambda i: (i, 0)),
                  pl.BlockSpec((d, d), lambda i: (0, 0), pipeline_mode=pl.Buffered(1)),
                  pl.BlockSpec((1, d), lambda i: (0, 0))],
        out_specs=(pl.BlockSpec((tm, d), lambda i: (i, 0)),
                   pl.BlockSpec((tm, d), lambda i: (i, 0))),
        compiler_params=_params(("parallel",), est),
        name="out_proj",
    )(x, merged, w_out, g_ffn)


def _ffn_kernel(x_ref, xn_ref, wg_ref, wu_ref, wd_ref, o_ref):
    f = pl.program_id(1)
    slab = x_ref.shape[1]
    n_slabs = o_ref.shape[1] // slab

    @pl.when(f == 0)
    def _():
        o_ref[...] = jnp.zeros_like(o_ref)

    xn = xn_ref[...]
    h = jax.nn.silu(_dot(xn, wg_ref[...])) * _dot(xn, wu_ref[...])
    o_ref[...] += _dot(h.astype(BF16), wd_ref[...])

    for s in range(n_slabs):
        @pl.when(f == s)
        def _(s=s):
            o_ref[:, s * slab:(s + 1) * slab] += x_ref[...]


def _ffn(x, xn, w_gate, w_up, w_down, *, tm, tf, slab):
    m, d = x.shape
    d_ff = w_down.shape[0]
    n_slabs = d // slab
    assert n_slabs <= d_ff // tf and w_gate.shape == w_up.shape == (d_ff // tf, d, tf)
    est = (2 * (_nbytes((tm, slab), F32) + _nbytes((tm, d), F32) + _nbytes((tm, d), BF16)
                + 3 * _nbytes((d, tf), BF16))
           + 4 * _nbytes((tm, tf), F32) + _nbytes((tm, d), F32))
    row = lambda i, f: (i, 0)
    return pl.pallas_call(
        _ffn_kernel,
        out_shape=jax.ShapeDtypeStruct((m, d), F32),
        grid=(m // tm, d_ff // tf),
        in_specs=[pl.BlockSpec((tm, slab), lambda i, f: (i, jnp.minimum(f, n_slabs - 1))),
                  pl.BlockSpec((tm, d), row),
                  pl.BlockSpec((None, d, tf), lambda i, f: (f, 0, 0)),
                  pl.BlockSpec((None, d, tf), lambda i, f: (f, 0, 0)),
                  pl.BlockSpec((tf, d), lambda i, f: (f, 0))],
        out_specs=pl.BlockSpec((tm, d), row),
        compiler_params=_params(("parallel", "arbitrary"), est),
        name="ffn",
    )(x, xn, w_gate, w_up, w_down)


def _rope_tables(pos, repeat, g, scale):
    half = HEAD_DIM // 2
    inv = jnp.exp(-math.log(ROPE_THETA) * jnp.arange(half, dtype=F32) * (2.0 / HEAD_DIM))
    ang = pos.astype(F32)[:, None] * inv[None, :]
    cos = jnp.cos(ang)
    sin = jnp.sin(ang)
    g_swap = jnp.concatenate([g[half:], g[:half]])
    t_same = g[None, :] * jnp.concatenate([cos, cos], axis=-1) * scale
    t_swap = g_swap[None, :] * jnp.concatenate([-sin, sin], axis=-1) * scale
    reps = (repeat, LANES // HEAD_DIM)
    return jnp.tile(t_same, reps), jnp.tile(t_swap, reps)


LATE_WEIGHTS = ("w_attn_proj", "w_conv_proj", "w_out", "w_ffn_gate", "w_ffn_up", "w_ffn_down")
FFN_IN_WEIGHTS = ("w_ffn_gate", "w_ffn_up")


def _layer(x, w, *, batch, seq, pos, tiles, conv_prev, sample_cache):
    d = x.shape[1]
    n_heads = w["sinks"].shape[0]
    d_attn = n_heads * HEAD_DIM
    d_kv = w["d_kv"]
    n_kv = d_kv // HEAD_DIM
    group = n_heads // n_kv
    d_conv = w["conv_w"].shape[1]
    col_b = 0
    col_c = col_b + d_conv
    col_h = col_c + d_conv
    col_ga = col_h + d_conv
    col_gc = col_ga + d
    tm = tiles["tm"]

    rope_rows = max(seq, tiles["tm_qkv"])
    log2e = math.log2(math.e)
    tables = (_rope_tables(pos, rope_rows // seq, w["q_norm_g"], HEAD_DIM ** -0.5 * log2e)
              + _rope_tables(pos, rope_rows // seq, w["k_norm_g"], 1.0))
    sinks = w["sinks"] * log2e

    w_f32 = None if "w_rest" in w else w["w_in_f32"]
    xn, q, k, v, k2, v2, *w_rest = _qkv_proj(
        x, w["norm_mix_g"], w["w_qkv"], tables, tiles["kv_tail"], w_f32,
        tm=tiles["tm_qkv"], d_attn=d_attn, d_kv=d_kv, rows=tiles["qkv_rows"])
    if w_rest:
        w = dict(w, w_rest=w_rest[0])
    late = [name for name in LATE_WEIGHTS if w[name].dtype != BF16]
    z, u_tail, *late_bf16 = _conv_branch(
        xn, w["w_rest"], w["conv_w"], conv_prev,
        [(w[name], tiles["tf"] if name in FFN_IN_WEIGHTS else None) for name in late],
        tm=tiles["tm_conv"], tn=tiles["tn_conv"], rows=tiles["conv_rows"], seq_len=seq,
        tail_rows=tiles["conv_tail"], col_b=col_b, col_c=col_c, col_h=col_h)
    w = dict(w, **dict(zip(late, late_bf16)))
    if sample_cache is None:
        attn = _attention(q, k2, v2, sinks, batch=batch, seq=seq, n_kv=n_kv, group=group)
    else:
        attn = _attention_sample(q, k2, v2, sample_cache[0], sample_cache[1], sinks,
                                 batch=batch, t_new=seq, n_kv=n_kv, group=group)
    merged = _merge(xn, attn, z, w["w_rest"], w["w_attn_proj"], w["w_conv_proj"],
                    tm=tm, tn=tiles["tn_merge"], col_ga=col_ga, col_gc=col_gc)
    x1, xn_ffn = _out_proj(x, merged, w["w_out"], w["norm_ffn_g"],
                           tm=tiles["tm_out"], rows=tiles["out_rows"])
    y = _ffn(x1, xn_ffn, w["w_ffn_gate"], w["w_ffn_up"], w["w_ffn_down"],
             tm=tiles["tm_ffn"], tf=tiles["tf"], slab=V7X_MXU_DIM)
    return y, k, v, u_tail, w


def kernel(x_prompt, x_sample, cache_k, cache_v, state_conv, norm_mix_g, w_in, q_norm_g, k_norm_g,
           attn_sinks, conv_w, w_attn_proj, w_conv_proj, w_out, norm_ffn_g, w_ffn_gate, w_ffn_up,
           w_ffn_down):
    depth = w_in.shape[0]
    assert depth == 1
    b_p, s_p, d = x_prompt.shape
    b_s, s_s, _ = x_sample.shape
    n_kv = cache_k.shape[3]
    d_kv = n_kv * HEAD_DIM
    win = cache_k.shape[2]
    n_state = state_conv.shape[2]

    w = {
        "d_kv": d_kv,
        "norm_mix_g": norm_mix_g[0].reshape(1, d),
        "w_qkv": w_in[0][:, :attn_sinks.shape[1] * HEAD_DIM + 2 * d_kv].astype(BF16),
        "w_in_f32": w_in[0],
        "q_norm_g": q_norm_g[0],
        "k_norm_g": k_norm_g[0],
        "sinks": attn_sinks[0],
        "conv_w": conv_w[0],
        "w_attn_proj": w_attn_proj[0],
        "w_conv_proj": w_conv_proj[0],
        "w_out": w_out[0],
        "norm_ffn_g": norm_ffn_g[0].reshape(1, d),
        "w_ffn_gate": w_ffn_gate[0],
        "w_ffn_up": w_ffn_up[0],
        "w_ffn_down": w_ffn_down[0],
    }

    m_p = b_p * s_p
    tm_p = min(1024, s_p)
    keep = min(WINDOW, s_p)
    tiles_p = dict(tm=tm_p, tm_qkv=min(512, s_p), qkv_rows=min(512, s_p), kv_tail=(s_p, keep), tm_conv=s_p, tn_conv=512, conv_tail=8,
                   conv_rows=min(512, s_p), tn_merge=512, tm_out=min(512, s_p), out_rows=256,
                   tm_ffn=tm_p, tf=512)
    y_p, k_p, v_p, u_p, w = _layer(x_prompt.reshape(m_p, d), w, batch=b_p, seq=s_p,
                                   pos=jnp.arange(s_p), tiles=tiles_p, conv_prev=None,
                                   sample_cache=None)
    new_k_p = k_p.reshape(b_p, n_kv, HEAD_DIM, keep).transpose(0, 3, 1, 2)[None]
    new_v_p = v_p.reshape(b_p, n_kv, HEAD_DIM, keep).transpose(0, 3, 1, 2)[None]
    new_c_p = u_p.reshape(b_p, tiles_p["conv_tail"], -1)[:, tiles_p["conv_tail"] - n_state:][None]

    m_s = b_s * s_s
    st = state_conv[0]
    prev1 = jnp.pad(st[:, 1:2], ((0, 0), (0, s_s - 1), (0, 0))).reshape(m_s, -1)
    prev2 = jnp.pad(st, ((0, 0), (0, s_s - n_state), (0, 0))).reshape(m_s, -1)
    tiles_s = dict(tm=m_s, tm_qkv=m_s, qkv_rows=m_s, kv_tail=None, tm_conv=m_s, tn_conv=512, conv_tail=m_s,
                   conv_rows=m_s, tn_merge=512, tm_out=m_s, out_rows=m_s, tm_ffn=m_s, tf=512)
    y_s, k_s, v_s, u_s, _ = _layer(x_sample.reshape(m_s, d), w, batch=b_s, seq=s_s,
                                   pos=PAST_LEN + jnp.arange(s_s), tiles=tiles_s,
                                   conv_prev=(prev1, prev2),
                                   sample_cache=(cache_k[0].reshape(b_s * win, d_kv),
                                                 cache_v[0].reshape(b_s * win, d_kv)))
    new_k_s = k_s.reshape(b_s, s_s, n_kv, HEAD_DIM)[None]
    new_v_s = v_s.reshape(b_s, s_s, n_kv, HEAD_DIM)[None]
    new_c_s = u_s.reshape(b_s, s_s, -1)[:, s_s - n_state:][None]

    return (y_p.reshape(b_p, s_p, d), y_s.reshape(b_s, s_s, d), new_k_p, new_v_p, new_c_p,
            new_k_s, new_v_s, new_c_s)
```

```python
import functools
import math

import jax
import jax.numpy as jnp
from jax import lax
from jax.experimental import pallas as pl
from jax.experimental.pallas import tpu as pltpu

F32 = jnp.float32
BF16 = jnp.bfloat16

CHUNK = 64
WINDOW = 128
PAST_LEN = 1024
ROPE_THETA = 10000.0
EPS = 1e-6
NEG_INF = -1e30
HEAD_DIM = 64
LANES = 128
BF16_SUBLANES = 16
PAIR = 2 * CHUNK
KEY_SPAN = PAIR + WINDOW
PAIRS_PER_TRIP = 3

V7X_MXU_DIM = 256
V7X_VMEM_BYTES = 64 * 1024 * 1024
VMEM_CAP_BYTES = V7X_VMEM_BYTES - 6 * 1024 * 1024


def _params(semantics, vmem_estimate):
    limit = int(min(max(vmem_estimate, 16 * 1024 * 1024), VMEM_CAP_BYTES))
    return pltpu.CompilerParams(dimension_semantics=semantics, vmem_limit_bytes=limit)


def _nbytes(shape, dtype):
    return math.prod(shape) * jnp.dtype(dtype).itemsize


def _dot(a, b):
    return jnp.dot(a, b, preferred_element_type=F32)


def _dot_nt(a, b):
    return lax.dot_general(a, b, (((1,), (1,)), ((), ())), preferred_element_type=F32)


def _lane_iota(shape):
    return lax.broadcasted_iota(jnp.int32, shape, 1)


def _rmsnorm_rows(x, g):
    y = x * lax.rsqrt(jnp.mean(x * x, axis=-1, keepdims=True) + EPS)
    return y * g


def _headnorm_rope(y, t_same, t_swap, seg):
    width = y.shape[1]
    s = y * y
    s_hi = s.astype(BF16)
    s_lo = (s - s_hi.astype(F32)).astype(BF16)
    ms = _dot(s_hi, seg) + _dot(s_lo, seg)
    half = HEAD_DIM // 2
    ahead = pltpu.roll(y, width - half, axis=1)
    behind = pltpu.roll(y, half, axis=1)
    swapped = jnp.where((_lane_iota(y.shape) & (HEAD_DIM - 1)) < half, ahead, behind)
    reps = width // t_same.shape[1]
    t_same = jnp.concatenate([t_same] * reps, axis=1)
    t_swap = jnp.concatenate([t_swap] * reps, axis=1)
    return (y * t_same + swapped * t_swap) * lax.rsqrt(ms + EPS)


def _dup_halves(x):
    low = _lane_iota(x.shape) < HEAD_DIM
    r = pltpu.roll(x, HEAD_DIM, axis=1)
    return jnp.where(low, x, r), jnp.where(low, r, x)


def _isolate_halves(x):
    low = _lane_iota(x.shape) < HEAD_DIM
    r = pltpu.roll(x, HEAD_DIM, axis=1)
    return (jnp.where(low, x, 0.0), jnp.where(low, 0.0, r),
            jnp.where(low, r, 0.0), jnp.where(low, 0.0, x))


def _qkv_kernel(*refs, n_q_blocks, d_kv, rows, tail_keep, n_rest):
    x_ref, g_ref, w_ref, seg_ref, qs_ref, qw_ref, ks_ref, kw_ref = refs[:8]
    rest_refs = refs[8:8 + n_rest]
    xn_ref, q_ref, k_ref, v_ref, k2_ref, v2_ref = refs[8 + n_rest:][:6]
    rest_out = refs[-1] if n_rest else None
    tm = q_ref.shape[0]
    tn = 2 * d_kv
    chunks = [slice(r * rows, (r + 1) * rows) for r in range(tm // rows)]
    seg = seg_ref[...]
    slab = seg.shape[0]

    for rs in chunks:
        xn = _rmsnorm_rows(x_ref[rs, :], g_ref[...]).astype(BF16)
        xn_ref[rs, :] = xn
        for jb in range(n_q_blocks):
            y = _dot(xn, w_ref[:, jb * tn:(jb + 1) * tn])
            for c in range(tn // slab):
                sl = slice(c * slab, (c + 1) * slab)
                q_ref[rs, jb * tn + c * slab:jb * tn + (c + 1) * slab] = _headnorm_rope(
                    y[:, sl], qs_ref[rs, :], qw_ref[rs, :], seg).astype(BF16)

        y = _dot(xn, w_ref[:, n_q_blocks * tn:])
        k_all = jnp.concatenate(
            [_headnorm_rope(y[:, c * slab:(c + 1) * slab], ks_ref[rs, :], kw_ref[rs, :], seg)
             for c in range(d_kv // slab)], axis=1)
        if tail_keep is None:
            k_ref[rs, :] = k_all
            v_ref[rs, :] = y[:, d_kv:]
        elif rs is chunks[-1]:
            k_ref[0] = k_all[rows - tail_keep:].T
            v_ref[0] = y[rows - tail_keep:, d_kv:].T
        for c in range(d_kv // LANES):
            sl = slice(c * LANES, (c + 1) * LANES)
            k = k_all[:, sl]
            v = y[:, d_kv + c * LANES:d_kv + (c + 1) * LANES]
            va, vb = _dup_halves(v)
            lo = slice(2 * c * LANES, (2 * c + 1) * LANES)
            hi = slice((2 * c + 1) * LANES, (2 * c + 2) * LANES)
            v2_ref[rs, lo] = va.astype(BF16)
            v2_ref[rs, hi] = vb.astype(BF16)
            for n, piece in enumerate(_isolate_halves(k)):
                k2_ref[rs, (4 * c + n) * LANES:(4 * c + n + 1) * LANES] = piece.astype(BF16)

        for n, ref in list(enumerate(rest_refs))[chunks.index(rs)::len(chunks)]:
            width = ref.shape[1]
            rest_out[:, n * width:(n + 1) * width] = ref[...].astype(BF16)


def _qkv_proj(x, g, w_in, tables, tail, w_f32, *, tm, d_attn, d_kv, rows):
    m, d = x.shape
    tn = 2 * d_kv
    n_q_blocks = d_attn // tn
    qkv_cols = d_attn + tn
    assert m % tm == 0 and tm % rows == 0 and d_attn % tn == 0 and w_in.shape == (d, qkv_cols)
    n_tiles = m // tm
    rest_specs, rest_shape, rest_out_spec, n_rest = [], (), (), 0
    if w_f32 is not None:
        n_rest = (w_f32.shape[1] - qkv_cols) // qkv_cols
        assert w_f32.shape[1] == (n_rest + 1) * qkv_cols and d % (n_tiles * BF16_SUBLANES) == 0
        slab_rows = d // n_tiles
        rest_specs = [pl.BlockSpec((slab_rows, qkv_cols), functools.partial(
            lambda n, i: (i, n + 1), n)) for n in range(n_rest)]
        rest_shape = (jax.ShapeDtypeStruct((d, n_rest * qkv_cols), BF16),)
        rest_out_spec = (pl.BlockSpec((slab_rows, n_rest * qkv_cols), lambda i: (i, 0)),)
    if tail is None:
        kv_shape = jax.ShapeDtypeStruct((m, d_kv), F32)
        kv_spec = pl.BlockSpec((tm, d_kv), lambda i: (i, 0))
    else:
        seq, keep = tail
        assert seq % tm == 0 and keep <= rows and keep % LANES == 0
        kv_shape = jax.ShapeDtypeStruct((m // seq, d_kv, keep), F32)
        kv_spec = pl.BlockSpec((1, d_kv, keep), lambda i: (i // (seq // tm), 0, 0))
    rope_blocks = tables[0].shape[0] // tm
    tab = pl.BlockSpec((tm, LANES), lambda i: (i % rope_blocks, 0))
    row = lambda width: pl.BlockSpec((tm, width), lambda i: (i, 0))
    fixed = lambda shape: pl.BlockSpec(shape, lambda i: (0, 0), pipeline_mode=pl.Buffered(1))
    head_of = jnp.arange(V7X_MXU_DIM) // HEAD_DIM
    seg = jnp.where(head_of[:, None] == head_of[None, :], 1.0 / HEAD_DIM, 0.0).astype(BF16)
    est = (2 * (_nbytes((tm, d), F32) + _nbytes((tm, d), BF16) + _nbytes((tm, d_attn), BF16)
                + 3 * _nbytes((tm, tn), BF16) + 2 * _nbytes((tm, d_kv), F32)
                + 4 * _nbytes((tm, LANES), F32))
           + _nbytes(w_in.shape, BF16) + 6 * _nbytes((rows, tn), F32) + 2 * _nbytes((rows, d), F32))
    if n_rest:
        est += 3 * _nbytes((d // n_tiles, n_rest * qkv_cols), F32)
    return pl.pallas_call(
        functools.partial(_qkv_kernel, n_q_blocks=n_q_blocks, d_kv=d_kv, rows=rows,
                          tail_keep=None if tail is None else tail[1], n_rest=n_rest),
        out_shape=(jax.ShapeDtypeStruct((m, d), BF16), jax.ShapeDtypeStruct((m, d_attn), BF16),
                   kv_shape, kv_shape,
                   jax.ShapeDtypeStruct((m, 2 * tn), BF16), jax.ShapeDtypeStruct((m, tn), BF16),
                   *rest_shape),
        grid=(n_tiles,),
        in_specs=[row(d), fixed((1, d)), fixed(w_in.shape), fixed(seg.shape),
                  tab, tab, tab, tab] + rest_specs,
        out_specs=(row(d), row(d_attn), kv_spec, kv_spec, row(2 * tn), row(tn), *rest_out_spec),
        compiler_params=_params(("arbitrary",), est),
        name="qkv_proj",
    )(x, g, w_in, seg, *tables, *([w_f32] * n_rest))


def _conv_kernel(*refs, seq_len, tail_rows, has_state, rows, cast_per_step):
    n_cast = len(cast_per_step)
    n_in = 7 if has_state else 5
    xn_ref, wb_ref, wc_ref, wh_ref, cw_ref = refs[:5]
    p1_ref, p2_ref = refs[5:n_in] if has_state else (None, None)
    z_ref, u_ref = refs[n_in + n_cast:n_in + n_cast + 2]
    casts = list(zip(refs[n_in:n_in + n_cast], refs[n_in + n_cast + 2:]))
    tm, tn = z_ref.shape
    w = cw_ref[...]
    row = lax.broadcasted_iota(jnp.int32, (rows, tn), 0)
    last1 = last2 = None
    for r in range(tm // rows):
        rs = slice(r * rows, (r + 1) * rows)
        xn = xn_ref[rs, :]
        u = _dot(xn, wc_ref[...]) * _dot(xn, wh_ref[...])
        u1 = pltpu.roll(u, 1, axis=0)
        u2 = pltpu.roll(u, 2, axis=0)
        if r > 0:
            u1 = jnp.where(row == 0, last1, u1)
            u2 = jnp.where(row == 0, last2, jnp.where(row == 1, last1, u2))
        if (r * rows) % seq_len == 0 or seq_len < rows:
            pos = (row + r * rows) & (seq_len - 1)
            u1 = jnp.where(pos == 0, p1_ref[rs, :] if has_state else 0.0, u1)
            u2 = jnp.where(pos < 2, p2_ref[rs, :] if has_state else 0.0, u2)
        y = u2 * w[0:1] + u1 * w[1:2] + u * w[2:3]
        z_ref[rs, :] = (_dot(xn, wb_ref[...]) * y).astype(BF16)
        last1, last2 = u[rows - 1:rows], u[rows - 2:rows - 1]
        for (src_ref, dst_ref), per_step in list(zip(casts, cast_per_step))[r::tm // rows]:
            if per_step:
                _copy_slab(src_ref, dst_ref)
            else:
                pl.when(pl.program_id(1) == 0)(functools.partial(_copy_slab, src_ref, dst_ref))
    u_ref[...] = u[rows - tail_rows:]


def _copy_slab(src_ref, dst_ref):
    if len(dst_ref.shape) == 2:
        dst_ref[...] = src_ref[...].astype(BF16)
    else:
        cb = dst_ref.shape[2]
        for b in range(dst_ref.shape[0]):
            dst_ref[b] = src_ref[:, b * cb:(b + 1) * cb].astype(BF16)


def _conv_branch(xn, w_in, conv_w, prev, cast, *, tm, tn, rows, seq_len, tail_rows,
                 col_b, col_c, col_h):
    m, d = xn.shape
    n = conv_w.shape[1]
    has_state = prev is not None
    assert seq_len & (seq_len - 1) == 0 and tm % seq_len == 0
    assert tm % rows == 0 and tail_rows <= rows and (rows % seq_len == 0 or seq_len % rows == 0)
    n_j = n // tn
    steps = (m // tm) * n_j
    wspec = lambda off: pl.BlockSpec((d, tn), lambda i, j: (0, off // tn + j))
    in_specs = [pl.BlockSpec((tm, d), lambda i, j: (i, 0)),
                wspec(col_b), wspec(col_c), wspec(col_h),
                pl.BlockSpec((conv_w.shape[0], tn), lambda i, j: (0, j))]
    args = [xn, w_in, w_in, w_in, conv_w]
    if has_state:
        in_specs += [pl.BlockSpec((tm, tn), lambda i, j: (i, j))] * 2
        args += list(prev)
    cast_per_step = tuple(c.shape[0] % (steps * BF16_SUBLANES) == 0 for c, _ in cast)
    slab_specs, copy_specs, copy_shapes, slab_bytes = [], [], [], 0
    for (c, col_block), per_step in zip(cast, cast_per_step):
        parts = steps if per_step else m // tm
        r, cols = c.shape
        assert r % (parts * BF16_SUBLANES) == 0
        step_of = (lambda i, j: i * n_j + j) if per_step else (lambda i, j: i)
        slab_specs.append(pl.BlockSpec((r // parts, cols),
                                       functools.partial(lambda s, i, j: (s(i, j), 0), step_of)))
        if col_block is None:
            copy_specs.append(slab_specs[-1])
            copy_shapes.append(jax.ShapeDtypeStruct((r, cols), BF16))
        else:
            assert cols % col_block == 0
            copy_specs.append(pl.BlockSpec(
                (cols // col_block, r // parts, col_block),
                functools.partial(lambda s, i, j: (0, s(i, j), 0), step_of)))
            copy_shapes.append(jax.ShapeDtypeStruct((cols // col_block, r, col_block), BF16))
        slab_bytes += 3 * _nbytes(c.shape, F32) // parts
    est = (2 * (_nbytes((tm, d), BF16) + 3 * _nbytes((d, tn), BF16) + _nbytes((tm, tn), BF16)
                + _nbytes((tail_rows, tn), F32)) + 8 * _nbytes((tm, tn), F32) + slab_bytes)
    return pl.pallas_call(
        functools.partial(_conv_kernel, seq_len=seq_len, tail_rows=tail_rows, has_state=has_state,
                          rows=rows, cast_per_step=cast_per_step),
        out_shape=(jax.ShapeDtypeStruct((m, n), BF16),
                   jax.ShapeDtypeStruct((m // tm * tail_rows, n), F32),
                   *copy_shapes),
        grid=(m // tm, n_j),
        in_specs=in_specs + slab_specs,
        out_specs=(pl.BlockSpec((tm, tn), lambda i, j: (i, j)),
                   pl.BlockSpec((tail_rows, tn), lambda i, j: (i, j)),
                   *copy_specs),
        compiler_params=_params(("parallel", "arbitrary"), est),
        name="conv_branch",
    )(*args, *[c for c, _ in cast])


def _stack_heads(q_ref, rows, col0, group):
    parts = []
    for g in range(group):
        blk = q_ref[rows, col0 + (g // 2) * LANES:col0 + (g // 2 + 1) * LANES]
        keep = (_lane_iota(blk.shape) >= HEAD_DIM) == bool(g % 2)
        parts.append(jnp.where(keep, blk, jnp.zeros_like(blk)))
    return jnp.concatenate(parts, axis=0)


def _sink_column(sink_ref, head0, group, rows):
    return jnp.concatenate(
        [jnp.full((rows, 1), sink_ref[head0 + g], F32) for g in range(group)], axis=0)


def _store_heads(o_ref, rows, col0, o, group, nrows):
    for lp in range(group // 2):
        a = o[(2 * lp) * nrows:(2 * lp + 1) * nrows]
        b = o[(2 * lp + 1) * nrows:(2 * lp + 2) * nrows]
        low = _lane_iota(a.shape) < HEAD_DIM
        o_ref[rows, col0 + lp * LANES:col0 + (lp + 1) * LANES] = jnp.where(low, a, b).astype(BF16)


def _attn_pairs(pairs, sink_ref, q_ref, k_ref, v_ref, o_ref, n_kv, group):
    def block_rows(index):
        start = index * PAIR
        return pl.ds(start if isinstance(start, int) else pl.multiple_of(start, PAIR), PAIR)

    first_q = _lane_iota((CHUNK, PAIR)) < CHUNK

    def keys_of(ref, pair, has_prev, tile):
        cols = slice(tile * LANES, (tile + 1) * LANES)
        if has_prev:
            return jnp.concatenate([ref[block_rows(pair - 1), cols], ref[block_rows(pair), cols]],
                                   axis=0)
        return ref[block_rows(pair), cols]

    def scores(pair, has_prev, kv):
        col0 = kv * group * HEAD_DIM
        q_tiles = jnp.concatenate(
            [q_ref[block_rows(pair), col0 + lp * LANES:col0 + (lp + 1) * LANES]
             for lp in range(group // 2)], axis=0)
        return [_dot_nt(keys_of(k_ref, pair, has_prev, 2 * kv + half), q_tiles)
                for half in range(2)]

    items = [(pair, has_prev, kv) for pair, has_prev in pairs for kv in range(n_kv)]
    st_next = scores(*items[0])
    for n, (pair, has_prev, kv) in enumerate(items):
        st = st_next
        if n + 1 < len(items):
            st_next = scores(*items[n + 1])
        rows = block_rows(pair)
        vt = keys_of(v_ref, pair, has_prev, kv).astype(F32).T.astype(BF16)
        col0 = kv * group * HEAD_DIM
        probs, rden = [], []
        for g in range(group):
            sg = st[g % 2][:, (g // 2) * PAIR:(g // 2 + 1) * PAIR]
            blocks = [sg[c * CHUNK:(c + 1) * CHUNK] for c in range(sg.shape[0] // CHUNK)]
            if has_prev:
                s = jnp.concatenate([jnp.where(first_q, blocks[0], blocks[3]),
                                     blocks[1], blocks[2]], axis=0)
            else:
                s = jnp.concatenate([blocks[0], jnp.where(first_q, NEG_INF, blocks[1])], axis=0)
            sink = sink_ref[kv * group + g]
            m = jnp.maximum(jnp.max(s, axis=0, keepdims=True), sink)
            e = jnp.exp2(s - m)
            den = jnp.sum(e, axis=0, keepdims=True) + jnp.exp2(sink - m)
            if has_prev:
                e_03, e_12 = e[:CHUNK], e[CHUNK:]
                e = jnp.concatenate([jnp.where(first_q, e_03, 0.0), e_12,
                                     jnp.where(first_q, 0.0, e_03)], axis=0)
            probs.append(e.astype(BF16))
            rden.append(1.0 / den)
        ot = _dot(vt, jnp.concatenate(probs, axis=1)) * jnp.concatenate(rden, axis=1)
        for lp in range(group // 2):
            a = ot[:HEAD_DIM, (2 * lp) * PAIR:(2 * lp + 1) * PAIR]
            b = ot[HEAD_DIM:, (2 * lp + 1) * PAIR:(2 * lp + 2) * PAIR]
            o_ref[rows, col0 + lp * LANES:col0 + (lp + 1) * LANES] = (
                jnp.concatenate([a, b], axis=0).T.astype(BF16))


def _attn_kernel(sink_ref, q_ref, k_ref, v_ref, o_ref, *, n_kv, group):
    def run(pairs):
        _attn_pairs(pairs, sink_ref, q_ref, k_ref, v_ref, o_ref, n_kv, group)

    def trip(it, carry):
        run([(1 + PAIRS_PER_TRIP * it + u, True) for u in range(PAIRS_PER_TRIP)])
        return carry

    n_pairs = q_ref.shape[0] // PAIR
    trips = (n_pairs - 1) // PAIRS_PER_TRIP
    run([(0, False)])
    lax.fori_loop(0, trips, trip, 0)
    leftover = range(1 + trips * PAIRS_PER_TRIP, n_pairs)
    if leftover:
        run([(pair, True) for pair in leftover])


def _attention(q, k2, v2, sinks, *, batch, seq, n_kv, group):
    m, d = q.shape
    assert WINDOW == PAIR and seq % PAIR == 0
    est = (2 * (2 * _nbytes((seq, d), BF16) + 3 * _nbytes((seq, n_kv * LANES), BF16))
           + 8 * _nbytes((group * PAIR, KEY_SPAN), F32))
    return pl.pallas_call(
        functools.partial(_attn_kernel, n_kv=n_kv, group=group),
        out_shape=jax.ShapeDtypeStruct((m, d), BF16),
        grid=(batch,),
        in_specs=[pl.BlockSpec(memory_space=pltpu.SMEM),
                  pl.BlockSpec((seq, d), lambda b: (b, 0)),
                  pl.BlockSpec((seq, 2 * n_kv * LANES), lambda b: (b, 0)),
                  pl.BlockSpec((seq, n_kv * LANES), lambda b: (b, 0))],
        out_specs=pl.BlockSpec((seq, d), lambda b: (b, 0)),
        compiler_params=_params(("parallel",), est),
        name="attention",
    )(sinks, q, k2, v2)


def _attn_sample_kernel(sink_ref, q_ref, kn_ref, vn_ref, kc_ref, vc_ref, o_ref, *, n_kv, group):
    t = q_ref.shape[0]
    for kv in range(n_kv):
        pair_sl = slice((kv // 2) * LANES, (kv // 2 + 1) * LANES)
        kc = _dup_halves(kc_ref[:, pair_sl])[kv % 2].astype(BF16)
        vc = _dup_halves(vc_ref[:, pair_sl])[kv % 2].astype(BF16)
        kn = (kn_ref[:, 2 * kv * LANES:(2 * kv + 1) * LANES]
              + kn_ref[:, (2 * kv + 1) * LANES:(2 * kv + 2) * LANES])
        vn = vn_ref[:, kv * LANES:(kv + 1) * LANES]
        col0 = kv * group * HEAD_DIM
        lhs = _stack_heads(q_ref, slice(0, t), col0, group)
        s_c = _dot_nt(lhs, kc)
        s_n = _dot_nt(lhs, kn)
        sink = _sink_column(sink_ref, kv * group, group, t)
        m = jnp.maximum(jnp.maximum(jnp.max(s_c, axis=1, keepdims=True),
                                    jnp.max(s_n, axis=1, keepdims=True)), sink)
        e_c = jnp.exp2(s_c - m)
        e_n = jnp.exp2(s_n - m)
        den = (jnp.sum(e_c, axis=1, keepdims=True) + jnp.sum(e_n, axis=1, keepdims=True)
               + jnp.exp2(sink - m))
        o = (_dot(e_c.astype(BF16), vc) + _dot(e_n.astype(BF16), vn)) / den
        _store_heads(o_ref, slice(0, t), col0, o, group, t)


def _attention_sample(q, k2, v2, cache_k, cache_v, sinks, *, batch, t_new, n_kv, group):
    m, d = q.shape
    win = cache_k.shape[0] // batch
    d_kv = cache_k.shape[1]
    est = 4 * 1024 * 1024
    return pl.pallas_call(
        functools.partial(_attn_sample_kernel, n_kv=n_kv, group=group),
        out_shape=jax.ShapeDtypeStruct((m, d), BF16),
        grid=(batch,),
        in_specs=[pl.BlockSpec(memory_space=pltpu.SMEM),
                  pl.BlockSpec((t_new, d), lambda b: (b, 0)),
                  pl.BlockSpec((t_new, 2 * n_kv * LANES), lambda b: (b, 0)),
                  pl.BlockSpec((t_new, n_kv * LANES), lambda b: (b, 0)),
                  pl.BlockSpec((win, d_kv), lambda b: (b, 0)),
                  pl.BlockSpec((win, d_kv), lambda b: (b, 0))],
        out_specs=pl.BlockSpec((t_new, d), lambda b: (b, 0)),
        compiler_params=_params(("parallel",), est),
        name="attention_sample",
    )(sinks, q, k2, v2, cache_k, cache_v)


def _merge_kernel(xn_ref, a_ref, z_ref, wga_ref, wgc_ref, wap_ref, wcp_ref, o_ref):
    xn = xn_ref[...]
    attn = jax.nn.sigmoid(_dot(xn, wga_ref[...])) * _dot(a_ref[...], wap_ref[...])
    conv = jax.nn.sigmoid(_dot(xn, wgc_ref[...])) * _dot(z_ref[...], wcp_ref[...])
    o_ref[...] = (attn + conv).astype(BF16)


def _merge(xn, attn, z, w_in, w_ap, w_cp, *, tm, tn, col_ga, col_gc):
    m, d = xn.shape
    act = pl.BlockSpec((tm, d), lambda i, j: (i, 0))
    wspec = lambda off: pl.BlockSpec((d, tn), lambda i, j: (0, off // tn + j))
    est = (2 * (3 * _nbytes((tm, d), BF16) + 4 * _nbytes((d, tn), BF16) + _nbytes((tm, tn), BF16))
           + 6 * _nbytes((tm, tn), F32))
    return pl.pallas_call(
        _merge_kernel,
        out_shape=jax.ShapeDtypeStruct((m, d), BF16),
        grid=(m // tm, d // tn),
        in_specs=[act, act, act, wspec(col_ga), wspec(col_gc), wspec(0), wspec(0)],
        out_specs=pl.BlockSpec((tm, tn), lambda i, j: (i, j)),
        compiler_params=_params(("parallel", "arbitrary"), est),
        name="merge",
    )(xn, attn, z, w_in, w_in, w_ap, w_cp)


def _out_kernel(x_ref, m_ref, w_ref, g_ref, x1_ref, xn_ref, *, rows):
    for r in range(x_ref.shape[0] // rows):
        rs = slice(r * rows, (r + 1) * rows)
        x1 = x_ref[rs, :] + _dot(m_ref[rs, :], w_ref[...])
        x1_ref[rs, :] = x1
        xn_ref[rs, :] = _rmsnorm_rows(x1, g_ref[...]).astype(BF16)


def _out_proj(x, merged, w_out, g_ffn, *, tm, rows):
    m, d = x.shape
    assert m % tm == 0 and tm % rows == 0
    est = (2 * (2 * _nbytes((tm, d), F32) + 2 * _nbytes((tm, d), BF16)) + _nbytes((d, d), BF16)
           + 4 * _nbytes((rows, d), F32))
    return pl.pallas_call(
        functools.partial(_out_kernel, rows=rows),
        out_shape=(jax.ShapeDtypeStruct((m, d), F32), jax.ShapeDtypeStruct((m, d), BF16)),
        grid=(m // tm,),
        in_specs=[pl.BlockSpec((tm, d), lambda i: (i, 0)),
                  pl.BlockSpec((tm, d), lambda i: (i, 0)),
                  pl.BlockSpec((d, d), lambda i: (0, 0), pipeline_mode=pl.Buffered(1)),
                  pl.BlockSpec((1, d), lambda i: (0, 0))],
        out_specs=(pl.BlockSpec((tm, d), lambda i: (i, 0)),
                   pl.BlockSpec((tm, d), lambda i: (i, 0))),
        compiler_params=_params(("parallel",), est),
        name="out_proj",
    )(x, merged, w_out, g_ffn)


def _ffn_kernel(x_ref, xn_ref, wg_ref, wu_ref, wd_ref, o_ref):
    f = pl.program_id(1)
    slab = x_ref.shape[1]
    n_slabs = o_ref.shape[1] // slab

    @pl.when(f == 0)
    def _():
        o_ref[...] = jnp.zeros_like(o_ref)

    xn = xn_ref[...]
    h = jax.nn.silu(_dot(xn, wg_ref[...])) * _dot(xn, wu_ref[...])
    o_ref[...] += _dot(h.astype(BF16), wd_ref[...])

    for s in range(n_slabs):
        @pl.when(f == s)
        def _(s=s):
            o_ref[:, s * slab:(s + 1) * slab] += x_ref[...]


def _ffn(x, xn, w_gate, w_up, w_down, *, tm, tf, slab):
    m, d = x.shape
    d_ff = w_down.shape[0]
    n_slabs = d // slab
    assert n_slabs <= d_ff // tf and w_gate.shape == w_up.shape == (d_ff // tf, d, tf)
    est = (2 * (_nbytes((tm, slab), F32) + _nbytes((tm, d), F32) + _nbytes((tm, d), BF16)
                + 3 * _nbytes((d, tf), BF16))
           + 4 * _nbytes((tm, tf), F32) + _nbytes((tm, d), F32))
    row = lambda i, f: (i, 0)
    return pl.pallas_call(
        _ffn_kernel,
        out_shape=jax.ShapeDtypeStruct((m, d), F32),
        grid=(m // tm, d_ff // tf),
        in_specs=[pl.BlockSpec((tm, slab), lambda i, f: (i, jnp.minimum(f, n_slabs - 1))),
                  pl.BlockSpec((tm, d), row),
                  pl.BlockSpec((None, d, tf), lambda i, f: (f, 0, 0)),
                  pl.BlockSpec((None, d, tf), lambda i, f: (f, 0, 0)),
                  pl.BlockSpec((tf, d), lambda i, f: (f, 0))],
        out_specs=pl.BlockSpec((tm, d), row),
        compiler_params=_params(("parallel", "arbitrary"), est),
        name="ffn",
    )(x, xn, w_gate, w_up, w_down)


def _rope_tables(pos, repeat, g, scale):
    half = HEAD_DIM // 2
    inv = jnp.exp(-math.log(ROPE_THETA) * jnp.arange(half, dtype=F32) * (2.0 / HEAD_DIM))
    ang = pos.astype(F32)[:, None] * inv[None, :]
    cos = jnp.cos(ang)
    sin = jnp.sin(ang)
    g_swap = jnp.concatenate([g[half:], g[:half]])
    t_same = g[None, :] * jnp.concatenate([cos, cos], axis=-1) * scale
    t_swap = g_swap[None, :] * jnp.concatenate([-sin, sin], axis=-1) * scale
    reps = (repeat, LANES // HEAD_DIM)
    return jnp.tile(t_same, reps), jnp.tile(t_swap, reps)


LATE_WEIGHTS = ("w_attn_proj", "w_conv_proj", "w_out", "w_ffn_gate", "w_ffn_up", "w_ffn_down")
FFN_IN_WEIGHTS = ("w_ffn_gate", "w_ffn_up")


def _layer(x, w, *, batch, seq, pos, tiles, conv_prev, sample_cache):
    d = x.shape[1]
    n_heads = w["sinks"].shape[0]
    d_attn = n_heads * HEAD_DIM
    d_kv = w["d_kv"]
    n_kv = d_kv // HEAD_DIM
    group = n_heads // n_kv
    d_conv = w["conv_w"].shape[1]
    col_b = 0
    col_c = col_b + d_conv
    col_h = col_c + d_conv
    col_ga = col_h + d_conv
    col_gc = col_ga + d
    tm = tiles["tm"]

    rope_rows = max(seq, tiles["tm_qkv"])
    log2e = math.log2(math.e)
    tables = (_rope_tables(pos, rope_rows // seq, w["q_norm_g"], HEAD_DIM ** -0.5 * log2e)
              + _rope_tables(pos, rope_rows // seq, w["k_norm_g"], 1.0))
    sinks = w["sinks"] * log2e

    w_f32 = None if "w_rest" in w else w["w_in_f32"]
    xn, q, k, v, k2, v2, *w_rest = _qkv_proj(
        x, w["norm_mix_g"], w["w_qkv"], tables, tiles["kv_tail"], w_f32,
        tm=tiles["tm_qkv"], d_attn=d_attn, d_kv=d_kv, rows=tiles["qkv_rows"])
    if w_rest:
        w = dict(w, w_rest=w_rest[0])
    late = [name for name in LATE_WEIGHTS if w[name].dtype != BF16]
    z, u_tail, *late_bf16 = _conv_branch(
        xn, w["w_rest"], w["conv_w"], conv_prev,
        [(w[name], tiles["tf"] if name in FFN_IN_WEIGHTS else None) for name in late],
        tm=tiles["tm_conv"], tn=tiles["tn_conv"], rows=tiles["conv_rows"], seq_len=seq,
        tail_rows=tiles["conv_tail"], col_b=col_b, col_c=col_c, col_h=col_h)
    w = dict(w, **dict(zip(late, late_bf16)))
    if sample_cache is None:
        attn = _attention(q, k2, v2, sinks, batch=batch, seq=seq, n_kv=n_kv, group=group)
    else:
        attn = _attention_sample(q, k2, v2, sample_cache[0], sample_cache[1], sinks,
                                 batch=batch, t_new=seq, n_kv=n_kv, group=group)
    merged = _merge(xn, attn, z, w["w_rest"], w["w_attn_proj"], w["w_conv_proj"],
                    tm=tm, tn=tiles["tn_merge"], col_ga=col_ga, col_gc=col_gc)
    x1, xn_ffn = _out_proj(x, merged, w["w_out"], w["norm_ffn_g"],
                           tm=tiles["tm_out"], rows=tiles["out_rows"])
    y = _ffn(x1, xn_ffn, w["w_ffn_gate"], w["w_ffn_up"], w["w_ffn_down"],
             tm=tiles["tm_ffn"], tf=tiles["tf"], slab=V7X_MXU_DIM)
    return y, k, v, u_tail, w


def kernel(x_prompt, x_sample, cache_k, cache_v, state_conv, norm_mix_g, w_in, q_norm_g, k_norm_g,
           attn_sinks, conv_w, w_attn_proj, w_conv_proj, w_out, norm_ffn_g, w_ffn_gate, w_ffn_up,
           w_ffn_down):
    depth = w_in.shape[0]
    assert depth == 1
    b_p, s_p, d = x_prompt.shape
    b_s, s_s, _ = x_sample.shape
    n_kv = cache_k.shape[3]
    d_kv = n_kv * HEAD_DIM
    win = cache_k.shape[2]
    n_state = state_conv.shape[2]

    w = {
        "d_kv": d_kv,
        "norm_mix_g": norm_mix_g[0].reshape(1, d),
        "w_qkv": w_in[0][:, :attn_sinks.shape[1] * HEAD_DIM + 2 * d_kv].astype(BF16),
        "w_in_f32": w_in[0],
        "q_norm_g": q_norm_g[0],
        "k_norm_g": k_norm_g[0],
        "sinks": attn_sinks[0],
        "conv_w": conv_w[0],
        "w_attn_proj": w_attn_proj[0],
        "w_conv_proj": w_conv_proj[0],
        "w_out": w_out[0],
        "norm_ffn_g": norm_ffn_g[0].reshape(1, d),
        "w_ffn_gate": w_ffn_gate[0],
        "w_ffn_up": w_ffn_up[0],
        "w_ffn_down": w_ffn_down[0],
    }

    m_p = b_p * s_p
    tm_p = min(1024, s_p)
    keep = min(WINDOW, s_p)
    tiles_p = dict(tm=tm_p, tm_qkv=min(512, s_p), qkv_rows=min(512, s_p), kv_tail=(s_p, keep), tm_conv=s_p, tn_conv=512, conv_tail=8,
                   conv_rows=min(1024, s_p), tn_merge=512, tm_out=min(512, s_p),
                   out_rows=min(512, s_p),
                   tm_ffn=tm_p, tf=512)
    y_p, k_p, v_p, u_p, w = _layer(x_prompt.reshape(m_p, d), w, batch=b_p, seq=s_p,
                                   pos=jnp.arange(s_p), tiles=tiles_p, conv_prev=None,
                                   sample_cache=None)
    new_k_p = k_p.reshape(b_p, n_kv, HEAD_DIM, keep).transpose(0, 3, 1, 2)[None]
    new_v_p = v_p.reshape(b_p, n_kv, HEAD_DIM, keep).transpose(0, 3, 1, 2)[None]
    new_c_p = u_p.reshape(b_p, tiles_p["conv_tail"], -1)[:, tiles_p["conv_tail"] - n_state:][None]

    m_s = b_s * s_s
    st = state_conv[0]
    prev1 = jnp.pad(st[:, 1:2], ((0, 0), (0, s_s - 1), (0, 0))).reshape(m_s, -1)
    prev2 = jnp.pad(st, ((0, 0), (0, s_s - n_state), (0, 0))).reshape(m_s, -1)
    tiles_s = dict(tm=m_s, tm_qkv=m_s, qkv_rows=m_s, kv_tail=None, tm_conv=m_s, tn_conv=512, conv_tail=m_s,
                   conv_rows=m_s, tn_merge=512, tm_out=m_s, out_rows=m_s, tm_ffn=m_s, tf=512)
    y_s, k_s, v_s, u_s, _ = _layer(x_sample.reshape(m_s, d), w, batch=b_s, seq=s_s,
                                   pos=PAST_LEN + jnp.arange(s_s), tiles=tiles_s,
                                   conv_prev=(prev1, prev2),
                                   sample_cache=(cache_k[0].reshape(b_s * win, d_kv),
                                                 cache_v[0].reshape(b_s * win, d_kv)))
    new_k_s = k_s.reshape(b_s, s_s, n_kv, HEAD_DIM)[None]
    new_v_s = v_s.reshape(b_s, s_s, n_kv, HEAD_DIM)[None]
    new_c_s = u_s.reshape(b_s, s_s, -1)[:, s_s - n_state:][None]

    return (y_p.reshape(b_p, s_p, d), y_s.reshape(b_s, s_s, d), new_k_p, new_v_p, new_c_p,
            new_k_s, new_v_s, new_c_s)
```

```python
import functools
import math

import jax
import jax.numpy as jnp
from jax import lax
from jax.experimental import pallas as pl
from jax.experimental.pallas import tpu as pltpu

F32 = jnp.float32
BF16 = jnp.bfloat16

CHUNK = 64
WINDOW = 128
PAST_LEN = 1024
ROPE_THETA = 10000.0
EPS = 1e-6
NEG_INF = -1e30
HEAD_DIM = 64
LANES = 128
BF16_SUBLANES = 16
PAIR = 2 * CHUNK
KEY_SPAN = PAIR + WINDOW
PAIRS_PER_TRIP = 5

V7X_MXU_DIM = 256
V7X_VMEM_BYTES = 64 * 1024 * 1024
VMEM_CAP_BYTES = V7X_VMEM_BYTES - 6 * 1024 * 1024


def _params(semantics, vmem_estimate):
    limit = int(min(max(vmem_estimate, 16 * 1024 * 1024), VMEM_CAP_BYTES))
    return pltpu.CompilerParams(dimension_semantics=semantics, vmem_limit_bytes=limit)


def _nbytes(shape, dtype):
    return math.prod(shape) * jnp.dtype(dtype).itemsize


def _dot(a, b):
    return jnp.dot(a, b, preferred_element_type=F32)


def _dot_nt(a, b):
    return lax.dot_general(a, b, (((1,), (1,)), ((), ())), preferred_element_type=F32)


def _lane_iota(shape):
    return lax.broadcasted_iota(jnp.int32, shape, 1)


def _rmsnorm_rows(x, g):
    y = x * lax.rsqrt(jnp.mean(x * x, axis=-1, keepdims=True) + EPS)
    return y * g


def _headnorm_rope(y, t_same, t_swap, seg):
    width = y.shape[1]
    s = y * y
    s_hi = s.astype(BF16)
    s_lo = (s - s_hi.astype(F32)).astype(BF16)
    ms = _dot(s_hi, seg) + _dot(s_lo, seg)
    half = HEAD_DIM // 2
    ahead = pltpu.roll(y, width - half, axis=1)
    behind = pltpu.roll(y, half, axis=1)
    swapped = jnp.where((_lane_iota(y.shape) & (HEAD_DIM - 1)) < half, ahead, behind)
    reps = width // t_same.shape[1]
    t_same = jnp.concatenate([t_same] * reps, axis=1)
    t_swap = jnp.concatenate([t_swap] * reps, axis=1)
    return (y * t_same + swapped * t_swap) * lax.rsqrt(ms + EPS)


def _dup_halves(x):
    low = _lane_iota(x.shape) < HEAD_DIM
    r = pltpu.roll(x, HEAD_DIM, axis=1)
    return jnp.where(low, x, r), jnp.where(low, r, x)


def _isolate_halves(x):
    low = _lane_iota(x.shape) < HEAD_DIM
    r = pltpu.roll(x, HEAD_DIM, axis=1)
    return (jnp.where(low, x, 0.0), jnp.where(low, 0.0, r),
            jnp.where(low, r, 0.0), jnp.where(low, 0.0, x))


def _qkv_kernel(*refs, n_q_blocks, d_kv, rows, tail_keep, n_rest):
    x_ref, g_ref, w_ref, seg_ref, qs_ref, qw_ref, ks_ref, kw_ref = refs[:8]
    rest_refs = refs[8:8 + n_rest]
    xn_ref, q_ref, k_ref, v_ref, k2_ref, v2_ref = refs[8 + n_rest:][:6]
    rest_out = refs[-1] if n_rest else None
    tm = q_ref.shape[0]
    tn = 2 * d_kv
    chunks = [slice(r * rows, (r + 1) * rows) for r in range(tm // rows)]
    seg = seg_ref[...]
    slab = seg.shape[0]

    for rs in chunks:
        xn = _rmsnorm_rows(x_ref[rs, :], g_ref[...]).astype(BF16)
        xn_ref[rs, :] = xn
        for jb in range(n_q_blocks):
            y = _dot(xn, w_ref[:, jb * tn:(jb + 1) * tn])
            for c in range(tn // slab):
                sl = slice(c * slab, (c + 1) * slab)
                q_ref[rs, jb * tn + c * slab:jb * tn + (c + 1) * slab] = _headnorm_rope(
                    y[:, sl], qs_ref[rs, :], qw_ref[rs, :], seg).astype(BF16)

        y = _dot(xn, w_ref[:, n_q_blocks * tn:])
        k_all = jnp.concatenate(
            [_headnorm_rope(y[:, c * slab:(c + 1) * slab], ks_ref[rs, :], kw_ref[rs, :], seg)
             for c in range(d_kv // slab)], axis=1)
        if tail_keep is None:
            k_ref[rs, :] = k_all
            v_ref[rs, :] = y[:, d_kv:]
        elif rs is chunks[-1]:
            k_ref[0] = k_all[rows - tail_keep:].T
            v_ref[0] = y[rows - tail_keep:, d_kv:].T
        for c in range(d_kv // LANES):
            sl = slice(c * LANES, (c + 1) * LANES)
            k = k_all[:, sl]
            v = y[:, d_kv + c * LANES:d_kv + (c + 1) * LANES]
            va, vb = _dup_halves(v)
            lo = slice(2 * c * LANES, (2 * c + 1) * LANES)
            hi = slice((2 * c + 1) * LANES, (2 * c + 2) * LANES)
            v2_ref[rs, lo] = va.astype(BF16)
            v2_ref[rs, hi] = vb.astype(BF16)
            for n, piece in enumerate(_isolate_halves(k)):
                k2_ref[rs, (4 * c + n) * LANES:(4 * c + n + 1) * LANES] = piece.astype(BF16)

        for n, ref in list(enumerate(rest_refs))[chunks.index(rs)::len(chunks)]:
            width = ref.shape[1]
            rest_out[:, n * width:(n + 1) * width] = ref[...].astype(BF16)


def _qkv_proj(x, g, w_in, tables, tail, w_f32, *, tm, d_attn, d_kv, rows):
    m, d = x.shape
    tn = 2 * d_kv
    n_q_blocks = d_attn // tn
    qkv_cols = d_attn + tn
    assert m % tm == 0 and tm % rows == 0 and d_attn % tn == 0 and w_in.shape == (d, qkv_cols)
    n_tiles = m // tm
    rest_specs, rest_shape, rest_out_spec, n_rest = [], (), (), 0
    if w_f32 is not None:
        n_rest = (w_f32.shape[1] - qkv_cols) // qkv_cols
        assert w_f32.shape[1] == (n_rest + 1) * qkv_cols and d % (n_tiles * BF16_SUBLANES) == 0
        slab_rows = d // n_tiles
        rest_specs = [pl.BlockSpec((slab_rows, qkv_cols), functools.partial(
            lambda n, i: (i, n + 1), n)) for n in range(n_rest)]
        rest_shape = (jax.ShapeDtypeStruct((d, n_rest * qkv_cols), BF16),)
        rest_out_spec = (pl.BlockSpec((slab_rows, n_rest * qkv_cols), lambda i: (i, 0)),)
    if tail is None:
        kv_shape = jax.ShapeDtypeStruct((m, d_kv), F32)
        kv_spec = pl.BlockSpec((tm, d_kv), lambda i: (i, 0))
    else:
        seq, keep = tail
        assert seq % tm == 0 and keep <= rows and keep % LANES == 0
        kv_shape = jax.ShapeDtypeStruct((m // seq, d_kv, keep), F32)
        kv_spec = pl.BlockSpec((1, d_kv, keep), lambda i: (i // (seq // tm), 0, 0))
    rope_blocks = tables[0].shape[0] // tm
    tab = pl.BlockSpec((tm, LANES), lambda i: (i % rope_blocks, 0))
    row = lambda width: pl.BlockSpec((tm, width), lambda i: (i, 0))
    fixed = lambda shape: pl.BlockSpec(shape, lambda i: (0, 0), pipeline_mode=pl.Buffered(1))
    head_of = jnp.arange(V7X_MXU_DIM) // HEAD_DIM
    seg = jnp.where(head_of[:, None] == head_of[None, :], 1.0 / HEAD_DIM, 0.0).astype(BF16)
    est = (2 * (_nbytes((tm, d), F32) + _nbytes((tm, d), BF16) + _nbytes((tm, d_attn), BF16)
                + 3 * _nbytes((tm, tn), BF16) + 2 * _nbytes((tm, d_kv), F32)
                + 4 * _nbytes((tm, LANES), F32))
           + _nbytes(w_in.shape, BF16) + 6 * _nbytes((rows, tn), F32) + 2 * _nbytes((rows, d), F32))
    if n_rest:
        est += 3 * _nbytes((d // n_tiles, n_rest * qkv_cols), F32)
    return pl.pallas_call(
        functools.partial(_qkv_kernel, n_q_blocks=n_q_blocks, d_kv=d_kv, rows=rows,
                          tail_keep=None if tail is None else tail[1], n_rest=n_rest),
        out_shape=(jax.ShapeDtypeStruct((m, d), BF16), jax.ShapeDtypeStruct((m, d_attn), BF16),
                   kv_shape, kv_shape,
                   jax.ShapeDtypeStruct((m, 2 * tn), BF16), jax.ShapeDtypeStruct((m, tn), BF16),
                   *rest_shape),
        grid=(n_tiles,),
        in_specs=[row(d), fixed((1, d)), fixed(w_in.shape), fixed(seg.shape),
                  tab, tab, tab, tab] + rest_specs,
        out_specs=(row(d), row(d_attn), kv_spec, kv_spec, row(2 * tn), row(tn), *rest_out_spec),
        compiler_params=_params(("arbitrary",), est),
        name="qkv_proj",
    )(x, g, w_in, seg, *tables, *([w_f32] * n_rest))


def _conv_kernel(*refs, seq_len, tail_rows, has_state, rows, cast_per_step):
    n_cast = len(cast_per_step)
    n_in = 7 if has_state else 5
    xn_ref, wb_ref, wc_ref, wh_ref, cw_ref = refs[:5]
    p1_ref, p2_ref = refs[5:n_in] if has_state else (None, None)
    z_ref, u_ref = refs[n_in + n_cast:n_in + n_cast + 2]
    casts = list(zip(refs[n_in:n_in + n_cast], refs[n_in + n_cast + 2:]))
    tm, tn = z_ref.shape
    w = cw_ref[...]
    row = lax.broadcasted_iota(jnp.int32, (rows, tn), 0)
    last1 = last2 = None
    for r in range(tm // rows):
        rs = slice(r * rows, (r + 1) * rows)
        xn = xn_ref[rs, :]
        u = _dot(xn, wc_ref[...]) * _dot(xn, wh_ref[...])
        u1 = pltpu.roll(u, 1, axis=0)
        u2 = pltpu.roll(u, 2, axis=0)
        if r > 0:
            u1 = jnp.where(row == 0, last1, u1)
            u2 = jnp.where(row == 0, last2, jnp.where(row == 1, last1, u2))
        if (r * rows) % seq_len == 0 or seq_len < rows:
            pos = (row + r * rows) & (seq_len - 1)
            u1 = jnp.where(pos == 0, p1_ref[rs, :] if has_state else 0.0, u1)
            u2 = jnp.where(pos < 2, p2_ref[rs, :] if has_state else 0.0, u2)
        y = u2 * w[0:1] + u1 * w[1:2] + u * w[2:3]
        z_ref[rs, :] = (_dot(xn, wb_ref[...]) * y).astype(BF16)
        last1, last2 = u[rows - 1:rows], u[rows - 2:rows - 1]
        for (src_ref, dst_ref), per_step in list(zip(casts, cast_per_step))[r::tm // rows]:
            if per_step:
                _copy_slab(src_ref, dst_ref)
            else:
                pl.when(pl.program_id(1) == 0)(functools.partial(_copy_slab, src_ref, dst_ref))
    u_ref[...] = u[rows - tail_rows:]


def _copy_slab(src_ref, dst_ref):
    if len(dst_ref.shape) == 2:
        dst_ref[...] = src_ref[...].astype(BF16)
    else:
        cb = dst_ref.shape[2]
        for b in range(dst_ref.shape[0]):
            dst_ref[b] = src_ref[:, b * cb:(b + 1) * cb].astype(BF16)


def _conv_branch(xn, w_in, conv_w, prev, cast, *, tm, tn, rows, seq_len, tail_rows,
                 col_b, col_c, col_h):
    m, d = xn.shape
    n = conv_w.shape[1]
    has_state = prev is not None
    assert seq_len & (seq_len - 1) == 0 and tm % seq_len == 0
    assert tm % rows == 0 and tail_rows <= rows and (rows % seq_len == 0 or seq_len % rows == 0)
    n_j = n // tn
    steps = (m // tm) * n_j
    wspec = lambda off: pl.BlockSpec((d, tn), lambda i, j: (0, off // tn + j))
    in_specs = [pl.BlockSpec((tm, d), lambda i, j: (i, 0)),
                wspec(col_b), wspec(col_c), wspec(col_h),
                pl.BlockSpec((conv_w.shape[0], tn), lambda i, j: (0, j))]
    args = [xn, w_in, w_in, w_in, conv_w]
    if has_state:
        in_specs += [pl.BlockSpec((tm, tn), lambda i, j: (i, j))] * 2
        args += list(prev)
    cast_per_step = tuple(c.shape[0] % (steps * BF16_SUBLANES) == 0 for c, _ in cast)
    slab_specs, copy_specs, copy_shapes, slab_bytes = [], [], [], 0
    for (c, col_block), per_step in zip(cast, cast_per_step):
        parts = steps if per_step else m // tm
        r, cols = c.shape
        assert r % (parts * BF16_SUBLANES) == 0
        step_of = (lambda i, j: i * n_j + j) if per_step else (lambda i, j: i)
        slab_specs.append(pl.BlockSpec((r // parts, cols),
                                       functools.partial(lambda s, i, j: (s(i, j), 0), step_of)))
        if col_block is None:
            copy_specs.append(slab_specs[-1])
            copy_shapes.append(jax.ShapeDtypeStruct((r, cols), BF16))
        else:
            assert cols % col_block == 0
            copy_specs.append(pl.BlockSpec(
                (cols // col_block, r // parts, col_block),
                functools.partial(lambda s, i, j: (0, s(i, j), 0), step_of)))
            copy_shapes.append(jax.ShapeDtypeStruct((cols // col_block, r, col_block), BF16))
        slab_bytes += 3 * _nbytes(c.shape, F32) // parts
    est = (2 * (_nbytes((tm, d), BF16) + 3 * _nbytes((d, tn), BF16) + _nbytes((tm, tn), BF16)
                + _nbytes((tail_rows, tn), F32)) + 8 * _nbytes((tm, tn), F32) + slab_bytes)
    return pl.pallas_call(
        functools.partial(_conv_kernel, seq_len=seq_len, tail_rows=tail_rows, has_state=has_state,
                          rows=rows, cast_per_step=cast_per_step),
        out_shape=(jax.ShapeDtypeStruct((m, n), BF16),
                   jax.ShapeDtypeStruct((m // tm * tail_rows, n), F32),
                   *copy_shapes),
        grid=(m // tm, n_j),
        in_specs=in_specs + slab_specs,
        out_specs=(pl.BlockSpec((tm, tn), lambda i, j: (i, j)),
                   pl.BlockSpec((tail_rows, tn), lambda i, j: (i, j)),
                   *copy_specs),
        compiler_params=_params(("parallel", "arbitrary"), est),
        name="conv_branch",
    )(*args, *[c for c, _ in cast])


def _stack_heads(q_ref, rows, col0, group):
    parts = []
    for g in range(group):
        blk = q_ref[rows, col0 + (g // 2) * LANES:col0 + (g // 2 + 1) * LANES]
        keep = (_lane_iota(blk.shape) >= HEAD_DIM) == bool(g % 2)
        parts.append(jnp.where(keep, blk, jnp.zeros_like(blk)))
    return jnp.concatenate(parts, axis=0)


def _sink_column(sink_ref, head0, group, rows):
    return jnp.concatenate(
        [jnp.full((rows, 1), sink_ref[head0 + g], F32) for g in range(group)], axis=0)


def _store_heads(o_ref, rows, col0, o, group, nrows):
    for lp in range(group // 2):
        a = o[(2 * lp) * nrows:(2 * lp + 1) * nrows]
        b = o[(2 * lp + 1) * nrows:(2 * lp + 2) * nrows]
        low = _lane_iota(a.shape) < HEAD_DIM
        o_ref[rows, col0 + lp * LANES:col0 + (lp + 1) * LANES] = jnp.where(low, a, b).astype(BF16)


def _attn_pairs(pairs, sink_ref, q_ref, k_ref, v_ref, o_ref, n_kv, group):
    def block_rows(index):
        start = index * PAIR
        return pl.ds(start if isinstance(start, int) else pl.multiple_of(start, PAIR), PAIR)

    first_q = _lane_iota((CHUNK, PAIR)) < CHUNK

    def keys_of(ref, pair, has_prev, tile):
        cols = slice(tile * LANES, (tile + 1) * LANES)
        if has_prev:
            return jnp.concatenate([ref[block_rows(pair - 1), cols], ref[block_rows(pair), cols]],
                                   axis=0)
        return ref[block_rows(pair), cols]

    def scores(pair, has_prev, kv):
        col0 = kv * group * HEAD_DIM
        q_tiles = jnp.concatenate(
            [q_ref[block_rows(pair), col0 + lp * LANES:col0 + (lp + 1) * LANES]
             for lp in range(group // 2)], axis=0)
        return [_dot_nt(keys_of(k_ref, pair, has_prev, 2 * kv + half), q_tiles)
                for half in range(2)]

    items = [(pair, has_prev, kv) for pair, has_prev in pairs for kv in range(n_kv)]
    st_next = scores(*items[0])
    for n, (pair, has_prev, kv) in enumerate(items):
        st = st_next
        if n + 1 < len(items):
            st_next = scores(*items[n + 1])
        rows = block_rows(pair)
        vt = keys_of(v_ref, pair, has_prev, kv).astype(F32).T.astype(BF16)
        col0 = kv * group * HEAD_DIM
        probs, rden = [], []
        for g in range(group):
            sg = st[g % 2][:, (g // 2) * PAIR:(g // 2 + 1) * PAIR]
            blocks = [sg[c * CHUNK:(c + 1) * CHUNK] for c in range(sg.shape[0] // CHUNK)]
            if has_prev:
                s = jnp.concatenate([jnp.where(first_q, blocks[0], blocks[3]),
                                     blocks[1], blocks[2]], axis=0)
            else:
                s = jnp.concatenate([blocks[0], jnp.where(first_q, NEG_INF, blocks[1])], axis=0)
            sink = sink_ref[kv * group + g]
            m = jnp.maximum(jnp.max(s, axis=0, keepdims=True), sink)
            e = jnp.exp2(s - m)
            den = jnp.sum(e, axis=0, keepdims=True) + jnp.exp2(sink - m)
            if has_prev:
                e_03, e_12 = e[:CHUNK], e[CHUNK:]
                e = jnp.concatenate([jnp.where(first_q, e_03, 0.0), e_12,
                                     jnp.where(first_q, 0.0, e_03)], axis=0)
            probs.append(e.astype(BF16))
            rden.append(1.0 / den)
        ot = _dot(vt, jnp.concatenate(probs, axis=1)) * jnp.concatenate(rden, axis=1)
        for lp in range(group // 2):
            a = ot[:HEAD_DIM, (2 * lp) * PAIR:(2 * lp + 1) * PAIR]
            b = ot[HEAD_DIM:, (2 * lp + 1) * PAIR:(2 * lp + 2) * PAIR]
            o_ref[rows, col0 + lp * LANES:col0 + (lp + 1) * LANES] = (
                jnp.concatenate([a, b], axis=0).T.astype(BF16))


def _attn_kernel(sink_ref, q_ref, k_ref, v_ref, o_ref, *, n_kv, group):
    def run(pairs):
        _attn_pairs(pairs, sink_ref, q_ref, k_ref, v_ref, o_ref, n_kv, group)

    def trip(it, carry):
        run([(1 + PAIRS_PER_TRIP * it + u, True) for u in range(PAIRS_PER_TRIP)])
        return carry

    n_pairs = q_ref.shape[0] // PAIR
    trips = (n_pairs - 1) // PAIRS_PER_TRIP
    run([(0, False)])
    lax.fori_loop(0, trips, trip, 0)
    leftover = range(1 + trips * PAIRS_PER_TRIP, n_pairs)
    if leftover:
        run([(pair, True) for pair in leftover])


def _attention(q, k2, v2, sinks, *, batch, seq, n_kv, group):
    m, d = q.shape
    assert WINDOW == PAIR and seq % PAIR == 0
    est = (2 * (2 * _nbytes((seq, d), BF16) + 3 * _nbytes((seq, n_kv * LANES), BF16))
           + 8 * _nbytes((group * PAIR, KEY_SPAN), F32))
    return pl.pallas_call(
        functools.partial(_attn_kernel, n_kv=n_kv, group=group),
        out_shape=jax.ShapeDtypeStruct((m, d), BF16),
        grid=(batch,),
        in_specs=[pl.BlockSpec(memory_space=pltpu.SMEM),
                  pl.BlockSpec((seq, d), lambda b: (b, 0)),
                  pl.BlockSpec((seq, 2 * n_kv * LANES), lambda b: (b, 0)),
                  pl.BlockSpec((seq, n_kv * LANES), lambda b: (b, 0))],
        out_specs=pl.BlockSpec((seq, d), lambda b: (b, 0)),
        compiler_params=_params(("parallel",), est),
        name="attention",
    )(sinks, q, k2, v2)


def _attn_sample_kernel(sink_ref, q_ref, kn_ref, vn_ref, kc_ref, vc_ref, o_ref, *, n_kv, group):
    t = q_ref.shape[0]
    for kv in range(n_kv):
        pair_sl = slice((kv // 2) * LANES, (kv // 2 + 1) * LANES)
        kc = _dup_halves(kc_ref[:, pair_sl])[kv % 2].astype(BF16)
        vc = _dup_halves(vc_ref[:, pair_sl])[kv % 2].astype(BF16)
        kn = (kn_ref[:, 2 * kv * LANES:(2 * kv + 1) * LANES]
              + kn_ref[:, (2 * kv + 1) * LANES:(2 * kv + 2) * LANES])
        vn = vn_ref[:, kv * LANES:(kv + 1) * LANES]
        col0 = kv * group * HEAD_DIM
        lhs = _stack_heads(q_ref, slice(0, t), col0, group)
        s_c = _dot_nt(lhs, kc)
        s_n = _dot_nt(lhs, kn)
        sink = _sink_column(sink_ref, kv * group, group, t)
        m = jnp.maximum(jnp.maximum(jnp.max(s_c, axis=1, keepdims=True),
                                    jnp.max(s_n, axis=1, keepdims=True)), sink)
        e_c = jnp.exp2(s_c - m)
        e_n = jnp.exp2(s_n - m)
        den = (jnp.sum(e_c, axis=1, keepdims=True) + jnp.sum(e_n, axis=1, keepdims=True)
               + jnp.exp2(sink - m))
        o = (_dot(e_c.astype(BF16), vc) + _dot(e_n.astype(BF16), vn)) / den
        _store_heads(o_ref, slice(0, t), col0, o, group, t)


def _attention_sample(q, k2, v2, cache_k, cache_v, sinks, *, batch, t_new, n_kv, group):
    m, d = q.shape
    win = cache_k.shape[0] // batch
    d_kv = cache_k.shape[1]
    est = 4 * 1024 * 1024
    return pl.pallas_call(
        functools.partial(_attn_sample_kernel, n_kv=n_kv, group=group),
        out_shape=jax.ShapeDtypeStruct((m, d), BF16),
        grid=(batch,),
        in_specs=[pl.BlockSpec(memory_space=pltpu.SMEM),
                  pl.BlockSpec((t_new, d), lambda b: (b, 0)),
                  pl.BlockSpec((t_new, 2 * n_kv * LANES), lambda b: (b, 0)),
                  pl.BlockSpec((t_new, n_kv * LANES), lambda b: (b, 0)),
                  pl.BlockSpec((win, d_kv), lambda b: (b, 0)),
                  pl.BlockSpec((win, d_kv), lambda b: (b, 0))],
        out_specs=pl.BlockSpec((t_new, d), lambda b: (b, 0)),
        compiler_params=_params(("parallel",), est),
        name="attention_sample",
    )(sinks, q, k2, v2, cache_k, cache_v)


def _merge_kernel(xn_ref, a_ref, z_ref, wga_ref, wgc_ref, wap_ref, wcp_ref, o_ref):
    xn = xn_ref[...]
    attn = jax.nn.sigmoid(_dot(xn, wga_ref[...])) * _dot(a_ref[...], wap_ref[...])
    conv = jax.nn.sigmoid(_dot(xn, wgc_ref[...])) * _dot(z_ref[...], wcp_ref[...])
    o_ref[...] = (attn + conv).astype(BF16)


def _merge(xn, attn, z, w_in, w_ap, w_cp, *, tm, tn, col_ga, col_gc):
    m, d = xn.shape
    act = pl.BlockSpec((tm, d), lambda i, j: (i, 0))
    wspec = lambda off: pl.BlockSpec((d, tn), lambda i, j: (0, off // tn + j))
    est = (2 * (3 * _nbytes((tm, d), BF16) + 4 * _nbytes((d, tn), BF16) + _nbytes((tm, tn), BF16))
           + 6 * _nbytes((tm, tn), F32))
    return pl.pallas_call(
        _merge_kernel,
        out_shape=jax.ShapeDtypeStruct((m, d), BF16),
        grid=(m // tm, d // tn),
        in_specs=[act, act, act, wspec(col_ga), wspec(col_gc), wspec(0), wspec(0)],
        out_specs=pl.BlockSpec((tm, tn), lambda i, j: (i, j)),
        compiler_params=_params(("parallel", "arbitrary"), est),
        name="merge",
    )(xn, attn, z, w_in, w_in, w_ap, w_cp)


def _out_kernel(x_ref, m_ref, w_ref, g_ref, x1_ref, xn_ref, *, rows):
    for r in range(x_ref.shape[0] // rows):
        rs = slice(r * rows, (r + 1) * rows)
        x1 = x_ref[rs, :] + _dot(m_ref[rs, :], w_ref[...])
        x1_ref[rs, :] = x1
        xn_ref[rs, :] = _rmsnorm_rows(x1, g_ref[...]).astype(BF16)


def _out_proj(x, merged, w_out, g_ffn, *, tm, rows):
    m, d = x.shape
    assert m % tm == 0 and tm % rows == 0
    est = (2 * (2 * _nbytes((tm, d), F32) + 2 * _nbytes((tm, d), BF16)) + _nbytes((d, d), BF16)
           + 4 * _nbytes((rows, d), F32))
    return pl.pallas_call(
        functools.partial(_out_kernel, rows=rows),
        out_shape=(jax.ShapeDtypeStruct((m, d), F32), jax.ShapeDtypeStruct((m, d), BF16)),
        grid=(m // tm,),
        in_specs=[pl.BlockSpec((tm, d), lambda i: (i, 0)),
                  pl.BlockSpec((tm, d), lambda i: (i, 0)),
                  pl.BlockSpec((d, d), lambda i: (0, 0), pipeline_mode=pl.Buffered(1)),
                  pl.BlockSpec((1, d), lambda i: (0, 0))],
        out_specs=(pl.BlockSpec((tm, d), lambda i: (i, 0)),
                   pl.BlockSpec((tm, d), lambda i: (i, 0))),
        compiler_params=_params(("parallel",), est),
        name="out_proj",
    )(x, merged, w_out, g_ffn)


def _ffn_kernel(x_ref, xn_ref, wg_ref, wu_ref, wd_ref, o_ref):
    f = pl.program_id(1)
    slab = x_ref.shape[1]
    n_slabs = o_ref.shape[1] // slab

    @pl.when(f == 0)
    def _():
        o_ref[...] = jnp.zeros_like(o_ref)

    xn = xn_ref[...]
    h = jax.nn.silu(_dot(xn, wg_ref[...])) * _dot(xn, wu_ref[...])
    o_ref[...] += _dot(h.astype(BF16), wd_ref[...])

    for s in range(n_slabs):
        @pl.when(f == s)
        def _(s=s):
            o_ref[:, s * slab:(s + 1) * slab] += x_ref[...]


def _ffn(x, xn, w_gate, w_up, w_down, *, tm, tf, slab):
    m, d = x.shape
    d_ff = w_down.shape[0]
    n_slabs = d // slab
    assert n_slabs <= d_ff // tf and w_gate.shape == w_up.shape == (d_ff // tf, d, tf)
    est = (2 * (_nbytes((tm, slab), F32) + _nbytes((tm, d), F32) + _nbytes((tm, d), BF16)
                + 3 * _nbytes((d, tf), BF16))
           + 4 * _nbytes((tm, tf), F32) + _nbytes((tm, d), F32))
    row = lambda i, f: (i, 0)
    return pl.pallas_call(
        _ffn_kernel,
        out_shape=jax.ShapeDtypeStruct((m, d), F32),
        grid=(m // tm, d_ff // tf),
        in_specs=[pl.BlockSpec((tm, slab), lambda i, f: (i, jnp.minimum(f, n_slabs - 1))),
                  pl.BlockSpec((tm, d), row),
                  pl.BlockSpec((None, d, tf), lambda i, f: (f, 0, 0)),
                  pl.BlockSpec((None, d, tf), lambda i, f: (f, 0, 0)),
                  pl.BlockSpec((tf, d), lambda i, f: (f, 0))],
        out_specs=pl.BlockSpec((tm, d), row),
        compiler_params=_params(("parallel", "arbitrary"), est),
        name="ffn",
    )(x, xn, w_gate, w_up, w_down)


def _rope_tables(pos, repeat, g, scale):
    half = HEAD_DIM // 2
    inv = jnp.exp(-math.log(ROPE_THETA) * jnp.arange(half, dtype=F32) * (2.0 / HEAD_DIM))
    ang = pos.astype(F32)[:, None] * inv[None, :]
    cos = jnp.cos(ang)
    sin = jnp.sin(ang)
    g_swap = jnp.concatenate([g[half:], g[:half]])
    t_same = g[None, :] * jnp.concatenate([cos, cos], axis=-1) * scale
    t_swap = g_swap[None, :] * jnp.concatenate([-sin, sin], axis=-1) * scale
    reps = (repeat, LANES // HEAD_DIM)
    return jnp.tile(t_same, reps), jnp.tile(t_swap, reps)


LATE_WEIGHTS = ("w_attn_proj", "w_conv_proj", "w_out", "w_ffn_gate", "w_ffn_up", "w_ffn_down")
FFN_IN_WEIGHTS = ("w_ffn_gate", "w_ffn_up")


def _layer(x, w, *, batch, seq, pos, tiles, conv_prev, sample_cache):
    d = x.shape[1]
    n_heads = w["sinks"].shape[0]
    d_attn = n_heads * HEAD_DIM
    d_kv = w["d_kv"]
    n_kv = d_kv // HEAD_DIM
    group = n_heads // n_kv
    d_conv = w["conv_w"].shape[1]
    col_b = 0
    col_c = col_b + d_conv
    col_h = col_c + d_conv
    col_ga = col_h + d_conv
    col_gc = col_ga + d
    tm = tiles["tm"]

    rope_rows = max(seq, tiles["tm_qkv"])
    log2e = math.log2(math.e)
    tables = (_rope_tables(pos, rope_rows // seq, w["q_norm_g"], HEAD_DIM ** -0.5 * log2e)
              + _rope_tables(pos, rope_rows // seq, w["k_norm_g"], 1.0))
    sinks = w["sinks"] * log2e

    w_f32 = None if "w_rest" in w else w["w_in_f32"]
    xn, q, k, v, k2, v2, *w_rest = _qkv_proj(
        x, w["norm_mix_g"], w["w_qkv"], tables, tiles["kv_tail"], w_f32,
        tm=tiles["tm_qkv"], d_attn=d_attn, d_kv=d_kv, rows=tiles["qkv_rows"])
    if w_rest:
        w = dict(w, w_rest=w_rest[0])
    late = [name for name in LATE_WEIGHTS if w[name].dtype != BF16]
    z, u_tail, *late_bf16 = _conv_branch(
        xn, w["w_rest"], w["conv_w"], conv_prev,
        [(w[name], tiles["tf"] if name in FFN_IN_WEIGHTS else None) for name in late],
        tm=tiles["tm_conv"], tn=tiles["tn_conv"], rows=tiles["conv_rows"], seq_len=seq,
        tail_rows=tiles["conv_tail"], col_b=col_b, col_c=col_c, col_h=col_h)
    w = dict(w, **dict(zip(late, late_bf16)))
    if sample_cache is None:
        attn = _attention(q, k2, v2, sinks, batch=batch, seq=seq, n_kv=n_kv, group=group)
    else:
        attn = _attention_sample(q, k2, v2, sample_cache[0], sample_cache[1], sinks,
                                 batch=batch, t_new=seq, n_kv=n_kv, group=group)
    merged = _merge(xn, attn, z, w["w_rest"], w["w_attn_proj"], w["w_conv_proj"],
                    tm=tm, tn=tiles["tn_merge"], col_ga=col_ga, col_gc=col_gc)
    x1, xn_ffn = _out_proj(x, merged, w["w_out"], w["norm_ffn_g"],
                           tm=tiles["tm_out"], rows=tiles["out_rows"])
    y = _ffn(x1, xn_ffn, w["w_ffn_gate"], w["w_ffn_up"], w["w_ffn_down"],
             tm=tiles["tm_ffn"], tf=tiles["tf"], slab=V7X_MXU_DIM)
    return y, k, v, u_tail, w


def kernel(x_prompt, x_sample, cache_k, cache_v, state_conv, norm_mix_g, w_in, q_norm_g, k_norm_g,
           attn_sinks, conv_w, w_attn_proj, w_conv_proj, w_out, norm_ffn_g, w_ffn_gate, w_ffn_up,
           w_ffn_down):
    depth = w_in.shape[0]
    assert depth == 1
    b_p, s_p, d = x_prompt.shape
    b_s, s_s, _ = x_sample.shape
    n_kv = cache_k.shape[3]
    d_kv = n_kv * HEAD_DIM
    win = cache_k.shape[2]
    n_state = state_conv.shape[2]

    w = {
        "d_kv": d_kv,
        "norm_mix_g": norm_mix_g[0].reshape(1, d),
        "w_qkv": w_in[0][:, :attn_sinks.shape[1] * HEAD_DIM + 2 * d_kv].astype(BF16),
        "w_in_f32": w_in[0],
        "q_norm_g": q_norm_g[0],
        "k_norm_g": k_norm_g[0],
        "sinks": attn_sinks[0],
        "conv_w": conv_w[0],
        "w_attn_proj": w_attn_proj[0],
        "w_conv_proj": w_conv_proj[0],
        "w_out": w_out[0],
        "norm_ffn_g": norm_ffn_g[0].reshape(1, d),
        "w_ffn_gate": w_ffn_gate[0],
        "w_ffn_up": w_ffn_up[0],
        "w_ffn_down": w_ffn_down[0],
    }

    m_p = b_p * s_p
    tm_p = min(1024, s_p)
    keep = min(WINDOW, s_p)
    tiles_p = dict(tm=tm_p, tm_qkv=min(512, s_p), qkv_rows=min(512, s_p), kv_tail=(s_p, keep), tm_conv=s_p, tn_conv=512, conv_tail=8,
                   conv_rows=min(1024, s_p), tn_merge=512, tm_out=min(512, s_p),
                   out_rows=min(512, s_p),
                   tm_ffn=tm_p, tf=512)
    y_p, k_p, v_p, u_p, w = _layer(x_prompt.reshape(m_p, d), w, batch=b_p, seq=s_p,
                                   pos=jnp.arange(s_p), tiles=tiles_p, conv_prev=None,
                                   sample_cache=None)
    new_k_p = k_p.reshape(b_p, n_kv, HEAD_DIM, keep).transpose(0, 3, 1, 2)[None]
    new_v_p = v_p.reshape(b_p, n_kv, HEAD_DIM, keep).transpose(0, 3, 1, 2)[None]
    new_c_p = u_p.reshape(b_p, tiles_p["conv_tail"], -1)[:, tiles_p["conv_tail"] - n_state:][None]

    m_s = b_s * s_s
    st = state_conv[0]
    prev1 = jnp.pad(st[:, 1:2], ((0, 0), (0, s_s - 1), (0, 0))).reshape(m_s, -1)
    prev2 = jnp.pad(st, ((0, 0), (0, s_s - n_state), (0, 0))).reshape(m_s, -1)
    tiles_s = dict(tm=m_s, tm_qkv=m_s, qkv_rows=m_s, kv_tail=None, tm_conv=m_s, tn_conv=512, conv_tail=m_s,
                   conv_rows=m_s, tn_merge=512, tm_out=m_s, out_rows=m_s, tm_ffn=m_s, tf=512)
    y_s, k_s, v_s, u_s, _ = _layer(x_sample.reshape(m_s, d), w, batch=b_s, seq=s_s,
                                   pos=PAST_LEN + jnp.arange(s_s), tiles=tiles_s,
                                   conv_prev=(prev1, prev2),
                                   sample_cache=(cache_k[0].reshape(b_s * win, d_kv),
                                                 cache_v[0].reshape(b_s * win, d_kv)))
    new_k_s = k_s.reshape(b_s, s_s, n_kv, HEAD_DIM)[None]
    new_v_s = v_s.reshape(b_s, s_s, n_kv, HEAD_DIM)[None]
    new_c_s = u_s.reshape(b_s, s_s, -1)[:, s_s - n_state:][None]

    return (y_p.reshape(b_p, s_p, d), y_s.reshape(b_s, s_s, d), new_k_p, new_v_p, new_c_p,
            new_k_s, new_v_s, new_c_s)
```

```python
import functools
import math

import jax
import jax.numpy as jnp
from jax import lax
from jax.experimental import pallas as pl
from jax.experimental.pallas import tpu as pltpu

F32 = jnp.float32
BF16 = jnp.bfloat16

CHUNK = 64
WINDOW = 128
PAST_LEN = 1024
ROPE_THETA = 10000.0
EPS = 1e-6
NEG_INF = -1e30
HEAD_DIM = 64
LANES = 128
BF16_SUBLANES = 16
PAIR = 2 * CHUNK
KEY_SPAN = PAIR + WINDOW
PAIRS_PER_TRIP = 5

V7X_MXU_DIM = 256
MIB = 1024 * 1024
V7X_VMEM_BYTES = 64 * MIB
VMEM_COMPILER_RESERVE_BYTES = 6 * MIB
VMEM_CAP_BYTES = V7X_VMEM_BYTES - VMEM_COMPILER_RESERVE_BYTES
VMEM_FLOOR_BYTES = 16 * MIB


def _params(semantics, vmem_estimate):
    limit = int(min(max(vmem_estimate, VMEM_FLOOR_BYTES), VMEM_CAP_BYTES))
    return pltpu.CompilerParams(dimension_semantics=semantics, vmem_limit_bytes=limit)


def _nbytes(shape, dtype):
    return math.prod(shape) * jnp.dtype(dtype).itemsize


def _dot(a, b):
    return jnp.dot(a, b, preferred_element_type=F32)


def _dot_nt(a, b):
    return lax.dot_general(a, b, (((1,), (1,)), ((), ())), preferred_element_type=F32)


def _lane_iota(shape):
    return lax.broadcasted_iota(jnp.int32, shape, 1)


def _rmsnorm_rows(x, g):
    y = x * lax.rsqrt(jnp.mean(x * x, axis=-1, keepdims=True) + EPS)
    return y * g


def _headnorm_rope(y, t_same, t_swap, seg):
    width = y.shape[1]
    s = y * y
    s_hi = s.astype(BF16)
    s_lo = (s - s_hi.astype(F32)).astype(BF16)
    ms = _dot(s_hi, seg) + _dot(s_lo, seg)
    half = HEAD_DIM // 2
    ahead = pltpu.roll(y, width - half, axis=1)
    behind = pltpu.roll(y, half, axis=1)
    swapped = jnp.where((_lane_iota(y.shape) & (HEAD_DIM - 1)) < half, ahead, behind)
    reps = width // t_same.shape[1]
    t_same = jnp.concatenate([t_same] * reps, axis=1)
    t_swap = jnp.concatenate([t_swap] * reps, axis=1)
    return (y * t_same + swapped * t_swap) * lax.rsqrt(ms + EPS)


def _dup_halves(x):
    low = _lane_iota(x.shape) < HEAD_DIM
    r = pltpu.roll(x, HEAD_DIM, axis=1)
    return jnp.where(low, x, r), jnp.where(low, r, x)


def _isolate_halves(x):
    low = _lane_iota(x.shape) < HEAD_DIM
    r = pltpu.roll(x, HEAD_DIM, axis=1)
    return (jnp.where(low, x, 0.0), jnp.where(low, 0.0, r),
            jnp.where(low, r, 0.0), jnp.where(low, 0.0, x))


def _qkv_kernel(*refs, n_q_blocks, d_kv, rows, tail_keep, n_rest):
    x_ref, g_ref, w_ref, seg_ref, qs_ref, qw_ref, ks_ref, kw_ref = refs[:8]
    rest_refs = refs[8:8 + n_rest]
    xn_ref, q_ref, k_ref, v_ref, k2_ref, v2_ref = refs[8 + n_rest:][:6]
    rest_out = refs[-1] if n_rest else None
    tm = q_ref.shape[0]
    tn = 2 * d_kv
    chunks = [slice(r * rows, (r + 1) * rows) for r in range(tm // rows)]
    seg = seg_ref[...]
    slab = seg.shape[0]

    for rs in chunks:
        xn = _rmsnorm_rows(x_ref[rs, :], g_ref[...]).astype(BF16)
        xn_ref[rs, :] = xn
        for jb in range(n_q_blocks):
            y = _dot(xn, w_ref[:, jb * tn:(jb + 1) * tn])
            for c in range(tn // slab):
                sl = slice(c * slab, (c + 1) * slab)
                q_ref[rs, jb * tn + c * slab:jb * tn + (c + 1) * slab] = _headnorm_rope(
                    y[:, sl], qs_ref[rs, :], qw_ref[rs, :], seg).astype(BF16)

        y = _dot(xn, w_ref[:, n_q_blocks * tn:])
        k_all = jnp.concatenate(
            [_headnorm_rope(y[:, c * slab:(c + 1) * slab], ks_ref[rs, :], kw_ref[rs, :], seg)
             for c in range(d_kv // slab)], axis=1)
        if tail_keep is None:
            k_ref[rs, :] = k_all
            v_ref[rs, :] = y[:, d_kv:]
        elif rs is chunks[-1]:
            k_ref[0] = k_all[rows - tail_keep:].T
            v_ref[0] = y[rows - tail_keep:, d_kv:].T
        for c in range(d_kv // LANES):
            sl = slice(c * LANES, (c + 1) * LANES)
            k = k_all[:, sl]
            v = y[:, d_kv + c * LANES:d_kv + (c + 1) * LANES]
            va, vb = _dup_halves(v)
            lo = slice(2 * c * LANES, (2 * c + 1) * LANES)
            hi = slice((2 * c + 1) * LANES, (2 * c + 2) * LANES)
            v2_ref[rs, lo] = va.astype(BF16)
            v2_ref[rs, hi] = vb.astype(BF16)
            for n, piece in enumerate(_isolate_halves(k)):
                k2_ref[rs, (4 * c + n) * LANES:(4 * c + n + 1) * LANES] = piece.astype(BF16)

        for n, ref in list(enumerate(rest_refs))[chunks.index(rs)::len(chunks)]:
            width = ref.shape[1]
            rest_out[:, n * width:(n + 1) * width] = ref[...].astype(BF16)


def _qkv_proj(x, g, w_in, tables, tail, w_f32, *, tm, d_attn, d_kv, rows):
    m, d = x.shape
    tn = 2 * d_kv
    n_q_blocks = d_attn // tn
    qkv_cols = d_attn + tn
    assert m % tm == 0 and tm % rows == 0 and d_attn % tn == 0 and w_in.shape == (d, qkv_cols)
    n_tiles = m // tm
    rest_specs, rest_shape, rest_out_spec, n_rest = [], (), (), 0
    if w_f32 is not None:
        n_rest = (w_f32.shape[1] - qkv_cols) // qkv_cols
        assert w_f32.shape[1] == (n_rest + 1) * qkv_cols and d % (n_tiles * BF16_SUBLANES) == 0
        slab_rows = d // n_tiles
        rest_specs = [pl.BlockSpec((slab_rows, qkv_cols), functools.partial(
            lambda n, i: (i, n + 1), n)) for n in range(n_rest)]
        rest_shape = (jax.ShapeDtypeStruct((d, n_rest * qkv_cols), BF16),)
        rest_out_spec = (pl.BlockSpec((slab_rows, n_rest * qkv_cols), lambda i: (i, 0)),)
    if tail is None:
        kv_shape = jax.ShapeDtypeStruct((m, d_kv), F32)
        kv_spec = pl.BlockSpec((tm, d_kv), lambda i: (i, 0))
    else:
        seq, keep = tail
        assert seq % tm == 0 and keep <= rows and keep % LANES == 0
        kv_shape = jax.ShapeDtypeStruct((m // seq, d_kv, keep), F32)
        kv_spec = pl.BlockSpec((1, d_kv, keep), lambda i: (i // (seq // tm), 0, 0))
    rope_blocks = tables[0].shape[0] // tm
    tab = pl.BlockSpec((tm, LANES), lambda i: (i % rope_blocks, 0))
    row = lambda width: pl.BlockSpec((tm, width), lambda i: (i, 0))
    fixed = lambda shape: pl.BlockSpec(shape, lambda i: (0, 0), pipeline_mode=pl.Buffered(1))
    head_of = jnp.arange(V7X_MXU_DIM) // HEAD_DIM
    seg = jnp.where(head_of[:, None] == head_of[None, :], 1.0 / HEAD_DIM, 0.0).astype(BF16)
    est = (2 * (_nbytes((tm, d), F32) + _nbytes((tm, d), BF16) + _nbytes((tm, d_attn), BF16)
                + 3 * _nbytes((tm, tn), BF16) + 2 * _nbytes((tm, d_kv), F32)
                + 4 * _nbytes((tm, LANES), F32))
           + _nbytes(w_in.shape, BF16) + 6 * _nbytes((rows, tn), F32) + 2 * _nbytes((rows, d), F32))
    if n_rest:
        est += 3 * _nbytes((d // n_tiles, n_rest * qkv_cols), F32)
    return pl.pallas_call(
        functools.partial(_qkv_kernel, n_q_blocks=n_q_blocks, d_kv=d_kv, rows=rows,
                          tail_keep=None if tail is None else tail[1], n_rest=n_rest),
        out_shape=(jax.ShapeDtypeStruct((m, d), BF16), jax.ShapeDtypeStruct((m, d_attn), BF16),
                   kv_shape, kv_shape,
                   jax.ShapeDtypeStruct((m, 2 * tn), BF16), jax.ShapeDtypeStruct((m, tn), BF16),
                   *rest_shape),
        grid=(n_tiles,),
        in_specs=[row(d), fixed((1, d)), fixed(w_in.shape), fixed(seg.shape),
                  tab, tab, tab, tab] + rest_specs,
        out_specs=(row(d), row(d_attn), kv_spec, kv_spec, row(2 * tn), row(tn), *rest_out_spec),
        compiler_params=_params(("arbitrary",), est),
        name="qkv_proj",
    )(x, g, w_in, seg, *tables, *([w_f32] * n_rest))


def _conv_kernel(*refs, seq_len, tail_rows, has_state, rows, cast_per_step):
    n_cast = len(cast_per_step)
    n_in = 7 if has_state else 5
    xn_ref, wb_ref, wc_ref, wh_ref, cw_ref = refs[:5]
    p1_ref, p2_ref = refs[5:n_in] if has_state else (None, None)
    z_ref, u_ref = refs[n_in + n_cast:n_in + n_cast + 2]
    casts = list(zip(refs[n_in:n_in + n_cast], refs[n_in + n_cast + 2:]))
    tm, tn = z_ref.shape
    w = cw_ref[...]
    row = lax.broadcasted_iota(jnp.int32, (rows, tn), 0)
    last1 = last2 = None
    for r in range(tm // rows):
        rs = slice(r * rows, (r + 1) * rows)
        xn = xn_ref[rs, :]
        u = _dot(xn, wc_ref[...]) * _dot(xn, wh_ref[...])
        u1 = pltpu.roll(u, 1, axis=0)
        u2 = pltpu.roll(u, 2, axis=0)
        if r > 0:
            u1 = jnp.where(row == 0, last1, u1)
            u2 = jnp.where(row == 0, last2, jnp.where(row == 1, last1, u2))
        if (r * rows) % seq_len == 0 or seq_len < rows:
            pos = (row + r * rows) & (seq_len - 1)
            u1 = jnp.where(pos == 0, p1_ref[rs, :] if has_state else 0.0, u1)
            u2 = jnp.where(pos < 2, p2_ref[rs, :] if has_state else 0.0, u2)
        y = u2 * w[0:1] + u1 * w[1:2] + u * w[2:3]
        z_ref[rs, :] = (_dot(xn, wb_ref[...]) * y).astype(BF16)
        last1, last2 = u[rows - 1:rows], u[rows - 2:rows - 1]
        for (src_ref, dst_ref), per_step in list(zip(casts, cast_per_step))[r::tm // rows]:
            if per_step:
                _copy_slab(src_ref, dst_ref)
            else:
                pl.when(pl.program_id(1) == 0)(functools.partial(_copy_slab, src_ref, dst_ref))
    u_ref[...] = u[rows - tail_rows:]


def _copy_slab(src_ref, dst_ref):
    if len(dst_ref.shape) == 2:
        dst_ref[...] = src_ref[...].astype(BF16)
    else:
        cb = dst_ref.shape[2]
        for b in range(dst_ref.shape[0]):
            dst_ref[b] = src_ref[:, b * cb:(b + 1) * cb].astype(BF16)


def _conv_branch(xn, w_in, conv_w, prev, cast, *, tm, tn, rows, seq_len, tail_rows,
                 col_b, col_c, col_h):
    m, d = xn.shape
    n = conv_w.shape[1]
    has_state = prev is not None
    assert seq_len & (seq_len - 1) == 0 and tm % seq_len == 0
    assert tm % rows == 0 and tail_rows <= rows and (rows % seq_len == 0 or seq_len % rows == 0)
    n_j = n // tn
    steps = (m // tm) * n_j
    wspec = lambda off: pl.BlockSpec((d, tn), lambda i, j: (0, off // tn + j))
    in_specs = [pl.BlockSpec((tm, d), lambda i, j: (i, 0)),
                wspec(col_b), wspec(col_c), wspec(col_h),
                pl.BlockSpec((conv_w.shape[0], tn), lambda i, j: (0, j))]
    args = [xn, w_in, w_in, w_in, conv_w]
    if has_state:
        in_specs += [pl.BlockSpec((tm, tn), lambda i, j: (i, j))] * 2
        args += list(prev)
    cast_per_step = tuple(c.shape[0] % (steps * BF16_SUBLANES) == 0 for c, _ in cast)
    slab_specs, copy_specs, copy_shapes, slab_bytes = [], [], [], 0
    for (c, col_block), per_step in zip(cast, cast_per_step):
        parts = steps if per_step else m // tm
        r, cols = c.shape
        assert r % (parts * BF16_SUBLANES) == 0
        step_of = (lambda i, j: i * n_j + j) if per_step else (lambda i, j: i)
        slab_specs.append(pl.BlockSpec((r // parts, cols),
                                       functools.partial(lambda s, i, j: (s(i, j), 0), step_of)))
        if col_block is None:
            copy_specs.append(slab_specs[-1])
            copy_shapes.append(jax.ShapeDtypeStruct((r, cols), BF16))
        else:
            assert cols % col_block == 0
            copy_specs.append(pl.BlockSpec(
                (cols // col_block, r // parts, col_block),
                functools.partial(lambda s, i, j: (0, s(i, j), 0), step_of)))
            copy_shapes.append(jax.ShapeDtypeStruct((cols // col_block, r, col_block), BF16))
        slab_bytes += 3 * _nbytes(c.shape, F32) // parts
    est = (2 * (_nbytes((tm, d), BF16) + 3 * _nbytes((d, tn), BF16) + _nbytes((tm, tn), BF16)
                + _nbytes((tail_rows, tn), F32)) + 8 * _nbytes((tm, tn), F32) + slab_bytes)
    return pl.pallas_call(
        functools.partial(_conv_kernel, seq_len=seq_len, tail_rows=tail_rows, has_state=has_state,
                          rows=rows, cast_per_step=cast_per_step),
        out_shape=(jax.ShapeDtypeStruct((m, n), BF16),
                   jax.ShapeDtypeStruct((m // tm * tail_rows, n), F32),
                   *copy_shapes),
        grid=(m // tm, n_j),
        in_specs=in_specs + slab_specs,
        out_specs=(pl.BlockSpec((tm, tn), lambda i, j: (i, j)),
                   pl.BlockSpec((tail_rows, tn), lambda i, j: (i, j)),
                   *copy_specs),
        compiler_params=_params(("parallel", "arbitrary"), est),
        name="conv_branch",
    )(*args, *[c for c, _ in cast])


def _stack_heads(q_ref, rows, col0, group):
    parts = []
    for g in range(group):
        blk = q_ref[rows, col0 + (g // 2) * LANES:col0 + (g // 2 + 1) * LANES]
        keep = (_lane_iota(blk.shape) >= HEAD_DIM) == bool(g % 2)
        parts.append(jnp.where(keep, blk, jnp.zeros_like(blk)))
    return jnp.concatenate(parts, axis=0)


def _sink_column(sink_ref, head0, group, rows):
    return jnp.concatenate(
        [jnp.full((rows, 1), sink_ref[head0 + g], F32) for g in range(group)], axis=0)


def _store_heads(o_ref, rows, col0, o, group, nrows):
    for lp in range(group // 2):
        a = o[(2 * lp) * nrows:(2 * lp + 1) * nrows]
        b = o[(2 * lp + 1) * nrows:(2 * lp + 2) * nrows]
        low = _lane_iota(a.shape) < HEAD_DIM
        o_ref[rows, col0 + lp * LANES:col0 + (lp + 1) * LANES] = jnp.where(low, a, b).astype(BF16)


def _attn_pairs(pairs, sink_ref, q_ref, k_ref, v_ref, o_ref, n_kv, group):
    def block_rows(index):
        start = index * PAIR
        return pl.ds(start if isinstance(start, int) else pl.multiple_of(start, PAIR), PAIR)

    first_q = _lane_iota((CHUNK, PAIR)) < CHUNK

    def keys_of(ref, pair, has_prev, tile):
        cols = slice(tile * LANES, (tile + 1) * LANES)
        if has_prev:
            return jnp.concatenate([ref[block_rows(pair - 1), cols], ref[block_rows(pair), cols]],
                                   axis=0)
        return ref[block_rows(pair), cols]

    def scores(pair, has_prev, kv):
        col0 = kv * group * HEAD_DIM
        q_tiles = jnp.concatenate(
            [q_ref[block_rows(pair), col0 + lp * LANES:col0 + (lp + 1) * LANES]
             for lp in range(group // 2)], axis=0)
        return [_dot_nt(keys_of(k_ref, pair, has_prev, 2 * kv + half), q_tiles)
                for half in range(2)]

    items = [(pair, has_prev, kv) for pair, has_prev in pairs for kv in range(n_kv)]
    st_next = scores(*items[0])
    for n, (pair, has_prev, kv) in enumerate(items):
        st = st_next
        if n + 1 < len(items):
            st_next = scores(*items[n + 1])
        rows = block_rows(pair)
        vt = keys_of(v_ref, pair, has_prev, kv).astype(F32).T.astype(BF16)
        col0 = kv * group * HEAD_DIM
        probs, rden = [], []
        for g in range(group):
            sg = st[g % 2][:, (g // 2) * PAIR:(g // 2 + 1) * PAIR]
            blocks = [sg[c * CHUNK:(c + 1) * CHUNK] for c in range(sg.shape[0] // CHUNK)]
            if has_prev:
                s = jnp.concatenate([jnp.where(first_q, blocks[0], blocks[3]),
                                     blocks[1], blocks[2]], axis=0)
            else:
                s = jnp.concatenate([blocks[0], jnp.where(first_q, NEG_INF, blocks[1])], axis=0)
            sink = sink_ref[kv * group + g]
            m = jnp.maximum(jnp.max(s, axis=0, keepdims=True), sink)
            e = jnp.exp2(s - m)
            den = jnp.sum(e, axis=0, keepdims=True) + jnp.exp2(sink - m)
            if has_prev:
                e_03, e_12 = e[:CHUNK], e[CHUNK:]
                e = jnp.concatenate([jnp.where(first_q, e_03, 0.0), e_12,
                                     jnp.where(first_q, 0.0, e_03)], axis=0)
            probs.append(e.astype(BF16))
            rden.append(1.0 / den)
        ot = _dot(vt, jnp.concatenate(probs, axis=1)) * jnp.concatenate(rden, axis=1)
        for lp in range(group // 2):
            a = ot[:HEAD_DIM, (2 * lp) * PAIR:(2 * lp + 1) * PAIR]
            b = ot[HEAD_DIM:, (2 * lp + 1) * PAIR:(2 * lp + 2) * PAIR]
            o_ref[rows, col0 + lp * LANES:col0 + (lp + 1) * LANES] = (
                jnp.concatenate([a, b], axis=0).T.astype(BF16))


def _attn_kernel(sink_ref, q_ref, k_ref, v_ref, o_ref, *, n_kv, group):
    def run(pairs):
        _attn_pairs(pairs, sink_ref, q_ref, k_ref, v_ref, o_ref, n_kv, group)

    def trip(it, carry):
        run([(1 + PAIRS_PER_TRIP * it + u, True) for u in range(PAIRS_PER_TRIP)])
        return carry

    n_pairs = q_ref.shape[0] // PAIR
    trips = (n_pairs - 1) // PAIRS_PER_TRIP
    run([(0, False)])
    lax.fori_loop(0, trips, trip, 0)
    leftover = range(1 + trips * PAIRS_PER_TRIP, n_pairs)
    if leftover:
        run([(pair, True) for pair in leftover])


def _attention(q, k2, v2, sinks, *, batch, seq, n_kv, group):
    m, d = q.shape
    assert WINDOW == PAIR and seq % PAIR == 0
    est = (2 * (2 * _nbytes((seq, d), BF16) + 3 * _nbytes((seq, n_kv * LANES), BF16))
           + 8 * _nbytes((group * PAIR, KEY_SPAN), F32))
    return pl.pallas_call(
        functools.partial(_attn_kernel, n_kv=n_kv, group=group),
        out_shape=jax.ShapeDtypeStruct((m, d), BF16),
        grid=(batch,),
        in_specs=[pl.BlockSpec(memory_space=pltpu.SMEM),
                  pl.BlockSpec((seq, d), lambda b: (b, 0)),
                  pl.BlockSpec((seq, 2 * n_kv * LANES), lambda b: (b, 0)),
                  pl.BlockSpec((seq, n_kv * LANES), lambda b: (b, 0))],
        out_specs=pl.BlockSpec((seq, d), lambda b: (b, 0)),
        compiler_params=_params(("parallel",), est),
        name="attention",
    )(sinks, q, k2, v2)


def _attn_sample_kernel(sink_ref, q_ref, kn_ref, vn_ref, kc_ref, vc_ref, o_ref, *, n_kv, group):
    t = q_ref.shape[0]
    for kv in range(n_kv):
        pair_sl = slice((kv // 2) * LANES, (kv // 2 + 1) * LANES)
        kc = _dup_halves(kc_ref[:, pair_sl])[kv % 2].astype(BF16)
        vc = _dup_halves(vc_ref[:, pair_sl])[kv % 2].astype(BF16)
        kn = (kn_ref[:, 2 * kv * LANES:(2 * kv + 1) * LANES]
              + kn_ref[:, (2 * kv + 1) * LANES:(2 * kv + 2) * LANES])
        vn = vn_ref[:, kv * LANES:(kv + 1) * LANES]
        col0 = kv * group * HEAD_DIM
        lhs = _stack_heads(q_ref, slice(0, t), col0, group)
        s_c = _dot_nt(lhs, kc)
        s_n = _dot_nt(lhs, kn)
        sink = _sink_column(sink_ref, kv * group, group, t)
        m = jnp.maximum(jnp.maximum(jnp.max(s_c, axis=1, keepdims=True),
                                    jnp.max(s_n, axis=1, keepdims=True)), sink)
        e_c = jnp.exp2(s_c - m)
        e_n = jnp.exp2(s_n - m)
        den = (jnp.sum(e_c, axis=1, keepdims=True) + jnp.sum(e_n, axis=1, keepdims=True)
               + jnp.exp2(sink - m))
        o = (_dot(e_c.astype(BF16), vc) + _dot(e_n.astype(BF16), vn)) / den
        _store_heads(o_ref, slice(0, t), col0, o, group, t)


def _attention_sample(q, k2, v2, cache_k, cache_v, sinks, *, batch, t_new, n_kv, group):
    m, d = q.shape
    win = cache_k.shape[0] // batch
    d_kv = cache_k.shape[1]
    est = (2 * (2 * _nbytes((t_new, d), BF16) + 3 * _nbytes((t_new, n_kv * LANES), BF16)
                + 2 * _nbytes((win, d_kv), F32)) + 8 * _nbytes((group * t_new, win), F32))
    return pl.pallas_call(
        functools.partial(_attn_sample_kernel, n_kv=n_kv, group=group),
        out_shape=jax.ShapeDtypeStruct((m, d), BF16),
        grid=(batch,),
        in_specs=[pl.BlockSpec(memory_space=pltpu.SMEM),
                  pl.BlockSpec((t_new, d), lambda b: (b, 0)),
                  pl.BlockSpec((t_new, 2 * n_kv * LANES), lambda b: (b, 0)),
                  pl.BlockSpec((t_new, n_kv * LANES), lambda b: (b, 0)),
                  pl.BlockSpec((win, d_kv), lambda b: (b, 0)),
                  pl.BlockSpec((win, d_kv), lambda b: (b, 0))],
        out_specs=pl.BlockSpec((t_new, d), lambda b: (b, 0)),
        compiler_params=_params(("parallel",), est),
        name="attention_sample",
    )(sinks, q, k2, v2, cache_k, cache_v)


def _merge_kernel(xn_ref, a_ref, z_ref, wga_ref, wgc_ref, wap_ref, wcp_ref, o_ref):
    xn = xn_ref[...]
    attn = jax.nn.sigmoid(_dot(xn, wga_ref[...])) * _dot(a_ref[...], wap_ref[...])
    conv = jax.nn.sigmoid(_dot(xn, wgc_ref[...])) * _dot(z_ref[...], wcp_ref[...])
    o_ref[...] = (attn + conv).astype(BF16)


def _merge(xn, attn, z, w_in, w_ap, w_cp, *, tm, tn, col_ga, col_gc):
    m, d = xn.shape
    act = pl.BlockSpec((tm, d), lambda i, j: (i, 0))
    wspec = lambda off: pl.BlockSpec((d, tn), lambda i, j: (0, off // tn + j))
    est = (2 * (3 * _nbytes((tm, d), BF16) + 4 * _nbytes((d, tn), BF16) + _nbytes((tm, tn), BF16))
           + 6 * _nbytes((tm, tn), F32))
    return pl.pallas_call(
        _merge_kernel,
        out_shape=jax.ShapeDtypeStruct((m, d), BF16),
        grid=(m // tm, d // tn),
        in_specs=[act, act, act, wspec(col_ga), wspec(col_gc), wspec(0), wspec(0)],
        out_specs=pl.BlockSpec((tm, tn), lambda i, j: (i, j)),
        compiler_params=_params(("parallel", "arbitrary"), est),
        name="merge",
    )(xn, attn, z, w_in, w_in, w_ap, w_cp)


def _out_kernel(x_ref, m_ref, w_ref, g_ref, x1_ref, xn_ref, *, rows):
    for r in range(x_ref.shape[0] // rows):
        rs = slice(r * rows, (r + 1) * rows)
        x1 = x_ref[rs, :] + _dot(m_ref[rs, :], w_ref[...])
        x1_ref[rs, :] = x1
        xn_ref[rs, :] = _rmsnorm_rows(x1, g_ref[...]).astype(BF16)


def _out_proj(x, merged, w_out, g_ffn, *, tm, rows):
    m, d = x.shape
    assert m % tm == 0 and tm % rows == 0
    est = (2 * (2 * _nbytes((tm, d), F32) + 2 * _nbytes((tm, d), BF16)) + _nbytes((d, d), BF16)
           + 4 * _nbytes((rows, d), F32))
    return pl.pallas_call(
        functools.partial(_out_kernel, rows=rows),
        out_shape=(jax.ShapeDtypeStruct((m, d), F32), jax.ShapeDtypeStruct((m, d), BF16)),
        grid=(m // tm,),
        in_specs=[pl.BlockSpec((tm, d), lambda i: (i, 0)),
                  pl.BlockSpec((tm, d), lambda i: (i, 0)),
                  pl.BlockSpec((d, d), lambda i: (0, 0), pipeline_mode=pl.Buffered(1)),
                  pl.BlockSpec((1, d), lambda i: (0, 0))],
        out_specs=(pl.BlockSpec((tm, d), lambda i: (i, 0)),
                   pl.BlockSpec((tm, d), lambda i: (i, 0))),
        compiler_params=_params(("parallel",), est),
        name="out_proj",
    )(x, merged, w_out, g_ffn)


def _ffn_kernel(x_ref, xn_ref, wg_ref, wu_ref, wd_ref, o_ref):
    f = pl.program_id(1)
    slab = x_ref.shape[1]
    n_slabs = o_ref.shape[1] // slab

    @pl.when(f == 0)
    def _():
        o_ref[...] = jnp.zeros_like(o_ref)

    xn = xn_ref[...]
    h = jax.nn.silu(_dot(xn, wg_ref[...])) * _dot(xn, wu_ref[...])
    o_ref[...] += _dot(h.astype(BF16), wd_ref[...])

    for s in range(n_slabs):
        @pl.when(f == s)
        def _(s=s):
            o_ref[:, s * slab:(s + 1) * slab] += x_ref[...]


def _ffn(x, xn, w_gate, w_up, w_down, *, tm, tf, slab):
    m, d = x.shape
    d_ff = w_down.shape[0]
    n_slabs = d // slab
    assert n_slabs <= d_ff // tf and w_gate.shape == w_up.shape == (d_ff // tf, d, tf)
    est = (2 * (_nbytes((tm, slab), F32) + _nbytes((tm, d), F32) + _nbytes((tm, d), BF16)
                + 3 * _nbytes((d, tf), BF16))
           + 4 * _nbytes((tm, tf), F32) + _nbytes((tm, d), F32))
    row = lambda i, f: (i, 0)
    return pl.pallas_call(
        _ffn_kernel,
        out_shape=jax.ShapeDtypeStruct((m, d), F32),
        grid=(m // tm, d_ff // tf),
        in_specs=[pl.BlockSpec((tm, slab), lambda i, f: (i, jnp.minimum(f, n_slabs - 1))),
                  pl.BlockSpec((tm, d), row),
                  pl.BlockSpec((None, d, tf), lambda i, f: (f, 0, 0)),
                  pl.BlockSpec((None, d, tf), lambda i, f: (f, 0, 0)),
                  pl.BlockSpec((tf, d), lambda i, f: (f, 0))],
        out_specs=pl.BlockSpec((tm, d), row),
        compiler_params=_params(("parallel", "arbitrary"), est),
        name="ffn",
    )(x, xn, w_gate, w_up, w_down)


def _rope_tables(pos, repeat, g, scale):
    half = HEAD_DIM // 2
    inv = jnp.exp(-math.log(ROPE_THETA) * jnp.arange(half, dtype=F32) * (2.0 / HEAD_DIM))
    ang = pos.astype(F32)[:, None] * inv[None, :]
    cos = jnp.cos(ang)
    sin = jnp.sin(ang)
    g_swap = jnp.concatenate([g[half:], g[:half]])
    t_same = g[None, :] * jnp.concatenate([cos, cos], axis=-1) * scale
    t_swap = g_swap[None, :] * jnp.concatenate([-sin, sin], axis=-1) * scale
    reps = (repeat, LANES // HEAD_DIM)
    return jnp.tile(t_same, reps), jnp.tile(t_swap, reps)


LATE_WEIGHTS = ("w_attn_proj", "w_conv_proj", "w_out", "w_ffn_gate", "w_ffn_up", "w_ffn_down")
FFN_IN_WEIGHTS = ("w_ffn_gate", "w_ffn_up")


def _layer(x, w, *, batch, seq, pos, tiles, conv_prev, sample_cache):
    d = x.shape[1]
    n_heads = w["sinks"].shape[0]
    d_attn = n_heads * HEAD_DIM
    d_kv = w["d_kv"]
    n_kv = d_kv // HEAD_DIM
    group = n_heads // n_kv
    d_conv = w["conv_w"].shape[1]
    col_b = 0
    col_c = col_b + d_conv
    col_h = col_c + d_conv
    col_ga = col_h + d_conv
    col_gc = col_ga + d
    tm = tiles["tm"]

    rope_rows = max(seq, tiles["tm_qkv"])
    log2e = math.log2(math.e)
    tables = (_rope_tables(pos, rope_rows // seq, w["q_norm_g"], HEAD_DIM ** -0.5 * log2e)
              + _rope_tables(pos, rope_rows // seq, w["k_norm_g"], 1.0))
    sinks = w["sinks"] * log2e

    w_f32 = None if "w_rest" in w else w["w_in_f32"]
    xn, q, k, v, k2, v2, *w_rest = _qkv_proj(
        x, w["norm_mix_g"], w["w_qkv"], tables, tiles["kv_tail"], w_f32,
        tm=tiles["tm_qkv"], d_attn=d_attn, d_kv=d_kv, rows=tiles["qkv_rows"])
    if w_rest:
        w = dict(w, w_rest=w_rest[0])
    late = [name for name in LATE_WEIGHTS if w[name].dtype != BF16]
    z, u_tail, *late_bf16 = _conv_branch(
        xn, w["w_rest"], w["conv_w"], conv_prev,
        [(w[name], tiles["tf"] if name in FFN_IN_WEIGHTS else None) for name in late],
        tm=tiles["tm_conv"], tn=tiles["tn_conv"], rows=tiles["conv_rows"], seq_len=seq,
        tail_rows=tiles["conv_tail"], col_b=col_b, col_c=col_c, col_h=col_h)
    w = dict(w, **dict(zip(late, late_bf16)))
    if sample_cache is None:
        attn = _attention(q, k2, v2, sinks, batch=batch, seq=seq, n_kv=n_kv, group=group)
    else:
        attn = _attention_sample(q, k2, v2, sample_cache[0], sample_cache[1], sinks,
                                 batch=batch, t_new=seq, n_kv=n_kv, group=group)
    merged = _merge(xn, attn, z, w["w_rest"], w["w_attn_proj"], w["w_conv_proj"],
                    tm=tm, tn=tiles["tn_merge"], col_ga=col_ga, col_gc=col_gc)
    x1, xn_ffn = _out_proj(x, merged, w["w_out"], w["norm_ffn_g"],
                           tm=tiles["tm_out"], rows=tiles["out_rows"])
    y = _ffn(x1, xn_ffn, w["w_ffn_gate"], w["w_ffn_up"], w["w_ffn_down"],
             tm=tiles["tm_ffn"], tf=tiles["tf"], slab=V7X_MXU_DIM)
    return y, k, v, u_tail, w


def kernel(x_prompt, x_sample, cache_k, cache_v, state_conv, norm_mix_g, w_in, q_norm_g, k_norm_g,
           attn_sinks, conv_w, w_attn_proj, w_conv_proj, w_out, norm_ffn_g, w_ffn_gate, w_ffn_up,
           w_ffn_down):
    depth = w_in.shape[0]
    assert depth == 1
    b_p, s_p, d = x_prompt.shape
    b_s, s_s, _ = x_sample.shape
    n_kv = cache_k.shape[3]
    d_kv = n_kv * HEAD_DIM
    win = cache_k.shape[2]
    n_state = state_conv.shape[2]

    w = {
        "d_kv": d_kv,
        "norm_mix_g": norm_mix_g[0].reshape(1, d),
        "w_qkv": w_in[0][:, :attn_sinks.shape[1] * HEAD_DIM + 2 * d_kv].astype(BF16),
        "w_in_f32": w_in[0],
        "q_norm_g": q_norm_g[0],
        "k_norm_g": k_norm_g[0],
        "sinks": attn_sinks[0],
        "conv_w": conv_w[0],
        "w_attn_proj": w_attn_proj[0],
        "w_conv_proj": w_conv_proj[0],
        "w_out": w_out[0],
        "norm_ffn_g": norm_ffn_g[0].reshape(1, d),
        "w_ffn_gate": w_ffn_gate[0],
        "w_ffn_up": w_ffn_up[0],
        "w_ffn_down": w_ffn_down[0],
    }

    m_p = b_p * s_p
    tm_p = min(1024, s_p)
    keep = min(WINDOW, s_p)
    tiles_p = dict(tm=tm_p, tm_qkv=min(512, s_p), qkv_rows=min(512, s_p), kv_tail=(s_p, keep), tm_conv=s_p, tn_conv=512, conv_tail=8,
                   conv_rows=min(1024, s_p), tn_merge=512, tm_out=min(512, s_p),
                   out_rows=min(512, s_p),
                   tm_ffn=tm_p, tf=512)
    y_p, k_p, v_p, u_p, w = _layer(x_prompt.reshape(m_p, d), w, batch=b_p, seq=s_p,
                                   pos=jnp.arange(s_p), tiles=tiles_p, conv_prev=None,
                                   sample_cache=None)
    new_k_p = k_p.reshape(b_p, n_kv, HEAD_DIM, keep).transpose(0, 3, 1, 2)[None]
    new_v_p = v_p.reshape(b_p, n_kv, HEAD_DIM, keep).transpose(0, 3, 1, 2)[None]
    new_c_p = u_p.reshape(b_p, tiles_p["conv_tail"], -1)[:, tiles_p["conv_tail"] - n_state:][None]

    m_s = b_s * s_s
    st = state_conv[0]
    prev1 = jnp.pad(st[:, 1:2], ((0, 0), (0, s_s - 1), (0, 0))).reshape(m_s, -1)
    prev2 = jnp.pad(st, ((0, 0), (0, s_s - n_state), (0, 0))).reshape(m_s, -1)
    tiles_s = dict(tm=m_s, tm_qkv=m_s, qkv_rows=m_s, kv_tail=None, tm_conv=m_s, tn_conv=512, conv_tail=m_s,
                   conv_rows=m_s, tn_merge=512, tm_out=m_s, out_rows=m_s, tm_ffn=m_s, tf=512)
    y_s, k_s, v_s, u_s, _ = _layer(x_sample.reshape(m_s, d), w, batch=b_s, seq=s_s,
                                   pos=PAST_LEN + jnp.arange(s_s), tiles=tiles_s,
                                   conv_prev=(prev1, prev2),
                                   sample_cache=(cache_k[0].reshape(b_s * win, d_kv),
                                                 cache_v[0].reshape(b_s * win, d_kv)))
    new_k_s = k_s.reshape(b_s, s_s, n_kv, HEAD_DIM)[None]
    new_v_s = v_s.reshape(b_s, s_s, n_kv, HEAD_DIM)[None]
    new_c_s = u_s.reshape(b_s, s_s, -1)[:, s_s - n_state:][None]

    return (y_p.reshape(b_p, s_p, d), y_s.reshape(b_s, s_s, d), new_k_p, new_v_p, new_c_p,
            new_k_s, new_v_s, new_c_s)
```

```python
import functools
import math

import jax
import jax.numpy as jnp
from jax import lax
from jax.experimental import pallas as pl
from jax.experimental.pallas import tpu as pltpu

F32 = jnp.float32
BF16 = jnp.bfloat16

CHUNK = 64
WINDOW = 128
PAST_LEN = 1024
ROPE_THETA = 10000.0
EPS = 1e-6
NEG_INF = -1e30
HEAD_DIM = 64
LANES = 128
BF16_SUBLANES = 16
PAIR = 2 * CHUNK
KEY_SPAN = PAIR + WINDOW
PAIRS_PER_TRIP = 5

V7X_MXU_DIM = 256
MIB = 1024 * 1024
V7X_VMEM_BYTES = 64 * MIB
VMEM_COMPILER_RESERVE_BYTES = 6 * MIB
VMEM_CAP_BYTES = V7X_VMEM_BYTES - VMEM_COMPILER_RESERVE_BYTES
VMEM_FLOOR_BYTES = 16 * MIB


def _params(semantics, vmem_estimate):
    limit = int(min(max(vmem_estimate, VMEM_FLOOR_BYTES), VMEM_CAP_BYTES))
    return pltpu.CompilerParams(dimension_semantics=semantics, vmem_limit_bytes=limit)


def _nbytes(shape, dtype):
    return math.prod(shape) * jnp.dtype(dtype).itemsize


def _dot(a, b):
    return jnp.dot(a, b, preferred_element_type=F32)


def _dot_nt(a, b):
    return lax.dot_general(a, b, (((1,), (1,)), ((), ())), preferred_element_type=F32)


def _lane_iota(shape):
    return lax.broadcasted_iota(jnp.int32, shape, 1)


def _rmsnorm_rows(x, g):
    y = x * lax.rsqrt(jnp.mean(x * x, axis=-1, keepdims=True) + EPS)
    return y * g


def _headnorm_rope(y, t_same, t_swap, seg):
    width = y.shape[1]
    s = y * y
    s_hi = s.astype(BF16)
    s_lo = (s - s_hi.astype(F32)).astype(BF16)
    ms = _dot(s_hi, seg) + _dot(s_lo, seg)
    half = HEAD_DIM // 2
    ahead = pltpu.roll(y, width - half, axis=1)
    behind = pltpu.roll(y, half, axis=1)
    swapped = jnp.where((_lane_iota(y.shape) & (HEAD_DIM - 1)) < half, ahead, behind)
    reps = width // t_same.shape[1]
    t_same = jnp.concatenate([t_same] * reps, axis=1)
    t_swap = jnp.concatenate([t_swap] * reps, axis=1)
    return (y * t_same + swapped * t_swap) * lax.rsqrt(ms + EPS)


def _dup_halves(x):
    low = _lane_iota(x.shape) < HEAD_DIM
    r = pltpu.roll(x, HEAD_DIM, axis=1)
    return jnp.where(low, x, r), jnp.where(low, r, x)


def _isolate_halves(x):
    low = _lane_iota(x.shape) < HEAD_DIM
    r = pltpu.roll(x, HEAD_DIM, axis=1)
    return (jnp.where(low, x, 0.0), jnp.where(low, 0.0, r),
            jnp.where(low, r, 0.0), jnp.where(low, 0.0, x))


def _qkv_kernel(*refs, n_q_blocks, d_kv, rows, tail_keep, n_rest):
    x_ref, g_ref, w_ref, seg_ref, qs_ref, qw_ref, ks_ref, kw_ref = refs[:8]
    rest_refs = refs[8:8 + n_rest]
    xn_ref, q_ref, k_ref, v_ref, k2_ref, v2_ref = refs[8 + n_rest:][:6]
    rest_out = refs[-1] if n_rest else None
    tm = q_ref.shape[0]
    tn = 2 * d_kv
    chunks = [slice(r * rows, (r + 1) * rows) for r in range(tm // rows)]
    seg = seg_ref[...]
    slab = seg.shape[0]

    for rs in chunks:
        xn = _rmsnorm_rows(x_ref[rs, :], g_ref[...]).astype(BF16)
        xn_ref[rs, :] = xn
        for jb in range(n_q_blocks):
            y = _dot(xn, w_ref[:, jb * tn:(jb + 1) * tn])
            for c in range(tn // slab):
                sl = slice(c * slab, (c + 1) * slab)
                q_ref[rs, jb * tn + c * slab:jb * tn + (c + 1) * slab] = _headnorm_rope(
                    y[:, sl], qs_ref[rs, :], qw_ref[rs, :], seg).astype(BF16)

        y = _dot(xn, w_ref[:, n_q_blocks * tn:])
        k_all = jnp.concatenate(
            [_headnorm_rope(y[:, c * slab:(c + 1) * slab], ks_ref[rs, :], kw_ref[rs, :], seg)
             for c in range(d_kv // slab)], axis=1)
        if tail_keep is None:
            k_ref[rs, :] = k_all
            v_ref[rs, :] = y[:, d_kv:]
        elif rs is chunks[-1]:
            k_ref[0] = k_all[rows - tail_keep:].T
            v_ref[0] = y[rows - tail_keep:, d_kv:].T
        for c in range(d_kv // LANES):
            sl = slice(c * LANES, (c + 1) * LANES)
            k = k_all[:, sl]
            v = y[:, d_kv + c * LANES:d_kv + (c + 1) * LANES]
            va, vb = _dup_halves(v)
            lo = slice(2 * c * LANES, (2 * c + 1) * LANES)
            hi = slice((2 * c + 1) * LANES, (2 * c + 2) * LANES)
            v2_ref[rs, lo] = va.astype(BF16)
            v2_ref[rs, hi] = vb.astype(BF16)
            for n, piece in enumerate(_isolate_halves(k)):
                k2_ref[rs, (4 * c + n) * LANES:(4 * c + n + 1) * LANES] = piece.astype(BF16)

        for n, ref in list(enumerate(rest_refs))[chunks.index(rs)::len(chunks)]:
            width = ref.shape[1]
            rest_out[:, n * width:(n + 1) * width] = ref[...].astype(BF16)


def _qkv_proj(x, g, w_in, tables, tail, w_f32, *, tm, d_attn, d_kv, rows):
    m, d = x.shape
    tn = 2 * d_kv
    n_q_blocks = d_attn // tn
    qkv_cols = d_attn + tn
    assert m % tm == 0 and tm % rows == 0 and d_attn % tn == 0 and w_in.shape == (d, qkv_cols)
    n_tiles = m // tm
    rest_specs, rest_shape, rest_out_spec, n_rest = [], (), (), 0
    if w_f32 is not None:
        n_rest = (w_f32.shape[1] - qkv_cols) // qkv_cols
        assert w_f32.shape[1] == (n_rest + 1) * qkv_cols and d % (n_tiles * BF16_SUBLANES) == 0
        slab_rows = d // n_tiles
        rest_specs = [pl.BlockSpec((slab_rows, qkv_cols), functools.partial(
            lambda n, i: (i, n + 1), n)) for n in range(n_rest)]
        rest_shape = (jax.ShapeDtypeStruct((d, n_rest * qkv_cols), BF16),)
        rest_out_spec = (pl.BlockSpec((slab_rows, n_rest * qkv_cols), lambda i: (i, 0)),)
    if tail is None:
        kv_shape = jax.ShapeDtypeStruct((m, d_kv), F32)
        kv_spec = pl.BlockSpec((tm, d_kv), lambda i: (i, 0))
    else:
        seq, keep = tail
        assert seq % tm == 0 and keep <= rows and keep % LANES == 0
        kv_shape = jax.ShapeDtypeStruct((m // seq, d_kv, keep), F32)
        kv_spec = pl.BlockSpec((1, d_kv, keep), lambda i: (i // (seq // tm), 0, 0))
    rope_blocks = tables[0].shape[0] // tm
    tab = pl.BlockSpec((tm, LANES), lambda i: (i % rope_blocks, 0))
    row = lambda width: pl.BlockSpec((tm, width), lambda i: (i, 0))
    fixed = lambda shape: pl.BlockSpec(shape, lambda i: (0, 0), pipeline_mode=pl.Buffered(1))
    head_of = jnp.arange(V7X_MXU_DIM) // HEAD_DIM
    seg = jnp.where(head_of[:, None] == head_of[None, :], 1.0 / HEAD_DIM, 0.0).astype(BF16)
    est = (2 * (_nbytes((tm, d), F32) + _nbytes((tm, d), BF16) + _nbytes((tm, d_attn), BF16)
                + 3 * _nbytes((tm, tn), BF16) + 2 * _nbytes((tm, d_kv), F32)
                + 4 * _nbytes((tm, LANES), F32))
           + _nbytes(w_in.shape, BF16) + 6 * _nbytes((rows, tn), F32) + 2 * _nbytes((rows, d), F32))
    if n_rest:
        est += 3 * _nbytes((d // n_tiles, n_rest * qkv_cols), F32)
    return pl.pallas_call(
        functools.partial(_qkv_kernel, n_q_blocks=n_q_blocks, d_kv=d_kv, rows=rows,
                          tail_keep=None if tail is None else tail[1], n_rest=n_rest),
        out_shape=(jax.ShapeDtypeStruct((m, d), BF16), jax.ShapeDtypeStruct((m, d_attn), BF16),
                   kv_shape, kv_shape,
                   jax.ShapeDtypeStruct((m, 2 * tn), BF16), jax.ShapeDtypeStruct((m, tn), BF16),
                   *rest_shape),
        grid=(n_tiles,),
        in_specs=[row(d), fixed((1, d)), fixed(w_in.shape), fixed(seg.shape),
                  tab, tab, tab, tab] + rest_specs,
        out_specs=(row(d), row(d_attn), kv_spec, kv_spec, row(2 * tn), row(tn), *rest_out_spec),
        compiler_params=_params(("arbitrary",), est),
        name="qkv_proj",
    )(x, g, w_in, seg, *tables, *([w_f32] * n_rest))


def _conv_kernel(*refs, seq_len, tail_rows, has_state, rows, cast_per_step):
    n_cast = len(cast_per_step)
    n_in = 7 if has_state else 5
    xn_ref, wb_ref, wc_ref, wh_ref, cw_ref = refs[:5]
    p1_ref, p2_ref = refs[5:n_in] if has_state else (None, None)
    z_ref, u_ref = refs[n_in + n_cast:n_in + n_cast + 2]
    casts = list(zip(refs[n_in:n_in + n_cast], refs[n_in + n_cast + 2:]))
    tm, tn = z_ref.shape
    w = cw_ref[...]
    row = lax.broadcasted_iota(jnp.int32, (rows, tn), 0)
    last1 = last2 = None
    for r in range(tm // rows):
        rs = slice(r * rows, (r + 1) * rows)
        xn = xn_ref[rs, :]
        u = _dot(xn, wc_ref[...]) * _dot(xn, wh_ref[...])
        u1 = pltpu.roll(u, 1, axis=0)
        u2 = pltpu.roll(u, 2, axis=0)
        if r > 0:
            u1 = jnp.where(row == 0, last1, u1)
            u2 = jnp.where(row == 0, last2, jnp.where(row == 1, last1, u2))
        if (r * rows) % seq_len == 0 or seq_len < rows:
            pos = (row + r * rows) & (seq_len - 1)
            u1 = jnp.where(pos == 0, p1_ref[rs, :] if has_state else 0.0, u1)
            u2 = jnp.where(pos < 2, p2_ref[rs, :] if has_state else 0.0, u2)
        y = u2 * w[0:1] + u1 * w[1:2] + u * w[2:3]
        z_ref[rs, :] = (_dot(xn, wb_ref[...]) * y).astype(BF16)
        last1, last2 = u[rows - 1:rows], u[rows - 2:rows - 1]
        for (src_ref, dst_ref), per_step in list(zip(casts, cast_per_step))[r::tm // rows]:
            if per_step:
                _copy_slab(src_ref, dst_ref)
            else:
                pl.when(pl.program_id(1) == 0)(functools.partial(_copy_slab, src_ref, dst_ref))
    u_ref[...] = u[rows - tail_rows:]


def _copy_slab(src_ref, dst_ref):
    if len(dst_ref.shape) == 2:
        dst_ref[...] = src_ref[...].astype(BF16)
    else:
        cb = dst_ref.shape[2]
        for b in range(dst_ref.shape[0]):
            dst_ref[b] = src_ref[:, b * cb:(b + 1) * cb].astype(BF16)


def _conv_branch(xn, w_in, conv_w, prev, cast, *, tm, tn, rows, seq_len, tail_rows,
                 col_b, col_c, col_h):
    m, d = xn.shape
    n = conv_w.shape[1]
    has_state = prev is not None
    assert seq_len & (seq_len - 1) == 0 and tm % seq_len == 0
    assert tm % rows == 0 and tail_rows <= rows and (rows % seq_len == 0 or seq_len % rows == 0)
    n_j = n // tn
    steps = (m // tm) * n_j
    wspec = lambda off: pl.BlockSpec((d, tn), lambda i, j: (0, off // tn + j))
    in_specs = [pl.BlockSpec((tm, d), lambda i, j: (i, 0)),
                wspec(col_b), wspec(col_c), wspec(col_h),
                pl.BlockSpec((conv_w.shape[0], tn), lambda i, j: (0, j))]
    args = [xn, w_in, w_in, w_in, conv_w]
    if has_state:
        in_specs += [pl.BlockSpec((tm, tn), lambda i, j: (i, j))] * 2
        args += list(prev)
    cast_per_step = tuple(c.shape[0] % (steps * BF16_SUBLANES) == 0 for c, _ in cast)
    slab_specs, copy_specs, copy_shapes, slab_bytes = [], [], [], 0
    for (c, col_block), per_step in zip(cast, cast_per_step):
        parts = steps if per_step else m // tm
        r, cols = c.shape
        assert r % (parts * BF16_SUBLANES) == 0
        step_of = (lambda i, j: i * n_j + j) if per_step else (lambda i, j: i)
        slab_specs.append(pl.BlockSpec((r // parts, cols),
                                       functools.partial(lambda s, i, j: (s(i, j), 0), step_of)))
        if col_block is None:
            copy_specs.append(slab_specs[-1])
            copy_shapes.append(jax.ShapeDtypeStruct((r, cols), BF16))
        else:
            assert cols % col_block == 0
            copy_specs.append(pl.BlockSpec(
                (cols // col_block, r // parts, col_block),
                functools.partial(lambda s, i, j: (0, s(i, j), 0), step_of)))
            copy_shapes.append(jax.ShapeDtypeStruct((cols // col_block, r, col_block), BF16))
        slab_bytes += 3 * _nbytes(c.shape, F32) // parts
    est = (2 * (_nbytes((tm, d), BF16) + 3 * _nbytes((d, tn), BF16) + _nbytes((tm, tn), BF16)
                + _nbytes((tail_rows, tn), F32)) + 8 * _nbytes((tm, tn), F32) + slab_bytes)
    return pl.pallas_call(
        functools.partial(_conv_kernel, seq_len=seq_len, tail_rows=tail_rows, has_state=has_state,
                          rows=rows, cast_per_step=cast_per_step),
        out_shape=(jax.ShapeDtypeStruct((m, n), BF16),
                   jax.ShapeDtypeStruct((m // tm * tail_rows, n), F32),
                   *copy_shapes),
        grid=(m // tm, n_j),
        in_specs=in_specs + slab_specs,
        out_specs=(pl.BlockSpec((tm, tn), lambda i, j: (i, j)),
                   pl.BlockSpec((tail_rows, tn), lambda i, j: (i, j)),
                   *copy_specs),
        compiler_params=_params(("parallel", "arbitrary"), est),
        name="conv_branch",
    )(*args, *[c for c, _ in cast])


def _stack_heads(q_ref, rows, col0, group):
    parts = []
    for g in range(group):
        blk = q_ref[rows, col0 + (g // 2) * LANES:col0 + (g // 2 + 1) * LANES]
        keep = (_lane_iota(blk.shape) >= HEAD_DIM) == bool(g % 2)
        parts.append(jnp.where(keep, blk, jnp.zeros_like(blk)))
    return jnp.concatenate(parts, axis=0)


def _sink_column(sink_ref, head0, group, rows):
    return jnp.concatenate(
        [jnp.full((rows, 1), sink_ref[head0 + g], F32) for g in range(group)], axis=0)


def _store_heads(o_ref, rows, col0, o, group, nrows):
    for lp in range(group // 2):
        a = o[(2 * lp) * nrows:(2 * lp + 1) * nrows]
        b = o[(2 * lp + 1) * nrows:(2 * lp + 2) * nrows]
        low = _lane_iota(a.shape) < HEAD_DIM
        o_ref[rows, col0 + lp * LANES:col0 + (lp + 1) * LANES] = jnp.where(low, a, b).astype(BF16)


def _attn_pairs(pairs, sink_ref, q_ref, k_ref, v_ref, o_ref, n_kv, group):
    def block_rows(index):
        start = index * PAIR
        return pl.ds(start if isinstance(start, int) else pl.multiple_of(start, PAIR), PAIR)

    first_q = _lane_iota((CHUNK, PAIR)) < CHUNK

    def keys_of(ref, pair, has_prev, tile):
        cols = slice(tile * LANES, (tile + 1) * LANES)
        if has_prev:
            return jnp.concatenate([ref[block_rows(pair - 1), cols], ref[block_rows(pair), cols]],
                                   axis=0)
        return ref[block_rows(pair), cols]

    def scores(pair, has_prev, kv):
        col0 = kv * group * HEAD_DIM
        q_tiles = jnp.concatenate(
            [q_ref[block_rows(pair), col0 + lp * LANES:col0 + (lp + 1) * LANES]
             for lp in range(group // 2)], axis=0)
        both = jnp.concatenate([keys_of(k_ref, pair, has_prev, 2 * kv + half)
                                for half in range(2)], axis=0)
        st = _dot_nt(both, q_tiles)
        return [st[:st.shape[0] // 2], st[st.shape[0] // 2:]]

    items = [(pair, has_prev, kv) for pair, has_prev in pairs for kv in range(n_kv)]
    st_next = scores(*items[0])
    for n, (pair, has_prev, kv) in enumerate(items):
        st = st_next
        if n + 1 < len(items):
            st_next = scores(*items[n + 1])
        rows = block_rows(pair)
        vt = keys_of(v_ref, pair, has_prev, kv).astype(F32).T.astype(BF16)
        col0 = kv * group * HEAD_DIM
        probs, rden = [], []
        for g in range(group):
            sg = st[g % 2][:, (g // 2) * PAIR:(g // 2 + 1) * PAIR]
            blocks = [sg[c * CHUNK:(c + 1) * CHUNK] for c in range(sg.shape[0] // CHUNK)]
            if has_prev:
                s = jnp.concatenate([jnp.where(first_q, blocks[0], blocks[3]),
                                     blocks[1], blocks[2]], axis=0)
            else:
                s = jnp.concatenate([blocks[0], jnp.where(first_q, NEG_INF, blocks[1])], axis=0)
            sink = sink_ref[kv * group + g]
            m = jnp.maximum(jnp.max(s, axis=0, keepdims=True), sink)
            e = jnp.exp2(s - m)
            den = jnp.sum(e, axis=0, keepdims=True) + jnp.exp2(sink - m)
            if has_prev:
                e_03, e_12 = e[:CHUNK], e[CHUNK:]
                e = jnp.concatenate([jnp.where(first_q, e_03, 0.0), e_12,
                                     jnp.where(first_q, 0.0, e_03)], axis=0)
            probs.append(e.astype(BF16))
            rden.append(1.0 / den)
        ot = _dot(vt, jnp.concatenate(probs, axis=1)) * jnp.concatenate(rden, axis=1)
        for lp in range(group // 2):
            a = ot[:HEAD_DIM, (2 * lp) * PAIR:(2 * lp + 1) * PAIR]
            b = ot[HEAD_DIM:, (2 * lp + 1) * PAIR:(2 * lp + 2) * PAIR]
            o_ref[rows, col0 + lp * LANES:col0 + (lp + 1) * LANES] = (
                jnp.concatenate([a, b], axis=0).T.astype(BF16))


def _attn_kernel(sink_ref, q_ref, k_ref, v_ref, o_ref, *, n_kv, group):
    def run(pairs):
        _attn_pairs(pairs, sink_ref, q_ref, k_ref, v_ref, o_ref, n_kv, group)

    def trip(it, carry):
        run([(1 + PAIRS_PER_TRIP * it + u, True) for u in range(PAIRS_PER_TRIP)])
        return carry

    n_pairs = q_ref.shape[0] // PAIR
    trips = (n_pairs - 1) // PAIRS_PER_TRIP
    run([(0, False)])
    lax.fori_loop(0, trips, trip, 0)
    leftover = range(1 + trips * PAIRS_PER_TRIP, n_pairs)
    if leftover:
        run([(pair, True) for pair in leftover])


def _attention(q, k2, v2, sinks, *, batch, seq, n_kv, group):
    m, d = q.shape
    assert WINDOW == PAIR and seq % PAIR == 0
    est = (2 * (2 * _nbytes((seq, d), BF16) + 3 * _nbytes((seq, n_kv * LANES), BF16))
           + 8 * _nbytes((group * PAIR, KEY_SPAN), F32))
    return pl.pallas_call(
        functools.partial(_attn_kernel, n_kv=n_kv, group=group),
        out_shape=jax.ShapeDtypeStruct((m, d), BF16),
        grid=(batch,),
        in_specs=[pl.BlockSpec(memory_space=pltpu.SMEM),
                  pl.BlockSpec((seq, d), lambda b: (b, 0)),
                  pl.BlockSpec((seq, 2 * n_kv * LANES), lambda b: (b, 0)),
                  pl.BlockSpec((seq, n_kv * LANES), lambda b: (b, 0))],
        out_specs=pl.BlockSpec((seq, d), lambda b: (b, 0)),
        compiler_params=_params(("parallel",), est),
        name="attention",
    )(sinks, q, k2, v2)


def _attn_sample_kernel(sink_ref, q_ref, kn_ref, vn_ref, kc_ref, vc_ref, o_ref, *, n_kv, group):
    t = q_ref.shape[0]
    for kv in range(n_kv):
        pair_sl = slice((kv // 2) * LANES, (kv // 2 + 1) * LANES)
        kc = _dup_halves(kc_ref[:, pair_sl])[kv % 2].astype(BF16)
        vc = _dup_halves(vc_ref[:, pair_sl])[kv % 2].astype(BF16)
        kn = (kn_ref[:, 2 * kv * LANES:(2 * kv + 1) * LANES]
              + kn_ref[:, (2 * kv + 1) * LANES:(2 * kv + 2) * LANES])
        vn = vn_ref[:, kv * LANES:(kv + 1) * LANES]
        col0 = kv * group * HEAD_DIM
        lhs = _stack_heads(q_ref, slice(0, t), col0, group)
        s_c = _dot_nt(lhs, kc)
        s_n = _dot_nt(lhs, kn)
        sink = _sink_column(sink_ref, kv * group, group, t)
        m = jnp.maximum(jnp.maximum(jnp.max(s_c, axis=1, keepdims=True),
                                    jnp.max(s_n, axis=1, keepdims=True)), sink)
        e_c = jnp.exp2(s_c - m)
        e_n = jnp.exp2(s_n - m)
        den = (jnp.sum(e_c, axis=1, keepdims=True) + jnp.sum(e_n, axis=1, keepdims=True)
               + jnp.exp2(sink - m))
        o = (_dot(e_c.astype(BF16), vc) + _dot(e_n.astype(BF16), vn)) / den
        _store_heads(o_ref, slice(0, t), col0, o, group, t)


def _attention_sample(q, k2, v2, cache_k, cache_v, sinks, *, batch, t_new, n_kv, group):
    m, d = q.shape
    win = cache_k.shape[0] // batch
    d_kv = cache_k.shape[1]
    est = (2 * (2 * _nbytes((t_new, d), BF16) + 3 * _nbytes((t_new, n_kv * LANES), BF16)
                + 2 * _nbytes((win, d_kv), F32)) + 8 * _nbytes((group * t_new, win), F32))
    return pl.pallas_call(
        functools.partial(_attn_sample_kernel, n_kv=n_kv, group=group),
        out_shape=jax.ShapeDtypeStruct((m, d), BF16),
        grid=(batch,),
        in_specs=[pl.BlockSpec(memory_space=pltpu.SMEM),
                  pl.BlockSpec((t_new, d), lambda b: (b, 0)),
                  pl.BlockSpec((t_new, 2 * n_kv * LANES), lambda b: (b, 0)),
                  pl.BlockSpec((t_new, n_kv * LANES), lambda b: (b, 0)),
                  pl.BlockSpec((win, d_kv), lambda b: (b, 0)),
                  pl.BlockSpec((win, d_kv), lambda b: (b, 0))],
        out_specs=pl.BlockSpec((t_new, d), lambda b: (b, 0)),
        compiler_params=_params(("parallel",), est),
        name="attention_sample",
    )(sinks, q, k2, v2, cache_k, cache_v)


def _merge_kernel(xn_ref, a_ref, z_ref, wga_ref, wgc_ref, wap_ref, wcp_ref, o_ref):
    xn = xn_ref[...]
    attn = jax.nn.sigmoid(_dot(xn, wga_ref[...])) * _dot(a_ref[...], wap_ref[...])
    conv = jax.nn.sigmoid(_dot(xn, wgc_ref[...])) * _dot(z_ref[...], wcp_ref[...])
    o_ref[...] = (attn + conv).astype(BF16)


def _merge(xn, attn, z, w_in, w_ap, w_cp, *, tm, tn, col_ga, col_gc):
    m, d = xn.shape
    act = pl.BlockSpec((tm, d), lambda i, j: (i, 0))
    wspec = lambda off: pl.BlockSpec((d, tn), lambda i, j: (0, off // tn + j))
    est = (2 * (3 * _nbytes((tm, d), BF16) + 4 * _nbytes((d, tn), BF16) + _nbytes((tm, tn), BF16))
           + 6 * _nbytes((tm, tn), F32))
    return pl.pallas_call(
        _merge_kernel,
        out_shape=jax.ShapeDtypeStruct((m, d), BF16),
        grid=(m // tm, d // tn),
        in_specs=[act, act, act, wspec(col_ga), wspec(col_gc), wspec(0), wspec(0)],
        out_specs=pl.BlockSpec((tm, tn), lambda i, j: (i, j)),
        compiler_params=_params(("parallel", "arbitrary"), est),
        name="merge",
    )(xn, attn, z, w_in, w_in, w_ap, w_cp)


def _out_kernel(x_ref, m_ref, w_ref, g_ref, x1_ref, xn_ref, *, rows):
    for r in range(x_ref.shape[0] // rows):
        rs = slice(r * rows, (r + 1) * rows)
        x1 = x_ref[rs, :] + _dot(m_ref[rs, :], w_ref[...])
        x1_ref[rs, :] = x1
        xn_ref[rs, :] = _rmsnorm_rows(x1, g_ref[...]).astype(BF16)


def _out_proj(x, merged, w_out, g_ffn, *, tm, rows):
    m, d = x.shape
    assert m % tm == 0 and tm % rows == 0
    est = (2 * (2 * _nbytes((tm, d), F32) + 2 * _nbytes((tm, d), BF16)) + _nbytes((d, d), BF16)
           + 4 * _nbytes((rows, d), F32))
    return pl.pallas_call(
        functools.partial(_out_kernel, rows=rows),
        out_shape=(jax.ShapeDtypeStruct((m, d), F32), jax.ShapeDtypeStruct((m, d), BF16)),
        grid=(m // tm,),
        in_specs=[pl.BlockSpec((tm, d), lambda i: (i, 0)),
                  pl.BlockSpec((tm, d), lambda i: (i, 0)),
                  pl.BlockSpec((d, d), lambda i: (0, 0), pipeline_mode=pl.Buffered(1)),
                  pl.BlockSpec((1, d), lambda i: (0, 0))],
        out_specs=(pl.BlockSpec((tm, d), lambda i: (i, 0)),
                   pl.BlockSpec((tm, d), lambda i: (i, 0))),
        compiler_params=_params(("parallel",), est),
        name="out_proj",
    )(x, merged, w_out, g_ffn)


def _ffn_kernel(x_ref, xn_ref, wg_ref, wu_ref, wd_ref, o_ref):
    f = pl.program_id(1)
    slab = x_ref.shape[1]
    n_slabs = o_ref.shape[1] // slab

    @pl.when(f == 0)
    def _():
        o_ref[...] = jnp.zeros_like(o_ref)

    xn = xn_ref[...]
    h = jax.nn.silu(_dot(xn, wg_ref[...])) * _dot(xn, wu_ref[...])
    o_ref[...] += _dot(h.astype(BF16), wd_ref[...])

    for s in range(n_slabs):
        @pl.when(f == s)
        def _(s=s):
            o_ref[:, s * slab:(s + 1) * slab] += x_ref[...]


def _ffn(x, xn, w_gate, w_up, w_down, *, tm, tf, slab):
    m, d = x.shape
    d_ff = w_down.shape[0]
    n_slabs = d // slab
    assert n_slabs <= d_ff // tf and w_gate.shape == w_up.shape == (d_ff // tf, d, tf)
    est = (2 * (_nbytes((tm, slab), F32) + _nbytes((tm, d), F32) + _nbytes((tm, d), BF16)
                + 3 * _nbytes((d, tf), BF16))
           + 4 * _nbytes((tm, tf), F32) + _nbytes((tm, d), F32))
    row = lambda i, f: (i, 0)
    return pl.pallas_call(
        _ffn_kernel,
        out_shape=jax.ShapeDtypeStruct((m, d), F32),
        grid=(m // tm, d_ff // tf),
        in_specs=[pl.BlockSpec((tm, slab), lambda i, f: (i, jnp.minimum(f, n_slabs - 1))),
                  pl.BlockSpec((tm, d), row),
                  pl.BlockSpec((None, d, tf), lambda i, f: (f, 0, 0)),
                  pl.BlockSpec((None, d, tf), lambda i, f: (f, 0, 0)),
                  pl.BlockSpec((tf, d), lambda i, f: (f, 0))],
        out_specs=pl.BlockSpec((tm, d), row),
        compiler_params=_params(("parallel", "arbitrary"), est),
        name="ffn",
    )(x, xn, w_gate, w_up, w_down)


def _rope_tables(pos, repeat, g, scale):
    half = HEAD_DIM // 2
    inv = jnp.exp(-math.log(ROPE_THETA) * jnp.arange(half, dtype=F32) * (2.0 / HEAD_DIM))
    ang = pos.astype(F32)[:, None] * inv[None, :]
    cos = jnp.cos(ang)
    sin = jnp.sin(ang)
    g_swap = jnp.concatenate([g[half:], g[:half]])
    t_same = g[None, :] * jnp.concatenate([cos, cos], axis=-1) * scale
    t_swap = g_swap[None, :] * jnp.concatenate([-sin, sin], axis=-1) * scale
    reps = (repeat, LANES // HEAD_DIM)
    return jnp.tile(t_same, reps), jnp.tile(t_swap, reps)


LATE_WEIGHTS = ("w_attn_proj", "w_conv_proj", "w_out", "w_ffn_gate", "w_ffn_up", "w_ffn_down")
FFN_IN_WEIGHTS = ("w_ffn_gate", "w_ffn_up")


def _layer(x, w, *, batch, seq, pos, tiles, conv_prev, sample_cache):
    d = x.shape[1]
    n_heads = w["sinks"].shape[0]
    d_attn = n_heads * HEAD_DIM
    d_kv = w["d_kv"]
    n_kv = d_kv // HEAD_DIM
    group = n_heads // n_kv
    d_conv = w["conv_w"].shape[1]
    col_b = 0
    col_c = col_b + d_conv
    col_h = col_c + d_conv
    col_ga = col_h + d_conv
    col_gc = col_ga + d
    tm = tiles["tm"]

    rope_rows = max(seq, tiles["tm_qkv"])
    log2e = math.log2(math.e)
    tables = (_rope_tables(pos, rope_rows // seq, w["q_norm_g"], HEAD_DIM ** -0.5 * log2e)
              + _rope_tables(pos, rope_rows // seq, w["k_norm_g"], 1.0))
    sinks = w["sinks"] * log2e

    w_f32 = None if "w_rest" in w else w["w_in_f32"]
    xn, q, k, v, k2, v2, *w_rest = _qkv_proj(
        x, w["norm_mix_g"], w["w_qkv"], tables, tiles["kv_tail"], w_f32,
        tm=tiles["tm_qkv"], d_attn=d_attn, d_kv=d_kv, rows=tiles["qkv_rows"])
    if w_rest:
        w = dict(w, w_rest=w_rest[0])
    late = [name for name in LATE_WEIGHTS if w[name].dtype != BF16]
    z, u_tail, *late_bf16 = _conv_branch(
        xn, w["w_rest"], w["conv_w"], conv_prev,
        [(w[name], tiles["tf"] if name in FFN_IN_WEIGHTS else None) for name in late],
        tm=tiles["tm_conv"], tn=tiles["tn_conv"], rows=tiles["conv_rows"], seq_len=seq,
        tail_rows=tiles["conv_tail"], col_b=col_b, col_c=col_c, col_h=col_h)
    w = dict(w, **dict(zip(late, late_bf16)))
    if sample_cache is None:
        attn = _attention(q, k2, v2, sinks, batch=batch, seq=seq, n_kv=n_kv, group=group)
    else:
        attn = _attention_sample(q, k2, v2, sample_cache[0], sample_cache[1], sinks,
                                 batch=batch, t_new=seq, n_kv=n_kv, group=group)
    merged = _merge(xn, attn, z, w["w_rest"], w["w_attn_proj"], w["w_conv_proj"],
                    tm=tm, tn=tiles["tn_merge"], col_ga=col_ga, col_gc=col_gc)
    x1, xn_ffn = _out_proj(x, merged, w["w_out"], w["norm_ffn_g"],
                           tm=tiles["tm_out"], rows=tiles["out_rows"])
    y = _ffn(x1, xn_ffn, w["w_ffn_gate"], w["w_ffn_up"], w["w_ffn_down"],
             tm=tiles["tm_ffn"], tf=tiles["tf"], slab=V7X_MXU_DIM)
    return y, k, v, u_tail, w


def kernel(x_prompt, x_sample, cache_k, cache_v, state_conv, norm_mix_g, w_in, q_norm_g, k_norm_g,
           attn_sinks, conv_w, w_attn_proj, w_conv_proj, w_out, norm_ffn_g, w_ffn_gate, w_ffn_up,
           w_ffn_down):
    depth = w_in.shape[0]
    assert depth == 1
    b_p, s_p, d = x_prompt.shape
    b_s, s_s, _ = x_sample.shape
    n_kv = cache_k.shape[3]
    d_kv = n_kv * HEAD_DIM
    win = cache_k.shape[2]
    n_state = state_conv.shape[2]

    w = {
        "d_kv": d_kv,
        "norm_mix_g": norm_mix_g[0].reshape(1, d),
        "w_qkv": w_in[0][:, :attn_sinks.shape[1] * HEAD_DIM + 2 * d_kv].astype(BF16),
        "w_in_f32": w_in[0],
        "q_norm_g": q_norm_g[0],
        "k_norm_g": k_norm_g[0],
        "sinks": attn_sinks[0],
        "conv_w": conv_w[0],
        "w_attn_proj": w_attn_proj[0],
        "w_conv_proj": w_conv_proj[0],
        "w_out": w_out[0],
        "norm_ffn_g": norm_ffn_g[0].reshape(1, d),
        "w_ffn_gate": w_ffn_gate[0],
        "w_ffn_up": w_ffn_up[0],
        "w_ffn_down": w_ffn_down[0],
    }

    m_p = b_p * s_p
    tm_p = min(1024, s_p)
    keep = min(WINDOW, s_p)
    tiles_p = dict(tm=tm_p, tm_qkv=min(512, s_p), qkv_rows=min(512, s_p), kv_tail=(s_p, keep), tm_conv=s_p, tn_conv=512, conv_tail=8,
                   conv_rows=min(1024, s_p), tn_merge=512, tm_out=min(512, s_p),
                   out_rows=min(512, s_p),
                   tm_ffn=tm_p, tf=512)
    y_p, k_p, v_p, u_p, w = _layer(x_prompt.reshape(m_p, d), w, batch=b_p, seq=s_p,
                                   pos=jnp.arange(s_p), tiles=tiles_p, conv_prev=None,
                                   sample_cache=None)
    new_k_p = k_p.reshape(b_p, n_kv, HEAD_DIM, keep).transpose(0, 3, 1, 2)[None]
    new_v_p = v_p.reshape(b_p, n_kv, HEAD_DIM, keep).transpose(0, 3, 1, 2)[None]
    new_c_p = u_p.reshape(b_p, tiles_p["conv_tail"], -1)[:, tiles_p["conv_tail"] - n_state:][None]

    m_s = b_s * s_s
    st = state_conv[0]
    prev1 = jnp.pad(st[:, 1:2], ((0, 0), (0, s_s - 1), (0, 0))).reshape(m_s, -1)
    prev2 = jnp.pad(st, ((0, 0), (0, s_s - n_state), (0, 0))).reshape(m_s, -1)
    tiles_s = dict(tm=m_s, tm_qkv=m_s, qkv_rows=m_s, kv_tail=None, tm_conv=m_s, tn_conv=512, conv_tail=m_s,
                   conv_rows=m_s, tn_merge=512, tm_out=m_s, out_rows=m_s, tm_ffn=m_s, tf=512)
    y_s, k_s, v_s, u_s, _ = _layer(x_sample.reshape(m_s, d), w, batch=b_s, seq=s_s,
                                   pos=PAST_LEN + jnp.arange(s_s), tiles=tiles_s,
                                   conv_prev=(prev1, prev2),
                                   sample_cache=(cache_k[0].reshape(b_s * win, d_kv),
                                                 cache_v[0].reshape(b_s * win, d_kv)))
    new_k_s = k_s.reshape(b_s, s_s, n_kv, HEAD_DIM)[None]
    new_v_s = v_s.reshape(b_s, s_s, n_kv, HEAD_DIM)[None]
    new_c_s = u_s.reshape(b_s, s_s, -1)[:, s_s - n_state:][None]

    return (y_p.reshape(b_p, s_p, d), y_s.reshape(b_s, s_s, d), new_k_p, new_v_p, new_c_p,
            new_k_s, new_v_s, new_c_s)
```

```python
import functools
import math

import jax
import jax.numpy as jnp
from jax import lax
from jax.experimental import pallas as pl
from jax.experimental.pallas import tpu as pltpu

F32 = jnp.float32
BF16 = jnp.bfloat16

CHUNK = 64
WINDOW = 128
PAST_LEN = 1024
ROPE_THETA = 10000.0
EPS = 1e-6
NEG_INF = -1e30
HEAD_DIM = 64
LANES = 128
BF16_SUBLANES = 16
PAIR = 2 * CHUNK
KEY_SPAN = PAIR + WINDOW
PAIRS_PER_TRIP = 15

V7X_MXU_DIM = 256
MIB = 1024 * 1024
V7X_VMEM_BYTES = 64 * MIB
VMEM_COMPILER_RESERVE_BYTES = 6 * MIB
VMEM_CAP_BYTES = V7X_VMEM_BYTES - VMEM_COMPILER_RESERVE_BYTES
VMEM_FLOOR_BYTES = 16 * MIB


def _params(semantics, vmem_estimate):
    limit = int(min(max(vmem_estimate, VMEM_FLOOR_BYTES), VMEM_CAP_BYTES))
    return pltpu.CompilerParams(dimension_semantics=semantics, vmem_limit_bytes=limit)


def _nbytes(shape, dtype):
    return math.prod(shape) * jnp.dtype(dtype).itemsize


def _dot(a, b):
    return jnp.dot(a, b, preferred_element_type=F32)


def _dot_nt(a, b):
    return lax.dot_general(a, b, (((1,), (1,)), ((), ())), preferred_element_type=F32)


def _lane_iota(shape):
    return lax.broadcasted_iota(jnp.int32, shape, 1)


def _rmsnorm_rows(x, g):
    y = x * lax.rsqrt(jnp.mean(x * x, axis=-1, keepdims=True) + EPS)
    return y * g


def _headnorm_rope(y, t_same, t_swap, seg):
    width = y.shape[1]
    s = y * y
    s_hi = s.astype(BF16)
    s_lo = (s - s_hi.astype(F32)).astype(BF16)
    ms = _dot(s_hi, seg) + _dot(s_lo, seg)
    half = HEAD_DIM // 2
    ahead = pltpu.roll(y, width - half, axis=1)
    behind = pltpu.roll(y, half, axis=1)
    swapped = jnp.where((_lane_iota(y.shape) & (HEAD_DIM - 1)) < half, ahead, behind)
    reps = width // t_same.shape[1]
    t_same = jnp.concatenate([t_same] * reps, axis=1)
    t_swap = jnp.concatenate([t_swap] * reps, axis=1)
    return (y * t_same + swapped * t_swap) * lax.rsqrt(ms + EPS)


def _dup_halves(x):
    low = _lane_iota(x.shape) < HEAD_DIM
    r = pltpu.roll(x, HEAD_DIM, axis=1)
    return jnp.where(low, x, r), jnp.where(low, r, x)


def _isolate_halves(x):
    low = _lane_iota(x.shape) < HEAD_DIM
    r = pltpu.roll(x, HEAD_DIM, axis=1)
    return (jnp.where(low, x, 0.0), jnp.where(low, 0.0, r),
            jnp.where(low, r, 0.0), jnp.where(low, 0.0, x))


def _qkv_kernel(*refs, n_q_blocks, d_kv, rows, tail_keep, n_rest):
    x_ref, g_ref, w_ref, seg_ref, qs_ref, qw_ref, ks_ref, kw_ref = refs[:8]
    rest_refs = refs[8:8 + n_rest]
    xn_ref, q_ref, k_ref, v_ref, k2_ref, v2_ref = refs[8 + n_rest:][:6]
    rest_out = refs[-1] if n_rest else None
    tm = q_ref.shape[0]
    tn = 2 * d_kv
    chunks = [slice(r * rows, (r + 1) * rows) for r in range(tm // rows)]
    seg = seg_ref[...]
    slab = seg.shape[0]

    for rs in chunks:
        xn = _rmsnorm_rows(x_ref[rs, :], g_ref[...]).astype(BF16)
        xn_ref[rs, :] = xn
        for jb in range(n_q_blocks):
            y = _dot(xn, w_ref[:, jb * tn:(jb + 1) * tn])
            for c in range(tn // slab):
                sl = slice(c * slab, (c + 1) * slab)
                q_ref[rs, jb * tn + c * slab:jb * tn + (c + 1) * slab] = _headnorm_rope(
                    y[:, sl], qs_ref[rs, :], qw_ref[rs, :], seg).astype(BF16)

        y = _dot(xn, w_ref[:, n_q_blocks * tn:])
        k_all = jnp.concatenate(
            [_headnorm_rope(y[:, c * slab:(c + 1) * slab], ks_ref[rs, :], kw_ref[rs, :], seg)
             for c in range(d_kv // slab)], axis=1)
        if tail_keep is None:
            k_ref[rs, :] = k_all
            v_ref[rs, :] = y[:, d_kv:]
        elif rs is chunks[-1]:
            k_ref[0] = k_all[rows - tail_keep:].T
            v_ref[0] = y[rows - tail_keep:, d_kv:].T
        for c in range(d_kv // LANES):
            sl = slice(c * LANES, (c + 1) * LANES)
            k = k_all[:, sl]
            v = y[:, d_kv + c * LANES:d_kv + (c + 1) * LANES]
            va, vb = _dup_halves(v)
            lo = slice(2 * c * LANES, (2 * c + 1) * LANES)
            hi = slice((2 * c + 1) * LANES, (2 * c + 2) * LANES)
            v2_ref[rs, lo] = va.astype(BF16)
            v2_ref[rs, hi] = vb.astype(BF16)
            for n, piece in enumerate(_isolate_halves(k)):
                k2_ref[rs, (4 * c + n) * LANES:(4 * c + n + 1) * LANES] = piece.astype(BF16)

        for n, ref in list(enumerate(rest_refs))[chunks.index(rs)::len(chunks)]:
            width = ref.shape[1]
            rest_out[:, n * width:(n + 1) * width] = ref[...].astype(BF16)


def _qkv_proj(x, g, w_in, tables, tail, w_f32, *, tm, d_attn, d_kv, rows):
    m, d = x.shape
    tn = 2 * d_kv
    n_q_blocks = d_attn // tn
    qkv_cols = d_attn + tn
    assert m % tm == 0 and tm % rows == 0 and d_attn % tn == 0 and w_in.shape == (d, qkv_cols)
    n_tiles = m // tm
    rest_specs, rest_shape, rest_out_spec, n_rest = [], (), (), 0
    if w_f32 is not None:
        n_rest = (w_f32.shape[1] - qkv_cols) // qkv_cols
        assert w_f32.shape[1] == (n_rest + 1) * qkv_cols and d % (n_tiles * BF16_SUBLANES) == 0
        slab_rows = d // n_tiles
        rest_specs = [pl.BlockSpec((slab_rows, qkv_cols), functools.partial(
            lambda n, i: (i, n + 1), n)) for n in range(n_rest)]
        rest_shape = (jax.ShapeDtypeStruct((d, n_rest * qkv_cols), BF16),)
        rest_out_spec = (pl.BlockSpec((slab_rows, n_rest * qkv_cols), lambda i: (i, 0)),)
    if tail is None:
        kv_shape = jax.ShapeDtypeStruct((m, d_kv), F32)
        kv_spec = pl.BlockSpec((tm, d_kv), lambda i: (i, 0))
    else:
        seq, keep = tail
        assert seq % tm == 0 and keep <= rows and keep % LANES == 0
        kv_shape = jax.ShapeDtypeStruct((m // seq, d_kv, keep), F32)
        kv_spec = pl.BlockSpec((1, d_kv, keep), lambda i: (i // (seq // tm), 0, 0))
    rope_blocks = tables[0].shape[0] // tm
    tab = pl.BlockSpec((tm, LANES), lambda i: (i % rope_blocks, 0))
    row = lambda width: pl.BlockSpec((tm, width), lambda i: (i, 0))
    fixed = lambda shape: pl.BlockSpec(shape, lambda i: (0, 0), pipeline_mode=pl.Buffered(1))
    head_of = jnp.arange(V7X_MXU_DIM) // HEAD_DIM
    seg = jnp.where(head_of[:, None] == head_of[None, :], 1.0 / HEAD_DIM, 0.0).astype(BF16)
    est = (2 * (_nbytes((tm, d), F32) + _nbytes((tm, d), BF16) + _nbytes((tm, d_attn), BF16)
                + 3 * _nbytes((tm, tn), BF16) + 2 * _nbytes((tm, d_kv), F32)
                + 4 * _nbytes((tm, LANES), F32))
           + _nbytes(w_in.shape, BF16) + 6 * _nbytes((rows, tn), F32) + 2 * _nbytes((rows, d), F32))
    if n_rest:
        est += 3 * _nbytes((d // n_tiles, n_rest * qkv_cols), F32)
    return pl.pallas_call(
        functools.partial(_qkv_kernel, n_q_blocks=n_q_blocks, d_kv=d_kv, rows=rows,
                          tail_keep=None if tail is None else tail[1], n_rest=n_rest),
        out_shape=(jax.ShapeDtypeStruct((m, d), BF16), jax.ShapeDtypeStruct((m, d_attn), BF16),
                   kv_shape, kv_shape,
                   jax.ShapeDtypeStruct((m, 2 * tn), BF16), jax.ShapeDtypeStruct((m, tn), BF16),
                   *rest_shape),
        grid=(n_tiles,),
        in_specs=[row(d), fixed((1, d)), fixed(w_in.shape), fixed(seg.shape),
                  tab, tab, tab, tab] + rest_specs,
        out_specs=(row(d), row(d_attn), kv_spec, kv_spec, row(2 * tn), row(tn), *rest_out_spec),
        compiler_params=_params(("arbitrary",), est),
        name="qkv_proj",
    )(x, g, w_in, seg, *tables, *([w_f32] * n_rest))


def _conv_kernel(*refs, seq_len, tail_rows, has_state, rows, cast_per_step):
    n_cast = len(cast_per_step)
    n_in = 7 if has_state else 5
    xn_ref, wb_ref, wc_ref, wh_ref, cw_ref = refs[:5]
    p1_ref, p2_ref = refs[5:n_in] if has_state else (None, None)
    z_ref, u_ref = refs[n_in + n_cast:n_in + n_cast + 2]
    casts = list(zip(refs[n_in:n_in + n_cast], refs[n_in + n_cast + 2:]))
    tm, tn = z_ref.shape
    w = cw_ref[...]
    row = lax.broadcasted_iota(jnp.int32, (rows, tn), 0)
    last1 = last2 = None
    for r in range(tm // rows):
        rs = slice(r * rows, (r + 1) * rows)
        xn = xn_ref[rs, :]
        u = _dot(xn, wc_ref[...]) * _dot(xn, wh_ref[...])
        u1 = pltpu.roll(u, 1, axis=0)
        u2 = pltpu.roll(u, 2, axis=0)
        if r > 0:
            u1 = jnp.where(row == 0, last1, u1)
            u2 = jnp.where(row == 0, last2, jnp.where(row == 1, last1, u2))
        if (r * rows) % seq_len == 0 or seq_len < rows:
            pos = (row + r * rows) & (seq_len - 1)
            u1 = jnp.where(pos == 0, p1_ref[rs, :] if has_state else 0.0, u1)
            u2 = jnp.where(pos < 2, p2_ref[rs, :] if has_state else 0.0, u2)
        y = u2 * w[0:1] + u1 * w[1:2] + u * w[2:3]
        z_ref[rs, :] = (_dot(xn, wb_ref[...]) * y).astype(BF16)
        last1, last2 = u[rows - 1:rows], u[rows - 2:rows - 1]
        for (src_ref, dst_ref), per_step in list(zip(casts, cast_per_step))[r::tm // rows]:
            if per_step:
                _copy_slab(src_ref, dst_ref)
            else:
                pl.when(pl.program_id(1) == 0)(functools.partial(_copy_slab, src_ref, dst_ref))
    u_ref[...] = u[rows - tail_rows:]


def _copy_slab(src_ref, dst_ref):
    if len(dst_ref.shape) == 2:
        dst_ref[...] = src_ref[...].astype(BF16)
    else:
        cb = dst_ref.shape[2]
        for b in range(dst_ref.shape[0]):
            dst_ref[b] = src_ref[:, b * cb:(b + 1) * cb].astype(BF16)


def _conv_branch(xn, w_in, conv_w, prev, cast, *, tm, tn, rows, seq_len, tail_rows,
                 col_b, col_c, col_h):
    m, d = xn.shape
    n = conv_w.shape[1]
    has_state = prev is not None
    assert seq_len & (seq_len - 1) == 0 and tm % seq_len == 0
    assert tm % rows == 0 and tail_rows <= rows and (rows % seq_len == 0 or seq_len % rows == 0)
    n_j = n // tn
    steps = (m // tm) * n_j
    wspec = lambda off: pl.BlockSpec((d, tn), lambda i, j: (0, off // tn + j))
    in_specs = [pl.BlockSpec((tm, d), lambda i, j: (i, 0)),
                wspec(col_b), wspec(col_c), wspec(col_h),
                pl.BlockSpec((conv_w.shape[0], tn), lambda i, j: (0, j))]
    args = [xn, w_in, w_in, w_in, conv_w]
    if has_state:
        in_specs += [pl.BlockSpec((tm, tn), lambda i, j: (i, j))] * 2
        args += list(prev)
    cast_per_step = tuple(c.shape[0] % (steps * BF16_SUBLANES) == 0 for c, _ in cast)
    slab_specs, copy_specs, copy_shapes, slab_bytes = [], [], [], 0
    for (c, col_block), per_step in zip(cast, cast_per_step):
        parts = steps if per_step else m // tm
        r, cols = c.shape
        assert r % (parts * BF16_SUBLANES) == 0
        step_of = (lambda i, j: i * n_j + j) if per_step else (lambda i, j: i)
        slab_specs.append(pl.BlockSpec((r // parts, cols),
                                       functools.partial(lambda s, i, j: (s(i, j), 0), step_of)))
        if col_block is None:
            copy_specs.append(slab_specs[-1])
            copy_shapes.append(jax.ShapeDtypeStruct((r, cols), BF16))
        else:
            assert cols % col_block == 0
            copy_specs.append(pl.BlockSpec(
                (cols // col_block, r // parts, col_block),
                functools.partial(lambda s, i, j: (0, s(i, j), 0), step_of)))
            copy_shapes.append(jax.ShapeDtypeStruct((cols // col_block, r, col_block), BF16))
        slab_bytes += 3 * _nbytes(c.shape, F32) // parts
    est = (2 * (_nbytes((tm, d), BF16) + 3 * _nbytes((d, tn), BF16) + _nbytes((tm, tn), BF16)
                + _nbytes((tail_rows, tn), F32)) + 8 * _nbytes((tm, tn), F32) + slab_bytes)
    return pl.pallas_call(
        functools.partial(_conv_kernel, seq_len=seq_len, tail_rows=tail_rows, has_state=has_state,
                          rows=rows, cast_per_step=cast_per_step),
        out_shape=(jax.ShapeDtypeStruct((m, n), BF16),
                   jax.ShapeDtypeStruct((m // tm * tail_rows, n), F32),
                   *copy_shapes),
        grid=(m // tm, n_j),
        in_specs=in_specs + slab_specs,
        out_specs=(pl.BlockSpec((tm, tn), lambda i, j: (i, j)),
                   pl.BlockSpec((tail_rows, tn), lambda i, j: (i, j)),
                   *copy_specs),
        compiler_params=_params(("parallel", "arbitrary"), est),
        name="conv_branch",
    )(*args, *[c for c, _ in cast])


def _stack_heads(q_ref, rows, col0, group):
    parts = []
    for g in range(group):
        blk = q_ref[rows, col0 + (g // 2) * LANES:col0 + (g // 2 + 1) * LANES]
        keep = (_lane_iota(blk.shape) >= HEAD_DIM) == bool(g % 2)
        parts.append(jnp.where(keep, blk, jnp.zeros_like(blk)))
    return jnp.concatenate(parts, axis=0)


def _sink_column(sink_ref, head0, group, rows):
    return jnp.concatenate(
        [jnp.full((rows, 1), sink_ref[head0 + g], F32) for g in range(group)], axis=0)


def _store_heads(o_ref, rows, col0, o, group, nrows):
    for lp in range(group // 2):
        a = o[(2 * lp) * nrows:(2 * lp + 1) * nrows]
        b = o[(2 * lp + 1) * nrows:(2 * lp + 2) * nrows]
        low = _lane_iota(a.shape) < HEAD_DIM
        o_ref[rows, col0 + lp * LANES:col0 + (lp + 1) * LANES] = jnp.where(low, a, b).astype(BF16)


def _attn_pairs(pairs, sink_ref, q_ref, k_ref, v_ref, o_ref, n_kv, group):
    def block_rows(index):
        start = index * PAIR
        return pl.ds(start if isinstance(start, int) else pl.multiple_of(start, PAIR), PAIR)

    first_q = _lane_iota((CHUNK, PAIR)) < CHUNK

    def keys_of(ref, pair, has_prev, tile):
        cols = slice(tile * LANES, (tile + 1) * LANES)
        if has_prev:
            return jnp.concatenate([ref[block_rows(pair - 1), cols], ref[block_rows(pair), cols]],
                                   axis=0)
        return ref[block_rows(pair), cols]

    def scores(pair, has_prev, kv):
        col0 = kv * group * HEAD_DIM
        q_tiles = jnp.concatenate(
            [q_ref[block_rows(pair), col0 + lp * LANES:col0 + (lp + 1) * LANES]
             for lp in range(group // 2)], axis=0)
        return [_dot_nt(keys_of(k_ref, pair, has_prev, 2 * kv + half), q_tiles)
                for half in range(2)]

    items = [(pair, has_prev, kv) for pair, has_prev in pairs for kv in range(n_kv)]
    st_next = scores(*items[0])
    for n, (pair, has_prev, kv) in enumerate(items):
        st = st_next
        if n + 1 < len(items):
            st_next = scores(*items[n + 1])
        rows = block_rows(pair)
        vt = keys_of(v_ref, pair, has_prev, kv).astype(F32).T.astype(BF16)
        col0 = kv * group * HEAD_DIM
        probs, rden = [], []
        for g in range(group):
            sg = st[g % 2][:, (g // 2) * PAIR:(g // 2 + 1) * PAIR]
            blocks = [sg[c * CHUNK:(c + 1) * CHUNK] for c in range(sg.shape[0] // CHUNK)]
            if has_prev:
                s = jnp.concatenate([jnp.where(first_q, blocks[0], blocks[3]),
                                     blocks[1], blocks[2]], axis=0)
            else:
                s = jnp.concatenate([blocks[0], jnp.where(first_q, NEG_INF, blocks[1])], axis=0)
            sink = sink_ref[kv * group + g]
            m = jnp.maximum(jnp.max(s, axis=0, keepdims=True), sink)
            e = jnp.exp2(s - m)
            den = jnp.sum(e, axis=0, keepdims=True) + jnp.exp2(sink - m)
            if has_prev:
                e_03, e_12 = e[:CHUNK], e[CHUNK:]
                e = jnp.concatenate([jnp.where(first_q, e_03, 0.0), e_12,
                                     jnp.where(first_q, 0.0, e_03)], axis=0)
            probs.append(e.astype(BF16))
            rden.append(1.0 / den)
        ot = _dot(vt, jnp.concatenate(probs, axis=1)) * jnp.concatenate(rden, axis=1)
        for lp in range(group // 2):
            a = ot[:HEAD_DIM, (2 * lp) * PAIR:(2 * lp + 1) * PAIR]
            b = ot[HEAD_DIM:, (2 * lp + 1) * PAIR:(2 * lp + 2) * PAIR]
            o_ref[rows, col0 + lp * LANES:col0 + (lp + 1) * LANES] = (
                jnp.concatenate([a, b], axis=0).T.astype(BF16))


def _attn_kernel(sink_ref, q_ref, k_ref, v_ref, o_ref, *, n_kv, group):
    def run(pairs):
        _attn_pairs(pairs, sink_ref, q_ref, k_ref, v_ref, o_ref, n_kv, group)

    def trip(it, carry):
        run([(1 + PAIRS_PER_TRIP * it + u, True) for u in range(PAIRS_PER_TRIP)])
        return carry

    n_pairs = q_ref.shape[0] // PAIR
    trips = (n_pairs - 1) // PAIRS_PER_TRIP
    run([(0, False)])
    lax.fori_loop(0, trips, trip, 0)
    leftover = range(1 + trips * PAIRS_PER_TRIP, n_pairs)
    if leftover:
        run([(pair, True) for pair in leftover])


def _attention(q, k2, v2, sinks, *, batch, seq, n_kv, group):
    m, d = q.shape
    assert WINDOW == PAIR and seq % PAIR == 0
    est = (2 * (2 * _nbytes((seq, d), BF16) + 3 * _nbytes((seq, n_kv * LANES), BF16))
           + 8 * _nbytes((group * PAIR, KEY_SPAN), F32))
    return pl.pallas_call(
        functools.partial(_attn_kernel, n_kv=n_kv, group=group),
        out_shape=jax.ShapeDtypeStruct((m, d), BF16),
        grid=(batch,),
        in_specs=[pl.BlockSpec(memory_space=pltpu.SMEM),
                  pl.BlockSpec((seq, d), lambda b: (b, 0)),
                  pl.BlockSpec((seq, 2 * n_kv * LANES), lambda b: (b, 0)),
                  pl.BlockSpec((seq, n_kv * LANES), lambda b: (b, 0))],
        out_specs=pl.BlockSpec((seq, d), lambda b: (b, 0)),
        compiler_params=_params(("parallel",), est),
        name="attention",
    )(sinks, q, k2, v2)


def _attn_sample_kernel(sink_ref, q_ref, kn_ref, vn_ref, kc_ref, vc_ref, o_ref, *, n_kv, group):
    t = q_ref.shape[0]
    for kv in range(n_kv):
        pair_sl = slice((kv // 2) * LANES, (kv // 2 + 1) * LANES)
        kc = _dup_halves(kc_ref[:, pair_sl])[kv % 2].astype(BF16)
        vc = _dup_halves(vc_ref[:, pair_sl])[kv % 2].astype(BF16)
        kn = (kn_ref[:, 2 * kv * LANES:(2 * kv + 1) * LANES]
              + kn_ref[:, (2 * kv + 1) * LANES:(2 * kv + 2) * LANES])
        vn = vn_ref[:, kv * LANES:(kv + 1) * LANES]
        col0 = kv * group * HEAD_DIM
        lhs = _stack_heads(q_ref, slice(0, t), col0, group)
        s_c = _dot_nt(lhs, kc)
        s_n = _dot_nt(lhs, kn)
        sink = _sink_column(sink_ref, kv * group, group, t)
        m = jnp.maximum(jnp.maximum(jnp.max(s_c, axis=1, keepdims=True),
                                    jnp.max(s_n, axis=1, keepdims=True)), sink)
        e_c = jnp.exp2(s_c - m)
        e_n = jnp.exp2(s_n - m)
        den = (jnp.sum(e_c, axis=1, keepdims=True) + jnp.sum(e_n, axis=1, keepdims=True)
               + jnp.exp2(sink - m))
        o = (_dot(e_c.astype(BF16), vc) + _dot(e_n.astype(BF16), vn)) / den
        _store_heads(o_ref, slice(0, t), col0, o, group, t)


def _attention_sample(q, k2, v2, cache_k, cache_v, sinks, *, batch, t_new, n_kv, group):
    m, d = q.shape
    win = cache_k.shape[0] // batch
    d_kv = cache_k.shape[1]
    est = (2 * (2 * _nbytes((t_new, d), BF16) + 3 * _nbytes((t_new, n_kv * LANES), BF16)
                + 2 * _nbytes((win, d_kv), F32)) + 8 * _nbytes((group * t_new, win), F32))
    return pl.pallas_call(
        functools.partial(_attn_sample_kernel, n_kv=n_kv, group=group),
        out_shape=jax.ShapeDtypeStruct((m, d), BF16),
        grid=(batch,),
        in_specs=[pl.BlockSpec(memory_space=pltpu.SMEM),
                  pl.BlockSpec((t_new, d), lambda b: (b, 0)),
                  pl.BlockSpec((t_new, 2 * n_kv * LANES), lambda b: (b, 0)),
                  pl.BlockSpec((t_new, n_kv * LANES), lambda b: (b, 0)),
                  pl.BlockSpec((win, d_kv), lambda b: (b, 0)),
                  pl.BlockSpec((win, d_kv), lambda b: (b, 0))],
        out_specs=pl.BlockSpec((t_new, d), lambda b: (b, 0)),
        compiler_params=_params(("parallel",), est),
        name="attention_sample",
    )(sinks, q, k2, v2, cache_k, cache_v)


def _merge_kernel(xn_ref, a_ref, z_ref, wga_ref, wgc_ref, wap_ref, wcp_ref, o_ref):
    xn = xn_ref[...]
    attn = jax.nn.sigmoid(_dot(xn, wga_ref[...])) * _dot(a_ref[...], wap_ref[...])
    conv = jax.nn.sigmoid(_dot(xn, wgc_ref[...])) * _dot(z_ref[...], wcp_ref[...])
    o_ref[...] = (attn + conv).astype(BF16)


def _merge(xn, attn, z, w_in, w_ap, w_cp, *, tm, tn, col_ga, col_gc):
    m, d = xn.shape
    act = pl.BlockSpec((tm, d), lambda i, j: (i, 0))
    wspec = lambda off: pl.BlockSpec((d, tn), lambda i, j: (0, off // tn + j))
    est = (2 * (3 * _nbytes((tm, d), BF16) + 4 * _nbytes((d, tn), BF16) + _nbytes((tm, tn), BF16))
           + 6 * _nbytes((tm, tn), F32))
    return pl.pallas_call(
        _merge_kernel,
        out_shape=jax.ShapeDtypeStruct((m, d), BF16),
        grid=(m // tm, d // tn),
        in_specs=[act, act, act, wspec(col_ga), wspec(col_gc), wspec(0), wspec(0)],
        out_specs=pl.BlockSpec((tm, tn), lambda i, j: (i, j)),
        compiler_params=_params(("parallel", "arbitrary"), est),
        name="merge",
    )(xn, attn, z, w_in, w_in, w_ap, w_cp)


def _out_kernel(x_ref, m_ref, w_ref, g_ref, x1_ref, xn_ref, *, rows):
    for r in range(x_ref.shape[0] // rows):
        rs = slice(r * rows, (r + 1) * rows)
        x1 = x_ref[rs, :] + _dot(m_ref[rs, :], w_ref[...])
        x1_ref[rs, :] = x1
        xn_ref[rs, :] = _rmsnorm_rows(x1, g_ref[...]).astype(BF16)


def _out_proj(x, merged, w_out, g_ffn, *, tm, rows):
    m, d = x.shape
    assert m % tm == 0 and tm % rows == 0
    est = (2 * (2 * _nbytes((tm, d), F32) + 2 * _nbytes((tm, d), BF16)) + _nbytes((d, d), BF16)
           + 4 * _nbytes((rows, d), F32))
    return pl.pallas_call(
        functools.partial(_out_kernel, rows=rows),
        out_shape=(jax.ShapeDtypeStruct((m, d), F32), jax.ShapeDtypeStruct((m, d), BF16)),
        grid=(m // tm,),
        in_specs=[pl.BlockSpec((tm, d), lambda i: (i, 0)),
                  pl.BlockSpec((tm, d), lambda i: (i, 0)),
                  pl.BlockSpec((d, d), lambda i: (0, 0), pipeline_mode=pl.Buffered(1)),
                  pl.BlockSpec((1, d), lambda i: (0, 0))],
        out_specs=(pl.BlockSpec((tm, d), lambda i: (i, 0)),
                   pl.BlockSpec((tm, d), lambda i: (i, 0))),
        compiler_params=_params(("parallel",), est),
        name="out_proj",
    )(x, merged, w_out, g_ffn)


def _ffn_kernel(x_ref, xn_ref, wg_ref, wu_ref, wd_ref, o_ref):
    f = pl.program_id(1)
    slab = x_ref.shape[1]
    n_slabs = o_ref.shape[1] // slab

    @pl.when(f == 0)
    def _():
        o_ref[...] = jnp.zeros_like(o_ref)

    xn = xn_ref[...]
    h = jax.nn.silu(_dot(xn, wg_ref[...])) * _dot(xn, wu_ref[...])
    o_ref[...] += _dot(h.astype(BF16), wd_ref[...])

    for s in range(n_slabs):
        @pl.when(f == s)
        def _(s=s):
            o_ref[:, s * slab:(s + 1) * slab] += x_ref[...]


def _ffn(x, xn, w_gate, w_up, w_down, *, tm, tf, slab):
    m, d = x.shape
    d_ff = w_down.shape[0]
    n_slabs = d // slab
    assert n_slabs <= d_ff // tf and w_gate.shape == w_up.shape == (d_ff // tf, d, tf)
    est = (2 * (_nbytes((tm, slab), F32) + _nbytes((tm, d), F32) + _nbytes((tm, d), BF16)
                + 3 * _nbytes((d, tf), BF16))
           + 4 * _nbytes((tm, tf), F32) + _nbytes((tm, d), F32))
    row = lambda i, f: (i, 0)
    return pl.pallas_call(
        _ffn_kernel,
        out_shape=jax.ShapeDtypeStruct((m, d), F32),
        grid=(m // tm, d_ff // tf),
        in_specs=[pl.BlockSpec((tm, slab), lambda i, f: (i, jnp.minimum(f, n_slabs - 1))),
                  pl.BlockSpec((tm, d), row),
                  pl.BlockSpec((None, d, tf), lambda i, f: (f, 0, 0)),
                  pl.BlockSpec((None, d, tf), lambda i, f: (f, 0, 0)),
                  pl.BlockSpec((tf, d), lambda i, f: (f, 0))],
        out_specs=pl.BlockSpec((tm, d), row),
        compiler_params=_params(("parallel", "arbitrary"), est),
        name="ffn",
    )(x, xn, w_gate, w_up, w_down)


def _rope_tables(pos, repeat, g, scale):
    half = HEAD_DIM // 2
    inv = jnp.exp(-math.log(ROPE_THETA) * jnp.arange(half, dtype=F32) * (2.0 / HEAD_DIM))
    ang = pos.astype(F32)[:, None] * inv[None, :]
    cos = jnp.cos(ang)
    sin = jnp.sin(ang)
    g_swap = jnp.concatenate([g[half:], g[:half]])
    t_same = g[None, :] * jnp.concatenate([cos, cos], axis=-1) * scale
    t_swap = g_swap[None, :] * jnp.concatenate([-sin, sin], axis=-1) * scale
    reps = (repeat, LANES // HEAD_DIM)
    return jnp.tile(t_same, reps), jnp.tile(t_swap, reps)


LATE_WEIGHTS = ("w_attn_proj", "w_conv_proj", "w_out", "w_ffn_gate", "w_ffn_up", "w_ffn_down")
FFN_IN_WEIGHTS = ("w_ffn_gate", "w_ffn_up")


def _layer(x, w, *, batch, seq, pos, tiles, conv_prev, sample_cache):
    d = x.shape[1]
    n_heads = w["sinks"].shape[0]
    d_attn = n_heads * HEAD_DIM
    d_kv = w["d_kv"]
    n_kv = d_kv // HEAD_DIM
    group = n_heads // n_kv
    d_conv = w["conv_w"].shape[1]
    col_b = 0
    col_c = col_b + d_conv
    col_h = col_c + d_conv
    col_ga = col_h + d_conv
    col_gc = col_ga + d
    tm = tiles["tm"]

    rope_rows = max(seq, tiles["tm_qkv"])
    log2e = math.log2(math.e)
    tables = (_rope_tables(pos, rope_rows // seq, w["q_norm_g"], HEAD_DIM ** -0.5 * log2e)
              + _rope_tables(pos, rope_rows // seq, w["k_norm_g"], 1.0))
    sinks = w["sinks"] * log2e

    w_f32 = None if "w_rest" in w else w["w_in_f32"]
    xn, q, k, v, k2, v2, *w_rest = _qkv_proj(
        x, w["norm_mix_g"], w["w_qkv"], tables, tiles["kv_tail"], w_f32,
        tm=tiles["tm_qkv"], d_attn=d_attn, d_kv=d_kv, rows=tiles["qkv_rows"])
    if w_rest:
        w = dict(w, w_rest=w_rest[0])
    late = [name for name in LATE_WEIGHTS if w[name].dtype != BF16]
    z, u_tail, *late_bf16 = _conv_branch(
        xn, w["w_rest"], w["conv_w"], conv_prev,
        [(w[name], tiles["tf"] if name in FFN_IN_WEIGHTS else None) for name in late],
        tm=tiles["tm_conv"], tn=tiles["tn_conv"], rows=tiles["conv_rows"], seq_len=seq,
        tail_rows=tiles["conv_tail"], col_b=col_b, col_c=col_c, col_h=col_h)
    w = dict(w, **dict(zip(late, late_bf16)))
    if sample_cache is None:
        attn = _attention(q, k2, v2, sinks, batch=batch, seq=seq, n_kv=n_kv, group=group)
    else:
        attn = _attention_sample(q, k2, v2, sample_cache[0], sample_cache[1], sinks,
                                 batch=batch, t_new=seq, n_kv=n_kv, group=group)
    merged = _merge(xn, attn, z, w["w_rest"], w["w_attn_proj"], w["w_conv_proj"],
                    tm=tm, tn=tiles["tn_merge"], col_ga=col_ga, col_gc=col_gc)
    x1, xn_ffn = _out_proj(x, merged, w["w_out"], w["norm_ffn_g"],
                           tm=tiles["tm_out"], rows=tiles["out_rows"])
    y = _ffn(x1, xn_ffn, w["w_ffn_gate"], w["w_ffn_up"], w["w_ffn_down"],
             tm=tiles["tm_ffn"], tf=tiles["tf"], slab=V7X_MXU_DIM)
    return y, k, v, u_tail, w


def kernel(x_prompt, x_sample, cache_k, cache_v, state_conv, norm_mix_g, w_in, q_norm_g, k_norm_g,
           attn_sinks, conv_w, w_attn_proj, w_conv_proj, w_out, norm_ffn_g, w_ffn_gate, w_ffn_up,
           w_ffn_down):
    depth = w_in.shape[0]
    assert depth == 1
    b_p, s_p, d = x_prompt.shape
    b_s, s_s, _ = x_sample.shape
    n_kv = cache_k.shape[3]
    d_kv = n_kv * HEAD_DIM
    win = cache_k.shape[2]
    n_state = state_conv.shape[2]

    w = {
        "d_kv": d_kv,
        "norm_mix_g": norm_mix_g[0].reshape(1, d),
        "w_qkv": w_in[0][:, :attn_sinks.shape[1] * HEAD_DIM + 2 * d_kv].astype(BF16),
        "w_in_f32": w_in[0],
        "q_norm_g": q_norm_g[0],
        "k_norm_g": k_norm_g[0],
        "sinks": attn_sinks[0],
        "conv_w": conv_w[0],
        "w_attn_proj": w_attn_proj[0],
        "w_conv_proj": w_conv_proj[0],
        "w_out": w_out[0],
        "norm_ffn_g": norm_ffn_g[0].reshape(1, d),
        "w_ffn_gate": w_ffn_gate[0],
        "w_ffn_up": w_ffn_up[0],
        "w_ffn_down": w_ffn_down[0],
    }

    m_p = b_p * s_p
    tm_p = min(1024, s_p)
    keep = min(WINDOW, s_p)
    tiles_p = dict(tm=tm_p, tm_qkv=min(512, s_p), qkv_rows=min(512, s_p), kv_tail=(s_p, keep), tm_conv=s_p, tn_conv=512, conv_tail=8,
                   conv_rows=min(1024, s_p), tn_merge=512, tm_out=min(512, s_p),
                   out_rows=min(512, s_p),
                   tm_ffn=tm_p, tf=512)
    y_p, k_p, v_p, u_p, w = _layer(x_prompt.reshape(m_p, d), w, batch=b_p, seq=s_p,
                                   pos=jnp.arange(s_p), tiles=tiles_p, conv_prev=None,
                                   sample_cache=None)
    new_k_p = k_p.reshape(b_p, n_kv, HEAD_DIM, keep).transpose(0, 3, 1, 2)[None]
    new_v_p = v_p.reshape(b_p, n_kv, HEAD_DIM, keep).transpose(0, 3, 1, 2)[None]
    new_c_p = u_p.reshape(b_p, tiles_p["conv_tail"], -1)[:, tiles_p["conv_tail"] - n_state:][None]

    m_s = b_s * s_s
    st = state_conv[0]
    prev1 = jnp.pad(st[:, 1:2], ((0, 0), (0, s_s - 1), (0, 0))).reshape(m_s, -1)
    prev2 = jnp.pad(st, ((0, 0), (0, s_s - n_state), (0, 0))).reshape(m_s, -1)
    tiles_s = dict(tm=m_s, tm_qkv=m_s, qkv_rows=m_s, kv_tail=None, tm_conv=m_s, tn_conv=512, conv_tail=m_s,
                   conv_rows=m_s, tn_merge=512, tm_out=m_s, out_rows=m_s, tm_ffn=m_s, tf=512)
    y_s, k_s, v_s, u_s, _ = _layer(x_sample.reshape(m_s, d), w, batch=b_s, seq=s_s,
                                   pos=PAST_LEN + jnp.arange(s_s), tiles=tiles_s,
                                   conv_prev=(prev1, prev2),
                                   sample_cache=(cache_k[0].reshape(b_s * win, d_kv),
                                                 cache_v[0].reshape(b_s * win, d_kv)))
    new_k_s = k_s.reshape(b_s, s_s, n_kv, HEAD_DIM)[None]
    new_v_s = v_s.reshape(b_s, s_s, n_kv, HEAD_DIM)[None]
    new_c_s = u_s.reshape(b_s, s_s, -1)[:, s_s - n_state:][None]

    return (y_p.reshape(b_p, s_p, d), y_s.reshape(b_s, s_s, d), new_k_p, new_v_p, new_c_p,
            new_k_s, new_v_s, new_c_s)
```

```python
import functools
import math

import jax
import jax.numpy as jnp
from jax import lax
from jax.experimental import pallas as pl
from jax.experimental.pallas import tpu as pltpu

F32 = jnp.float32
BF16 = jnp.bfloat16

CHUNK = 64
WINDOW = 128
PAST_LEN = 1024
ROPE_THETA = 10000.0
EPS = 1e-6
NEG_INF = -1e30
HEAD_DIM = 64
LANES = 128
BF16_SUBLANES = 16
PAIR = 2 * CHUNK
KEY_SPAN = PAIR + WINDOW
PAIRS_PER_TRIP = 15

V7X_MXU_DIM = 256
MIB = 1024 * 1024
V7X_VMEM_BYTES = 64 * MIB
VMEM_COMPILER_RESERVE_BYTES = 6 * MIB
VMEM_CAP_BYTES = V7X_VMEM_BYTES - VMEM_COMPILER_RESERVE_BYTES
VMEM_FLOOR_BYTES = 16 * MIB


def _params(semantics, vmem_estimate):
    limit = int(min(max(vmem_estimate, VMEM_FLOOR_BYTES), VMEM_CAP_BYTES))
    return pltpu.CompilerParams(dimension_semantics=semantics, vmem_limit_bytes=limit)


def _nbytes(shape, dtype):
    return math.prod(shape) * jnp.dtype(dtype).itemsize


def _dot(a, b):
    return jnp.dot(a, b, preferred_element_type=F32)


def _dot_nt(a, b):
    return lax.dot_general(a, b, (((1,), (1,)), ((), ())), preferred_element_type=F32)


def _lane_iota(shape):
    return lax.broadcasted_iota(jnp.int32, shape, 1)


def _rmsnorm_rows(x, g):
    y = x * lax.rsqrt(jnp.mean(x * x, axis=-1, keepdims=True) + EPS)
    return y * g


def _headnorm_rope(y, t_same, t_swap, seg):
    width = y.shape[1]
    s = y * y
    s_hi = s.astype(BF16)
    s_lo = (s - s_hi.astype(F32)).astype(BF16)
    ms = _dot(s_hi, seg) + _dot(s_lo, seg)
    half = HEAD_DIM // 2
    ahead = pltpu.roll(y, width - half, axis=1)
    behind = pltpu.roll(y, half, axis=1)
    swapped = jnp.where((_lane_iota(y.shape) & (HEAD_DIM - 1)) < half, ahead, behind)
    reps = width // t_same.shape[1]
    t_same = jnp.concatenate([t_same] * reps, axis=1)
    t_swap = jnp.concatenate([t_swap] * reps, axis=1)
    return (y * t_same + swapped * t_swap) * lax.rsqrt(ms + EPS)


def _dup_halves(x):
    low = _lane_iota(x.shape) < HEAD_DIM
    r = pltpu.roll(x, HEAD_DIM, axis=1)
    return jnp.where(low, x, r), jnp.where(low, r, x)


def _isolate_halves(x):
    low = _lane_iota(x.shape) < HEAD_DIM
    r = pltpu.roll(x, HEAD_DIM, axis=1)
    return (jnp.where(low, x, 0.0), jnp.where(low, 0.0, r),
            jnp.where(low, r, 0.0), jnp.where(low, 0.0, x))


def _qkv_kernel(*refs, n_q_blocks, d_kv, rows, tail_keep, n_rest):
    x_ref, g_ref, w_ref, seg_ref, qs_ref, qw_ref, ks_ref, kw_ref = refs[:8]
    rest_refs = refs[8:8 + n_rest]
    xn_ref, q_ref, k_ref, v_ref, k2_ref, v2_ref = refs[8 + n_rest:][:6]
    rest_out = refs[-1] if n_rest else None
    tm = q_ref.shape[0]
    tn = 2 * d_kv
    chunks = [slice(r * rows, (r + 1) * rows) for r in range(tm // rows)]
    seg = seg_ref[...]
    slab = seg.shape[0]

    for rs in chunks:
        xn = _rmsnorm_rows(x_ref[rs, :], g_ref[...]).astype(BF16)
        xn_ref[rs, :] = xn
        for jb in range(n_q_blocks):
            y = _dot(xn, w_ref[:, jb * tn:(jb + 1) * tn])
            for c in range(tn // slab):
                sl = slice(c * slab, (c + 1) * slab)
                q_ref[rs, jb * tn + c * slab:jb * tn + (c + 1) * slab] = _headnorm_rope(
                    y[:, sl], qs_ref[rs, :], qw_ref[rs, :], seg).astype(BF16)

        y = _dot(xn, w_ref[:, n_q_blocks * tn:])
        k_all = jnp.concatenate(
            [_headnorm_rope(y[:, c * slab:(c + 1) * slab], ks_ref[rs, :], kw_ref[rs, :], seg)
             for c in range(d_kv // slab)], axis=1)
        if tail_keep is None:
            k_ref[rs, :] = k_all
            v_ref[rs, :] = y[:, d_kv:]
        elif rs is chunks[-1]:
            k_ref[0] = k_all[rows - tail_keep:].T
            v_ref[0] = y[rows - tail_keep:, d_kv:].T
        for c in range(d_kv // LANES):
            sl = slice(c * LANES, (c + 1) * LANES)
            k = k_all[:, sl]
            v = y[:, d_kv + c * LANES:d_kv + (c + 1) * LANES]
            va, vb = _dup_halves(v)
            lo = slice(2 * c * LANES, (2 * c + 1) * LANES)
            hi = slice((2 * c + 1) * LANES, (2 * c + 2) * LANES)
            v2_ref[rs, lo] = va.astype(BF16)
            v2_ref[rs, hi] = vb.astype(BF16)
            for n, piece in enumerate(_isolate_halves(k)):
                k2_ref[rs, (4 * c + n) * LANES:(4 * c + n + 1) * LANES] = piece.astype(BF16)

        for n, ref in list(enumerate(rest_refs))[chunks.index(rs)::len(chunks)]:
            width = ref.shape[1]
            rest_out[:, n * width:(n + 1) * width] = ref[...].astype(BF16)


def _qkv_proj(x, g, w_in, tables, tail, w_f32, *, tm, d_attn, d_kv, rows):
    m, d = x.shape
    tn = 2 * d_kv
    n_q_blocks = d_attn // tn
    qkv_cols = d_attn + tn
    assert m % tm == 0 and tm % rows == 0 and d_attn % tn == 0 and w_in.shape == (d, qkv_cols)
    n_tiles = m // tm
    rest_specs, rest_shape, rest_out_spec, n_rest = [], (), (), 0
    if w_f32 is not None:
        n_rest = (w_f32.shape[1] - qkv_cols) // qkv_cols
        assert w_f32.shape[1] == (n_rest + 1) * qkv_cols and d % (n_tiles * BF16_SUBLANES) == 0
        slab_rows = d // n_tiles
        rest_specs = [pl.BlockSpec((slab_rows, qkv_cols), functools.partial(
            lambda n, i: (i, n + 1), n)) for n in range(n_rest)]
        rest_shape = (jax.ShapeDtypeStruct((d, n_rest * qkv_cols), BF16),)
        rest_out_spec = (pl.BlockSpec((slab_rows, n_rest * qkv_cols), lambda i: (i, 0)),)
    if tail is None:
        kv_shape = jax.ShapeDtypeStruct((m, d_kv), F32)
        kv_spec = pl.BlockSpec((tm, d_kv), lambda i: (i, 0))
    else:
        seq, keep = tail
        assert seq % tm == 0 and keep <= rows and keep % LANES == 0
        kv_shape = jax.ShapeDtypeStruct((m // seq, d_kv, keep), F32)
        kv_spec = pl.BlockSpec((1, d_kv, keep), lambda i: (i // (seq // tm), 0, 0))
    rope_blocks = tables[0].shape[0] // tm
    tab = pl.BlockSpec((tm, LANES), lambda i: (i % rope_blocks, 0))
    row = lambda width: pl.BlockSpec((tm, width), lambda i: (i, 0))
    fixed = lambda shape: pl.BlockSpec(shape, lambda i: (0, 0), pipeline_mode=pl.Buffered(1))
    head_of = jnp.arange(V7X_MXU_DIM) // HEAD_DIM
    seg = jnp.where(head_of[:, None] == head_of[None, :], 1.0 / HEAD_DIM, 0.0).astype(BF16)
    est = (2 * (_nbytes((tm, d), F32) + _nbytes((tm, d), BF16) + _nbytes((tm, d_attn), BF16)
                + 3 * _nbytes((tm, tn), BF16) + 2 * _nbytes((tm, d_kv), F32)
                + 4 * _nbytes((tm, LANES), F32))
           + _nbytes(w_in.shape, BF16) + 6 * _nbytes((rows, tn), F32) + 2 * _nbytes((rows, d), F32))
    if n_rest:
        est += 3 * _nbytes((d // n_tiles, n_rest * qkv_cols), F32)
    return pl.pallas_call(
        functools.partial(_qkv_kernel, n_q_blocks=n_q_blocks, d_kv=d_kv, rows=rows,
                          tail_keep=None if tail is None else tail[1], n_rest=n_rest),
        out_shape=(jax.ShapeDtypeStruct((m, d), BF16), jax.ShapeDtypeStruct((m, d_attn), BF16),
                   kv_shape, kv_shape,
                   jax.ShapeDtypeStruct((m, 2 * tn), BF16), jax.ShapeDtypeStruct((m, tn), BF16),
                   *rest_shape),
        grid=(n_tiles,),
        in_specs=[row(d), fixed((1, d)), fixed(w_in.shape), fixed(seg.shape),
                  tab, tab, tab, tab] + rest_specs,
        out_specs=(row(d), row(d_attn), kv_spec, kv_spec, row(2 * tn), row(tn), *rest_out_spec),
        compiler_params=_params(("arbitrary",), est),
        name="qkv_proj",
    )(x, g, w_in, seg, *tables, *([w_f32] * n_rest))


def _conv_kernel(*refs, seq_len, tail_rows, has_state, rows, cast_per_step):
    n_cast = len(cast_per_step)
    n_in = 7 if has_state else 5
    xn_ref, wb_ref, wc_ref, wh_ref, cw_ref = refs[:5]
    p1_ref, p2_ref = refs[5:n_in] if has_state else (None, None)
    z_ref, u_ref = refs[n_in + n_cast:n_in + n_cast + 2]
    casts = list(zip(refs[n_in:n_in + n_cast], refs[n_in + n_cast + 2:]))
    tm, tn = z_ref.shape
    w = cw_ref[...]
    row = lax.broadcasted_iota(jnp.int32, (rows, tn), 0)
    last1 = last2 = None
    for r in range(tm // rows):
        rs = slice(r * rows, (r + 1) * rows)
        xn = xn_ref[rs, :]
        u = _dot(xn, wc_ref[...]) * _dot(xn, wh_ref[...])
        u1 = pltpu.roll(u, 1, axis=0)
        u2 = pltpu.roll(u, 2, axis=0)
        if r > 0:
            u1 = jnp.where(row == 0, last1, u1)
            u2 = jnp.where(row == 0, last2, jnp.where(row == 1, last1, u2))
        if (r * rows) % seq_len == 0 or seq_len < rows:
            pos = (row + r * rows) & (seq_len - 1)
            u1 = jnp.where(pos == 0, p1_ref[rs, :] if has_state else 0.0, u1)
            u2 = jnp.where(pos < 2, p2_ref[rs, :] if has_state else 0.0, u2)
        y = u2 * w[0:1] + u1 * w[1:2] + u * w[2:3]
        z_ref[rs, :] = (_dot(xn, wb_ref[...]) * y).astype(BF16)
        last1, last2 = u[rows - 1:rows], u[rows - 2:rows - 1]
        for (src_ref, dst_ref), per_step in list(zip(casts, cast_per_step))[r::tm // rows]:
            if per_step:
                _copy_slab(src_ref, dst_ref)
            else:
                pl.when(pl.program_id(1) == 0)(functools.partial(_copy_slab, src_ref, dst_ref))
    u_ref[...] = u[rows - tail_rows:]


def _copy_slab(src_ref, dst_ref):
    if len(dst_ref.shape) == 2:
        dst_ref[...] = src_ref[...].astype(BF16)
    else:
        cb = dst_ref.shape[2]
        for b in range(dst_ref.shape[0]):
            dst_ref[b] = src_ref[:, b * cb:(b + 1) * cb].astype(BF16)


def _conv_branch(xn, w_in, conv_w, prev, cast, *, tm, tn, rows, seq_len, tail_rows,
                 col_b, col_c, col_h):
    m, d = xn.shape
    n = conv_w.shape[1]
    has_state = prev is not None
    assert seq_len & (seq_len - 1) == 0 and tm % seq_len == 0
    assert tm % rows == 0 and tail_rows <= rows and (rows % seq_len == 0 or seq_len % rows == 0)
    n_j = n // tn
    steps = (m // tm) * n_j
    wspec = lambda off: pl.BlockSpec((d, tn), lambda i, j: (0, off // tn + j))
    in_specs = [pl.BlockSpec((tm, d), lambda i, j: (i, 0)),
                wspec(col_b), wspec(col_c), wspec(col_h),
                pl.BlockSpec((conv_w.shape[0], tn), lambda i, j: (0, j))]
    args = [xn, w_in, w_in, w_in, conv_w]
    if has_state:
        in_specs += [pl.BlockSpec((tm, tn), lambda i, j: (i, j))] * 2
        args += list(prev)
    cast_per_step = tuple(c.shape[0] % (steps * BF16_SUBLANES) == 0 for c, _ in cast)
    slab_specs, copy_specs, copy_shapes, slab_bytes = [], [], [], 0
    for (c, col_block), per_step in zip(cast, cast_per_step):
        parts = steps if per_step else m // tm
        r, cols = c.shape
        assert r % (parts * BF16_SUBLANES) == 0
        step_of = (lambda i, j: i * n_j + j) if per_step else (lambda i, j: i)
        slab_specs.append(pl.BlockSpec((r // parts, cols),
                                       functools.partial(lambda s, i, j: (s(i, j), 0), step_of)))
        if col_block is None:
            copy_specs.append(slab_specs[-1])
            copy_shapes.append(jax.ShapeDtypeStruct((r, cols), BF16))
        else:
            assert cols % col_block == 0
            copy_specs.append(pl.BlockSpec(
                (cols // col_block, r // parts, col_block),
                functools.partial(lambda s, i, j: (0, s(i, j), 0), step_of)))
            copy_shapes.append(jax.ShapeDtypeStruct((cols // col_block, r, col_block), BF16))
        slab_bytes += 3 * _nbytes(c.shape, F32) // parts
    est = (2 * (_nbytes((tm, d), BF16) + 3 * _nbytes((d, tn), BF16) + _nbytes((tm, tn), BF16)
                + _nbytes((tail_rows, tn), F32)) + 8 * _nbytes((tm, tn), F32) + slab_bytes)
    return pl.pallas_call(
        functools.partial(_conv_kernel, seq_len=seq_len, tail_rows=tail_rows, has_state=has_state,
                          rows=rows, cast_per_step=cast_per_step),
        out_shape=(jax.ShapeDtypeStruct((m, n), BF16),
                   jax.ShapeDtypeStruct((m // tm * tail_rows, n), F32),
                   *copy_shapes),
        grid=(m // tm, n_j),
        in_specs=in_specs + slab_specs,
        out_specs=(pl.BlockSpec((tm, tn), lambda i, j: (i, j)),
                   pl.BlockSpec((tail_rows, tn), lambda i, j: (i, j)),
                   *copy_specs),
        compiler_params=_params(("parallel", "arbitrary"), est),
        name="conv_branch",
    )(*args, *[c for c, _ in cast])


def _stack_heads(q_ref, rows, col0, group):
    parts = []
    for g in range(group):
        blk = q_ref[rows, col0 + (g // 2) * LANES:col0 + (g // 2 + 1) * LANES]
        keep = (_lane_iota(blk.shape) >= HEAD_DIM) == bool(g % 2)
        parts.append(jnp.where(keep, blk, jnp.zeros_like(blk)))
    return jnp.concatenate(parts, axis=0)


def _sink_column(sink_ref, head0, group, rows):
    return jnp.concatenate(
        [jnp.full((rows, 1), sink_ref[head0 + g], F32) for g in range(group)], axis=0)


def _store_heads(o_ref, rows, col0, o, group, nrows):
    for lp in range(group // 2):
        a = o[(2 * lp) * nrows:(2 * lp + 1) * nrows]
        b = o[(2 * lp + 1) * nrows:(2 * lp + 2) * nrows]
        low = _lane_iota(a.shape) < HEAD_DIM
        o_ref[rows, col0 + lp * LANES:col0 + (lp + 1) * LANES] = jnp.where(low, a, b).astype(BF16)


def _attn_pairs(pairs, sink_ref, q_ref, k_ref, v_ref, o_ref, n_kv, group):
    def block_rows(index):
        start = index * PAIR
        return pl.ds(start if isinstance(start, int) else pl.multiple_of(start, PAIR), PAIR)

    first_q = _lane_iota((CHUNK, PAIR)) < CHUNK

    def keys_of(ref, pair, has_prev, tile):
        cols = slice(tile * LANES, (tile + 1) * LANES)
        if has_prev:
            return jnp.concatenate([ref[block_rows(pair - 1), cols], ref[block_rows(pair), cols]],
                                   axis=0)
        return ref[block_rows(pair), cols]

    def scores(pair, has_prev, kv):
        col0 = kv * group * HEAD_DIM
        q_tiles = jnp.concatenate(
            [q_ref[block_rows(pair), col0 + lp * LANES:col0 + (lp + 1) * LANES]
             for lp in range(group // 2)], axis=0)
        return [_dot_nt(keys_of(k_ref, pair, has_prev, 2 * kv + half), q_tiles)
                for half in range(2)]

    items = [(pair, has_prev, kv) for pair, has_prev in pairs for kv in range(n_kv)]
    st_next = scores(*items[0])
    for n, (pair, has_prev, kv) in enumerate(items):
        st = st_next
        if n + 1 < len(items):
            st_next = scores(*items[n + 1])
        rows = block_rows(pair)
        vt = keys_of(v_ref, pair, has_prev, kv).astype(F32).T.astype(BF16)
        col0 = kv * group * HEAD_DIM
        probs, rden = [], []
        for g in range(group):
            sg = st[g % 2][:, (g // 2) * PAIR:(g // 2 + 1) * PAIR]
            blocks = [sg[c * CHUNK:(c + 1) * CHUNK] for c in range(sg.shape[0] // CHUNK)]
            if has_prev:
                s = jnp.concatenate([jnp.where(first_q, blocks[0], blocks[3]),
                                     blocks[1], blocks[2]], axis=0)
            else:
                s = jnp.concatenate([blocks[0], jnp.where(first_q, NEG_INF, blocks[1])], axis=0)
            sink = sink_ref[kv * group + g]
            m = jnp.maximum(jnp.max(s, axis=0, keepdims=True), sink)
            e = jnp.exp2(s - m)
            den = jnp.sum(e, axis=0, keepdims=True) + jnp.exp2(sink - m)
            if has_prev:
                e_03, e_12 = e[:CHUNK], e[CHUNK:]
                e = jnp.concatenate([jnp.where(first_q, e_03, 0.0), e_12,
                                     jnp.where(first_q, 0.0, e_03)], axis=0)
            probs.append(e.astype(BF16))
            rden.append(1.0 / den)
        ot = _dot(vt, jnp.concatenate(probs, axis=1)) * jnp.concatenate(rden, axis=1)
        for lp in range(group // 2):
            a = ot[:HEAD_DIM, (2 * lp) * PAIR:(2 * lp + 1) * PAIR]
            b = ot[HEAD_DIM:, (2 * lp + 1) * PAIR:(2 * lp + 2) * PAIR]
            o_ref[rows, col0 + lp * LANES:col0 + (lp + 1) * LANES] = (
                jnp.concatenate([a, b], axis=0).T.astype(BF16))


def _attn_kernel(sink_ref, q_ref, k_ref, v_ref, o_ref, *, n_kv, group):
    def run(pairs):
        _attn_pairs(pairs, sink_ref, q_ref, k_ref, v_ref, o_ref, n_kv, group)

    def trip(it, carry):
        run([(1 + PAIRS_PER_TRIP * it + u, True) for u in range(PAIRS_PER_TRIP)])
        return carry

    n_pairs = q_ref.shape[0] // PAIR
    if n_pairs - 1 <= PAIRS_PER_TRIP:
        run([(0, False)] + [(pair, True) for pair in range(1, n_pairs)])
        return
    trips = (n_pairs - 1) // PAIRS_PER_TRIP
    run([(0, False)])
    lax.fori_loop(0, trips, trip, 0)
    leftover = range(1 + trips * PAIRS_PER_TRIP, n_pairs)
    if leftover:
        run([(pair, True) for pair in leftover])


def _attention(q, k2, v2, sinks, *, batch, seq, n_kv, group):
    m, d = q.shape
    assert WINDOW == PAIR and seq % PAIR == 0
    est = (2 * (2 * _nbytes((seq, d), BF16) + 3 * _nbytes((seq, n_kv * LANES), BF16))
           + 8 * _nbytes((group * PAIR, KEY_SPAN), F32))
    return pl.pallas_call(
        functools.partial(_attn_kernel, n_kv=n_kv, group=group),
        out_shape=jax.ShapeDtypeStruct((m, d), BF16),
        grid=(batch,),
        in_specs=[pl.BlockSpec(memory_space=pltpu.SMEM),
                  pl.BlockSpec((seq, d), lambda b: (b, 0)),
                  pl.BlockSpec((seq, 2 * n_kv * LANES), lambda b: (b, 0)),
                  pl.BlockSpec((seq, n_kv * LANES), lambda b: (b, 0))],
        out_specs=pl.BlockSpec((seq, d), lambda b: (b, 0)),
        compiler_params=_params(("parallel",), est),
        name="attention",
    )(sinks, q, k2, v2)


def _attn_sample_kernel(sink_ref, q_ref, kn_ref, vn_ref, kc_ref, vc_ref, o_ref, *, n_kv, group):
    t = q_ref.shape[0]
    for kv in range(n_kv):
        pair_sl = slice((kv // 2) * LANES, (kv // 2 + 1) * LANES)
        kc = _dup_halves(kc_ref[:, pair_sl])[kv % 2].astype(BF16)
        vc = _dup_halves(vc_ref[:, pair_sl])[kv % 2].astype(BF16)
        kn = (kn_ref[:, 2 * kv * LANES:(2 * kv + 1) * LANES]
              + kn_ref[:, (2 * kv + 1) * LANES:(2 * kv + 2) * LANES])
        vn = vn_ref[:, kv * LANES:(kv + 1) * LANES]
        col0 = kv * group * HEAD_DIM
        lhs = _stack_heads(q_ref, slice(0, t), col0, group)
        s_c = _dot_nt(lhs, kc)
        s_n = _dot_nt(lhs, kn)
        sink = _sink_column(sink_ref, kv * group, group, t)
        m = jnp.maximum(jnp.maximum(jnp.max(s_c, axis=1, keepdims=True),
                                    jnp.max(s_n, axis=1, keepdims=True)), sink)
        e_c = jnp.exp2(s_c - m)
        e_n = jnp.exp2(s_n - m)
        den = (jnp.sum(e_c, axis=1, keepdims=True) + jnp.sum(e_n, axis=1, keepdims=True)
               + jnp.exp2(sink - m))
        o = (_dot(e_c.astype(BF16), vc) + _dot(e_n.astype(BF16), vn)) / den
        _store_heads(o_ref, slice(0, t), col0, o, group, t)


def _attention_sample(q, k2, v2, cache_k, cache_v, sinks, *, batch, t_new, n_kv, group):
    m, d = q.shape
    win = cache_k.shape[0] // batch
    d_kv = cache_k.shape[1]
    est = (2 * (2 * _nbytes((t_new, d), BF16) + 3 * _nbytes((t_new, n_kv * LANES), BF16)
                + 2 * _nbytes((win, d_kv), F32)) + 8 * _nbytes((group * t_new, win), F32))
    return pl.pallas_call(
        functools.partial(_attn_sample_kernel, n_kv=n_kv, group=group),
        out_shape=jax.ShapeDtypeStruct((m, d), BF16),
        grid=(batch,),
        in_specs=[pl.BlockSpec(memory_space=pltpu.SMEM),
                  pl.BlockSpec((t_new, d), lambda b: (b, 0)),
                  pl.BlockSpec((t_new, 2 * n_kv * LANES), lambda b: (b, 0)),
                  pl.BlockSpec((t_new, n_kv * LANES), lambda b: (b, 0)),
                  pl.BlockSpec((win, d_kv), lambda b: (b, 0)),
                  pl.BlockSpec((win, d_kv), lambda b: (b, 0))],
        out_specs=pl.BlockSpec((t_new, d), lambda b: (b, 0)),
        compiler_params=_params(("parallel",), est),
        name="attention_sample",
    )(sinks, q, k2, v2, cache_k, cache_v)


def _merge_kernel(xn_ref, a_ref, z_ref, wga_ref, wgc_ref, wap_ref, wcp_ref, o_ref):
    xn = xn_ref[...]
    attn = jax.nn.sigmoid(_dot(xn, wga_ref[...])) * _dot(a_ref[...], wap_ref[...])
    conv = jax.nn.sigmoid(_dot(xn, wgc_ref[...])) * _dot(z_ref[...], wcp_ref[...])
    o_ref[...] = (attn + conv).astype(BF16)


def _merge(xn, attn, z, w_in, w_ap, w_cp, *, tm, tn, col_ga, col_gc):
    m, d = xn.shape
    act = pl.BlockSpec((tm, d), lambda i, j: (i, 0))
    wspec = lambda off: pl.BlockSpec((d, tn), lambda i, j: (0, off // tn + j))
    est = (2 * (3 * _nbytes((tm, d), BF16) + 4 * _nbytes((d, tn), BF16) + _nbytes((tm, tn), BF16))
           + 6 * _nbytes((tm, tn), F32))
    return pl.pallas_call(
        _merge_kernel,
        out_shape=jax.ShapeDtypeStruct((m, d), BF16),
        grid=(m // tm, d // tn),
        in_specs=[act, act, act, wspec(col_ga), wspec(col_gc), wspec(0), wspec(0)],
        out_specs=pl.BlockSpec((tm, tn), lambda i, j: (i, j)),
        compiler_params=_params(("parallel", "arbitrary"), est),
        name="merge",
    )(xn, attn, z, w_in, w_in, w_ap, w_cp)


def _out_kernel(x_ref, m_ref, w_ref, g_ref, x1_ref, xn_ref, *, rows):
    for r in range(x_ref.shape[0] // rows):
        rs = slice(r * rows, (r + 1) * rows)
        x1 = x_ref[rs, :] + _dot(m_ref[rs, :], w_ref[...])
        x1_ref[rs, :] = x1
        xn_ref[rs, :] = _rmsnorm_rows(x1, g_ref[...]).astype(BF16)


def _out_proj(x, merged, w_out, g_ffn, *, tm, rows):
    m, d = x.shape
    assert m % tm == 0 and tm % rows == 0
    est = (2 * (2 * _nbytes((tm, d), F32) + 2 * _nbytes((tm, d), BF16)) + _nbytes((d, d), BF16)
           + 4 * _nbytes((rows, d), F32))
    return pl.pallas_call(
        functools.partial(_out_kernel, rows=rows),
        out_shape=(jax.ShapeDtypeStruct((m, d), F32), jax.ShapeDtypeStruct((m, d), BF16)),
        grid=(m // tm,),
        in_specs=[pl.BlockSpec((tm, d), lambda i: (i, 0)),
                  pl.BlockSpec((tm, d), lambda i: (i, 0)),
                  pl.BlockSpec((d, d), lambda i: (0, 0), pipeline_mode=pl.Buffered(1)),
                  pl.BlockSpec((1, d), lambda i: (0, 0))],
        out_specs=(pl.BlockSpec((tm, d), lambda i: (i, 0)),
                   pl.BlockSpec((tm, d), lambda i: (i, 0))),
        compiler_params=_params(("parallel",), est),
        name="out_proj",
    )(x, merged, w_out, g_ffn)


def _ffn_kernel(x_ref, xn_ref, wg_ref, wu_ref, wd_ref, o_ref):
    f = pl.program_id(1)
    slab = x_ref.shape[1]
    n_slabs = o_ref.shape[1] // slab

    @pl.when(f == 0)
    def _():
        o_ref[...] = jnp.zeros_like(o_ref)

    xn = xn_ref[...]
    h = jax.nn.silu(_dot(xn, wg_ref[...])) * _dot(xn, wu_ref[...])
    o_ref[...] += _dot(h.astype(BF16), wd_ref[...])

    for s in range(n_slabs):
        @pl.when(f == s)
        def _(s=s):
            o_ref[:, s * slab:(s + 1) * slab] += x_ref[...]


def _ffn(x, xn, w_gate, w_up, w_down, *, tm, tf, slab):
    m, d = x.shape
    d_ff = w_down.shape[0]
    n_slabs = d // slab
    assert n_slabs <= d_ff // tf and w_gate.shape == w_up.shape == (d_ff // tf, d, tf)
    est = (2 * (_nbytes((tm, slab), F32) + _nbytes((tm, d), F32) + _nbytes((tm, d), BF16)
                + 3 * _nbytes((d, tf), BF16))
           + 4 * _nbytes((tm, tf), F32) + _nbytes((tm, d), F32))
    row = lambda i, f: (i, 0)
    return pl.pallas_call(
        _ffn_kernel,
        out_shape=jax.ShapeDtypeStruct((m, d), F32),
        grid=(m // tm, d_ff // tf),
        in_specs=[pl.BlockSpec((tm, slab), lambda i, f: (i, jnp.minimum(f, n_slabs - 1))),
                  pl.BlockSpec((tm, d), row),
                  pl.BlockSpec((None, d, tf), lambda i, f: (f, 0, 0)),
                  pl.BlockSpec((None, d, tf), lambda i, f: (f, 0, 0)),
                  pl.BlockSpec((tf, d), lambda i, f: (f, 0))],
        out_specs=pl.BlockSpec((tm, d), row),
        compiler_params=_params(("parallel", "arbitrary"), est),
        name="ffn",
    )(x, xn, w_gate, w_up, w_down)


def _rope_tables(pos, repeat, g, scale):
    half = HEAD_DIM // 2
    inv = jnp.exp(-math.log(ROPE_THETA) * jnp.arange(half, dtype=F32) * (2.0 / HEAD_DIM))
    ang = pos.astype(F32)[:, None] * inv[None, :]
    cos = jnp.cos(ang)
    sin = jnp.sin(ang)
    g_swap = jnp.concatenate([g[half:], g[:half]])
    t_same = g[None, :] * jnp.concatenate([cos, cos], axis=-1) * scale
    t_swap = g_swap[None, :] * jnp.concatenate([-sin, sin], axis=-1) * scale
    reps = (repeat, LANES // HEAD_DIM)
    return jnp.tile(t_same, reps), jnp.tile(t_swap, reps)


LATE_WEIGHTS = ("w_attn_proj", "w_conv_proj", "w_out", "w_ffn_gate", "w_ffn_up", "w_ffn_down")
FFN_IN_WEIGHTS = ("w_ffn_gate", "w_ffn_up")


def _layer(x, w, *, batch, seq, pos, tiles, conv_prev, sample_cache):
    d = x.shape[1]
    n_heads = w["sinks"].shape[0]
    d_attn = n_heads * HEAD_DIM
    d_kv = w["d_kv"]
    n_kv = d_kv // HEAD_DIM
    group = n_heads // n_kv
    d_conv = w["conv_w"].shape[1]
    col_b = 0
    col_c = col_b + d_conv
    col_h = col_c + d_conv
    col_ga = col_h + d_conv
    col_gc = col_ga + d
    tm = tiles["tm"]

    rope_rows = max(seq, tiles["tm_qkv"])
    log2e = math.log2(math.e)
    tables = (_rope_tables(pos, rope_rows // seq, w["q_norm_g"], HEAD_DIM ** -0.5 * log2e)
              + _rope_tables(pos, rope_rows // seq, w["k_norm_g"], 1.0))
    sinks = w["sinks"] * log2e

    w_f32 = None if "w_rest" in w else w["w_in_f32"]
    xn, q, k, v, k2, v2, *w_rest = _qkv_proj(
        x, w["norm_mix_g"], w["w_qkv"], tables, tiles["kv_tail"], w_f32,
        tm=tiles["tm_qkv"], d_attn=d_attn, d_kv=d_kv, rows=tiles["qkv_rows"])
    if w_rest:
        w = dict(w, w_rest=w_rest[0])
    late = [name for name in LATE_WEIGHTS if w[name].dtype != BF16]
    z, u_tail, *late_bf16 = _conv_branch(
        xn, w["w_rest"], w["conv_w"], conv_prev,
        [(w[name], tiles["tf"] if name in FFN_IN_WEIGHTS else None) for name in late],
        tm=tiles["tm_conv"], tn=tiles["tn_conv"], rows=tiles["conv_rows"], seq_len=seq,
        tail_rows=tiles["conv_tail"], col_b=col_b, col_c=col_c, col_h=col_h)
    w = dict(w, **dict(zip(late, late_bf16)))
    if sample_cache is None:
        attn = _attention(q, k2, v2, sinks, batch=batch, seq=seq, n_kv=n_kv, group=group)
    else:
        attn = _attention_sample(q, k2, v2, sample_cache[0], sample_cache[1], sinks,
                                 batch=batch, t_new=seq, n_kv=n_kv, group=group)
    merged = _merge(xn, attn, z, w["w_rest"], w["w_attn_proj"], w["w_conv_proj"],
                    tm=tm, tn=tiles["tn_merge"], col_ga=col_ga, col_gc=col_gc)
    x1, xn_ffn = _out_proj(x, merged, w["w_out"], w["norm_ffn_g"],
                           tm=tiles["tm_out"], rows=tiles["out_rows"])
    y = _ffn(x1, xn_ffn, w["w_ffn_gate"], w["w_ffn_up"], w["w_ffn_down"],
             tm=tiles["tm_ffn"], tf=tiles["tf"], slab=V7X_MXU_DIM)
    return y, k, v, u_tail, w


def kernel(x_prompt, x_sample, cache_k, cache_v, state_conv, norm_mix_g, w_in, q_norm_g, k_norm_g,
           attn_sinks, conv_w, w_attn_proj, w_conv_proj, w_out, norm_ffn_g, w_ffn_gate, w_ffn_up,
           w_ffn_down):
    depth = w_in.shape[0]
    assert depth == 1
    b_p, s_p, d = x_prompt.shape
    b_s, s_s, _ = x_sample.shape
    n_kv = cache_k.shape[3]
    d_kv = n_kv * HEAD_DIM
    win = cache_k.shape[2]
    n_state = state_conv.shape[2]

    w = {
        "d_kv": d_kv,
        "norm_mix_g": norm_mix_g[0].reshape(1, d),
        "w_qkv": w_in[0][:, :attn_sinks.shape[1] * HEAD_DIM + 2 * d_kv].astype(BF16),
        "w_in_f32": w_in[0],
        "q_norm_g": q_norm_g[0],
        "k_norm_g": k_norm_g[0],
        "sinks": attn_sinks[0],
        "conv_w": conv_w[0],
        "w_attn_proj": w_attn_proj[0],
        "w_conv_proj": w_conv_proj[0],
        "w_out": w_out[0],
        "norm_ffn_g": norm_ffn_g[0].reshape(1, d),
        "w_ffn_gate": w_ffn_gate[0],
        "w_ffn_up": w_ffn_up[0],
        "w_ffn_down": w_ffn_down[0],
    }

    m_p = b_p * s_p
    tm_p = min(1024, s_p)
    keep = min(WINDOW, s_p)
    tiles_p = dict(tm=tm_p, tm_qkv=min(512, s_p), qkv_rows=min(512, s_p), kv_tail=(s_p, keep), tm_conv=s_p, tn_conv=512, conv_tail=8,
                   conv_rows=min(1024, s_p), tn_merge=512, tm_out=min(512, s_p),
                   out_rows=min(512, s_p),
                   tm_ffn=tm_p, tf=512)
    y_p, k_p, v_p, u_p, w = _layer(x_prompt.reshape(m_p, d), w, batch=b_p, seq=s_p,
                                   pos=jnp.arange(s_p), tiles=tiles_p, conv_prev=None,
                                   sample_cache=None)
    new_k_p = k_p.reshape(b_p, n_kv, HEAD_DIM, keep).transpose(0, 3, 1, 2)[None]
    new_v_p = v_p.reshape(b_p, n_kv, HEAD_DIM, keep).transpose(0, 3, 1, 2)[None]
    new_c_p = u_p.reshape(b_p, tiles_p["conv_tail"], -1)[:, tiles_p["conv_tail"] - n_state:][None]

    m_s = b_s * s_s
    st = state_conv[0]
    prev1 = jnp.pad(st[:, 1:2], ((0, 0), (0, s_s - 1), (0, 0))).reshape(m_s, -1)
    prev2 = jnp.pad(st, ((0, 0), (0, s_s - n_state), (0, 0))).reshape(m_s, -1)
    tiles_s = dict(tm=m_s, tm_qkv=m_s, qkv_rows=m_s, kv_tail=None, tm_conv=m_s, tn_conv=512, conv_tail=m_s,
                   conv_rows=m_s, tn_merge=512, tm_out=m_s, out_rows=m_s, tm_ffn=m_s, tf=512)
    y_s, k_s, v_s, u_s, _ = _layer(x_sample.reshape(m_s, d), w, batch=b_s, seq=s_s,
                                   pos=PAST_LEN + jnp.arange(s_s), tiles=tiles_s,
                                   conv_prev=(prev1, prev2),
                                   sample_cache=(cache_k[0].reshape(b_s * win, d_kv),
                                                 cache_v[0].reshape(b_s * win, d_kv)))
    new_k_s = k_s.reshape(b_s, s_s, n_kv, HEAD_DIM)[None]
    new_v_s = v_s.reshape(b_s, s_s, n_kv, HEAD_DIM)[None]
    new_c_s = u_s.reshape(b_s, s_s, -1)[:, s_s - n_state:][None]

    return (y_p.reshape(b_p, s_p, d), y_s.reshape(b_s, s_s, d), new_k_p, new_v_p, new_c_p,
            new_k_s, new_v_s, new_c_s)
```

```python
import functools
import math

import jax
import jax.numpy as jnp
from jax import lax
from jax.experimental import pallas as pl
from jax.experimental.pallas import tpu as pltpu

F32 = jnp.float32
BF16 = jnp.bfloat16

CHUNK = 64
WINDOW = 128
PAST_LEN = 1024
ROPE_THETA = 10000.0
EPS = 1e-6
NEG_INF = -1e30
HEAD_DIM = 64
LANES = 128
BF16_SUBLANES = 16
PAIR = 2 * CHUNK
KEY_SPAN = PAIR + WINDOW
FFN_HIDDEN_TILE = 512
PAIRS_PER_TRIP = 15

V7X_MXU_DIM = 256
MIB = 1024 * 1024
V7X_VMEM_BYTES = 64 * MIB
VMEM_COMPILER_RESERVE_BYTES = 6 * MIB
VMEM_CAP_BYTES = V7X_VMEM_BYTES - VMEM_COMPILER_RESERVE_BYTES
VMEM_FLOOR_BYTES = 16 * MIB


def _params(semantics, vmem_estimate):
    limit = int(min(max(vmem_estimate, VMEM_FLOOR_BYTES), VMEM_CAP_BYTES))
    return pltpu.CompilerParams(dimension_semantics=semantics, vmem_limit_bytes=limit)


def _nbytes(shape, dtype):
    return math.prod(shape) * jnp.dtype(dtype).itemsize


def _dot(a, b):
    return jnp.dot(a, b, preferred_element_type=F32)


def _dot_nt(a, b):
    return lax.dot_general(a, b, (((1,), (1,)), ((), ())), preferred_element_type=F32)


def _lane_iota(shape):
    return lax.broadcasted_iota(jnp.int32, shape, 1)


def _rmsnorm_rows(x, g):
    y = x * lax.rsqrt(jnp.mean(x * x, axis=-1, keepdims=True) + EPS)
    return y * g


def _headnorm_rope(y, t_same, t_swap, seg):
    width = y.shape[1]
    s = y * y
    s_hi = s.astype(BF16)
    s_lo = (s - s_hi.astype(F32)).astype(BF16)
    ms = _dot(s_hi, seg) + _dot(s_lo, seg)
    half = HEAD_DIM // 2
    ahead = pltpu.roll(y, width - half, axis=1)
    behind = pltpu.roll(y, half, axis=1)
    swapped = jnp.where((_lane_iota(y.shape) & (HEAD_DIM - 1)) < half, ahead, behind)
    reps = width // t_same.shape[1]
    t_same = jnp.concatenate([t_same] * reps, axis=1)
    t_swap = jnp.concatenate([t_swap] * reps, axis=1)
    return (y * t_same + swapped * t_swap) * lax.rsqrt(ms + EPS)


def _dup_halves(x):
    low = _lane_iota(x.shape) < HEAD_DIM
    r = pltpu.roll(x, HEAD_DIM, axis=1)
    return jnp.where(low, x, r), jnp.where(low, r, x)


def _isolate_halves(x):
    low = _lane_iota(x.shape) < HEAD_DIM
    r = pltpu.roll(x, HEAD_DIM, axis=1)
    return (jnp.where(low, x, 0.0), jnp.where(low, 0.0, r),
            jnp.where(low, r, 0.0), jnp.where(low, 0.0, x))


def _qkv_kernel(*refs, n_q_blocks, d_kv, rows, tail_keep, n_rest):
    x_ref, g_ref, w_ref, seg_ref, qs_ref, qw_ref, ks_ref, kw_ref = refs[:8]
    rest_refs = refs[8:8 + n_rest]
    xn_ref, q_ref, k_ref, v_ref, k2_ref, v2_ref = refs[8 + n_rest:][:6]
    rest_out = refs[-1] if n_rest else None
    tm = q_ref.shape[0]
    tn = 2 * d_kv
    chunks = [slice(r * rows, (r + 1) * rows) for r in range(tm // rows)]
    seg = seg_ref[...]
    slab = seg.shape[0]

    for rs in chunks:
        xn = _rmsnorm_rows(x_ref[rs, :], g_ref[...]).astype(BF16)
        xn_ref[rs, :] = xn
        for jb in range(n_q_blocks):
            y = _dot(xn, w_ref[:, jb * tn:(jb + 1) * tn])
            for c in range(tn // slab):
                sl = slice(c * slab, (c + 1) * slab)
                q_ref[rs, jb * tn + c * slab:jb * tn + (c + 1) * slab] = _headnorm_rope(
                    y[:, sl], qs_ref[rs, :], qw_ref[rs, :], seg).astype(BF16)

        y = _dot(xn, w_ref[:, n_q_blocks * tn:])
        k_all = jnp.concatenate(
            [_headnorm_rope(y[:, c * slab:(c + 1) * slab], ks_ref[rs, :], kw_ref[rs, :], seg)
             for c in range(d_kv // slab)], axis=1)
        if tail_keep is None:
            k_ref[rs, :] = k_all
            v_ref[rs, :] = y[:, d_kv:]
        elif rs is chunks[-1]:
            k_ref[0] = k_all[rows - tail_keep:].T
            v_ref[0] = y[rows - tail_keep:, d_kv:].T
        for c in range(d_kv // LANES):
            sl = slice(c * LANES, (c + 1) * LANES)
            k = k_all[:, sl]
            v = y[:, d_kv + c * LANES:d_kv + (c + 1) * LANES]
            va, vb = _dup_halves(v)
            lo = slice(2 * c * LANES, (2 * c + 1) * LANES)
            hi = slice((2 * c + 1) * LANES, (2 * c + 2) * LANES)
            v2_ref[rs, lo] = va.astype(BF16)
            v2_ref[rs, hi] = vb.astype(BF16)
            for n, piece in enumerate(_isolate_halves(k)):
                k2_ref[rs, (4 * c + n) * LANES:(4 * c + n + 1) * LANES] = piece.astype(BF16)

        for n, ref in list(enumerate(rest_refs))[chunks.index(rs)::len(chunks)]:
            width = ref.shape[1]
            rest_out[:, n * width:(n + 1) * width] = ref[...].astype(BF16)


def _qkv_proj(x, g, w_in, tables, tail, w_f32, *, tm, d_attn, d_kv, rows):
    m, d = x.shape
    tn = 2 * d_kv
    n_q_blocks = d_attn // tn
    qkv_cols = d_attn + tn
    assert m % tm == 0 and tm % rows == 0 and d_attn % tn == 0 and w_in.shape == (d, qkv_cols)
    n_tiles = m // tm
    rest_specs, rest_shape, rest_out_spec, n_rest = [], (), (), 0
    if w_f32 is not None:
        n_rest = (w_f32.shape[1] - qkv_cols) // qkv_cols
        assert w_f32.shape[1] == (n_rest + 1) * qkv_cols and d % (n_tiles * BF16_SUBLANES) == 0
        slab_rows = d // n_tiles
        rest_specs = [pl.BlockSpec((slab_rows, qkv_cols), functools.partial(
            lambda n, i: (i, n + 1), n)) for n in range(n_rest)]
        rest_shape = (jax.ShapeDtypeStruct((d, n_rest * qkv_cols), BF16),)
        rest_out_spec = (pl.BlockSpec((slab_rows, n_rest * qkv_cols), lambda i: (i, 0)),)
    if tail is None:
        kv_shape = jax.ShapeDtypeStruct((m, d_kv), F32)
        kv_spec = pl.BlockSpec((tm, d_kv), lambda i: (i, 0))
    else:
        seq, keep = tail
        assert seq % tm == 0 and keep <= rows and keep % LANES == 0
        kv_shape = jax.ShapeDtypeStruct((m // seq, d_kv, keep), F32)
        kv_spec = pl.BlockSpec((1, d_kv, keep), lambda i: (i // (seq // tm), 0, 0))
    rope_blocks = tables[0].shape[0] // tm
    tab = pl.BlockSpec((tm, LANES), lambda i: (i % rope_blocks, 0))
    row = lambda width: pl.BlockSpec((tm, width), lambda i: (i, 0))
    fixed = lambda shape: pl.BlockSpec(shape, lambda i: (0, 0), pipeline_mode=pl.Buffered(1))
    head_of = jnp.arange(V7X_MXU_DIM) // HEAD_DIM
    seg = jnp.where(head_of[:, None] == head_of[None, :], 1.0 / HEAD_DIM, 0.0).astype(BF16)
    est = (2 * (_nbytes((tm, d), F32) + _nbytes((tm, d), BF16) + _nbytes((tm, d_attn), BF16)
                + 3 * _nbytes((tm, tn), BF16) + 2 * _nbytes((tm, d_kv), F32)
                + 4 * _nbytes((tm, LANES), F32))
           + _nbytes(w_in.shape, BF16) + 6 * _nbytes((rows, tn), F32) + 2 * _nbytes((rows, d), F32))
    if n_rest:
        est += 3 * _nbytes((d // n_tiles, n_rest * qkv_cols), F32)
    return pl.pallas_call(
        functools.partial(_qkv_kernel, n_q_blocks=n_q_blocks, d_kv=d_kv, rows=rows,
                          tail_keep=None if tail is None else tail[1], n_rest=n_rest),
        out_shape=(jax.ShapeDtypeStruct((m, d), BF16), jax.ShapeDtypeStruct((m, d_attn), BF16),
                   kv_shape, kv_shape,
                   jax.ShapeDtypeStruct((m, 2 * tn), BF16), jax.ShapeDtypeStruct((m, tn), BF16),
                   *rest_shape),
        grid=(n_tiles,),
        in_specs=[row(d), fixed((1, d)), fixed(w_in.shape), fixed(seg.shape),
                  tab, tab, tab, tab] + rest_specs,
        out_specs=(row(d), row(d_attn), kv_spec, kv_spec, row(2 * tn), row(tn), *rest_out_spec),
        compiler_params=_params(("arbitrary",), est),
        name="qkv_proj",
    )(x, g, w_in, seg, *tables, *([w_f32] * n_rest))


def _conv_kernel(*refs, seq_len, tail_rows, has_state, rows, cast_per_step):
    n_cast = len(cast_per_step)
    n_in = 7 if has_state else 5
    xn_ref, wb_ref, wc_ref, wh_ref, cw_ref = refs[:5]
    p1_ref, p2_ref = refs[5:n_in] if has_state else (None, None)
    z_ref, u_ref = refs[n_in + n_cast:n_in + n_cast + 2]
    casts = list(zip(refs[n_in:n_in + n_cast], refs[n_in + n_cast + 2:]))
    tm, tn = z_ref.shape
    w = cw_ref[...]
    row = lax.broadcasted_iota(jnp.int32, (rows, tn), 0)
    last1 = last2 = None
    for r in range(tm // rows):
        rs = slice(r * rows, (r + 1) * rows)
        xn = xn_ref[rs, :]
        u = _dot(xn, wc_ref[...]) * _dot(xn, wh_ref[...])
        u1 = pltpu.roll(u, 1, axis=0)
        u2 = pltpu.roll(u, 2, axis=0)
        if r > 0:
            u1 = jnp.where(row == 0, last1, u1)
            u2 = jnp.where(row == 0, last2, jnp.where(row == 1, last1, u2))
        if (r * rows) % seq_len == 0 or seq_len < rows:
            pos = (row + r * rows) & (seq_len - 1)
            u1 = jnp.where(pos == 0, p1_ref[rs, :] if has_state else 0.0, u1)
            u2 = jnp.where(pos < 2, p2_ref[rs, :] if has_state else 0.0, u2)
        y = u2 * w[0:1] + u1 * w[1:2] + u * w[2:3]
        z_ref[rs, :] = (_dot(xn, wb_ref[...]) * y).astype(BF16)
        last1, last2 = u[rows - 1:rows], u[rows - 2:rows - 1]
        for (src_ref, dst_ref), per_step in list(zip(casts, cast_per_step))[r::tm // rows]:
            if per_step:
                _copy_slab(src_ref, dst_ref)
            else:
                pl.when(pl.program_id(1) == 0)(functools.partial(_copy_slab, src_ref, dst_ref))
    u_ref[...] = u[rows - tail_rows:]


def _copy_slab(src_ref, dst_ref):
    if len(dst_ref.shape) == 2:
        dst_ref[...] = src_ref[...].astype(BF16)
    else:
        cb = dst_ref.shape[2]
        for b in range(dst_ref.shape[0]):
            dst_ref[b] = src_ref[:, b * cb:(b + 1) * cb].astype(BF16)


def _conv_branch(xn, w_in, conv_w, prev, cast, *, tm, tn, rows, seq_len, tail_rows,
                 col_b, col_c, col_h):
    m, d = xn.shape
    n = conv_w.shape[1]
    has_state = prev is not None
    assert seq_len & (seq_len - 1) == 0 and tm % seq_len == 0
    assert tm % rows == 0 and tail_rows <= rows and (rows % seq_len == 0 or seq_len % rows == 0)
    n_j = n // tn
    steps = (m // tm) * n_j
    wspec = lambda off: pl.BlockSpec((d, tn), lambda i, j: (0, off // tn + j))
    in_specs = [pl.BlockSpec((tm, d), lambda i, j: (i, 0)),
                wspec(col_b), wspec(col_c), wspec(col_h),
                pl.BlockSpec((conv_w.shape[0], tn), lambda i, j: (0, j))]
    args = [xn, w_in, w_in, w_in, conv_w]
    if has_state:
        in_specs += [pl.BlockSpec((tm, tn), lambda i, j: (i, j))] * 2
        args += list(prev)
    cast_per_step = tuple(c.shape[0] % (steps * BF16_SUBLANES) == 0 for c, _ in cast)
    slab_specs, copy_specs, copy_shapes, slab_bytes = [], [], [], 0
    for (c, col_block), per_step in zip(cast, cast_per_step):
        parts = steps if per_step else m // tm
        r, cols = c.shape
        assert r % (parts * BF16_SUBLANES) == 0
        step_of = (lambda i, j: i * n_j + j) if per_step else (lambda i, j: i)
        slab_specs.append(pl.BlockSpec((r // parts, cols),
                                       functools.partial(lambda s, i, j: (s(i, j), 0), step_of)))
        if col_block is None:
            copy_specs.append(slab_specs[-1])
            copy_shapes.append(jax.ShapeDtypeStruct((r, cols), BF16))
        else:
            assert cols % col_block == 0
            copy_specs.append(pl.BlockSpec(
                (cols // col_block, r // parts, col_block),
                functools.partial(lambda s, i, j: (0, s(i, j), 0), step_of)))
            copy_shapes.append(jax.ShapeDtypeStruct((cols // col_block, r, col_block), BF16))
        slab_bytes += 3 * _nbytes(c.shape, F32) // parts
    est = (2 * (_nbytes((tm, d), BF16) + 3 * _nbytes((d, tn), BF16) + _nbytes((tm, tn), BF16)
                + _nbytes((tail_rows, tn), F32)) + 8 * _nbytes((tm, tn), F32) + slab_bytes)
    return pl.pallas_call(
        functools.partial(_conv_kernel, seq_len=seq_len, tail_rows=tail_rows, has_state=has_state,
                          rows=rows, cast_per_step=cast_per_step),
        out_shape=(jax.ShapeDtypeStruct((m, n), BF16),
                   jax.ShapeDtypeStruct((m // tm * tail_rows, n), F32),
                   *copy_shapes),
        grid=(m // tm, n_j),
        in_specs=in_specs + slab_specs,
        out_specs=(pl.BlockSpec((tm, tn), lambda i, j: (i, j)),
                   pl.BlockSpec((tail_rows, tn), lambda i, j: (i, j)),
                   *copy_specs),
        compiler_params=_params(("parallel", "arbitrary"), est),
        name="conv_branch",
    )(*args, *[c for c, _ in cast])


def _stack_heads(q_ref, rows, col0, group):
    parts = []
    for g in range(group):
        blk = q_ref[rows, col0 + (g // 2) * LANES:col0 + (g // 2 + 1) * LANES]
        keep = (_lane_iota(blk.shape) >= HEAD_DIM) == bool(g % 2)
        parts.append(jnp.where(keep, blk, jnp.zeros_like(blk)))
    return jnp.concatenate(parts, axis=0)


def _sink_column(sink_ref, head0, group, rows):
    return jnp.concatenate(
        [jnp.full((rows, 1), sink_ref[head0 + g], F32) for g in range(group)], axis=0)


def _store_heads(o_ref, rows, col0, o, group, nrows):
    for lp in range(group // 2):
        a = o[(2 * lp) * nrows:(2 * lp + 1) * nrows]
        b = o[(2 * lp + 1) * nrows:(2 * lp + 2) * nrows]
        low = _lane_iota(a.shape) < HEAD_DIM
        o_ref[rows, col0 + lp * LANES:col0 + (lp + 1) * LANES] = jnp.where(low, a, b).astype(BF16)


def _attn_pairs(pairs, sink_ref, q_ref, k_ref, v_ref, o_ref, n_kv, group):
    def block_rows(index):
        start = index * PAIR
        return pl.ds(start if isinstance(start, int) else pl.multiple_of(start, PAIR), PAIR)

    first_q = _lane_iota((CHUNK, PAIR)) < CHUNK

    def keys_of(ref, pair, has_prev, tile):
        cols = slice(tile * LANES, (tile + 1) * LANES)
        if has_prev:
            return jnp.concatenate([ref[block_rows(pair - 1), cols], ref[block_rows(pair), cols]],
                                   axis=0)
        return ref[block_rows(pair), cols]

    def scores(pair, has_prev, kv):
        col0 = kv * group * HEAD_DIM
        q_tiles = jnp.concatenate(
            [q_ref[block_rows(pair), col0 + lp * LANES:col0 + (lp + 1) * LANES]
             for lp in range(group // 2)], axis=0)
        return [_dot_nt(keys_of(k_ref, pair, has_prev, 2 * kv + half), q_tiles)
                for half in range(2)]

    items = [(pair, has_prev, kv) for pair, has_prev in pairs for kv in range(n_kv)]
    st_next = scores(*items[0])
    for n, (pair, has_prev, kv) in enumerate(items):
        st = st_next
        if n + 1 < len(items):
            st_next = scores(*items[n + 1])
        rows = block_rows(pair)
        vt = keys_of(v_ref, pair, has_prev, kv).astype(F32).T.astype(BF16)
        col0 = kv * group * HEAD_DIM
        probs, rden = [], []
        for g in range(group):
            sg = st[g % 2][:, (g // 2) * PAIR:(g // 2 + 1) * PAIR]
            blocks = [sg[c * CHUNK:(c + 1) * CHUNK] for c in range(sg.shape[0] // CHUNK)]
            if has_prev:
                s = jnp.concatenate([jnp.where(first_q, blocks[0], blocks[3]),
                                     blocks[1], blocks[2]], axis=0)
            else:
                s = jnp.concatenate([blocks[0], jnp.where(first_q, NEG_INF, blocks[1])], axis=0)
            sink = sink_ref[kv * group + g]
            m = jnp.maximum(jnp.max(s, axis=0, keepdims=True), sink)
            e = jnp.exp2(s - m)
            den = jnp.sum(e, axis=0, keepdims=True) + jnp.exp2(sink - m)
            if has_prev:
                e_03, e_12 = e[:CHUNK], e[CHUNK:]
                e = jnp.concatenate([jnp.where(first_q, e_03, 0.0), e_12,
                                     jnp.where(first_q, 0.0, e_03)], axis=0)
            probs.append(e.astype(BF16))
            rden.append(1.0 / den)
        ot = _dot(vt, jnp.concatenate(probs, axis=1)) * jnp.concatenate(rden, axis=1)
        for lp in range(group // 2):
            a = ot[:HEAD_DIM, (2 * lp) * PAIR:(2 * lp + 1) * PAIR]
            b = ot[HEAD_DIM:, (2 * lp + 1) * PAIR:(2 * lp + 2) * PAIR]
            o_ref[rows, col0 + lp * LANES:col0 + (lp + 1) * LANES] = (
                jnp.concatenate([a, b], axis=0).T.astype(BF16))


def _attn_kernel(sink_ref, q_ref, k_ref, v_ref, o_ref, *, n_kv, group):
    def run(pairs):
        _attn_pairs(pairs, sink_ref, q_ref, k_ref, v_ref, o_ref, n_kv, group)

    def trip(it, carry):
        run([(1 + PAIRS_PER_TRIP * it + u, True) for u in range(PAIRS_PER_TRIP)])
        return carry

    n_pairs = q_ref.shape[0] // PAIR
    trips = (n_pairs - 1) // PAIRS_PER_TRIP
    run([(0, False)])
    lax.fori_loop(0, trips, trip, 0)
    leftover = range(1 + trips * PAIRS_PER_TRIP, n_pairs)
    if leftover:
        run([(pair, True) for pair in leftover])


def _attention(q, k2, v2, sinks, *, batch, seq, n_kv, group):
    m, d = q.shape
    assert WINDOW == PAIR and seq % PAIR == 0
    est = (2 * (2 * _nbytes((seq, d), BF16) + 3 * _nbytes((seq, n_kv * LANES), BF16))
           + 8 * _nbytes((group * PAIR, KEY_SPAN), F32))
    return pl.pallas_call(
        functools.partial(_attn_kernel, n_kv=n_kv, group=group),
        out_shape=jax.ShapeDtypeStruct((m, d), BF16),
        grid=(batch,),
        in_specs=[pl.BlockSpec(memory_space=pltpu.SMEM),
                  pl.BlockSpec((seq, d), lambda b: (b, 0)),
                  pl.BlockSpec((seq, 2 * n_kv * LANES), lambda b: (b, 0)),
                  pl.BlockSpec((seq, n_kv * LANES), lambda b: (b, 0))],
        out_specs=pl.BlockSpec((seq, d), lambda b: (b, 0)),
        compiler_params=_params(("parallel",), est),
        name="attention",
    )(sinks, q, k2, v2)


def _attn_sample_kernel(sink_ref, q_ref, kn_ref, vn_ref, kc_ref, vc_ref, o_ref, *, n_kv, group):
    t = q_ref.shape[0]
    for kv in range(n_kv):
        pair_sl = slice((kv // 2) * LANES, (kv // 2 + 1) * LANES)
        kc = _dup_halves(kc_ref[:, pair_sl])[kv % 2].astype(BF16)
        vc = _dup_halves(vc_ref[:, pair_sl])[kv % 2].astype(BF16)
        kn = (kn_ref[:, 2 * kv * LANES:(2 * kv + 1) * LANES]
              + kn_ref[:, (2 * kv + 1) * LANES:(2 * kv + 2) * LANES])
        vn = vn_ref[:, kv * LANES:(kv + 1) * LANES]
        col0 = kv * group * HEAD_DIM
        lhs = _stack_heads(q_ref, slice(0, t), col0, group)
        s_c = _dot_nt(lhs, kc)
        s_n = _dot_nt(lhs, kn)
        sink = _sink_column(sink_ref, kv * group, group, t)
        m = jnp.maximum(jnp.maximum(jnp.max(s_c, axis=1, keepdims=True),
                                    jnp.max(s_n, axis=1, keepdims=True)), sink)
        e_c = jnp.exp2(s_c - m)
        e_n = jnp.exp2(s_n - m)
        den = (jnp.sum(e_c, axis=1, keepdims=True) + jnp.sum(e_n, axis=1, keepdims=True)
               + jnp.exp2(sink - m))
        o = (_dot(e_c.astype(BF16), vc) + _dot(e_n.astype(BF16), vn)) / den
        _store_heads(o_ref, slice(0, t), col0, o, group, t)


def _attention_sample(q, k2, v2, cache_k, cache_v, sinks, *, batch, t_new, n_kv, group):
    m, d = q.shape
    win = cache_k.shape[0] // batch
    d_kv = cache_k.shape[1]
    est = (2 * (2 * _nbytes((t_new, d), BF16) + 3 * _nbytes((t_new, n_kv * LANES), BF16)
                + 2 * _nbytes((win, d_kv), F32)) + 8 * _nbytes((group * t_new, win), F32))
    return pl.pallas_call(
        functools.partial(_attn_sample_kernel, n_kv=n_kv, group=group),
        out_shape=jax.ShapeDtypeStruct((m, d), BF16),
        grid=(batch,),
        in_specs=[pl.BlockSpec(memory_space=pltpu.SMEM),
                  pl.BlockSpec((t_new, d), lambda b: (b, 0)),
                  pl.BlockSpec((t_new, 2 * n_kv * LANES), lambda b: (b, 0)),
                  pl.BlockSpec((t_new, n_kv * LANES), lambda b: (b, 0)),
                  pl.BlockSpec((win, d_kv), lambda b: (b, 0)),
                  pl.BlockSpec((win, d_kv), lambda b: (b, 0))],
        out_specs=pl.BlockSpec((t_new, d), lambda b: (b, 0)),
        compiler_params=_params(("parallel",), est),
        name="attention_sample",
    )(sinks, q, k2, v2, cache_k, cache_v)


def _merge_kernel(xn_ref, a_ref, z_ref, wga_ref, wgc_ref, wap_ref, wcp_ref, o_ref):
    xn = xn_ref[...]
    attn = jax.nn.sigmoid(_dot(xn, wga_ref[...])) * _dot(a_ref[...], wap_ref[...])
    conv = jax.nn.sigmoid(_dot(xn, wgc_ref[...])) * _dot(z_ref[...], wcp_ref[...])
    o_ref[...] = (attn + conv).astype(BF16)


def _merge(xn, attn, z, w_in, w_ap, w_cp, *, tm, tn, col_ga, col_gc):
    m, d = xn.shape
    act = pl.BlockSpec((tm, d), lambda i, j: (i, 0))
    wspec = lambda off: pl.BlockSpec((d, tn), lambda i, j: (0, off // tn + j))
    est = (2 * (3 * _nbytes((tm, d), BF16) + 4 * _nbytes((d, tn), BF16) + _nbytes((tm, tn), BF16))
           + 6 * _nbytes((tm, tn), F32))
    return pl.pallas_call(
        _merge_kernel,
        out_shape=jax.ShapeDtypeStruct((m, d), BF16),
        grid=(m // tm, d // tn),
        in_specs=[act, act, act, wspec(col_ga), wspec(col_gc), wspec(0), wspec(0)],
        out_specs=pl.BlockSpec((tm, tn), lambda i, j: (i, j)),
        compiler_params=_params(("parallel", "arbitrary"), est),
        name="merge",
    )(xn, attn, z, w_in, w_in, w_ap, w_cp)


def _out_kernel(x_ref, m_ref, w_ref, g_ref, x1_ref, xn_ref, *, rows):
    for r in range(x_ref.shape[0] // rows):
        rs = slice(r * rows, (r + 1) * rows)
        x1 = x_ref[rs, :] + _dot(m_ref[rs, :], w_ref[...])
        x1_ref[rs, :] = x1
        xn_ref[rs, :] = _rmsnorm_rows(x1, g_ref[...]).astype(BF16)


def _out_proj(x, merged, w_out, g_ffn, *, tm, rows):
    m, d = x.shape
    assert m % tm == 0 and tm % rows == 0
    est = (2 * (2 * _nbytes((tm, d), F32) + 2 * _nbytes((tm, d), BF16)) + _nbytes((d, d), BF16)
           + 4 * _nbytes((rows, d), F32))
    return pl.pallas_call(
        functools.partial(_out_kernel, rows=rows),
        out_shape=(jax.ShapeDtypeStruct((m, d), F32), jax.ShapeDtypeStruct((m, d), BF16)),
        grid=(m // tm,),
        in_specs=[pl.BlockSpec((tm, d), lambda i: (i, 0)),
                  pl.BlockSpec((tm, d), lambda i: (i, 0)),
                  pl.BlockSpec((d, d), lambda i: (0, 0), pipeline_mode=pl.Buffered(1)),
                  pl.BlockSpec((1, d), lambda i: (0, 0))],
        out_specs=(pl.BlockSpec((tm, d), lambda i: (i, 0)),
                   pl.BlockSpec((tm, d), lambda i: (i, 0))),
        compiler_params=_params(("parallel",), est),
        name="out_proj",
    )(x, merged, w_out, g_ffn)


def _ffn_hidden_kernel(xn_ref, wg_ref, wu_ref, h_ref):
    xn = xn_ref[...]
    h_ref[...] = (jax.nn.silu(_dot(xn, wg_ref[...])) * _dot(xn, wu_ref[...])).astype(BF16)


def _ffn_down_kernel(x_ref, h_ref, wd_ref, o_ref):
    o_ref[...] = x_ref[...] + _dot(h_ref[...], wd_ref[...])


def _ffn(x, xn, w_gate, w_up, w_down, *, tm, tm_hidden, tf, tn):
    m, d = x.shape
    d_ff = w_down.shape[0]
    assert w_gate.shape == w_up.shape == (d_ff // tf, d, tf) and m % tm == 0 and d % tn == 0
    th = tm_hidden
    assert m % th == 0
    est = (2 * (_nbytes((th, d), BF16) + 2 * _nbytes((d, tf), BF16) + _nbytes((th, tf), BF16))
           + 4 * _nbytes((th, tf), F32))
    hidden = pl.pallas_call(
        _ffn_hidden_kernel,
        out_shape=jax.ShapeDtypeStruct((m, d_ff), BF16),
        grid=(m // th, d_ff // tf),
        in_specs=[pl.BlockSpec((th, d), lambda i, f: (i, 0)),
                  pl.BlockSpec((None, d, tf), lambda i, f: (f, 0, 0)),
                  pl.BlockSpec((None, d, tf), lambda i, f: (f, 0, 0))],
        out_specs=pl.BlockSpec((th, tf), lambda i, f: (i, f)),
        compiler_params=_params(("parallel", "arbitrary"), est),
        name="ffn_hidden",
    )(xn, w_gate, w_up)
    est = (2 * (2 * _nbytes((tm, tn), F32) + _nbytes((tm, d_ff), BF16) + _nbytes((d_ff, tn), BF16))
           + 2 * _nbytes((tm, tn), F32))
    return pl.pallas_call(
        _ffn_down_kernel,
        out_shape=jax.ShapeDtypeStruct((m, d), F32),
        grid=(m // tm, d // tn),
        in_specs=[pl.BlockSpec((tm, tn), lambda i, j: (i, j)),
                  pl.BlockSpec((tm, d_ff), lambda i, j: (i, 0)),
                  pl.BlockSpec((d_ff, tn), lambda i, j: (0, j))],
        out_specs=pl.BlockSpec((tm, tn), lambda i, j: (i, j)),
        compiler_params=_params(("parallel", "arbitrary"), est),
        name="ffn_down",
    )(x, hidden, w_down)


def _rope_tables(pos, repeat, g, scale):
    half = HEAD_DIM // 2
    inv = jnp.exp(-math.log(ROPE_THETA) * jnp.arange(half, dtype=F32) * (2.0 / HEAD_DIM))
    ang = pos.astype(F32)[:, None] * inv[None, :]
    cos = jnp.cos(ang)
    sin = jnp.sin(ang)
    g_swap = jnp.concatenate([g[half:], g[:half]])
    t_same = g[None, :] * jnp.concatenate([cos, cos], axis=-1) * scale
    t_swap = g_swap[None, :] * jnp.concatenate([-sin, sin], axis=-1) * scale
    reps = (repeat, LANES // HEAD_DIM)
    return jnp.tile(t_same, reps), jnp.tile(t_swap, reps)


LATE_WEIGHTS = ("w_attn_proj", "w_conv_proj", "w_out", "w_ffn_gate", "w_ffn_up", "w_ffn_down")
FFN_IN_WEIGHTS = ("w_ffn_gate", "w_ffn_up")


def _layer(x, w, *, batch, seq, pos, tiles, conv_prev, sample_cache):
    d = x.shape[1]
    n_heads = w["sinks"].shape[0]
    d_attn = n_heads * HEAD_DIM
    d_kv = w["d_kv"]
    n_kv = d_kv // HEAD_DIM
    group = n_heads // n_kv
    d_conv = w["conv_w"].shape[1]
    col_b = 0
    col_c = col_b + d_conv
    col_h = col_c + d_conv
    col_ga = col_h + d_conv
    col_gc = col_ga + d
    tm = tiles["tm"]

    rope_rows = max(seq, tiles["tm_qkv"])
    log2e = math.log2(math.e)
    tables = (_rope_tables(pos, rope_rows // seq, w["q_norm_g"], HEAD_DIM ** -0.5 * log2e)
              + _rope_tables(pos, rope_rows // seq, w["k_norm_g"], 1.0))
    sinks = w["sinks"] * log2e

    w_f32 = None if "w_rest" in w else w["w_in_f32"]
    xn, q, k, v, k2, v2, *w_rest = _qkv_proj(
        x, w["norm_mix_g"], w["w_qkv"], tables, tiles["kv_tail"], w_f32,
        tm=tiles["tm_qkv"], d_attn=d_attn, d_kv=d_kv, rows=tiles["qkv_rows"])
    if w_rest:
        w = dict(w, w_rest=w_rest[0])
    late = [name for name in LATE_WEIGHTS if w[name].dtype != BF16]
    z, u_tail, *late_bf16 = _conv_branch(
        xn, w["w_rest"], w["conv_w"], conv_prev,
        [(w[name], tiles["tf"] if name in FFN_IN_WEIGHTS else None) for name in late],
        tm=tiles["tm_conv"], tn=tiles["tn_conv"], rows=tiles["conv_rows"], seq_len=seq,
        tail_rows=tiles["conv_tail"], col_b=col_b, col_c=col_c, col_h=col_h)
    w = dict(w, **dict(zip(late, late_bf16)))
    if sample_cache is None:
        attn = _attention(q, k2, v2, sinks, batch=batch, seq=seq, n_kv=n_kv, group=group)
    else:
        attn = _attention_sample(q, k2, v2, sample_cache[0], sample_cache[1], sinks,
                                 batch=batch, t_new=seq, n_kv=n_kv, group=group)
    merged = _merge(xn, attn, z, w["w_rest"], w["w_attn_proj"], w["w_conv_proj"],
                    tm=tm, tn=tiles["tn_merge"], col_ga=col_ga, col_gc=col_gc)
    x1, xn_ffn = _out_proj(x, merged, w["w_out"], w["norm_ffn_g"],
                           tm=tiles["tm_out"], rows=tiles["out_rows"])
    y = _ffn(x1, xn_ffn, w["w_ffn_gate"], w["w_ffn_up"], w["w_ffn_down"],
             tm=tiles["tm_ffn"], tf=tiles["tf"], tm_hidden=tiles["tm_hidden"], tn=tiles["tn_down"])
    return y, k, v, u_tail, w


def kernel(x_prompt, x_sample, cache_k, cache_v, state_conv, norm_mix_g, w_in, q_norm_g, k_norm_g,
           attn_sinks, conv_w, w_attn_proj, w_conv_proj, w_out, norm_ffn_g, w_ffn_gate, w_ffn_up,
           w_ffn_down):
    depth = w_in.shape[0]
    assert depth == 1
    b_p, s_p, d = x_prompt.shape
    b_s, s_s, _ = x_sample.shape
    n_kv = cache_k.shape[3]
    d_kv = n_kv * HEAD_DIM
    win = cache_k.shape[2]
    n_state = state_conv.shape[2]

    w = {
        "d_kv": d_kv,
        "norm_mix_g": norm_mix_g[0].reshape(1, d),
        "w_qkv": w_in[0][:, :attn_sinks.shape[1] * HEAD_DIM + 2 * d_kv].astype(BF16),
        "w_in_f32": w_in[0],
        "q_norm_g": q_norm_g[0],
        "k_norm_g": k_norm_g[0],
        "sinks": attn_sinks[0],
        "conv_w": conv_w[0],
        "w_attn_proj": w_attn_proj[0],
        "w_conv_proj": w_conv_proj[0],
        "w_out": w_out[0],
        "norm_ffn_g": norm_ffn_g[0].reshape(1, d),
        "w_ffn_gate": w_ffn_gate[0],
        "w_ffn_up": w_ffn_up[0],
        "w_ffn_down": w_ffn_down[0],
    }

    m_p = b_p * s_p
    tm_p = min(1024, s_p)
    keep = min(WINDOW, s_p)
    tiles_p = dict(tm=tm_p, tm_qkv=min(512, s_p), qkv_rows=min(512, s_p), kv_tail=(s_p, keep), tm_conv=s_p, tn_conv=512, conv_tail=8,
                   conv_rows=min(1024, s_p), tn_merge=512, tm_out=min(512, s_p),
                   out_rows=min(512, s_p),
                   tm_ffn=tm_p, tm_hidden=s_p, tf=FFN_HIDDEN_TILE, tn_down=512)
    y_p, k_p, v_p, u_p, w = _layer(x_prompt.reshape(m_p, d), w, batch=b_p, seq=s_p,
                                   pos=jnp.arange(s_p), tiles=tiles_p, conv_prev=None,
                                   sample_cache=None)
    new_k_p = k_p.reshape(b_p, n_kv, HEAD_DIM, keep).transpose(0, 3, 1, 2)[None]
    new_v_p = v_p.reshape(b_p, n_kv, HEAD_DIM, keep).transpose(0, 3, 1, 2)[None]
    new_c_p = u_p.reshape(b_p, tiles_p["conv_tail"], -1)[:, tiles_p["conv_tail"] - n_state:][None]

    m_s = b_s * s_s
    st = state_conv[0]
    prev1 = jnp.pad(st[:, 1:2], ((0, 0), (0, s_s - 1), (0, 0))).reshape(m_s, -1)
    prev2 = jnp.pad(st, ((0, 0), (0, s_s - n_state), (0, 0))).reshape(m_s, -1)
    tiles_s = dict(tm=m_s, tm_qkv=m_s, qkv_rows=m_s, kv_tail=None, tm_conv=m_s, tn_conv=512, conv_tail=m_s,
                   conv_rows=m_s, tn_merge=512, tm_out=m_s, out_rows=m_s, tm_ffn=m_s,
                   tm_hidden=m_s, tf=FFN_HIDDEN_TILE, tn_down=512)
    y_s, k_s, v_s, u_s, _ = _layer(x_sample.reshape(m_s, d), w, batch=b_s, seq=s_s,
                                   pos=PAST_LEN + jnp.arange(s_s), tiles=tiles_s,
                                   conv_prev=(prev1, prev2),
                                   sample_cache=(cache_k[0].reshape(b_s * win, d_kv),
                                                 cache_v[0].reshape(b_s * win, d_kv)))
    new_k_s = k_s.reshape(b_s, s_s, n_kv, HEAD_DIM)[None]
    new_v_s = v_s.reshape(b_s, s_s, n_kv, HEAD_DIM)[None]
    new_c_s = u_s.reshape(b_s, s_s, -1)[:, s_s - n_state:][None]

    return (y_p.reshape(b_p, s_p, d), y_s.reshape(b_s, s_s, d), new_k_p, new_v_p, new_c_p,
            new_k_s, new_v_s, new_c_s)
```
